```python
import math
import jax
import jax.numpy as jnp
from jax import lax
import numpy as np

D_MODEL = 1024
BATCH = 8
SEQ = 2048
DEPTH = 4

D_MIX = 2 * D_MODEL
SSD_WIDTH = D_MIX // 2
ATTN_WIDTH = D_MIX // 4
CM_CHANNELS = D_MIX // 4
SSD_HEAD_DIM = 64
SSD_HEADS = SSD_WIDTH // SSD_HEAD_DIM
SSD_STATE = 128
SSD_GROUPS = 2
SSD_CONV = 4
SSD_CHUNK = 128
SSD_XBC = SSD_WIDTH + 2 * SSD_GROUPS * SSD_STATE
ATTN_HEAD_DIM = 64
ATTN_Q_HEADS = ATTN_WIDTH // ATTN_HEAD_DIM
ATTN_KV_HEADS = 2
WINDOW = 128
ATTN_BLOCK = WINDOW
ROPE_THETA = 10000.0
CM_CONV_WIDTH = 31
D_FF = 4 * D_MODEL
RMS_EPS = 1e-6
LN_EPS = 1e-5
IN_SIZES = (SSD_WIDTH, SSD_XBC, SSD_HEADS,
            ATTN_Q_HEADS * ATTN_HEAD_DIM, ATTN_KV_HEADS * ATTN_HEAD_DIM, ATTN_KV_HEADS * ATTN_HEAD_DIM,
            2 * CM_CHANNELS)
N_IN = sum(IN_SIZES)

kernel_name = "hybrid_ssd_swa_conformer_parallel_heads"


def _split(u, sizes):
    idx, acc = [], 0
    for s in sizes[:-1]:
        acc += s
        idx.append(acc)
    return jnp.split(u, idx, axis=-1)


def rms_norm(x, w, eps=RMS_EPS):
    xf = x.astype(jnp.float32)
    y = xf * lax.rsqrt(jnp.mean(xf * xf, axis=-1, keepdims=True) + eps)
    return (y * w.astype(jnp.float32)).astype(x.dtype)


def layer_norm(x, w, b, eps=LN_EPS):
    xf = x.astype(jnp.float32)
    mu = jnp.mean(xf, axis=-1, keepdims=True)
    var = jnp.mean(jnp.square(xf - mu), axis=-1, keepdims=True)
    y = (xf - mu) * lax.rsqrt(var + eps)
    return (y * w.astype(jnp.float32) + b.astype(jnp.float32)).astype(x.dtype)


def gated_rms_norm(y, z, w):
    g = y.astype(jnp.float32) * jax.nn.silu(z.astype(jnp.float32))
    shp = g.shape
    g = g.reshape(shp[:-1] + (SSD_GROUPS, shp[-1] // SSD_GROUPS))
    g = g * lax.rsqrt(jnp.mean(g * g, axis=-1, keepdims=True) + RMS_EPS)
    return (g.reshape(shp) * w.astype(jnp.float32)).astype(y.dtype)


def causal_depthwise_conv(u, w, b):
    k_w, ch = w.shape
    out = lax.conv_general_dilated(
        u, w[:, None, :].astype(u.dtype), window_strides=(1,), padding=[(k_w - 1, 0)],
        dimension_numbers=("NWC", "WIO", "NWC"), feature_group_count=ch)
    return out + b.astype(u.dtype)


def rope_tables(seq_len, dim):
    inv_freq = ROPE_THETA ** (-jnp.arange(0, dim, 2, dtype=jnp.float32) / dim)
    ang = jnp.arange(seq_len, dtype=jnp.float32)[:, None] * inv_freq[None, :]
    return jnp.cos(ang), jnp.sin(ang)


def apply_rope(x, cos, sin):
    xf = x.astype(jnp.float32)
    x1, x2 = jnp.split(xf, 2, axis=-1)
    c = cos[None, :, None, :]
    s = sin[None, :, None, :]
    return jnp.concatenate([x1 * c - x2 * s, x2 * c + x1 * s], axis=-1).astype(x.dtype)


def ssd_chunked(x, dt, a, bm, cm, d_skip):
    bsz, seq, nh, hp = x.shape
    q = SSD_CHUNK
    nc = seq // q
    r = nh // SSD_GROUPS
    xf = x.astype(jnp.float32).reshape(bsz, nc, q, SSD_GROUPS, r, hp)
    dtc = dt.reshape(bsz, nc, q, SSD_GROUPS, r)
    bf = bm.astype(jnp.float32).reshape(bsz, nc, q, SSD_GROUPS, SSD_STATE)
    cf = cm.astype(jnp.float32).reshape(bsz, nc, q, SSD_GROUPS, SSD_STATE)
    a_dt = jnp.moveaxis(dtc * a.reshape(SSD_GROUPS, r), 2, -1)
    a_cs = jnp.cumsum(a_dt, axis=-1)
    xdt = xf * dtc[..., None]
    causal = jnp.tril(jnp.ones((q, q), dtype=bool))
    seg = a_cs[..., :, None] - a_cs[..., None, :]
    decay_ls = jnp.exp(jnp.where(causal, seg, -jnp.inf))
    cb = jnp.einsum("bclgn,bcsgn->bcgls", cf, bf)
    y_diag = jnp.einsum("bcgrls,bcsgrp->bclgrp", cb[:, :, :, None] * decay_ls, xdt)
    decay_s = jnp.exp(a_cs[..., -1:] - a_cs)
    states = jnp.einsum("bcsgn,bcgrs,bcsgrp->bcgrpn", bf, decay_s, xdt)
    chunk_decay = jnp.exp(a_cs[..., -1])

    def step(h, inp):
        s_c, d_c = inp
        return h * d_c[..., None, None] + s_c, h

    h0 = jnp.zeros((bsz, SSD_GROUPS, r, hp, SSD_STATE), jnp.float32)
    _, prev = lax.scan(step, h0, (jnp.moveaxis(states, 1, 0), jnp.moveaxis(chunk_decay, 1, 0)))
    prev = jnp.moveaxis(prev, 0, 1)
    y_off = jnp.einsum("bclgn,bcgrpn,bcgrl->bclgrp", cf, prev, jnp.exp(a_cs))
    y = y_diag + y_off + xf * d_skip.astype(jnp.float32).reshape(SSD_GROUPS, r)[..., None]
    return y.reshape(bsz, seq, nh * hp).astype(x.dtype)


def sliding_window_gqa(q, k, v, sinks):
    bsz, seq, hq, hd = q.shape
    hkv = k.shape[2]
    r = hq // hkv
    blk = ATTN_BLOCK
    nb = seq // blk
    qb = q.astype(jnp.float32).reshape(bsz, nb, blk, hkv, r, hd)

    def with_prev(t):
        tb = t.reshape(bsz, nb, blk, hkv, hd)
        tp = jnp.pad(tb, ((0, 0), (1, 0), (0, 0), (0, 0), (0, 0)))[:, :-1]
        return jnp.concatenate([tp, tb], axis=2)

    kc = with_prev(k).astype(jnp.float32)
    vc = with_prev(v)
    s = jnp.einsum("bnqhrd,bnkhd->bnhrqk", qb, kc) * (1.0 / math.sqrt(hd))
    qi = jnp.arange(blk)[:, None]
    ki = jnp.arange(2 * blk)[None, :]
    diff = qi + blk - ki
    band = (diff >= 0) & (diff < WINDOW)
    kpos = jnp.arange(nb)[:, None, None] * blk + ki[None] - blk
    mask = band[None] & (kpos >= 0)
    s = jnp.where(mask[None, :, None, None], s, -jnp.inf)
    sink = jnp.broadcast_to(sinks.astype(jnp.float32).reshape(1, 1, hkv, r, 1, 1), s.shape[:-1] + (1,))
    p = jax.nn.softmax(jnp.concatenate([s, sink], axis=-1), axis=-1)[..., :-1]
    o = jnp.einsum("bnhrqk,bnkhd->bnqhrd", p.astype(v.dtype), vc)
    return o.reshape(bsz, seq, hq * hd)


def conformer_conv(u, dw_w, dw_b, ln_w, ln_b):
    a, g = jnp.split(u, 2, axis=-1)
    h = a * jax.nn.sigmoid(g)
    h = causal_depthwise_conv(h, dw_w, dw_b)
    return jax.nn.silu(layer_norm(h, ln_w, ln_b))


def hybrid_layer(x, cos, sin, norm_mix_w, w_in, ssd_conv_w, ssd_conv_b, ssd_dt_bias, ssd_a_log,
                 ssd_d, ssd_norm_w, q_norm_w, k_norm_w, attn_sinks, cm_dw_w, cm_dw_b,
                 cm_ln_w, cm_ln_b, w_out, norm_mlp_w, w_mlp_up, w_mlp_down):
    bsz, seq, _ = x.shape
    h = rms_norm(x, norm_mix_w)
    u = h @ w_in
    z, xbc, dt_raw, q, k, v, glu = _split(u, IN_SIZES)
    xbc = jax.nn.silu(causal_depthwise_conv(xbc, ssd_conv_w, ssd_conv_b))
    xs, bm, cm = _split(xbc, (SSD_WIDTH, SSD_GROUPS * SSD_STATE, SSD_GROUPS * SSD_STATE))
    dt = jax.nn.softplus(dt_raw.astype(jnp.float32) + ssd_dt_bias.astype(jnp.float32))
    a = -jnp.exp(ssd_a_log.astype(jnp.float32))
    y_ssd = ssd_chunked(xs.reshape(bsz, seq, SSD_HEADS, SSD_HEAD_DIM), dt, a,
                        bm.reshape(bsz, seq, SSD_GROUPS, SSD_STATE),
                        cm.reshape(bsz, seq, SSD_GROUPS, SSD_STATE), ssd_d)
    y_ssd = gated_rms_norm(y_ssd, z, ssd_norm_w)
    q = rms_norm(q.reshape(bsz, seq, ATTN_Q_HEADS, ATTN_HEAD_DIM), q_norm_w)
    k = rms_norm(k.reshape(bsz, seq, ATTN_KV_HEADS, ATTN_HEAD_DIM), k_norm_w)
    q = apply_rope(q, cos, sin)
    k = apply_rope(k, cos, sin)
    y_attn = sliding_window_gqa(q, k, v.reshape(bsz, seq, ATTN_KV_HEADS, ATTN_HEAD_DIM), attn_sinks)
    y_conv = conformer_conv(glu, cm_dw_w, cm_dw_b, cm_ln_w, cm_ln_b)
    x = x + jnp.concatenate([y_ssd, y_attn, y_conv], axis=-1) @ w_out
    hm = rms_norm(x, norm_mlp_w)
    x = x + jnp.square(jax.nn.relu(hm @ w_mlp_up)) @ w_mlp_down
    return x


def _fwd_setup_inputs(seed: int = 0) -> dict:
    key = jax.random.key(seed)
    ks = jax.random.split(key, 24)
    f32 = jnp.float32
    nrm = lambda k, shp, scale: jax.random.normal(k, shp, f32) * scale
    dt_init = jnp.exp(jax.random.uniform(ks[5], (DEPTH, SSD_HEADS), f32)
                      * (math.log(0.1) - math.log(0.001)) + math.log(0.001))
    return {
        "x": nrm(ks[0], (BATCH, SEQ, D_MODEL), 1.0),
        "norm_mix_w": 1.0 + nrm(ks[1], (DEPTH, D_MODEL), 0.02),
        "w_in": nrm(ks[2], (DEPTH, D_MODEL, N_IN), D_MODEL ** -0.5),
        "ssd_conv_w": nrm(ks[3], (DEPTH, SSD_CONV, SSD_XBC), SSD_CONV ** -0.5),
        "ssd_conv_b": nrm(ks[4], (DEPTH, SSD_XBC), 0.02),
        "ssd_dt_bias": dt_init + jnp.log(-jnp.expm1(-dt_init)),
        "ssd_a_log": jnp.log(jax.random.uniform(ks[6], (DEPTH, SSD_HEADS), f32, 1.0, 16.0)),
        "ssd_d": 1.0 + nrm(ks[7], (DEPTH, SSD_HEADS), 0.02),
        "ssd_norm_w": 1.0 + nrm(ks[8], (DEPTH, SSD_WIDTH), 0.02),
        "q_norm_w": 1.0 + nrm(ks[9], (DEPTH, ATTN_HEAD_DIM), 0.02),
        "k_norm_w": 1.0 + nrm(ks[10], (DEPTH, ATTN_HEAD_DIM), 0.02),
        "attn_sinks": nrm(ks[11], (DEPTH, ATTN_Q_HEADS), 0.5),
        "cm_dw_w": nrm(ks[12], (DEPTH, CM_CONV_WIDTH, CM_CHANNELS), CM_CONV_WIDTH ** -0.5),
        "cm_dw_b": nrm(ks[13], (DEPTH, CM_CHANNELS), 0.02),
        "cm_ln_w": 1.0 + nrm(ks[14], (DEPTH, CM_CHANNELS), 0.02),
        "cm_ln_b": nrm(ks[15], (DEPTH, CM_CHANNELS), 0.02),
        "w_out": nrm(ks[16], (DEPTH, D_MIX, D_MODEL), D_MIX ** -0.5),
        "norm_mlp_w": 1.0 + nrm(ks[17], (DEPTH, D_MODEL), 0.02),
        "w_mlp_up": nrm(ks[18], (DEPTH, D_MODEL, D_FF), D_MODEL ** -0.5),
        "w_mlp_down": nrm(ks[19], (DEPTH, D_FF, D_MODEL), D_FF ** -0.5),
    }


def _fwd_reference(x, norm_mix_w, w_in, ssd_conv_w, ssd_conv_b, ssd_dt_bias, ssd_a_log, ssd_d,
              ssd_norm_w, q_norm_w, k_norm_w, attn_sinks, cm_dw_w, cm_dw_b, cm_ln_w, cm_ln_b,
              w_out, norm_mlp_w, w_mlp_up, w_mlp_down):
    cos, sin = rope_tables(x.shape[1], ATTN_HEAD_DIM)
    for i in range(DEPTH):
        x = hybrid_layer(x, cos, sin, norm_mix_w[i], w_in[i], ssd_conv_w[i], ssd_conv_b[i],
                         ssd_dt_bias[i], ssd_a_log[i], ssd_d[i], ssd_norm_w[i], q_norm_w[i],
                         k_norm_w[i], attn_sinks[i], cm_dw_w[i], cm_dw_b[i], cm_ln_w[i],
                         cm_ln_b[i], w_out[i], norm_mlp_w[i], w_mlp_up[i], w_mlp_down[i])
    return x


import jax as _jax
import jax.numpy as _jnp

TWIN_FORMAT = 'train_step'
FWD_PARAMS = ['x', 'norm_mix_w', 'w_in', 'ssd_conv_w', 'ssd_conv_b', 'ssd_dt_bias', 'ssd_a_log', 'ssd_d', 'ssd_norm_w', 'q_norm_w', 'k_norm_w', 'attn_sinks', 'cm_dw_w', 'cm_dw_b', 'cm_ln_w', 'cm_ln_b', 'w_out', 'norm_mlp_w', 'w_mlp_up', 'w_mlp_down']
TWIN_WEIGHTS = ['norm_mix_w', 'w_in', 'ssd_conv_w', 'ssd_conv_b', 'ssd_dt_bias', 'ssd_a_log', 'ssd_d', 'ssd_norm_w', 'q_norm_w', 'k_norm_w', 'attn_sinks', 'cm_dw_w', 'cm_dw_b', 'cm_ln_w', 'cm_ln_b', 'w_out', 'norm_mlp_w', 'w_mlp_up', 'w_mlp_down']
TWIN_DIFF_INPUT = 'x'
TWIN_INPUTS = ['x', 'norm_mix_w', 'w_in', 'ssd_conv_w', 'ssd_conv_b', 'ssd_dt_bias', 'ssd_a_log', 'ssd_d', 'ssd_norm_w', 'q_norm_w', 'k_norm_w', 'attn_sinks', 'cm_dw_w', 'cm_dw_b', 'cm_ln_w', 'cm_ln_b', 'w_out', 'norm_mlp_w', 'w_mlp_up', 'w_mlp_down', 'loss_target', 'm_norm_mix_w', 'm_w_in', 'm_ssd_conv_w', 'm_ssd_conv_b', 'm_ssd_dt_bias', 'm_ssd_a_log', 'm_ssd_d', 'm_ssd_norm_w', 'm_q_norm_w', 'm_k_norm_w', 'm_attn_sinks', 'm_cm_dw_w', 'm_cm_dw_b', 'm_cm_ln_w', 'm_cm_ln_b', 'm_w_out', 'm_norm_mlp_w', 'm_w_mlp_up', 'm_w_mlp_down', 'v_norm_mix_w', 'v_w_in', 'v_ssd_conv_w', 'v_ssd_conv_b', 'v_ssd_dt_bias', 'v_ssd_a_log', 'v_ssd_d', 'v_ssd_norm_w', 'v_q_norm_w', 'v_k_norm_w', 'v_attn_sinks', 'v_cm_dw_w', 'v_cm_dw_b', 'v_cm_ln_w', 'v_cm_ln_b', 'v_w_out', 'v_norm_mlp_w', 'v_w_mlp_up', 'v_w_mlp_down']
TWIN_OUTPUTS = ['loss', 'grad_x', 'grad_norm_mix_w', 'grad_w_in', 'grad_ssd_conv_w', 'grad_ssd_conv_b', 'grad_ssd_dt_bias', 'grad_ssd_a_log', 'grad_ssd_d', 'grad_ssd_norm_w', 'grad_q_norm_w', 'grad_k_norm_w', 'grad_attn_sinks', 'grad_cm_dw_w', 'grad_cm_dw_b', 'grad_cm_ln_w', 'grad_cm_ln_b', 'grad_w_out', 'grad_norm_mlp_w', 'grad_w_mlp_up', 'grad_w_mlp_down', 'delta_norm_mix_w', 'delta_w_in', 'delta_ssd_conv_w', 'delta_ssd_conv_b', 'delta_ssd_dt_bias', 'delta_ssd_a_log', 'delta_ssd_d', 'delta_ssd_norm_w', 'delta_q_norm_w', 'delta_k_norm_w', 'delta_attn_sinks', 'delta_cm_dw_w', 'delta_cm_dw_b', 'delta_cm_ln_w', 'delta_cm_ln_b', 'delta_w_out', 'delta_norm_mlp_w', 'delta_w_mlp_up', 'delta_w_mlp_down', 'new_m_norm_mix_w', 'new_m_w_in', 'new_m_ssd_conv_w', 'new_m_ssd_conv_b', 'new_m_ssd_dt_bias', 'new_m_ssd_a_log', 'new_m_ssd_d', 'new_m_ssd_norm_w', 'new_m_q_norm_w', 'new_m_k_norm_w', 'new_m_attn_sinks', 'new_m_cm_dw_w', 'new_m_cm_dw_b', 'new_m_cm_ln_w', 'new_m_cm_ln_b', 'new_m_w_out', 'new_m_norm_mlp_w', 'new_m_w_mlp_up', 'new_m_w_mlp_down', 'new_v_norm_mix_w', 'new_v_w_in', 'new_v_ssd_conv_w', 'new_v_ssd_conv_b', 'new_v_ssd_dt_bias', 'new_v_ssd_a_log', 'new_v_ssd_d', 'new_v_ssd_norm_w', 'new_v_q_norm_w', 'new_v_k_norm_w', 'new_v_attn_sinks', 'new_v_cm_dw_w', 'new_v_cm_dw_b', 'new_v_cm_ln_w', 'new_v_cm_ln_b', 'new_v_w_out', 'new_v_norm_mlp_w', 'new_v_w_mlp_up', 'new_v_w_mlp_down']
TWIN_LEAF_KINDS = {'loss': 'loss', 'grad_x': 'grad_x', 'grad_norm_mix_w': 'grad_w', 'grad_w_in': 'grad_w', 'grad_ssd_conv_w': 'grad_w', 'grad_ssd_conv_b': 'grad_w', 'grad_ssd_dt_bias': 'grad_w', 'grad_ssd_a_log': 'grad_w', 'grad_ssd_d': 'grad_w', 'grad_ssd_norm_w': 'grad_w', 'grad_q_norm_w': 'grad_w', 'grad_k_norm_w': 'grad_w', 'grad_attn_sinks': 'grad_w', 'grad_cm_dw_w': 'grad_w', 'grad_cm_dw_b': 'grad_w', 'grad_cm_ln_w': 'grad_w', 'grad_cm_ln_b': 'grad_w', 'grad_w_out': 'grad_w', 'grad_norm_mlp_w': 'grad_w', 'grad_w_mlp_up': 'grad_w', 'grad_w_mlp_down': 'grad_w', 'delta_norm_mix_w': 'delta_w', 'delta_w_in': 'delta_w', 'delta_ssd_conv_w': 'delta_w', 'delta_ssd_conv_b': 'delta_w', 'delta_ssd_dt_bias': 'delta_w', 'delta_ssd_a_log': 'delta_w', 'delta_ssd_d': 'delta_w', 'delta_ssd_norm_w': 'delta_w', 'delta_q_norm_w': 'delta_w', 'delta_k_norm_w': 'delta_w', 'delta_attn_sinks': 'delta_w', 'delta_cm_dw_w': 'delta_w', 'delta_cm_dw_b': 'delta_w', 'delta_cm_ln_w': 'delta_w', 'delta_cm_ln_b': 'delta_w', 'delta_w_out': 'delta_w', 'delta_norm_mlp_w': 'delta_w', 'delta_w_mlp_up': 'delta_w', 'delta_w_mlp_down': 'delta_w', 'new_m_norm_mix_w': 'new_m', 'new_m_w_in': 'new_m', 'new_m_ssd_conv_w': 'new_m', 'new_m_ssd_conv_b': 'new_m', 'new_m_ssd_dt_bias': 'new_m', 'new_m_ssd_a_log': 'new_m', 'new_m_ssd_d': 'new_m', 'new_m_ssd_norm_w': 'new_m', 'new_m_q_norm_w': 'new_m', 'new_m_k_norm_w': 'new_m', 'new_m_attn_sinks': 'new_m', 'new_m_cm_dw_w': 'new_m', 'new_m_cm_dw_b': 'new_m', 'new_m_cm_ln_w': 'new_m', 'new_m_cm_ln_b': 'new_m', 'new_m_w_out': 'new_m', 'new_m_norm_mlp_w': 'new_m', 'new_m_w_mlp_up': 'new_m', 'new_m_w_mlp_down': 'new_m', 'new_v_norm_mix_w': 'new_v', 'new_v_w_in': 'new_v', 'new_v_ssd_conv_w': 'new_v', 'new_v_ssd_conv_b': 'new_v', 'new_v_ssd_dt_bias': 'new_v', 'new_v_ssd_a_log': 'new_v', 'new_v_ssd_d': 'new_v', 'new_v_ssd_norm_w': 'new_v', 'new_v_q_norm_w': 'new_v', 'new_v_k_norm_w': 'new_v', 'new_v_attn_sinks': 'new_v', 'new_v_cm_dw_w': 'new_v', 'new_v_cm_dw_b': 'new_v', 'new_v_cm_ln_w': 'new_v', 'new_v_cm_ln_b': 'new_v', 'new_v_w_out': 'new_v', 'new_v_norm_mlp_w': 'new_v', 'new_v_w_mlp_up': 'new_v', 'new_v_w_mlp_down': 'new_v'}


def _forward(args):
    return _fwd_reference(*[args[k] for k in FWD_PARAMS])


def _output_shape():
    out = _jax.eval_shape(lambda: _forward(_fwd_setup_inputs(0)))
    return out.shape, out.dtype

N_MICROBATCH = 1
ADAM_LR = 0.001
ADAM_B1 = 0.9
ADAM_B2 = 0.999
ADAM_EPS = 1e-08
ADAM_WD = 0.01
ADAM_STEP = 10
PER_EXAMPLE_BATCH_AXIS = {'x': 0, 'loss_target': 0}
SHARED_INPUTS = []
_WEIGHT_DTYPES = {'norm_mix_w': _jnp.float32, 'w_in': _jnp.float32, 'ssd_conv_w': _jnp.float32, 'ssd_conv_b': _jnp.float32, 'ssd_dt_bias': _jnp.float32, 'ssd_a_log': _jnp.float32, 'ssd_d': _jnp.float32, 'ssd_norm_w': _jnp.float32, 'q_norm_w': _jnp.float32, 'k_norm_w': _jnp.float32, 'attn_sinks': _jnp.float32, 'cm_dw_w': _jnp.float32, 'cm_dw_b': _jnp.float32, 'cm_ln_w': _jnp.float32, 'cm_ln_b': _jnp.float32, 'w_out': _jnp.float32, 'norm_mlp_w': _jnp.float32, 'w_mlp_up': _jnp.float32, 'w_mlp_down': _jnp.float32}
MOMENT_SCALE = {'norm_mix_w': 6.016333e+00, 'w_in': 2.749541e+00, 'ssd_conv_w': 3.394669e+00, 'ssd_conv_b': 8.525499e+00, 'ssd_dt_bias': 2.508757e+00, 'ssd_a_log': 1.480448e+01, 'ssd_d': 1.484533e+01, 'ssd_norm_w': 1.663920e+01, 'q_norm_w': 7.201151e-01, 'k_norm_w': 6.988906e-01, 'attn_sinks': 4.827973e-01, 'cm_dw_w': 2.883170e+00, 'cm_dw_b': 1.875441e+01, 'cm_ln_w': 8.783448e+00, 'cm_ln_b': 1.086147e+01, 'w_out': 7.353469e+00, 'norm_mlp_w': 4.952601e+01, 'w_mlp_up': 4.023578e+00, 'w_mlp_down': 1.447804e+01}


def _to_microbatches(a, axis):
    t = _jnp.moveaxis(a, axis, 0)
    t = t.reshape((N_MICROBATCH, t.shape[0] // N_MICROBATCH) + t.shape[1:])
    return _jnp.moveaxis(t, 1, axis + 1)


def setup_inputs(seed: int = 0) -> dict:
    inp = _fwd_setup_inputs(seed)
    key = _jax.random.fold_in(_jax.random.key(seed), 7919)
    shape, _ = _output_shape()
    out = dict(inp)
    out["loss_target"] = _jax.random.normal(_jax.random.fold_in(key, 0), shape, _jnp.float32)
    for i, name in enumerate(TWIN_WEIGHTS):
        w = inp[name].astype(_jnp.float32)
        if MOMENT_SCALE is None:
            s = _jnp.sqrt(_jnp.mean(_jnp.square(w)) + 1e-30)
        else:
            s = MOMENT_SCALE[name]
        km, kv = _jax.random.split(_jax.random.fold_in(key, i + 1))
        out[name] = w
        out["m_" + name] = s * _jax.random.normal(km, w.shape, _jnp.float32)
        out["v_" + name] = (s * s) * _jax.random.uniform(kv, w.shape, _jnp.float32, 0.5, 1.5)
    if N_MICROBATCH > 1:
        for name, axis in PER_EXAMPLE_BATCH_AXIS.items():
            out[name] = _to_microbatches(out[name], axis)
    return {'x': out['x'], 'norm_mix_w': out['norm_mix_w'], 'w_in': out['w_in'], 'ssd_conv_w': out['ssd_conv_w'], 'ssd_conv_b': out['ssd_conv_b'], 'ssd_dt_bias': out['ssd_dt_bias'], 'ssd_a_log': out['ssd_a_log'], 'ssd_d': out['ssd_d'], 'ssd_norm_w': out['ssd_norm_w'], 'q_norm_w': out['q_norm_w'], 'k_norm_w': out['k_norm_w'], 'attn_sinks': out['attn_sinks'], 'cm_dw_w': out['cm_dw_w'], 'cm_dw_b': out['cm_dw_b'], 'cm_ln_w': out['cm_ln_w'], 'cm_ln_b': out['cm_ln_b'], 'w_out': out['w_out'], 'norm_mlp_w': out['norm_mlp_w'], 'w_mlp_up': out['w_mlp_up'], 'w_mlp_down': out['w_mlp_down'], 'loss_target': out['loss_target'], 'm_norm_mix_w': out['m_norm_mix_w'], 'm_w_in': out['m_w_in'], 'm_ssd_conv_w': out['m_ssd_conv_w'], 'm_ssd_conv_b': out['m_ssd_conv_b'], 'm_ssd_dt_bias': out['m_ssd_dt_bias'], 'm_ssd_a_log': out['m_ssd_a_log'], 'm_ssd_d': out['m_ssd_d'], 'm_ssd_norm_w': out['m_ssd_norm_w'], 'm_q_norm_w': out['m_q_norm_w'], 'm_k_norm_w': out['m_k_norm_w'], 'm_attn_sinks': out['m_attn_sinks'], 'm_cm_dw_w': out['m_cm_dw_w'], 'm_cm_dw_b': out['m_cm_dw_b'], 'm_cm_ln_w': out['m_cm_ln_w'], 'm_cm_ln_b': out['m_cm_ln_b'], 'm_w_out': out['m_w_out'], 'm_norm_mlp_w': out['m_norm_mlp_w'], 'm_w_mlp_up': out['m_w_mlp_up'], 'm_w_mlp_down': out['m_w_mlp_down'], 'v_norm_mix_w': out['v_norm_mix_w'], 'v_w_in': out['v_w_in'], 'v_ssd_conv_w': out['v_ssd_conv_w'], 'v_ssd_conv_b': out['v_ssd_conv_b'], 'v_ssd_dt_bias': out['v_ssd_dt_bias'], 'v_ssd_a_log': out['v_ssd_a_log'], 'v_ssd_d': out['v_ssd_d'], 'v_ssd_norm_w': out['v_ssd_norm_w'], 'v_q_norm_w': out['v_q_norm_w'], 'v_k_norm_w': out['v_k_norm_w'], 'v_attn_sinks': out['v_attn_sinks'], 'v_cm_dw_w': out['v_cm_dw_w'], 'v_cm_dw_b': out['v_cm_dw_b'], 'v_cm_ln_w': out['v_cm_ln_w'], 'v_cm_ln_b': out['v_cm_ln_b'], 'v_w_out': out['v_w_out'], 'v_norm_mlp_w': out['v_norm_mlp_w'], 'v_w_mlp_up': out['v_w_mlp_up'], 'v_w_mlp_down': out['v_w_mlp_down']}


def _loss(weights, diff, rest, loss_target):
    with _jax.named_scope("forward"):
        args = {**rest, TWIN_DIFF_INPUT: diff, **{k: w.astype(_WEIGHT_DTYPES[k]) for k, w in weights.items()}}
        y = _forward(args)
    with _jax.named_scope("loss_head"):
        err = _jnp.square(y.astype(_jnp.float32) - loss_target)
        return 0.5 * _jnp.sum(_jnp.mean(err, axis=-1)) if err.ndim else 0.5 * err


def _adamw(w, g, m, v):
    m = ADAM_B1 * m + (1.0 - ADAM_B1) * g
    v = ADAM_B2 * v + (1.0 - ADAM_B2) * _jnp.square(g)
    m_hat = m / (1.0 - ADAM_B1 ** ADAM_STEP)
    v_hat = v / (1.0 - ADAM_B2 ** ADAM_STEP)
    delta = -ADAM_LR * (m_hat / (_jnp.sqrt(v_hat) + ADAM_EPS) + ADAM_WD * w)
    return delta, m, v


def reference(x, norm_mix_w, w_in, ssd_conv_w, ssd_conv_b, ssd_dt_bias, ssd_a_log, ssd_d, ssd_norm_w, q_norm_w, k_norm_w, attn_sinks, cm_dw_w, cm_dw_b, cm_ln_w, cm_ln_b, w_out, norm_mlp_w, w_mlp_up, w_mlp_down, loss_target, m_norm_mix_w, m_w_in, m_ssd_conv_w, m_ssd_conv_b, m_ssd_dt_bias, m_ssd_a_log, m_ssd_d, m_ssd_norm_w, m_q_norm_w, m_k_norm_w, m_attn_sinks, m_cm_dw_w, m_cm_dw_b, m_cm_ln_w, m_cm_ln_b, m_w_out, m_norm_mlp_w, m_w_mlp_up, m_w_mlp_down, v_norm_mix_w, v_w_in, v_ssd_conv_w, v_ssd_conv_b, v_ssd_dt_bias, v_ssd_a_log, v_ssd_d, v_ssd_norm_w, v_q_norm_w, v_k_norm_w, v_attn_sinks, v_cm_dw_w, v_cm_dw_b, v_cm_ln_w, v_cm_ln_b, v_w_out, v_norm_mlp_w, v_w_mlp_up, v_w_mlp_down):
    given = dict(x=x, norm_mix_w=norm_mix_w, w_in=w_in, ssd_conv_w=ssd_conv_w, ssd_conv_b=ssd_conv_b, ssd_dt_bias=ssd_dt_bias, ssd_a_log=ssd_a_log, ssd_d=ssd_d, ssd_norm_w=ssd_norm_w, q_norm_w=q_norm_w, k_norm_w=k_norm_w, attn_sinks=attn_sinks, cm_dw_w=cm_dw_w, cm_dw_b=cm_dw_b, cm_ln_w=cm_ln_w, cm_ln_b=cm_ln_b, w_out=w_out, norm_mlp_w=norm_mlp_w, w_mlp_up=w_mlp_up, w_mlp_down=w_mlp_down, loss_target=loss_target, m_norm_mix_w=m_norm_mix_w, m_w_in=m_w_in, m_ssd_conv_w=m_ssd_conv_w, m_ssd_conv_b=m_ssd_conv_b, m_ssd_dt_bias=m_ssd_dt_bias, m_ssd_a_log=m_ssd_a_log, m_ssd_d=m_ssd_d, m_ssd_norm_w=m_ssd_norm_w, m_q_norm_w=m_q_norm_w, m_k_norm_w=m_k_norm_w, m_attn_sinks=m_attn_sinks, m_cm_dw_w=m_cm_dw_w, m_cm_dw_b=m_cm_dw_b, m_cm_ln_w=m_cm_ln_w, m_cm_ln_b=m_cm_ln_b, m_w_out=m_w_out, m_norm_mlp_w=m_norm_mlp_w, m_w_mlp_up=m_w_mlp_up, m_w_mlp_down=m_w_mlp_down, v_norm_mix_w=v_norm_mix_w, v_w_in=v_w_in, v_ssd_conv_w=v_ssd_conv_w, v_ssd_conv_b=v_ssd_conv_b, v_ssd_dt_bias=v_ssd_dt_bias, v_ssd_a_log=v_ssd_a_log, v_ssd_d=v_ssd_d, v_ssd_norm_w=v_ssd_norm_w, v_q_norm_w=v_q_norm_w, v_k_norm_w=v_k_norm_w, v_attn_sinks=v_attn_sinks, v_cm_dw_w=v_cm_dw_w, v_cm_dw_b=v_cm_dw_b, v_cm_ln_w=v_cm_ln_w, v_cm_ln_b=v_cm_ln_b, v_w_out=v_w_out, v_norm_mlp_w=v_norm_mlp_w, v_w_mlp_up=v_w_mlp_up, v_w_mlp_down=v_w_mlp_down)
    weights = {n: given[n] for n in TWIN_WEIGHTS}
    shared = {n: given[n] for n in SHARED_INPUTS}
    per_example = {n: given[n] for n in ['x']}
    grad_fn = _jax.value_and_grad(_loss, argnums=(0, 1))

    def one_microbatch(ex, loss_target):
        ex = dict(ex)
        diff = ex.pop(TWIN_DIFF_INPUT)
        return grad_fn(weights, diff, {**shared, **ex}, loss_target)

    if N_MICROBATCH == 1:
        loss, (grad_w, grad_x) = one_microbatch(per_example, given["loss_target"])
    else:
        def body(carry, xs):
            loss_sum, grad_sum = carry
            l_k, (gw_k, gx_k) = one_microbatch(xs[0], xs[1])
            with _jax.named_scope("update"):
                return (loss_sum + l_k, _jax.tree.map(_jnp.add, grad_sum, gw_k)), gx_k

        init = (_jnp.zeros((), _jnp.float32), _jax.tree.map(_jnp.zeros_like, weights))
        (loss, grad_w), grad_x = _jax.lax.scan(body, init, (per_example, given["loss_target"]))
    with _jax.named_scope("update"):
        delta_w, new_m, new_v = {}, {}, {}
        for n in TWIN_WEIGHTS:
            delta_w[n], new_m[n], new_v[n] = _adamw(weights[n], grad_w[n], given["m_" + n], given["v_" + n])
    return (loss, grad_x, *[grad_w[n] for n in TWIN_WEIGHTS], *[delta_w[n] for n in TWIN_WEIGHTS],
            *[new_m[n] for n in TWIN_WEIGHTS], *[new_v[n] for n in TWIN_WEIGHTS])
```

```python
import math

import jax
import jax.numpy as jnp
from jax import lax
from jax.experimental import pallas as pl
from jax.experimental.pallas import tpu as pltpu

F32 = jnp.float32
_MM = jnp.bfloat16

D_MODEL = 1024
DEPTH = 4
SSD_WIDTH = 1024
SSD_HEADS = 16
SSD_STATE = 128
SSD_GROUPS = 2
SSD_CONV = 4
SSD_XBC = 1536
Q = 128
ATTN_HEAD_DIM = 64
ATTN_Q_HEADS = 8
CM_CHANNELS = 512
CM_CONV = 31
D_FF = 4096
D_MIX = 2048
N_IN = 4368
RMS_EPS = 1e-6
LN_EPS = 1e-5
ROPE_THETA = 10000.0
ADAM_LR = 0.001
ADAM_B1 = 0.9
ADAM_B2 = 0.999
ADAM_EPS = 1e-08
ADAM_WD = 0.01
ADAM_STEP = 10

N_DEV = 8
LANES = 128
TM = 256
N_IN_P = 4480
U_Z, U_XBC, U_QKV, U_GLU, U_DT = (0, 1024), (1024, 2560), (2560, 3328), (3328, 4352), (4352, 4480)
W_IN_SHARD = N_IN // N_DEV
W_IN_SHARD_P = 640
FF_SHARD = D_FF // N_DEV
OUT_SHARD = D_MIX // N_DEV

_NN = (((1,), (0,)), ((), ()))
_NT = (((1,), (1,)), ((), ()))
_TN = (((0,), (0,)), ((), ()))
_VMEM_LIMIT = 56 * 1024 * 1024


def _mm(a, b, dims=_NN):
    return lax.dot_general(a.astype(_MM), b.astype(_MM), dims, preferred_element_type=F32)


def _mmx(a, b, dims=_NN):
    return lax.dot_general(a, b, dims, precision=lax.Precision.HIGHEST, preferred_element_type=F32)


def _sds(shape, dtype=F32):
    return jax.ShapeDtypeStruct(tuple(shape), dtype)


def _full(shape):
    nd = len(shape)
    return pl.BlockSpec(tuple(shape), lambda *_: (0,) * nd)


def _rows(cols, tm=TM, col=0):
    return pl.BlockSpec((tm, cols), lambda i: (i, col))


def _lp(n, i):
    return pl.BlockSpec((1, 1, n), lambda *_: (i, 0, 0))


def _lw(arr, i):
    return pl.BlockSpec((N_DEV, 1) + arr.shape[2:], lambda *_: (0, i, 0, 0))


def _cparams(sem=None):
    return pltpu.CompilerParams(dimension_semantics=sem, vmem_limit_bytes=_VMEM_LIMIT)


def _sigmoid(x):
    return 1.0 / (1.0 + jnp.exp(-x))


def _silu(x):
    return x * _sigmoid(x)


def _dsilu(x):
    s = _sigmoid(x)
    return s * (1.0 + x * (1.0 - s))


def _rms_bwd(dy, x, w, inv_n):
    r = lax.rsqrt(jnp.sum(x * x, axis=-1, keepdims=True) * inv_n + RMS_EPS)
    xh = x * r
    dxh = dy * w
    dx = r * (dxh - xh * (jnp.sum(dxh * xh, axis=-1, keepdims=True) * inv_n))
    return dx, dy * xh


def _cast_shard(w, cols_p=None):
    _, R, C = w.shape
    cp = C if cols_p is None else cols_p

    def body(w_ref, o_ref):
        v = w_ref[0]
        if cp != C:
            v = jnp.concatenate([v, jnp.zeros((R, cp - C), F32)], axis=1)
        o_ref[0] = v.astype(_MM)

    return pl.pallas_call(
        body, name="cast_shard", grid=(DEPTH,),
        in_specs=[pl.BlockSpec((1, R, C), lambda l: (l, 0, 0))],
        out_specs=pl.BlockSpec((1, R, cp), lambda l: (l, 0, 0)), out_shape=_sds((DEPTH, R, cp), _MM),
        compiler_params=_cparams(("parallel",)),
    )(w)


def _w_in_regroup(w8, i):
    a, b = U_XBC[1], U_XBC[1] + SSD_HEADS

    def body(w_ref, o_ref):
        w = jnp.concatenate([w_ref[j, 0][:, :W_IN_SHARD].astype(F32) for j in range(N_DEV)], axis=1)
        r = jnp.concatenate([w[:, :a], w[:, b:], w[:, a:b], jnp.zeros((TM, N_IN_P - N_IN), F32)], axis=1)
        o_ref[...] = r.astype(_MM)

    return pl.pallas_call(
        body, name="w_in_regroup", grid=(D_MODEL // TM,),
        in_specs=[pl.BlockSpec((N_DEV, 1, TM, W_IN_SHARD_P), lambda r: (0, i, r, 0))],
        out_specs=_rows(N_IN_P), out_shape=_sds((D_MODEL, N_IN_P), _MM),
        compiler_params=_cparams(("parallel",)),
    )(w8)


def _g_in_split(g):
    a = U_XBC[1]

    def body(g_ref, o_ref):
        v = g_ref[...]
        w = jnp.concatenate([v[:, :a], v[:, U_DT[0]:U_DT[0] + SSD_HEADS], v[:, a:U_DT[0]]], axis=1)
        pad = jnp.zeros((TM, W_IN_SHARD_P - W_IN_SHARD), F32)
        for j in range(N_DEV):
            o_ref[j] = jnp.concatenate([w[:, j * W_IN_SHARD:(j + 1) * W_IN_SHARD], pad], axis=1).astype(_MM)

    return pl.pallas_call(
        body, name="g_in_split", grid=(D_MODEL // TM,),
        in_specs=[_rows(N_IN_P)],
        out_specs=pl.BlockSpec((N_DEV, TM, W_IN_SHARD_P), lambda r: (0, r, 0)),
        out_shape=_sds((N_DEV, D_MODEL, W_IN_SHARD_P), _MM),
        compiler_params=_cparams(("parallel",)),
    )(g)


def _in_proj(x, nw, i, w):
    L = x.shape[0]
    splits = (U_Z, U_XBC, U_QKV, U_GLU, U_DT)

    def body(x_ref, nw_ref, w_ref, h_ref, *out_refs):
        xf = x_ref[...]
        r = lax.rsqrt(jnp.mean(xf * xf, axis=-1, keepdims=True) + RMS_EPS)
        h = (xf * r * nw_ref[0]).astype(_MM)
        h_ref[...] = h
        for ref, (a, b) in zip(out_refs, splits):
            ref[...] = lax.dot_general(h, w_ref[:, a:b], _NN, preferred_element_type=F32)

    return pl.pallas_call(
        body, name="in_proj", grid=(L // TM,),
        in_specs=[_rows(D_MODEL), _lp(D_MODEL, i), _full(w.shape)],
        out_specs=[_rows(D_MODEL)] + [_rows(b - a) for a, b in splits],
        out_shape=[_sds((L, D_MODEL), _MM)] + [_sds((L, b - a)) for a, b in splits],
        compiler_params=_cparams(("parallel",)),
    )(x, nw, w)


def _mlp_up(x, nw, i, w8):
    L = x.shape[0]

    def body(x_ref, nw_ref, w_ref, h_ref, up_ref, act_ref):
        xf = x_ref[...]
        r = lax.rsqrt(jnp.mean(xf * xf, axis=-1, keepdims=True) + RMS_EPS)
        h = (xf * r * nw_ref[0]).astype(_MM)
        h_ref[...] = h
        for j in range(N_DEV):
            sl = slice(j * FF_SHARD, (j + 1) * FF_SHARD)
            up = lax.dot_general(h, w_ref[j, 0], _NN, preferred_element_type=F32)
            up_ref[:, sl] = up
            act_ref[:, sl] = jnp.square(jnp.maximum(up, 0.0)).astype(_MM)

    return pl.pallas_call(
        body, name="mlp_up", grid=(L // TM,),
        in_specs=[_rows(D_MODEL), _lp(D_MODEL, i), _lw(w8, i)],
        out_specs=[_rows(D_MODEL), _rows(D_FF), _rows(D_FF)],
        out_shape=[_sds((L, D_MODEL), _MM), _sds((L, D_FF)), _sds((L, D_FF), _MM)],
        compiler_params=_cparams(("parallel",)),
    )(x, nw, w8)


def _mm_res(a, w8, i, res, name):
    L, K = a.shape
    N = w8.shape[3]

    def body(a_ref, w_ref, res_ref, o_ref):
        w = w_ref[:, 0].reshape(K, N)
        o_ref[...] = res_ref[...] + lax.dot_general(a_ref[...], w, _NN, preferred_element_type=F32)

    return pl.pallas_call(
        body, name=name, grid=(L // TM,),
        in_specs=[_rows(K), _lw(w8, i), _rows(N)],
        out_specs=_rows(N), out_shape=_sds((L, N)),
        compiler_params=_cparams(("parallel",)),
    )(a, w8, res)


def _out_proj_bwd(a, w8, i):
    L = a.shape[0]

    def body(a_ref, w_ref, o_ref):
        w = w_ref[:, 0].reshape(D_MIX, D_MODEL)
        o_ref[...] = lax.dot_general(a_ref[...].astype(_MM), w, _NT, preferred_element_type=F32)

    return pl.pallas_call(
        body, name="out_proj_bwd", grid=(L // TM,),
        in_specs=[_rows(D_MODEL), _lw(w8, i)],
        out_specs=_rows(D_MIX), out_shape=_sds((L, D_MIX)),
        compiler_params=_cparams(("parallel",)),
    )(a, w8)


def _mlp_down_bwd(dy, w8, i, up):
    L = dy.shape[0]

    def body(dy_ref, w_ref, up_ref, o_ref):
        d = dy_ref[...].astype(_MM)
        for j in range(N_DEV):
            sl = slice(j * FF_SHARD, (j + 1) * FF_SHARD)
            da = lax.dot_general(d, w_ref[j, 0], _NT, preferred_element_type=F32)
            o_ref[:, sl] = (da * (2.0 * jnp.maximum(up_ref[:, sl], 0.0))).astype(_MM)

    return pl.pallas_call(
        body, name="mlp_down_bwd", grid=(L // TM,),
        in_specs=[_rows(D_MODEL), _lw(w8, i), _rows(D_FF)],
        out_specs=_rows(D_FF), out_shape=_sds((L, D_FF), _MM),
        compiler_params=_cparams(("parallel",)),
    )(dy, w8, up)


def _rms_bwd_epilogue(dh, res_ref, x_ref, nw_ref, dx_ref, dnw_ref):
    dx, dwx = _rms_bwd(dh, x_ref[...], nw_ref[0], 1.0 / D_MODEL)
    dx_ref[...] = res_ref[...] + dx

    @pl.when(pl.program_id(0) == 0)
    def _():
        dnw_ref[...] = jnp.zeros_like(dnw_ref)

    dnw_ref[...] += jnp.sum(dwx, axis=0, keepdims=True)


def _mlp_up_bwd(d_up, w8, i, res, x, nw):
    L = d_up.shape[0]

    def body(a_ref, w_ref, res_ref, x_ref, nw_ref, dx_ref, dnw_ref):
        dh = jnp.zeros((TM, D_MODEL), F32)
        for j in range(N_DEV):
            dh = dh + lax.dot_general(a_ref[:, j * FF_SHARD:(j + 1) * FF_SHARD], w_ref[j, 0], _NT, preferred_element_type=F32)
        _rms_bwd_epilogue(dh, res_ref, x_ref, nw_ref, dx_ref, dnw_ref)

    return pl.pallas_call(
        body, name="mlp_up_bwd", grid=(L // TM,),
        in_specs=[_rows(D_FF), _lw(w8, i), _rows(D_MODEL), _rows(D_MODEL), _lp(D_MODEL, i)],
        out_specs=[_rows(D_MODEL), _full((1, D_MODEL))],
        out_shape=[_sds((L, D_MODEL)), _sds((1, D_MODEL))],
        compiler_params=_cparams(("arbitrary",)),
    )(d_up, w8, res, x, nw)


def _in_proj_bwd(du, w, res, x, nw, i):
    L = du.shape[0]

    def body(a_ref, w_ref, res_ref, x_ref, nw_ref, dx_ref, dnw_ref):
        dh = lax.dot_general(a_ref[...], w_ref[...], _NT, preferred_element_type=F32)
        _rms_bwd_epilogue(dh, res_ref, x_ref, nw_ref, dx_ref, dnw_ref)

    return pl.pallas_call(
        body, name="in_proj_bwd", grid=(L // TM,),
        in_specs=[_rows(N_IN_P), _full(w.shape), _rows(D_MODEL), _rows(D_MODEL), _lp(D_MODEL, i)],
        out_specs=[_rows(D_MODEL), _full((1, D_MODEL))],
        out_shape=[_sds((L, D_MODEL)), _sds((1, D_MODEL))],
        compiler_params=_cparams(("arbitrary",)),
    )(du, w, res, x, nw)


def _mm_tn(a, g, name, tk, tn, out):
    L, K = a.shape
    N = g.shape[1]

    def body(a_ref, g_ref, o_ref):
        r = lax.dot_general(a_ref[...].astype(_MM), g_ref[...].astype(_MM), _TN, preferred_element_type=F32)
        o_ref[...] = r.astype(o_ref.dtype).reshape(o_ref.shape)

    if out == "f32":
        out_spec, out_shape = pl.BlockSpec((tk, tn), lambda i, j: (i, j)), _sds((K, N))
    elif out == "rows":
        assert tn == N and tk % (K // N_DEV) == 0
        nblk = tk // (K // N_DEV)
        out_spec, out_shape = pl.BlockSpec((nblk, K // N_DEV, N), lambda i, j: (i, 0, 0)), _sds((N_DEV, K // N_DEV, N), _MM)
    else:
        assert tk == K and tn == N // N_DEV
        out_spec, out_shape = pl.BlockSpec((1, K, tn), lambda i, j: (j, 0, 0)), _sds((N_DEV, K, tn), _MM)
    return pl.pallas_call(
        body, name=name, grid=(K // tk, N // tn),
        in_specs=[pl.BlockSpec((L, tk), lambda i, j: (0, i)), pl.BlockSpec((L, tn), lambda i, j: (0, j))],
        out_specs=out_spec, out_shape=out_shape,
        compiler_params=_cparams(("parallel", "parallel")),
    )(a, g)


def _loss_head(y, t):
    L = y.shape[0]

    def body(y_ref, t_ref, dy_ref, l_ref):
        e = y_ref[...] - t_ref[...]
        dy_ref[...] = e * (1.0 / D_MODEL)

        @pl.when(pl.program_id(0) == 0)
        def _():
            l_ref[...] = jnp.zeros_like(l_ref)

        l_ref[...] += jnp.sum(jnp.sum(e * e, axis=1, keepdims=True), axis=0, keepdims=True) * (0.5 / D_MODEL)

    return pl.pallas_call(
        body, name="loss_head", grid=(L // TM,),
        in_specs=[_rows(D_MODEL), _rows(D_MODEL)],
        out_specs=[_rows(D_MODEL), _full((8, LANES))],
        out_shape=[_sds((L, D_MODEL)), _sds((8, LANES))],
        compiler_params=_cparams(("arbitrary",)),
    )(y, t)


def _du_pack(pieces):
    L = pieces[0].shape[0]

    def body(*refs):
        o_ref = refs[-1]
        off = 0
        for r in refs[:-1]:
            n = r.shape[1]
            o_ref[:, off:off + n] = r[...].astype(_MM)
            off += n

    return pl.pallas_call(
        body, name="du_pack", grid=(L // TM,),
        in_specs=[_rows(p.shape[1]) for p in pieces],
        out_specs=_rows(N_IN_P), out_shape=_sds((L, N_IN_P), _MM),
        compiler_params=_cparams(("parallel",)),
    )(*pieces)


def _shift_down(x, s):
    if s == 0:
        return x
    r = lax.broadcasted_iota(jnp.int32, x.shape, 0)
    return jnp.where(r >= s, pltpu.roll(x, s, axis=0), 0.0)


def _shift_up(x, s):
    if s == 0:
        return x
    n = x.shape[0]
    r = lax.broadcasted_iota(jnp.int32, x.shape, 0)
    return jnp.where(r < n - s, pltpu.roll(x, n - s, axis=0), 0.0)


def _conv_taps(x, w_ref, b, k_w):
    acc = jnp.broadcast_to(b, x.shape)
    for k in range(k_w):
        acc = acc + w_ref[0, k:k + 1, :] * _shift_down(x, k_w - 1 - k)
    return acc


def _conv_bwd_taps(x, dc, w_ref, dw_ref, db_ref, k_w):
    dx = jnp.zeros_like(x)
    for k in range(k_w):
        s = k_w - 1 - k
        dx = dx + w_ref[0, k:k + 1, :] * _shift_up(dc, s)
        dw_ref[k:k + 1, :] = jnp.sum(dc * _shift_down(x, s), axis=0, keepdims=True)
    db_ref[...] = jnp.sum(dc, axis=0, keepdims=True)
    return dx


def _cols(L, cb, off=0):
    return pl.BlockSpec((L, cb), lambda j: (0, j + off))


def _lcols(k, cb, i):
    return pl.BlockSpec((1, k, cb), lambda j: (i, 0, j))


SSD_CB = 256


def _ssd_conv_fwd(x, w, b, i):
    L, C = x.shape
    cb = SSD_CB

    def body(x_ref, w_ref, b_ref, o_ref):
        o_ref[...] = _silu(_conv_taps(x_ref[...], w_ref, b_ref[0], SSD_CONV))

    return pl.pallas_call(
        body, name="ssd_conv_fwd", grid=(C // cb,),
        in_specs=[_cols(L, cb), _lcols(SSD_CONV, cb, i), _lcols(1, cb, i)],
        out_specs=_cols(L, cb), out_shape=_sds((L, C)),
        compiler_params=_cparams(("parallel",)),
    )(x, w, b)


def _ssd_conv_bwd(x, w, b, i, dy):
    L, C = x.shape
    cb = SSD_CB

    def body(x_ref, w_ref, b_ref, dy_ref, dx_ref, dw_ref, db_ref):
        x_ = x_ref[...]
        c = _conv_taps(x_, w_ref, b_ref[0], SSD_CONV)
        dc = dy_ref[...] * _dsilu(c)
        dx_ref[...] = _conv_bwd_taps(x_, dc, w_ref, dw_ref, db_ref, SSD_CONV)

    return pl.pallas_call(
        body, name="ssd_conv_bwd", grid=(C // cb,),
        in_specs=[_cols(L, cb), _lcols(SSD_CONV, cb, i), _lcols(1, cb, i), _cols(L, cb)],
        out_specs=[_cols(L, cb), _cols(SSD_CONV, cb), _cols(1, cb)],
        out_shape=[_sds((L, C)), _sds((SSD_CONV, C)), _sds((1, C))],
        compiler_params=_cparams(("parallel",)),
    )(x, w, b, dy)


def _cm_conv_fwd(glu, w, b, i):
    L = glu.shape[0]
    cb = LANES
    nb = CM_CHANNELS // cb

    def body(a_ref, g_ref, w_ref, b_ref, o_ref):
        h = a_ref[...] * _sigmoid(g_ref[...])
        o_ref[...] = _conv_taps(h, w_ref, b_ref[0], CM_CONV)

    return pl.pallas_call(
        body, name="cm_conv_fwd", grid=(nb,),
        in_specs=[_cols(L, cb), _cols(L, cb, nb), _lcols(CM_CONV, cb, i), _lcols(1, cb, i)],
        out_specs=_cols(L, cb), out_shape=_sds((L, CM_CHANNELS)),
        compiler_params=_cparams(("parallel",)),
    )(glu, glu, w, b)


def _cm_conv_bwd(glu, w, i, dc):
    L = glu.shape[0]
    cb = LANES
    nb = CM_CHANNELS // cb

    def body(a_ref, g_ref, w_ref, dc_ref, da_ref, dg_ref, dw_ref, db_ref):
        a = a_ref[...]
        sg = _sigmoid(g_ref[...])
        dh = _conv_bwd_taps(a * sg, dc_ref[...], w_ref, dw_ref, db_ref, CM_CONV)
        da_ref[...] = dh * sg
        dg_ref[...] = dh * a * sg * (1.0 - sg)

    return pl.pallas_call(
        body, name="cm_conv_bwd", grid=(nb,),
        in_specs=[_cols(L, cb), _cols(L, cb, nb), _lcols(CM_CONV, cb, i), _cols(L, cb)],
        out_specs=[_cols(L, cb), _cols(L, cb), _cols(CM_CONV, cb), _cols(1, cb)],
        out_shape=[_sds((L, CM_CHANNELS)), _sds((L, CM_CHANNELS)), _sds((CM_CONV, CM_CHANNELS)), _sds((1, CM_CHANNELS))],
        compiler_params=_cparams(("parallel",)),
    )(glu, glu, w, dc)


GRP = SSD_WIDTH // SSD_GROUPS


def _mix_post(y, z, attn, c, snw, lw, lb, i):
    L = y.shape[0]

    def body(y_ref, z_ref, a_ref, c_ref, snw_ref, lw_ref, lb_ref, o_ref):
        g = y_ref[...] * _silu(z_ref[...])
        for k in range(SSD_GROUPS):
            sl = slice(k * GRP, (k + 1) * GRP)
            gg = g[:, sl]
            r = lax.rsqrt(jnp.mean(gg * gg, axis=-1, keepdims=True) + RMS_EPS)
            o_ref[:, sl] = (gg * r * snw_ref[0, :, sl]).astype(_MM)
        o_ref[:, SSD_WIDTH:SSD_WIDTH + 512] = a_ref[...].astype(_MM)
        cv = c_ref[...]
        mu = jnp.mean(cv, axis=-1, keepdims=True)
        xc = cv - mu
        rs = lax.rsqrt(jnp.mean(xc * xc, axis=-1, keepdims=True) + LN_EPS)
        o_ref[:, SSD_WIDTH + 512:] = _silu(xc * rs * lw_ref[0] + lb_ref[0]).astype(_MM)

    return pl.pallas_call(
        body, name="mix_post", grid=(L // TM,),
        in_specs=[_rows(SSD_WIDTH), _rows(SSD_WIDTH), _rows(512), _rows(512),
                  _lp(SSD_WIDTH, i), _lp(512, i), _lp(512, i)],
        out_specs=_rows(D_MIX), out_shape=_sds((L, D_MIX), _MM),
        compiler_params=_cparams(("parallel",)),
    )(y, z, attn, c, snw, lw, lb)


def _mix_post_bwd(dcat, y, z, c, snw, lw, lb, i):
    L = y.shape[0]

    def body(d_ref, y_ref, z_ref, c_ref, snw_ref, lw_ref, lb_ref, dy_ref, dz_ref, dc_ref, dsnw_ref, dlw_ref, dlb_ref):
        @pl.when(pl.program_id(0) == 0)
        def _():
            dsnw_ref[...] = jnp.zeros_like(dsnw_ref)
            dlw_ref[...] = jnp.zeros_like(dlw_ref)
            dlb_ref[...] = jnp.zeros_like(dlb_ref)

        yv = y_ref[...]
        zv = z_ref[...]
        sz = _silu(zv)
        g = yv * sz
        for k in range(SSD_GROUPS):
            sl = slice(k * GRP, (k + 1) * GRP)
            dgg, dwx = _rms_bwd(d_ref[:, sl], g[:, sl], snw_ref[0, :, sl], 1.0 / GRP)
            dsnw_ref[:, sl] += jnp.sum(dwx, axis=0, keepdims=True)
            dy_ref[:, sl] = dgg * sz[:, sl]
            dz_ref[:, sl] = dgg * yv[:, sl] * _dsilu(zv[:, sl])
        cv = c_ref[...]
        mu = jnp.mean(cv, axis=-1, keepdims=True)
        xc = cv - mu
        rs = lax.rsqrt(jnp.mean(xc * xc, axis=-1, keepdims=True) + LN_EPS)
        xh = xc * rs
        ln = xh * lw_ref[0] + lb_ref[0]
        dln = d_ref[:, SSD_WIDTH + 512:] * _dsilu(ln)
        dlb_ref[...] += jnp.sum(dln, axis=0, keepdims=True)
        dlw_ref[...] += jnp.sum(dln * xh, axis=0, keepdims=True)
        dxh = dln * lw_ref[0]
        dc_ref[...] = rs * (dxh - jnp.mean(dxh, axis=-1, keepdims=True)
                            - xh * jnp.mean(dxh * xh, axis=-1, keepdims=True))

    return pl.pallas_call(
        body, name="mix_post_bwd", grid=(L // TM,),
        in_specs=[_rows(D_MIX), _rows(SSD_WIDTH), _rows(SSD_WIDTH), _rows(512),
                  _lp(SSD_WIDTH, i), _lp(512, i), _lp(512, i)],
        out_specs=[_rows(SSD_WIDTH), _rows(SSD_WIDTH), _rows(512), _full((1, SSD_WIDTH)), _full((1, 512)), _full((1, 512))],
        out_shape=[_sds((L, SSD_WIDTH)), _sds((L, SSD_WIDTH)), _sds((L, 512)), _sds((1, SSD_WIDTH)), _sds((1, 512)), _sds((1, 512))],
        compiler_params=_cparams(("arbitrary",)),
    )(dcat, y, z, c, snw, lw, lb)


def _seg_mean_matrix():
    i = lax.broadcasted_iota(jnp.int32, (LANES, LANES), 0)
    j = lax.broadcasted_iota(jnp.int32, (LANES, LANES), 1)
    return jnp.where(i // ATTN_HEAD_DIM == j // ATTN_HEAD_DIM, 1.0 / ATTN_HEAD_DIM, 0.0).astype(F32)


def _rot_matrix():
    i = lax.broadcasted_iota(jnp.int32, (LANES, LANES), 0)
    j = lax.broadcasted_iota(jnp.int32, (LANES, LANES), 1)
    half = ATTN_HEAD_DIM // 2
    lo = (j % ATTN_HEAD_DIM) < half
    return jnp.where(lo & (i == j + half), -1.0, jnp.where((~lo) & (i == j - half), 1.0, 0.0)).astype(F32)


N_QK_TILES = 5
QK_W = N_QK_TILES * LANES


def _qk_prep(qkv, qw, kw, i, cos, sin):
    L = qkv.shape[0]

    def body(x_ref, qw_ref, kw_ref, c_ref, s_ref, o_ref):
        m64 = _seg_mean_matrix()
        rot = _rot_matrix()
        cs, sn = c_ref[...], s_ref[...]
        for t in range(N_QK_TILES):
            sl = slice(t * LANES, (t + 1) * LANES)
            x = x_ref[:, sl]
            w = qw_ref[0] if t < 4 else kw_ref[0]
            xn = x * lax.rsqrt(_mmx(x * x, m64) + RMS_EPS) * w
            o_ref[:, sl] = xn * cs + _mmx(xn, rot) * sn

    return pl.pallas_call(
        body, name="qk_prep", grid=(L // TM,),
        in_specs=[_rows(QK_W), _lp(LANES, i), _lp(LANES, i), _rows(LANES), _rows(LANES)],
        out_specs=_rows(QK_W), out_shape=_sds((L, QK_W)),
        compiler_params=_cparams(("parallel",)),
    )(qkv, qw, kw, cos, sin)


def _qk_prep_bwd(dq, dk, qkv, qw, kw, i, cos, sin):
    L = qkv.shape[0]

    def body(dq_ref, dk_ref, x_ref, qw_ref, kw_ref, c_ref, s_ref, dx_ref, dqw_ref, dkw_ref):
        @pl.when(pl.program_id(0) == 0)
        def _():
            dqw_ref[...] = jnp.zeros_like(dqw_ref)
            dkw_ref[...] = jnp.zeros_like(dkw_ref)

        m64 = _seg_mean_matrix()
        rot = _rot_matrix()
        cs, sn = c_ref[...], s_ref[...]
        for t in range(N_QK_TILES):
            sl = slice(t * LANES, (t + 1) * LANES)
            x = x_ref[:, sl]
            dy = dq_ref[:, sl] if t < 4 else dk_ref[...]
            w = qw_ref[0] if t < 4 else kw_ref[0]
            dxn = dy * cs - _mmx(dy * sn, rot)
            r = lax.rsqrt(_mmx(x * x, m64) + RMS_EPS)
            xh = x * r
            dxh = dxn * w
            dx_ref[:, sl] = r * (dxh - xh * _mmx(dxh * xh, m64))
            dw = jnp.sum(dxn * xh, axis=0, keepdims=True)
            if t < 4:
                dqw_ref[...] += dw
            else:
                dkw_ref[...] += dw

    return pl.pallas_call(
        body, name="qk_prep_bwd", grid=(L // TM,),
        in_specs=[_rows(512), _rows(LANES), _rows(QK_W), _lp(LANES, i), _lp(LANES, i), _rows(LANES), _rows(LANES)],
        out_specs=[_rows(QK_W), _full((1, LANES)), _full((1, LANES))],
        out_shape=[_sds((L, QK_W)), _sds((1, LANES)), _sds((1, LANES))],
        compiler_params=_cparams(("arbitrary",)),
    )(dq, dk, qkv, qw, kw, cos, sin)


HPG = 4
SCALE = 1.0 / math.sqrt(ATTN_HEAD_DIM)


def _heads_to_rows(q, g):
    return jnp.concatenate([q[:, (HPG * g + r) * ATTN_HEAD_DIM:(HPG * g + r + 1) * ATTN_HEAD_DIM] for r in range(HPG)], axis=0)


def _rows_to_heads(parts):
    return jnp.concatenate([p[r * Q:(r + 1) * Q] for p in parts for r in range(HPG)], axis=1)


def _attn_probs(q, kc, n, s0, sink_ref, base):
    s = _mm(q, kc, _NT) * SCALE
    rows = lax.broadcasted_iota(jnp.int32, s.shape, 0)
    ki = lax.broadcasted_iota(jnp.int32, s.shape, 1)
    diff = (n * Q + rows % Q) - (s0 + ki)
    s = jnp.where((diff >= 0) & (diff < Q), s, -jnp.inf)
    hrow = lax.broadcasted_iota(jnp.int32, (HPG * Q, 1), 0) // Q
    sink = jnp.zeros((HPG * Q, 1), F32)
    for r in range(HPG):
        sink = jnp.where(hrow == r, sink_ref[base + r], sink)
    m = jnp.maximum(jnp.max(s, axis=1, keepdims=True), sink)
    p = jnp.exp(s - m)
    es = jnp.exp(sink - m)
    inv = 1.0 / (jnp.sum(p, axis=1, keepdims=True) + es)
    return p * inv, es * inv


def _attn_fwd(qk, qkv, sinks, i):
    L = qk.shape[0]

    def body(sink_ref, q_ref, k_ref, v_ref, o_ref):
        n = pl.program_id(0)
        s0 = pl.multiple_of(jnp.maximum(n - 1, 0) * Q, Q)
        q = q_ref[...]
        kc = k_ref[pl.ds(s0, 2 * Q), :]
        vc = v_ref[pl.ds(s0, 2 * Q), :]
        outs = []
        for g in range(2):
            sl = slice(g * ATTN_HEAD_DIM, (g + 1) * ATTN_HEAD_DIM)
            p, _ = _attn_probs(_heads_to_rows(q, g), kc[:, sl], n, s0, sink_ref, i * ATTN_Q_HEADS + g * HPG)
            outs.append(_mm(p, vc[:, sl]))
        o_ref[...] = _rows_to_heads(outs)

    return pl.pallas_call(
        body, name="attn_fwd", grid=(L // Q,),
        in_specs=[pl.BlockSpec(memory_space=pltpu.SMEM), _rows(512, Q),
                  pl.BlockSpec((L, LANES), lambda n: (0, 4)), pl.BlockSpec((L, LANES), lambda n: (0, 5))],
        out_specs=_rows(512, Q), out_shape=_sds((L, 512)),
        compiler_params=_cparams(("parallel",)),
    )(sinks, qk, qk, qkv)


def _attn_bwd(qk, qkv, sinks, i, dcat):
    L = qk.shape[0]

    def body(sink_ref, q_ref, k_ref, v_ref, do_ref, dq_ref, dk_ref, dv_ref, ds_ref):
        n = pl.program_id(0)

        @pl.when(n == 0)
        def _():
            dk_ref[...] = jnp.zeros_like(dk_ref)
            dv_ref[...] = jnp.zeros_like(dv_ref)
            ds_ref[...] = jnp.zeros_like(ds_ref)

        s0 = pl.multiple_of(jnp.maximum(n - 1, 0) * Q, Q)
        q = q_ref[...]
        do_all = do_ref[...]
        kc = k_ref[pl.ds(s0, 2 * Q), :]
        vc = v_ref[pl.ds(s0, 2 * Q), :]
        hrow = lax.broadcasted_iota(jnp.int32, (HPG * Q, 1), 0) // Q
        orow = lax.broadcasted_iota(jnp.int32, (8, LANES), 0)
        dqs, dks, dvs = [], [], []
        acc = jnp.zeros((8, LANES), F32)
        for g in range(2):
            sl = slice(g * ATTN_HEAD_DIM, (g + 1) * ATTN_HEAD_DIM)
            qg = _heads_to_rows(q, g)
            do = _heads_to_rows(do_all, g)
            p, ps = _attn_probs(qg, kc[:, sl], n, s0, sink_ref, i * ATTN_Q_HEADS + g * HPG)
            dp = _mm(do, vc[:, sl], _NT)
            delta = jnp.sum(p * dp, axis=1, keepdims=True)
            ds = p * (dp - delta)
            dqs.append(_mm(ds, kc[:, sl]) * SCALE)
            dks.append(_mm(ds, qg, _TN) * SCALE)
            dvs.append(_mm(p, do, _TN))
            dsink = -(ps * delta)
            for r in range(HPG):
                tot = jnp.sum(jnp.where(hrow == r, dsink, 0.0), axis=0, keepdims=True)
                acc = acc + jnp.where(orow == g * HPG + r, tot, 0.0)
        dq_ref[...] = _rows_to_heads(dqs)
        dk_ref[pl.ds(s0, 2 * Q), :] += jnp.concatenate(dks, axis=1)
        dv_ref[pl.ds(s0, 2 * Q), :] += jnp.concatenate(dvs, axis=1)
        ds_ref[...] += acc

    return pl.pallas_call(
        body, name="attn_bwd", grid=(L // Q,),
        in_specs=[pl.BlockSpec(memory_space=pltpu.SMEM), _rows(512, Q),
                  pl.BlockSpec((L, LANES), lambda n: (0, 4)), pl.BlockSpec((L, LANES), lambda n: (0, 5)),
                  _rows(512, Q, 2)],
        out_specs=[_rows(512, Q), _full((L, LANES)), _full((L, LANES)), _full((8, LANES))],
        out_shape=[_sds((L, 512)), _sds((L, LANES)), _sds((L, LANES)), _sds((8, LANES))],
        compiler_params=_cparams(("arbitrary",)),
    )(sinks, qk, qk, qkv, dcat)


N_PAIR = SSD_HEADS // 2
P = 64
OFF_B = SSD_WIDTH
OFF_C = SSD_WIDTH + SSD_GROUPS * SSD_STATE


def _expand_matrix():
    i = lax.broadcasted_iota(jnp.int32, (LANES, SSD_WIDTH), 0)
    j = lax.broadcasted_iota(jnp.int32, (LANES, SSD_WIDTH), 1)
    return jnp.where(j // P == i, 1.0, 0.0).astype(F32)


def _ssd_chunk_common(dtr_ref, bias_ref, alog_ref):
    dt = jax.nn.softplus(dtr_ref[...] + bias_ref[0])
    a = -jnp.exp(alog_ref[0])
    adt = dt * a
    ri = lax.broadcasted_iota(jnp.int32, (Q, Q), 0)
    ci = lax.broadcasted_iota(jnp.int32, (Q, Q), 1)
    causal = ri >= ci
    tri = jnp.where(causal, 1.0, 0.0).astype(F32)
    acs = _mmx(tri, adt)
    em = _expand_matrix()
    acs_e = _mmx(acs, em)
    dt_e = _mmx(dt, em)
    alast_e = acs_e[Q - 1:Q, :]
    return dt, a, acs, causal, tri, em, acs_e, dt_e, alast_e


def _ssd_fwd(xbc, dtr, bias, alog, d_e, i):
    L = xbc.shape[0]
    nc = L // Q

    def body(xbc_ref, dtr_ref, bias_ref, alog_ref, de_ref, y_ref, hp_ref, st_ref):
        @pl.when(pl.program_id(0) == 0)
        def _():
            st_ref[...] = jnp.zeros_like(st_ref)

        dt, a, acs, causal, tri, em, acs_e, dt_e, alast_e = _ssd_chunk_common(dtr_ref, bias_ref, alog_ref)
        acs_t = acs.T
        x = xbc_ref[:, :SSD_WIDTH]
        xdt = x * dt_e
        ea_e = jnp.exp(acs_e)
        xds = xdt * jnp.exp(alast_e - acs_e)
        cd_e = jnp.exp(alast_e)
        lane = lax.broadcasted_iota(jnp.int32, (Q, LANES), 1)
        lo = lane < P
        for g in range(SSD_GROUPS):
            bg = xbc_ref[:, OFF_B + g * SSD_STATE:OFF_B + (g + 1) * SSD_STATE]
            cg = xbc_ref[:, OFF_C + g * SSD_STATE:OFF_C + (g + 1) * SSD_STATE]
            cb = _mm(cg, bg, _NT)
            for pp in range(N_PAIR // SSD_GROUPS):
                pr = g * (N_PAIR // SSD_GROUPS) + pp
                sl = slice(pr * LANES, (pr + 1) * LANES)
                xdt_p = xdt[:, sl]
                yd = jnp.zeros((Q, LANES), F32)
                for half in range(2):
                    h = 2 * pr + half
                    rowb = jnp.broadcast_to(acs_t[h:h + 1, :], (Q, Q))
                    lm = jnp.exp(jnp.where(causal, rowb.T - rowb, -jnp.inf))
                    xh = jnp.where(lo if half == 0 else ~lo, xdt_p, 0.0)
                    yd = yd + _mm(cb * lm, xh)
                hp = st_ref[pr]
                hp_ref[0, pr] = hp
                yoff = _mm(cg, hp) * ea_e[:, sl]
                y_ref[:, sl] = yd + yoff + x[:, sl] * de_ref[0, :, sl]
                st_ref[pr] = hp * cd_e[:, sl] + _mm(bg, xds[:, sl], _TN)

    return pl.pallas_call(
        body, name="ssd_fwd", grid=(nc,),
        in_specs=[_rows(SSD_XBC, Q), _rows(LANES, Q), _lp(LANES, i), _lp(LANES, i), _lp(SSD_WIDTH, i)],
        out_specs=[_rows(SSD_WIDTH, Q), pl.BlockSpec((1, N_PAIR, SSD_STATE, LANES), lambda c: (c, 0, 0, 0))],
        out_shape=[_sds((L, SSD_WIDTH)), _sds((nc, N_PAIR, SSD_STATE, LANES))],
        scratch_shapes=[pltpu.VMEM((N_PAIR, SSD_STATE, LANES), F32)],
        compiler_params=_cparams(("arbitrary",)),
    )(xbc, dtr, bias, alog, d_e)


def _ssd_bwd(xbc, dtr, bias, alog, d_e, i, hprev, dy):
    L = xbc.shape[0]
    nc = L // Q
    rev = lambda c: (nc - 1 - c, 0)

    def body(xbc_ref, dtr_ref, bias_ref, alog_ref, de_ref, hp_ref, dy_ref,
             dxbc_ref, ddtr_ref, dbias_ref, dalog_ref, dd_ref, dst_ref):
        @pl.when(pl.program_id(0) == 0)
        def _():
            dst_ref[...] = jnp.zeros_like(dst_ref)
            dbias_ref[...] = jnp.zeros_like(dbias_ref)
            dalog_ref[...] = jnp.zeros_like(dalog_ref)
            dd_ref[...] = jnp.zeros_like(dd_ref)

        dt, a, acs, causal, tri, em, acs_e, dt_e, alast_e = _ssd_chunk_common(dtr_ref, bias_ref, alog_ref)
        acs_t = acs.T
        x = xbc_ref[:, :SSD_WIDTH]
        dy = dy_ref[...]
        xdt = x * dt_e
        ea_e = jnp.exp(acs_e)
        dse = jnp.exp(alast_e - acs_e)
        xds = xdt * dse
        cd_e = jnp.exp(alast_e)
        lane = lax.broadcasted_iota(jnp.int32, (Q, LANES), 1)
        lo = lane < P
        sub = lax.broadcasted_iota(jnp.int32, (Q, Q), 0)
        lan = lax.broadcasted_iota(jnp.int32, (Q, Q), 1)

        da_rows = jnp.zeros((Q, Q), F32)
        da_cols_t = jnp.zeros((Q, Q), F32)
        dxdt_parts = []
        wyoff_parts = []
        dcd_parts = []
        dxds_parts = []
        for g in range(SSD_GROUPS):
            bg = xbc_ref[:, OFF_B + g * SSD_STATE:OFF_B + (g + 1) * SSD_STATE]
            cg = xbc_ref[:, OFF_C + g * SSD_STATE:OFF_C + (g + 1) * SSD_STATE]
            cb = _mm(cg, bg, _NT)
            dcb = jnp.zeros((Q, Q), F32)
            dcg = jnp.zeros((Q, SSD_STATE), F32)
            dbg = jnp.zeros((Q, SSD_STATE), F32)
            for pp in range(N_PAIR // SSD_GROUPS):
                pr = g * (N_PAIR // SSD_GROUPS) + pp
                sl = slice(pr * LANES, (pr + 1) * LANES)
                xdt_p = xdt[:, sl]
                dy_p = dy[:, sl]
                dxdt_p = jnp.zeros((Q, LANES), F32)
                for half in range(2):
                    h = 2 * pr + half
                    hm = lo if half == 0 else ~lo
                    rowb = jnp.broadcast_to(acs_t[h:h + 1, :], (Q, Q))
                    lm = jnp.exp(jnp.where(causal, rowb.T - rowb, -jnp.inf))
                    m = cb * lm
                    dyh = jnp.where(hm, dy_p, 0.0)
                    gmat = _mm(dyh, xdt_p, _NT)
                    w = gmat * m
                    da_rows = da_rows + jnp.where(lan == h, jnp.sum(w, axis=1, keepdims=True), 0.0)
                    da_cols_t = da_cols_t + jnp.where(sub == h, jnp.sum(w, axis=0, keepdims=True), 0.0)
                    dcb = dcb + gmat * lm
                    dxdt_p = dxdt_p + _mm(m, dyh, _TN)
                hp = hp_ref[0, pr]
                dt_off = dy_p * ea_e[:, sl]
                t_off = _mm(cg, hp)
                wyoff_parts.append(dt_off * t_off)
                dcg = dcg + _mm(dt_off, hp, _NT)
                dhp = _mm(cg, dt_off, _TN)
                dS = dst_ref[pr]
                dxds_p = _mm(bg, dS)
                dbg = dbg + _mm(xds[:, sl], dS, _NT)
                dxds_parts.append(dxds_p)
                dxdt_parts.append(dxdt_p + dxds_p * dse[:, sl])
                dcd_parts.append(jnp.sum(dS * hp, axis=0, keepdims=True))
                dst_ref[pr] = dS * cd_e[:, sl] + dhp
            dcg = dcg + _mm(dcb, bg)
            dbg = dbg + _mm(dcb, cg, _TN)
            dxbc_ref[:, OFF_C + g * SSD_STATE:OFF_C + (g + 1) * SSD_STATE] = dcg
            dxbc_ref[:, OFF_B + g * SSD_STATE:OFF_B + (g + 1) * SSD_STATE] = dbg
        dxdt = jnp.concatenate(dxdt_parts, axis=1)
        dxds = jnp.concatenate(dxds_parts, axis=1)
        wyoff = jnp.concatenate(wyoff_parts, axis=1)
        dcd = jnp.concatenate(dcd_parts, axis=1)
        dxbc_ref[:, :SSD_WIDTH] = dy * de_ref[0] + dxdt * dt_e
        zds = dxds * xds
        dacs = _mmx(wyoff - zds, em, _NT) + da_rows - da_cols_t.T
        dalast = _mmx(jnp.broadcast_to(jnp.sum(zds, axis=0, keepdims=True) + dcd * cd_e, (8, SSD_WIDTH)), em, _NT)[0:1, :]
        dacs = dacs + jnp.where(sub == Q - 1, dalast, 0.0)
        dadt = _mmx(tri, dacs, _TN)
        ddt = dadt * a + _mmx(dxdt * x, em, _NT)
        ddtr = ddt * _sigmoid(dtr_ref[...] + bias_ref[0])
        ddtr_ref[...] = ddtr
        row0 = lax.broadcasted_iota(jnp.int32, (8, LANES), 0) == 0
        dbias_ref[...] += jnp.where(row0, jnp.sum(ddtr, axis=0, keepdims=True), 0.0)
        dalog_ref[...] += jnp.where(row0, jnp.sum(dadt * dt, axis=0, keepdims=True) * a, 0.0)
        ddx = _mmx(jnp.broadcast_to(jnp.sum(dy * x, axis=0, keepdims=True), (8, SSD_WIDTH)), em, _NT)
        dd_ref[...] += jnp.where(row0, ddx, 0.0)

    acc = _full((8, LANES))
    return pl.pallas_call(
        body, name="ssd_bwd", grid=(nc,),
        in_specs=[pl.BlockSpec((Q, SSD_XBC), rev), pl.BlockSpec((Q, LANES), rev),
                  _lp(LANES, i), _lp(LANES, i), _lp(SSD_WIDTH, i),
                  pl.BlockSpec((1, N_PAIR, SSD_STATE, LANES), lambda c: (nc - 1 - c, 0, 0, 0)), pl.BlockSpec((Q, SSD_WIDTH), rev)],
        out_specs=[pl.BlockSpec((Q, SSD_XBC), rev), pl.BlockSpec((Q, LANES), rev), acc, acc, acc],
        out_shape=[_sds((L, SSD_XBC)), _sds((L, LANES)), _sds((8, LANES)), _sds((8, LANES)), _sds((8, LANES))],
        scratch_shapes=[pltpu.VMEM((N_PAIR, SSD_STATE, LANES), F32)],
        compiler_params=_cparams(("arbitrary",)),
    )(xbc, dtr, bias, alog, d_e, hprev, dy)


def _rope_tables(L):
    inv_freq = ROPE_THETA ** (-jnp.arange(0, ATTN_HEAD_DIM, 2, dtype=F32) / ATTN_HEAD_DIM)
    ang = jnp.arange(L, dtype=F32)[:, None] * inv_freq[None, :]
    return jnp.tile(jnp.cos(ang), (1, 4)), jnp.tile(jnp.sin(ang), (1, 4))


def _stacked_params(small, conv_w, cm_w):
    row = lambda a: a[:, None, :]
    pad = lambda a: jnp.pad(a, ((0, 0), (0, LANES - a.shape[1])))[:, None, :]
    return dict(
        nw_mix=row(small["norm_mix_w"]), conv_w=conv_w, conv_b=row(small["ssd_conv_b"]),
        dt_bias=pad(small["ssd_dt_bias"]), a_log=pad(small["ssd_a_log"]),
        d_e=row(jnp.repeat(small["ssd_d"], P, axis=1)), snw=row(small["ssd_norm_w"]),
        qw=row(jnp.tile(small["q_norm_w"], (1, 2))), kw=row(jnp.tile(small["k_norm_w"], (1, 2))),
        sinks=small["attn_sinks"].reshape(-1), cm_w=cm_w, cm_b=row(small["cm_dw_b"]),
        ln_w=row(small["cm_ln_w"]), ln_b=row(small["cm_ln_b"]), nw_mlp=row(small["norm_mlp_w"]))


def _layer_fwd(x, p, wts, i, cos, sin):
    w_in8, w_out8, w_up8, w_down8 = wts
    w_in = _w_in_regroup(w_in8, i)
    h, z, xbc, qkv, glu, dtr = _in_proj(x, p["nw_mix"], i, w_in)
    xbc_c = _ssd_conv_fwd(xbc, p["conv_w"], p["conv_b"], i)
    y_ssd, hprev = _ssd_fwd(xbc_c, dtr, p["dt_bias"], p["a_log"], p["d_e"], i)
    qk = _qk_prep(qkv, p["qw"], p["kw"], i, cos, sin)
    attn = _attn_fwd(qk, qkv, p["sinks"], i)
    c = _cm_conv_fwd(glu, p["cm_w"], p["cm_b"], i)
    ycat = _mix_post(y_ssd, z, attn, c, p["snw"], p["ln_w"], p["ln_b"], i)
    x1 = _mm_res(ycat, w_out8, i, x, "out_proj")
    hm, up, act = _mlp_up(x1, p["nw_mlp"], i, w_up8)
    x2 = _mm_res(act, w_down8, i, x1, "mlp_down")
    saved = dict(x=x, h=h, z=z, xbc=xbc, qkv=qkv, glu=glu, dtr=dtr, xbc_c=xbc_c, y_ssd=y_ssd, hprev=hprev,
                 qk=qk, c=c, ycat=ycat, x1=x1, hm=hm, up=up, act=act, w_in=w_in)
    return x2, saved


def _layer_bwd(dx2, p, wts, i, s, cos, sin):
    _, w_out8, w_up8, w_down8 = wts
    d_up = _mlp_down_bwd(dx2, w_down8, i, s["up"])
    g_down = _mm_tn(s["act"], dx2, "dw_down", 512, D_MODEL, "rows")
    g_up = _mm_tn(s["hm"], d_up, "dw_up", D_MODEL, FF_SHARD, "cols")
    dx1, g_nw_mlp = _mlp_up_bwd(d_up, w_up8, i, dx2, s["x1"], p["nw_mlp"])
    dcat = _out_proj_bwd(dx1, w_out8, i)
    g_out = _mm_tn(s["ycat"], dx1, "dw_out", 512, D_MODEL, "rows")
    dy_ssd, dz, dc, g_snw, g_lw, g_lb = _mix_post_bwd(dcat, s["y_ssd"], s["z"], s["c"], p["snw"], p["ln_w"], p["ln_b"], i)
    da, dg, g_cmw, g_cmb = _cm_conv_bwd(s["glu"], p["cm_w"], i, dc)
    dq, dk, dv, dsk = _attn_bwd(s["qk"], s["qkv"], p["sinks"], i, dcat)
    dqk_raw, g_qw, g_kw = _qk_prep_bwd(dq, dk, s["qkv"], p["qw"], p["kw"], i, cos, sin)
    dxbc_c, ddtr, g_bias, g_alog, g_d = _ssd_bwd(s["xbc_c"], s["dtr"], p["dt_bias"], p["a_log"], p["d_e"], i, s["hprev"], dy_ssd)
    dxbc, g_convw, g_convb = _ssd_conv_bwd(s["xbc"], p["conv_w"], p["conv_b"], i, dxbc_c)
    du = _du_pack([dz, dxbc, dqk_raw, dv, da, dg, ddtr])
    g_in = _g_in_split(_mm_tn(s["h"], du, "dw_in", 512, 640, "f32"))
    dx, g_nw_mix = _in_proj_bwd(du, s["w_in"], dx1, s["x"], p["nw_mix"], i)
    big = (g_in, g_out, g_up, g_down)
    half = ATTN_HEAD_DIM
    small = dict(
        norm_mix_w=g_nw_mix[0], ssd_conv_b=g_convb[0], ssd_dt_bias=g_bias[0, :SSD_HEADS], ssd_a_log=g_alog[0, :SSD_HEADS],
        ssd_d=g_d[0, :SSD_HEADS], ssd_norm_w=g_snw[0], q_norm_w=g_qw[0, :half] + g_qw[0, half:],
        k_norm_w=g_kw[0, :half] + g_kw[0, half:], attn_sinks=dsk[:, 0],
        cm_dw_b=g_cmb[0], cm_ln_w=g_lw[0], cm_ln_b=g_lb[0], norm_mlp_w=g_nw_mlp[0],
        ssd_conv_w=g_convw, cm_dw_w=g_cmw)
    return dx, big, small


MESH = pl.DeviceIdType.MESH
_ANY = pl.BlockSpec(memory_space=pl.ANY)


def _coords():
    return lax.axis_index("x"), lax.axis_index("y"), lax.axis_index("c")


def _all_gather(xs, name):
    nt = len(xs)

    def body(*refs):
        x_refs, out_refs = refs[:nt], refs[nt:2 * nt]
        send_sems, recv_sems, local_sems = refs[2 * nt:]
        x, y, c = _coords()
        me, sibling = (x, y, c), (x, y, 1 - c)
        chips = [(1 - x, y), (x, 1 - y), (1 - x, 1 - y)]

        def slot(t, px, py, pc):
            return out_refs[t].at[4 * px + 2 * py + pc]

        def copy(t, k, block, to, src=None):
            return pltpu.make_async_remote_copy(
                src_ref=slot(t, *block) if src is None else src, dst_ref=slot(t, *block),
                send_sem=send_sems.at[7 * t + k], recv_sem=recv_sems.at[7 * t + k], device_id=to, device_id_type=MESH)

        mine = [pltpu.make_async_copy(x_refs[t], slot(t, *me), local_sems.at[t]) for t in range(nt)]
        for cp in mine:
            cp.start()
        first = []
        for t in range(nt):
            first.append(copy(t, 0, me, sibling, src=x_refs[t]))
            first += [copy(t, 1 + j, me, (*chip, c), src=x_refs[t]) for j, chip in enumerate(chips)]
        for cp in first:
            cp.start()
        passed = []
        for j, chip in enumerate(chips):
            for t in range(nt):
                copy(t, 1 + j, (*chip, c), me).wait_recv()
                passed.append(copy(t, 4 + j, (*chip, c), sibling))
                passed[-1].start()
        for t in range(nt):
            copy(t, 0, sibling, me).wait_recv()
            for j, chip in enumerate(chips):
                copy(t, 4 + j, (*chip, 1 - c), me).wait_recv()
        for cp in first + passed:
            cp.wait_send()
        for cp in mine:
            cp.wait()

    return pl.pallas_call(
        body, name=name, out_shape=[_sds((N_DEV,) + a.shape, a.dtype) for a in xs],
        in_specs=[_ANY] * nt, out_specs=[_ANY] * nt,
        scratch_shapes=[pltpu.SemaphoreType.DMA((7 * nt,)), pltpu.SemaphoreType.DMA((7 * nt,)), pltpu.SemaphoreType.DMA((nt,))],
    )(*xs)


def _rs_sib(grads):
    nt = len(grads)
    flat = [g for kind in grads for g in kind]

    def body(*refs):
        s_refs, ra_refs = refs[:nt * DEPTH], refs[nt * DEPTH:nt * DEPTH + nt]
        send_sems, recv_sems = refs[nt * DEPTH + nt:]
        x, y, c = _coords()
        cps = [pltpu.make_async_remote_copy(
            src_ref=s_refs[t * DEPTH + l].at[:, 1 - c], dst_ref=ra_refs[t].at[l],
            send_sem=send_sems.at[t * DEPTH + l], recv_sem=recv_sems.at[t * DEPTH + l],
            device_id=(x, y, 1 - c), device_id_type=MESH) for t in range(nt) for l in range(DEPTH)]
        for cp in cps:
            cp.start()
        for cp in cps:
            cp.wait()

    return pl.pallas_call(
        body, name="rs_sibling",
        out_shape=[_sds((DEPTH, 4) + kind[0].shape[2:], kind[0].dtype) for kind in grads],
        in_specs=[_ANY] * len(flat), out_specs=[_ANY] * nt,
        scratch_shapes=[pltpu.SemaphoreType.DMA((nt * DEPTH,)), pltpu.SemaphoreType.DMA((nt * DEPTH,))],
    )(*flat)


def _rs_add(kind, ra, core):
    _, _, R, C = kind[0].shape

    def body(c_ref, *refs):
        s_refs, ra_ref, q_ref = refs[:DEPTH], refs[DEPTH], refs[DEPTH + 1]
        for l in range(DEPTH):
            q_ref[l, 0] = (s_refs[l][0, 0].astype(F32) + ra_ref[l, 0].astype(F32)).astype(q_ref.dtype)

    stacked = pl.BlockSpec((DEPTH, 1, TM, C), lambda j, r, c: (0, j, r, 0))
    return pl.pallas_call(
        body, name="rs_add", out_shape=_sds(ra.shape, ra.dtype),
        grid_spec=pltpu.PrefetchScalarGridSpec(
            num_scalar_prefetch=1, grid=(4, R // TM),
            in_specs=[pl.BlockSpec((1, 1, TM, C), lambda j, r, c: (j, c[0], r, 0))] * DEPTH + [stacked],
            out_specs=stacked),
        compiler_params=_cparams(("parallel", "parallel")),
    )(core, *kind, ra)


def _rs_chip(qs):
    nt = len(qs)

    def body(*refs):
        q_refs, rb_refs = refs[:nt], refs[nt:2 * nt]
        send_sems, recv_sems, local_sems = refs[2 * nt:]
        x, y, c = _coords()
        mychip = 2 * x + y
        own = [pltpu.make_async_copy(q_refs[t].at[:, mychip], rb_refs[t].at[:, mychip], local_sems.at[t]) for t in range(nt)]
        for cp in own:
            cp.start()
        sends, recvs = [], []
        for k in range(3):
            px = 1 - x if (k + 1) & 2 else x
            py = 1 - y if (k + 1) & 1 else y
            peer = 2 * px + py
            for t in range(nt):
                sems = dict(send_sem=send_sems.at[3 * t + k], recv_sem=recv_sems.at[3 * t + k],
                            device_id=(px, py, c), device_id_type=MESH)
                sends.append(pltpu.make_async_remote_copy(src_ref=q_refs[t].at[:, peer], dst_ref=rb_refs[t].at[:, mychip], **sems))
                recvs.append(pltpu.make_async_remote_copy(src_ref=q_refs[t].at[:, peer], dst_ref=rb_refs[t].at[:, peer], **sems))
        for cp in sends:
            cp.start()
        for cp in recvs:
            cp.wait_recv()
        for cp in sends:
            cp.wait_send()
        for cp in own:
            cp.wait()

    return pl.pallas_call(
        body, name="rs_chip", out_shape=[_sds(q.shape, q.dtype) for q in qs],
        in_specs=[_ANY] * nt, out_specs=[_ANY] * nt,
        scratch_shapes=[pltpu.SemaphoreType.DMA((3 * nt,)), pltpu.SemaphoreType.DMA((3 * nt,)), pltpu.SemaphoreType.DMA((nt,))],
    )(*qs)


def _adamw(w, g, m, v):
    m = ADAM_B1 * m + (1.0 - ADAM_B1) * g
    v = ADAM_B2 * v + (1.0 - ADAM_B2) * jnp.square(g)
    m_hat = m / (1.0 - ADAM_B1 ** ADAM_STEP)
    v_hat = v / (1.0 - ADAM_B2 ** ADAM_STEP)
    delta = -ADAM_LR * (m_hat / (jnp.sqrt(v_hat) + ADAM_EPS) + ADAM_WD * w)
    return delta, m, v


def _rs_final(rb, w, m, v):
    _, R, C = w.shape
    cp = rb.shape[3]

    def body(rb_ref, w_ref, m_ref, v_ref, g_ref, d_ref, m2_ref, v2_ref):
        g = rb_ref[0, 0].astype(F32)
        for j in range(1, 4):
            g = g + rb_ref[0, j].astype(F32)
        g = g[:, :C]
        g_ref[0] = g
        d_ref[0], m2_ref[0], v2_ref[0] = _adamw(w_ref[0], g, m_ref[0], v_ref[0])

    blk = pl.BlockSpec((1, TM, C), lambda l, r: (l, r, 0))
    return pl.pallas_call(
        body, name="rs_final_adamw", grid=(DEPTH, R // TM),
        in_specs=[pl.BlockSpec((1, 4, TM, cp), lambda l, r: (l, 0, r, 0)), blk, blk, blk],
        out_specs=[blk] * 4, out_shape=[_sds(w.shape)] * 4,
        compiler_params=_cparams(("parallel", "parallel")),
    )(rb, w, m, v)


def _sum8(g8):
    _, R, C = g8.shape

    def body(g_ref, o_ref):
        acc = g_ref[0]
        for d in range(1, N_DEV):
            acc = acc + g_ref[d]
        o_ref[...] = acc

    return pl.pallas_call(body, name="small_sum", out_shape=_sds((R, C)))(g8)


def _adamw_small(w, g, m, v):
    def body(w_ref, g_ref, m_ref, v_ref, d_ref, m2_ref, v2_ref):
        d_ref[...], m2_ref[...], v2_ref[...] = _adamw(w_ref[...], g_ref[...], m_ref[...], v_ref[...])

    return pl.pallas_call(body, name="small_adamw", out_shape=[_sds(w.shape)] * 3)(w, g, m, v)


REP = (("norm_mix_w", 1024), ("ssd_conv_b", 1536), ("ssd_dt_bias", 16), ("ssd_a_log", 16), ("ssd_d", 16),
       ("ssd_norm_w", 1024), ("q_norm_w", 64), ("k_norm_w", 64), ("attn_sinks", 8), ("cm_dw_b", 512),
       ("cm_ln_w", 512), ("cm_ln_b", 512), ("norm_mlp_w", 1024))
WEIGHTS = ("norm_mix_w", "w_in", "ssd_conv_w", "ssd_conv_b", "ssd_dt_bias", "ssd_a_log", "ssd_d", "ssd_norm_w",
           "q_norm_w", "k_norm_w", "attn_sinks", "cm_dw_w", "cm_dw_b", "cm_ln_w", "cm_ln_b", "w_out", "norm_mlp_w",
           "w_mlp_up", "w_mlp_down")
BIG = ("w_in", "w_out", "w_mlp_up", "w_mlp_down")
N_REP = DEPTH * sum(n for _, n in REP)
CONVW_SHARD = SSD_XBC // N_DEV
CMW_SHARD = CM_CHANNELS // N_DEV


def _to_rows(flat, rows):
    return jnp.pad(flat, (0, rows * LANES - flat.shape[0])).reshape(rows, LANES)


def kernel(x, norm_mix_w, w_in, ssd_conv_w, ssd_conv_b, ssd_dt_bias, ssd_a_log, ssd_d, ssd_norm_w, q_norm_w, k_norm_w, attn_sinks, cm_dw_w, cm_dw_b, cm_ln_w, cm_ln_b, w_out, norm_mlp_w, w_mlp_up, w_mlp_down, loss_target, m_norm_mix_w, m_w_in, m_ssd_conv_w, m_ssd_conv_b, m_ssd_dt_bias, m_ssd_a_log, m_ssd_d, m_ssd_norm_w, m_q_norm_w, m_k_norm_w, m_attn_sinks, m_cm_dw_w, m_cm_dw_b, m_cm_ln_w, m_cm_ln_b, m_w_out, m_norm_mlp_w, m_w_mlp_up, m_w_mlp_down, v_norm_mix_w, v_w_in, v_ssd_conv_w, v_ssd_conv_b, v_ssd_dt_bias, v_ssd_a_log, v_ssd_d, v_ssd_norm_w, v_q_norm_w, v_k_norm_w, v_attn_sinks, v_cm_dw_w, v_cm_dw_b, v_cm_ln_w, v_cm_ln_b, v_w_out, v_norm_mlp_w, v_w_mlp_up, v_w_mlp_down):
    w = dict(norm_mix_w=norm_mix_w, w_in=w_in, ssd_conv_w=ssd_conv_w, ssd_conv_b=ssd_conv_b, ssd_dt_bias=ssd_dt_bias, ssd_a_log=ssd_a_log, ssd_d=ssd_d, ssd_norm_w=ssd_norm_w, q_norm_w=q_norm_w, k_norm_w=k_norm_w, attn_sinks=attn_sinks, cm_dw_w=cm_dw_w, cm_dw_b=cm_dw_b, cm_ln_w=cm_ln_w, cm_ln_b=cm_ln_b, w_out=w_out, norm_mlp_w=norm_mlp_w, w_mlp_up=w_mlp_up, w_mlp_down=w_mlp_down)
    m = dict(norm_mix_w=m_norm_mix_w, w_in=m_w_in, ssd_conv_w=m_ssd_conv_w, ssd_conv_b=m_ssd_conv_b, ssd_dt_bias=m_ssd_dt_bias, ssd_a_log=m_ssd_a_log, ssd_d=m_ssd_d, ssd_norm_w=m_ssd_norm_w, q_norm_w=m_q_norm_w, k_norm_w=m_k_norm_w, attn_sinks=m_attn_sinks, cm_dw_w=m_cm_dw_w, cm_dw_b=m_cm_dw_b, cm_ln_w=m_cm_ln_w, cm_ln_b=m_cm_ln_b, w_out=m_w_out, norm_mlp_w=m_norm_mlp_w, w_mlp_up=m_w_mlp_up, w_mlp_down=m_w_mlp_down)
    v = dict(norm_mix_w=v_norm_mix_w, w_in=v_w_in, ssd_conv_w=v_ssd_conv_w, ssd_conv_b=v_ssd_conv_b, ssd_dt_bias=v_ssd_dt_bias, ssd_a_log=v_ssd_a_log, ssd_d=v_ssd_d, ssd_norm_w=v_ssd_norm_w, q_norm_w=v_q_norm_w, k_norm_w=v_k_norm_w, attn_sinks=v_attn_sinks, cm_dw_w=v_cm_dw_w, cm_dw_b=v_cm_dw_b, cm_ln_w=v_cm_ln_w, cm_ln_b=v_cm_ln_b, w_out=v_w_out, norm_mlp_w=v_norm_mlp_w, w_mlp_up=v_w_mlp_up, w_mlp_down=v_w_mlp_down)
    L = x.shape[1]
    xi, yi, ci = _coords()
    me = 4 * xi + 2 * yi + ci
    n_conv = DEPTH * SSD_CONV * CONVW_SHARD
    n_cm = DEPTH * CM_CONV * CMW_SHARD

    conv_rows = 88
    cw8, = _all_gather([_to_rows(jnp.concatenate([ssd_conv_w.reshape(-1), cm_dw_w.reshape(-1)]), conv_rows)], "ag_conv_w")
    cw8 = cw8.reshape(N_DEV, -1)
    conv_full = cw8[:, :n_conv].reshape(N_DEV, DEPTH, SSD_CONV, CONVW_SHARD).transpose(1, 2, 0, 3).reshape(DEPTH, SSD_CONV, SSD_XBC)
    cm_full = cw8[:, n_conv:n_conv + n_cm].reshape(N_DEV, DEPTH, CM_CONV, CMW_SHARD).transpose(1, 2, 0, 3).reshape(DEPTH, CM_CONV, CM_CHANNELS)
    wts = _all_gather([_cast_shard(w_in, W_IN_SHARD_P), _cast_shard(w_out), _cast_shard(w_mlp_up), _cast_shard(w_mlp_down)], "ag_weights")

    cos, sin = _rope_tables(L)
    p = _stacked_params({k: w[k] for k, _ in REP}, conv_full, cm_full)
    saved = []
    h = x[0]
    for i in range(DEPTH):
        h, s = _layer_fwd(h, p, wts, i, cos, sin)
        saved.append(s)
    d, loss_tile = _loss_head(h, loss_target[0])

    bigs, smalls = [None] * DEPTH, [None] * DEPTH
    for i in reversed(range(DEPTH)):
        d, bigs[i], smalls[i] = _layer_bwd(d, p, wts, i, saved[i], cos, sin)

    kinds = [[bigs[l][t].reshape((4, 2) + bigs[l][t].shape[1:]) for l in range(DEPTH)] for t in range(len(BIG))]
    ras = _rs_sib(kinds)
    core = jnp.reshape(ci, (1,)).astype(jnp.int32)
    rbs = _rs_chip([_rs_add(kinds[t], ras[t], core) for t in range(len(BIG))])
    big_out = {k: _rs_final(rbs[t], w[k], m[k], v[k]) for t, k in enumerate(BIG)}

    gvec = jnp.concatenate(
        [jnp.stack([smalls[i][k] for i in range(DEPTH)]).reshape(-1) for k, _ in REP]
        + [jnp.stack([smalls[i][k] for i in range(DEPTH)]).reshape(-1) for k in ("ssd_conv_w", "cm_dw_w")]
        + [loss_tile[0, :1]])
    g_rows = -(-gvec.shape[0] // (8 * LANES)) * 8
    g8, = _all_gather([_to_rows(gvec, g_rows)], "ag_small_grads")
    gsum = _sum8(g8).reshape(-1)
    o_conv = N_REP
    o_cm = o_conv + DEPTH * SSD_CONV * SSD_XBC
    o_loss = o_cm + DEPTH * CM_CONV * CM_CHANNELS
    g_conv = lax.dynamic_slice_in_dim(gsum[o_conv:o_cm].reshape(DEPTH, SSD_CONV, SSD_XBC), me * CONVW_SHARD, CONVW_SHARD, axis=2)
    g_cm = lax.dynamic_slice_in_dim(gsum[o_cm:o_loss].reshape(DEPTH, CM_CONV, CM_CHANNELS), me * CMW_SHARD, CMW_SHARD, axis=2)
    loss = gsum[o_loss]
    s_rows = -(-(N_REP + n_conv + n_cm) // (8 * LANES)) * 8

    def pack_small(t):
        return _to_rows(jnp.concatenate([t[k].reshape(-1) for k, _ in REP] + [t["ssd_conv_w"].reshape(-1), t["cm_dw_w"].reshape(-1)]), s_rows)

    g_small = _to_rows(jnp.concatenate([gsum[:N_REP], g_conv.reshape(-1), g_cm.reshape(-1)]), s_rows)
    small_out = [g_small] + list(_adamw_small(pack_small(w), g_small, pack_small(m), pack_small(v)))

    def unpack_small(t):
        flat = t.reshape(-1)
        out, off = {}, 0
        for k, n in REP:
            out[k] = flat[off:off + DEPTH * n].reshape(DEPTH, n)
            off += DEPTH * n
        out["ssd_conv_w"] = flat[off:off + n_conv].reshape(DEPTH, SSD_CONV, CONVW_SHARD)
        off += n_conv
        out["cm_dw_w"] = flat[off:off + n_cm].reshape(DEPTH, CM_CONV, CMW_SHARD)
        return out

    outs = [loss, d[None]]
    for j, small_t in enumerate(small_out):
        t = unpack_small(small_t)
        for k in BIG:
            t[k] = big_out[k][j]
        outs += [t[k] for k in WEIGHTS]
    return tuple(outs)
```

```python
import math

import jax
import jax.numpy as jnp
from jax import lax
from jax.experimental import pallas as pl
from jax.experimental.pallas import tpu as pltpu

F32 = jnp.float32
_MM = jnp.bfloat16

D_MODEL = 1024
DEPTH = 4
SSD_WIDTH = 1024
SSD_HEADS = 16
SSD_STATE = 128
SSD_GROUPS = 2
SSD_CONV = 4
SSD_XBC = 1536
Q = 128
ATTN_HEAD_DIM = 64
ATTN_Q_HEADS = 8
CM_CHANNELS = 512
CM_CONV = 31
D_FF = 4096
D_MIX = 2048
N_IN = 4368
RMS_EPS = 1e-6
LN_EPS = 1e-5
ROPE_THETA = 10000.0
ADAM_LR = 0.001
ADAM_B1 = 0.9
ADAM_B2 = 0.999
ADAM_EPS = 1e-08
ADAM_WD = 0.01
ADAM_STEP = 10

N_DEV = 8
LANES = 128
TM = 256
N_IN_P = 4480
U_Z, U_XBC, U_QKV, U_GLU, U_DT = (0, 1024), (1024, 2560), (2560, 3328), (3328, 4352), (4352, 4480)
W_IN_SHARD = N_IN // N_DEV
W_IN_SHARD_P = 640
FF_SHARD = D_FF // N_DEV
OUT_SHARD = D_MIX // N_DEV

_NN = (((1,), (0,)), ((), ()))
_NT = (((1,), (1,)), ((), ()))
_TN = (((0,), (0,)), ((), ()))
_VMEM_LIMIT = 56 * 1024 * 1024


def _mm(a, b, dims=_NN):
    return lax.dot_general(a.astype(_MM), b.astype(_MM), dims, preferred_element_type=F32)


def _mmx(a, b, dims=_NN):
    return lax.dot_general(a, b, dims, precision=lax.Precision.HIGHEST, preferred_element_type=F32)


def _sds(shape, dtype=F32):
    return jax.ShapeDtypeStruct(tuple(shape), dtype)


def _full(shape):
    nd = len(shape)
    return pl.BlockSpec(tuple(shape), lambda *_: (0,) * nd)


def _rows(cols, tm=TM, col=0):
    return pl.BlockSpec((tm, cols), lambda i: (i, col))


def _lp(n, i):
    return pl.BlockSpec((1, 1, n), lambda *_: (i, 0, 0))


def _lw(arr):
    return pl.BlockSpec(arr.shape, lambda *_: (0, 0, 0, 0))


_ANY = pl.BlockSpec(memory_space=pl.ANY)


def _cparams(sem=None):
    return pltpu.CompilerParams(dimension_semantics=sem, vmem_limit_bytes=_VMEM_LIMIT)


def _sigmoid(x):
    return 1.0 / (1.0 + jnp.exp(-x))


def _silu(x):
    return x * _sigmoid(x)


def _dsilu(x):
    s = _sigmoid(x)
    return s * (1.0 + x * (1.0 - s))


def _rms_bwd(dy, x, w, inv_n):
    r = lax.rsqrt(jnp.sum(x * x, axis=-1, keepdims=True) * inv_n + RMS_EPS)
    xh = x * r
    dxh = dy * w
    dx = r * (dxh - xh * (jnp.sum(dxh * xh, axis=-1, keepdims=True) * inv_n))
    return dx, dy * xh


def _cast_shard(w, me, cols_p=None):
    _, R, C = w.shape
    cp = C if cols_p is None else cols_p

    def body(me_ref, w_ref, *o_refs):
        v = w_ref[0]
        if cp != C:
            v = jnp.concatenate([v, jnp.zeros((R, cp - C), F32)], axis=1)
        for k in range(DEPTH):
            @pl.when(pl.program_id(0) == k)
            def _():
                o_refs[k][0, 0] = v.astype(_MM)

    return pl.pallas_call(
        body, name="cast_shard", out_shape=[_sds((N_DEV, 1, R, cp), _MM)] * DEPTH,
        grid_spec=pltpu.PrefetchScalarGridSpec(
            num_scalar_prefetch=1, grid=(DEPTH,),
            in_specs=[pl.BlockSpec((1, R, C), lambda l, me: (l, 0, 0))],
            out_specs=[pl.BlockSpec((1, 1, R, cp), lambda l, me: (me[0], 0, 0, 0))] * DEPTH),
        compiler_params=_cparams(("arbitrary",)),
    )(me, w)


def _w_in_regroup(w8, after):
    a, b = U_XBC[1], U_XBC[1] + SSD_HEADS

    def body(w_ref, after_ref, o_ref):
        w = jnp.concatenate([w_ref[j, 0][:, :W_IN_SHARD].astype(F32) for j in range(N_DEV)], axis=1)
        r = jnp.concatenate([w[:, :a], w[:, b:], w[:, a:b], jnp.zeros((TM, N_IN_P - N_IN), F32)], axis=1)
        o_ref[...] = r.astype(_MM)

    return pl.pallas_call(
        body, name="w_in_regroup", grid=(D_MODEL // TM,),
        in_specs=[pl.BlockSpec((N_DEV, 1, TM, W_IN_SHARD_P), lambda r: (0, 0, r, 0)), _ANY],
        out_specs=_rows(N_IN_P), out_shape=_sds((D_MODEL, N_IN_P), _MM),
        compiler_params=_cparams(("parallel",)),
    )(w8, after)


def _g_in_split(g):
    a = U_XBC[1]

    def body(g_ref, o_ref):
        v = g_ref[...]
        w = jnp.concatenate([v[:, :a], v[:, U_DT[0]:U_DT[0] + SSD_HEADS], v[:, a:U_DT[0]]], axis=1)
        pad = jnp.zeros((TM, W_IN_SHARD_P - W_IN_SHARD), F32)
        for j in range(N_DEV):
            o_ref[j] = jnp.concatenate([w[:, j * W_IN_SHARD:(j + 1) * W_IN_SHARD], pad], axis=1).astype(_MM)

    return pl.pallas_call(
        body, name="g_in_split", grid=(D_MODEL // TM,),
        in_specs=[_rows(N_IN_P)],
        out_specs=pl.BlockSpec((N_DEV, TM, W_IN_SHARD_P), lambda r: (0, r, 0)),
        out_shape=_sds((N_DEV, D_MODEL, W_IN_SHARD_P), _MM),
        compiler_params=_cparams(("parallel",)),
    )(g)


def _in_proj(x, nw, i, w):
    L = x.shape[0]
    splits = (U_Z, U_XBC, U_QKV, U_GLU, U_DT)

    def body(x_ref, nw_ref, w_ref, h_ref, *out_refs):
        xf = x_ref[...]
        r = lax.rsqrt(jnp.mean(xf * xf, axis=-1, keepdims=True) + RMS_EPS)
        h = (xf * r * nw_ref[0]).astype(_MM)
        h_ref[...] = h
        for ref, (a, b) in zip(out_refs, splits):
            ref[...] = lax.dot_general(h, w_ref[:, a:b], _NN, preferred_element_type=F32)

    return pl.pallas_call(
        body, name="in_proj", grid=(L // TM,),
        in_specs=[_rows(D_MODEL), _lp(D_MODEL, i), _full(w.shape)],
        out_specs=[_rows(D_MODEL)] + [_rows(b - a) for a, b in splits],
        out_shape=[_sds((L, D_MODEL), _MM)] + [_sds((L, b - a)) for a, b in splits],
        compiler_params=_cparams(("parallel",)),
    )(x, nw, w)


def _mlp_up(x, nw, i, w8):
    L = x.shape[0]

    def body(x_ref, nw_ref, w_ref, h_ref, up_ref, act_ref):
        xf = x_ref[...]
        r = lax.rsqrt(jnp.mean(xf * xf, axis=-1, keepdims=True) + RMS_EPS)
        h = (xf * r * nw_ref[0]).astype(_MM)
        h_ref[...] = h
        for j in range(N_DEV):
            sl = slice(j * FF_SHARD, (j + 1) * FF_SHARD)
            up = lax.dot_general(h, w_ref[j, 0], _NN, preferred_element_type=F32)
            up_ref[:, sl] = up
            act_ref[:, sl] = jnp.square(jnp.maximum(up, 0.0)).astype(_MM)

    return pl.pallas_call(
        body, name="mlp_up", grid=(L // TM,),
        in_specs=[_rows(D_MODEL), _lp(D_MODEL, i), _lw(w8)],
        out_specs=[_rows(D_MODEL), _rows(D_FF), _rows(D_FF)],
        out_shape=[_sds((L, D_MODEL), _MM), _sds((L, D_FF)), _sds((L, D_FF), _MM)],
        compiler_params=_cparams(("parallel",)),
    )(x, nw, w8)


def _mm_res(a, w8, i, res, name):
    L, K = a.shape
    N = w8.shape[3]

    def body(a_ref, w_ref, res_ref, o_ref):
        w = w_ref[:, 0].reshape(K, N)
        o_ref[...] = res_ref[...] + lax.dot_general(a_ref[...], w, _NN, preferred_element_type=F32)

    return pl.pallas_call(
        body, name=name, grid=(L // TM,),
        in_specs=[_rows(K), _lw(w8), _rows(N)],
        out_specs=_rows(N), out_shape=_sds((L, N)),
        compiler_params=_cparams(("parallel",)),
    )(a, w8, res)


def _out_proj_bwd(a, w8, i):
    L = a.shape[0]

    def body(a_ref, w_ref, o_ref):
        w = w_ref[:, 0].reshape(D_MIX, D_MODEL)
        o_ref[...] = lax.dot_general(a_ref[...].astype(_MM), w, _NT, preferred_element_type=F32)

    return pl.pallas_call(
        body, name="out_proj_bwd", grid=(L // TM,),
        in_specs=[_rows(D_MODEL), _lw(w8)],
        out_specs=_rows(D_MIX), out_shape=_sds((L, D_MIX)),
        compiler_params=_cparams(("parallel",)),
    )(a, w8)


def _mlp_down_bwd(dy, w8, i, up, after):
    L = dy.shape[0]

    def body(dy_ref, w_ref, up_ref, after_ref, o_ref):
        d = dy_ref[...].astype(_MM)
        for j in range(N_DEV):
            sl = slice(j * FF_SHARD, (j + 1) * FF_SHARD)
            da = lax.dot_general(d, w_ref[j, 0], _NT, preferred_element_type=F32)
            o_ref[:, sl] = (da * (2.0 * jnp.maximum(up_ref[:, sl], 0.0))).astype(_MM)

    return pl.pallas_call(
        body, name="mlp_down_bwd", grid=(L // TM,),
        in_specs=[_rows(D_MODEL), _lw(w8), _rows(D_FF), _ANY],
        out_specs=_rows(D_FF), out_shape=_sds((L, D_FF), _MM),
        compiler_params=_cparams(("parallel",)),
    )(dy, w8, up, after)


def _rms_bwd_epilogue(dh, res_ref, x_ref, nw_ref, dx_ref, dnw_ref):
    dx, dwx = _rms_bwd(dh, x_ref[...], nw_ref[0], 1.0 / D_MODEL)
    dx_ref[...] = res_ref[...] + dx

    @pl.when(pl.program_id(0) == 0)
    def _():
        dnw_ref[...] = jnp.zeros_like(dnw_ref)

    dnw_ref[...] += jnp.sum(dwx, axis=0, keepdims=True)


def _mlp_up_bwd(d_up, w8, i, res, x, nw):
    L = d_up.shape[0]

    def body(a_ref, w_ref, res_ref, x_ref, nw_ref, dx_ref, dnw_ref):
        dh = jnp.zeros((TM, D_MODEL), F32)
        for j in range(N_DEV):
            dh = dh + lax.dot_general(a_ref[:, j * FF_SHARD:(j + 1) * FF_SHARD], w_ref[j, 0], _NT, preferred_element_type=F32)
        _rms_bwd_epilogue(dh, res_ref, x_ref, nw_ref, dx_ref, dnw_ref)

    return pl.pallas_call(
        body, name="mlp_up_bwd", grid=(L // TM,),
        in_specs=[_rows(D_FF), _lw(w8), _rows(D_MODEL), _rows(D_MODEL), _lp(D_MODEL, i)],
        out_specs=[_rows(D_MODEL), _full((1, D_MODEL))],
        out_shape=[_sds((L, D_MODEL)), _sds((1, D_MODEL))],
        compiler_params=_cparams(("arbitrary",)),
    )(d_up, w8, res, x, nw)


def _in_proj_bwd(du, w, res, x, nw, i):
    L = du.shape[0]

    def body(a_ref, w_ref, res_ref, x_ref, nw_ref, dx_ref, dnw_ref):
        dh = lax.dot_general(a_ref[...], w_ref[...], _NT, preferred_element_type=F32)
        _rms_bwd_epilogue(dh, res_ref, x_ref, nw_ref, dx_ref, dnw_ref)

    return pl.pallas_call(
        body, name="in_proj_bwd", grid=(L // TM,),
        in_specs=[_rows(N_IN_P), _full(w.shape), _rows(D_MODEL), _rows(D_MODEL), _lp(D_MODEL, i)],
        out_specs=[_rows(D_MODEL), _full((1, D_MODEL))],
        out_shape=[_sds((L, D_MODEL)), _sds((1, D_MODEL))],
        compiler_params=_cparams(("arbitrary",)),
    )(du, w, res, x, nw)


def _mm_tn(a, g, name, tk, tn, out):
    L, K = a.shape
    N = g.shape[1]

    def body(a_ref, g_ref, o_ref):
        r = lax.dot_general(a_ref[...].astype(_MM), g_ref[...].astype(_MM), _TN, preferred_element_type=F32)
        o_ref[...] = r.astype(o_ref.dtype).reshape(o_ref.shape)

    if out == "f32":
        out_spec, out_shape = pl.BlockSpec((tk, tn), lambda i, j: (i, j)), _sds((K, N))
    elif out == "rows":
        assert tn == N and tk % (K // N_DEV) == 0
        nblk = tk // (K // N_DEV)
        out_spec, out_shape = pl.BlockSpec((nblk, K // N_DEV, N), lambda i, j: (i, 0, 0)), _sds((N_DEV, K // N_DEV, N), _MM)
    else:
        assert tk == K and tn == N // N_DEV
        out_spec, out_shape = pl.BlockSpec((1, K, tn), lambda i, j: (j, 0, 0)), _sds((N_DEV, K, tn), _MM)
    return pl.pallas_call(
        body, name=name, grid=(K // tk, N // tn),
        in_specs=[pl.BlockSpec((L, tk), lambda i, j: (0, i)), pl.BlockSpec((L, tn), lambda i, j: (0, j))],
        out_specs=out_spec, out_shape=out_shape,
        compiler_params=_cparams(("parallel", "parallel")),
    )(a, g)


def _loss_head(y, t):
    L = y.shape[0]

    def body(y_ref, t_ref, dy_ref, l_ref):
        e = y_ref[...] - t_ref[...]
        dy_ref[...] = e * (1.0 / D_MODEL)

        @pl.when(pl.program_id(0) == 0)
        def _():
            l_ref[...] = jnp.zeros_like(l_ref)

        l_ref[...] += jnp.sum(jnp.sum(e * e, axis=1, keepdims=True), axis=0, keepdims=True) * (0.5 / D_MODEL)

    return pl.pallas_call(
        body, name="loss_head", grid=(L // TM,),
        in_specs=[_rows(D_MODEL), _rows(D_MODEL)],
        out_specs=[_rows(D_MODEL), _full((8, LANES))],
        out_shape=[_sds((L, D_MODEL)), _sds((8, LANES))],
        compiler_params=_cparams(("arbitrary",)),
    )(y, t)


def _du_pack(pieces):
    L = pieces[0].shape[0]

    def body(*refs):
        o_ref = refs[-1]
        off = 0
        for r in refs[:-1]:
            n = r.shape[1]
            o_ref[:, off:off + n] = r[...].astype(_MM)
            off += n

    return pl.pallas_call(
        body, name="du_pack", grid=(L // TM,),
        in_specs=[_rows(p.shape[1]) for p in pieces],
        out_specs=_rows(N_IN_P), out_shape=_sds((L, N_IN_P), _MM),
        compiler_params=_cparams(("parallel",)),
    )(*pieces)


def _shift_down(x, s):
    if s == 0:
        return x
    r = lax.broadcasted_iota(jnp.int32, x.shape, 0)
    return jnp.where(r >= s, pltpu.roll(x, s, axis=0), 0.0)


def _shift_up(x, s):
    if s == 0:
        return x
    n = x.shape[0]
    r = lax.broadcasted_iota(jnp.int32, x.shape, 0)
    return jnp.where(r < n - s, pltpu.roll(x, n - s, axis=0), 0.0)


def _conv_taps(x, w_ref, b, k_w):
    acc = jnp.broadcast_to(b, x.shape)
    for k in range(k_w):
        acc = acc + w_ref[0, k:k + 1, :] * _shift_down(x, k_w - 1 - k)
    return acc


def _conv_bwd_taps(x, dc, w_ref, dw_ref, db_ref, k_w):
    dx = jnp.zeros_like(x)
    for k in range(k_w):
        s = k_w - 1 - k
        dx = dx + w_ref[0, k:k + 1, :] * _shift_up(dc, s)
        dw_ref[k:k + 1, :] = jnp.sum(dc * _shift_down(x, s), axis=0, keepdims=True)
    db_ref[...] = jnp.sum(dc, axis=0, keepdims=True)
    return dx


def _cols(L, cb, off=0):
    return pl.BlockSpec((L, cb), lambda j: (0, j + off))


def _lcols(k, cb, i):
    return pl.BlockSpec((1, k, cb), lambda j: (i, 0, j))


SSD_CB = 256


def _ssd_conv_fwd(x, w, b, i):
    L, C = x.shape
    cb = SSD_CB

    def body(x_ref, w_ref, b_ref, o_ref):
        o_ref[...] = _silu(_conv_taps(x_ref[...], w_ref, b_ref[0], SSD_CONV))

    return pl.pallas_call(
        body, name="ssd_conv_fwd", grid=(C // cb,),
        in_specs=[_cols(L, cb), _lcols(SSD_CONV, cb, i), _lcols(1, cb, i)],
        out_specs=_cols(L, cb), out_shape=_sds((L, C)),
        compiler_params=_cparams(("parallel",)),
    )(x, w, b)


def _ssd_conv_bwd(x, w, b, i, dy):
    L, C = x.shape
    cb = SSD_CB

    def body(x_ref, w_ref, b_ref, dy_ref, dx_ref, dw_ref, db_ref):
        x_ = x_ref[...]
        c = _conv_taps(x_, w_ref, b_ref[0], SSD_CONV)
        dc = dy_ref[...] * _dsilu(c)
        dx_ref[...] = _conv_bwd_taps(x_, dc, w_ref, dw_ref, db_ref, SSD_CONV)

    return pl.pallas_call(
        body, name="ssd_conv_bwd", grid=(C // cb,),
        in_specs=[_cols(L, cb), _lcols(SSD_CONV, cb, i), _lcols(1, cb, i), _cols(L, cb)],
        out_specs=[_cols(L, cb), _cols(SSD_CONV, cb), _cols(1, cb)],
        out_shape=[_sds((L, C)), _sds((SSD_CONV, C)), _sds((1, C))],
        compiler_params=_cparams(("parallel",)),
    )(x, w, b, dy)


def _cm_conv_fwd(glu, w, b, i):
    L = glu.shape[0]
    cb = LANES
    nb = CM_CHANNELS // cb

    def body(a_ref, g_ref, w_ref, b_ref, o_ref):
        h = a_ref[...] * _sigmoid(g_ref[...])
        o_ref[...] = _conv_taps(h, w_ref, b_ref[0], CM_CONV)

    return pl.pallas_call(
        body, name="cm_conv_fwd", grid=(nb,),
        in_specs=[_cols(L, cb), _cols(L, cb, nb), _lcols(CM_CONV, cb, i), _lcols(1, cb, i)],
        out_specs=_cols(L, cb), out_shape=_sds((L, CM_CHANNELS)),
        compiler_params=_cparams(("parallel",)),
    )(glu, glu, w, b)


def _cm_conv_bwd(glu, w, i, dc):
    L = glu.shape[0]
    cb = LANES
    nb = CM_CHANNELS // cb

    def body(a_ref, g_ref, w_ref, dc_ref, da_ref, dg_ref, dw_ref, db_ref):
        a = a_ref[...]
        sg = _sigmoid(g_ref[...])
        dh = _conv_bwd_taps(a * sg, dc_ref[...], w_ref, dw_ref, db_ref, CM_CONV)
        da_ref[...] = dh * sg
        dg_ref[...] = dh * a * sg * (1.0 - sg)

    return pl.pallas_call(
        body, name="cm_conv_bwd", grid=(nb,),
        in_specs=[_cols(L, cb), _cols(L, cb, nb), _lcols(CM_CONV, cb, i), _cols(L, cb)],
        out_specs=[_cols(L, cb), _cols(L, cb), _cols(CM_CONV, cb), _cols(1, cb)],
        out_shape=[_sds((L, CM_CHANNELS)), _sds((L, CM_CHANNELS)), _sds((CM_CONV, CM_CHANNELS)), _sds((1, CM_CHANNELS))],
        compiler_params=_cparams(("parallel",)),
    )(glu, glu, w, dc)


GRP = SSD_WIDTH // SSD_GROUPS


def _mix_post(y, z, attn, c, snw, lw, lb, i):
    L = y.shape[0]

    def body(y_ref, z_ref, a_ref, c_ref, snw_ref, lw_ref, lb_ref, o_ref):
        g = y_ref[...] * _silu(z_ref[...])
        for k in range(SSD_GROUPS):
            sl = slice(k * GRP, (k + 1) * GRP)
            gg = g[:, sl]
            r = lax.rsqrt(jnp.mean(gg * gg, axis=-1, keepdims=True) + RMS_EPS)
            o_ref[:, sl] = (gg * r * snw_ref[0, :, sl]).astype(_MM)
        o_ref[:, SSD_WIDTH:SSD_WIDTH + 512] = a_ref[...].astype(_MM)
        cv = c_ref[...]
        mu = jnp.mean(cv, axis=-1, keepdims=True)
        xc = cv - mu
        rs = lax.rsqrt(jnp.mean(xc * xc, axis=-1, keepdims=True) + LN_EPS)
        o_ref[:, SSD_WIDTH + 512:] = _silu(xc * rs * lw_ref[0] + lb_ref[0]).astype(_MM)

    return pl.pallas_call(
        body, name="mix_post", grid=(L // TM,),
        in_specs=[_rows(SSD_WIDTH), _rows(SSD_WIDTH), _rows(512), _rows(512),
                  _lp(SSD_WIDTH, i), _lp(512, i), _lp(512, i)],
        out_specs=_rows(D_MIX), out_shape=_sds((L, D_MIX), _MM),
        compiler_params=_cparams(("parallel",)),
    )(y, z, attn, c, snw, lw, lb)


def _mix_post_bwd(dcat, y, z, c, snw, lw, lb, i):
    L = y.shape[0]

    def body(d_ref, y_ref, z_ref, c_ref, snw_ref, lw_ref, lb_ref, dy_ref, dz_ref, dc_ref, dsnw_ref, dlw_ref, dlb_ref):
        @pl.when(pl.program_id(0) == 0)
        def _():
            dsnw_ref[...] = jnp.zeros_like(dsnw_ref)
            dlw_ref[...] = jnp.zeros_like(dlw_ref)
            dlb_ref[...] = jnp.zeros_like(dlb_ref)

        yv = y_ref[...]
        zv = z_ref[...]
        sz = _silu(zv)
        g = yv * sz
        for k in range(SSD_GROUPS):
            sl = slice(k * GRP, (k + 1) * GRP)
            dgg, dwx = _rms_bwd(d_ref[:, sl], g[:, sl], snw_ref[0, :, sl], 1.0 / GRP)
            dsnw_ref[:, sl] += jnp.sum(dwx, axis=0, keepdims=True)
            dy_ref[:, sl] = dgg * sz[:, sl]
            dz_ref[:, sl] = dgg * yv[:, sl] * _dsilu(zv[:, sl])
        cv = c_ref[...]
        mu = jnp.mean(cv, axis=-1, keepdims=True)
        xc = cv - mu
        rs = lax.rsqrt(jnp.mean(xc * xc, axis=-1, keepdims=True) + LN_EPS)
        xh = xc * rs
        ln = xh * lw_ref[0] + lb_ref[0]
        dln = d_ref[:, SSD_WIDTH + 512:] * _dsilu(ln)
        dlb_ref[...] += jnp.sum(dln, axis=0, keepdims=True)
        dlw_ref[...] += jnp.sum(dln * xh, axis=0, keepdims=True)
        dxh = dln * lw_ref[0]
        dc_ref[...] = rs * (dxh - jnp.mean(dxh, axis=-1, keepdims=True)
                            - xh * jnp.mean(dxh * xh, axis=-1, keepdims=True))

    return pl.pallas_call(
        body, name="mix_post_bwd", grid=(L // TM,),
        in_specs=[_rows(D_MIX), _rows(SSD_WIDTH), _rows(SSD_WIDTH), _rows(512),
                  _lp(SSD_WIDTH, i), _lp(512, i), _lp(512, i)],
        out_specs=[_rows(SSD_WIDTH), _rows(SSD_WIDTH), _rows(512), _full((1, SSD_WIDTH)), _full((1, 512)), _full((1, 512))],
        out_shape=[_sds((L, SSD_WIDTH)), _sds((L, SSD_WIDTH)), _sds((L, 512)), _sds((1, SSD_WIDTH)), _sds((1, 512)), _sds((1, 512))],
        compiler_params=_cparams(("arbitrary",)),
    )(dcat, y, z, c, snw, lw, lb)


def _seg_mean_matrix():
    i = lax.broadcasted_iota(jnp.int32, (LANES, LANES), 0)
    j = lax.broadcasted_iota(jnp.int32, (LANES, LANES), 1)
    return jnp.where(i // ATTN_HEAD_DIM == j // ATTN_HEAD_DIM, 1.0 / ATTN_HEAD_DIM, 0.0).astype(F32)


def _rot_matrix():
    i = lax.broadcasted_iota(jnp.int32, (LANES, LANES), 0)
    j = lax.broadcasted_iota(jnp.int32, (LANES, LANES), 1)
    half = ATTN_HEAD_DIM // 2
    lo = (j % ATTN_HEAD_DIM) < half
    return jnp.where(lo & (i == j + half), -1.0, jnp.where((~lo) & (i == j - half), 1.0, 0.0)).astype(F32)


N_QK_TILES = 5
QK_W = N_QK_TILES * LANES


def _qk_prep(qkv, qw, kw, i, cos, sin):
    L = qkv.shape[0]

    def body(x_ref, qw_ref, kw_ref, c_ref, s_ref, o_ref):
        m64 = _seg_mean_matrix()
        rot = _rot_matrix()
        cs, sn = c_ref[...], s_ref[...]
        for t in range(N_QK_TILES):
            sl = slice(t * LANES, (t + 1) * LANES)
            x = x_ref[:, sl]
            w = qw_ref[0] if t < 4 else kw_ref[0]
            xn = x * lax.rsqrt(_mmx(x * x, m64) + RMS_EPS) * w
            o_ref[:, sl] = xn * cs + _mmx(xn, rot) * sn

    return pl.pallas_call(
        body, name="qk_prep", grid=(L // TM,),
        in_specs=[_rows(QK_W), _lp(LANES, i), _lp(LANES, i), _rows(LANES), _rows(LANES)],
        out_specs=_rows(QK_W), out_shape=_sds((L, QK_W)),
        compiler_params=_cparams(("parallel",)),
    )(qkv, qw, kw, cos, sin)


def _qk_prep_bwd(dq, dk, qkv, qw, kw, i, cos, sin):
    L = qkv.shape[0]

    def body(dq_ref, dk_ref, x_ref, qw_ref, kw_ref, c_ref, s_ref, dx_ref, dqw_ref, dkw_ref):
        @pl.when(pl.program_id(0) == 0)
        def _():
            dqw_ref[...] = jnp.zeros_like(dqw_ref)
            dkw_ref[...] = jnp.zeros_like(dkw_ref)

        m64 = _seg_mean_matrix()
        rot = _rot_matrix()
        cs, sn = c_ref[...], s_ref[...]
        for t in range(N_QK_TILES):
            sl = slice(t * LANES, (t + 1) * LANES)
            x = x_ref[:, sl]
            dy = dq_ref[:, sl] if t < 4 else dk_ref[...]
            w = qw_ref[0] if t < 4 else kw_ref[0]
            dxn = dy * cs - _mmx(dy * sn, rot)
            r = lax.rsqrt(_mmx(x * x, m64) + RMS_EPS)
            xh = x * r
            dxh = dxn * w
            dx_ref[:, sl] = r * (dxh - xh * _mmx(dxh * xh, m64))
            dw = jnp.sum(dxn * xh, axis=0, keepdims=True)
            if t < 4:
                dqw_ref[...] += dw
            else:
                dkw_ref[...] += dw

    return pl.pallas_call(
        body, name="qk_prep_bwd", grid=(L // TM,),
        in_specs=[_rows(512), _rows(LANES), _rows(QK_W), _lp(LANES, i), _lp(LANES, i), _rows(LANES), _rows(LANES)],
        out_specs=[_rows(QK_W), _full((1, LANES)), _full((1, LANES))],
        out_shape=[_sds((L, QK_W)), _sds((1, LANES)), _sds((1, LANES))],
        compiler_params=_cparams(("arbitrary",)),
    )(dq, dk, qkv, qw, kw, cos, sin)


HPG = 4
SCALE = 1.0 / math.sqrt(ATTN_HEAD_DIM)


def _heads_to_rows(q, g):
    return jnp.concatenate([q[:, (HPG * g + r) * ATTN_HEAD_DIM:(HPG * g + r + 1) * ATTN_HEAD_DIM] for r in range(HPG)], axis=0)


def _rows_to_heads(parts):
    return jnp.concatenate([p[r * Q:(r + 1) * Q] for p in parts for r in range(HPG)], axis=1)


def _attn_probs(q, kc, n, s0, sink_ref, base):
    s = _mm(q, kc, _NT) * SCALE
    rows = lax.broadcasted_iota(jnp.int32, s.shape, 0)
    ki = lax.broadcasted_iota(jnp.int32, s.shape, 1)
    diff = (n * Q + rows % Q) - (s0 + ki)
    s = jnp.where((diff >= 0) & (diff < Q), s, -jnp.inf)
    hrow = lax.broadcasted_iota(jnp.int32, (HPG * Q, 1), 0) // Q
    sink = jnp.zeros((HPG * Q, 1), F32)
    for r in range(HPG):
        sink = jnp.where(hrow == r, sink_ref[base + r], sink)
    m = jnp.maximum(jnp.max(s, axis=1, keepdims=True), sink)
    p = jnp.exp(s - m)
    es = jnp.exp(sink - m)
    inv = 1.0 / (jnp.sum(p, axis=1, keepdims=True) + es)
    return p * inv, es * inv


def _attn_fwd(qk, qkv, sinks, i):
    L = qk.shape[0]

    def body(sink_ref, q_ref, k_ref, v_ref, o_ref):
        n = pl.program_id(0)
        s0 = pl.multiple_of(jnp.maximum(n - 1, 0) * Q, Q)
        q = q_ref[...]
        kc = k_ref[pl.ds(s0, 2 * Q), :]
        vc = v_ref[pl.ds(s0, 2 * Q), :]
        outs = []
        for g in range(2):
            sl = slice(g * ATTN_HEAD_DIM, (g + 1) * ATTN_HEAD_DIM)
            p, _ = _attn_probs(_heads_to_rows(q, g), kc[:, sl], n, s0, sink_ref, i * ATTN_Q_HEADS + g * HPG)
            outs.append(_mm(p, vc[:, sl]))
        o_ref[...] = _rows_to_heads(outs)

    return pl.pallas_call(
        body, name="attn_fwd", grid=(L // Q,),
        in_specs=[pl.BlockSpec(memory_space=pltpu.SMEM), _rows(512, Q),
                  pl.BlockSpec((L, LANES), lambda n: (0, 4)), pl.BlockSpec((L, LANES), lambda n: (0, 5))],
        out_specs=_rows(512, Q), out_shape=_sds((L, 512)),
        compiler_params=_cparams(("parallel",)),
    )(sinks, qk, qk, qkv)


def _attn_bwd(qk, qkv, sinks, i, dcat):
    L = qk.shape[0]

    def body(sink_ref, q_ref, k_ref, v_ref, do_ref, dq_ref, dk_ref, dv_ref, ds_ref):
        n = pl.program_id(0)

        @pl.when(n == 0)
        def _():
            dk_ref[...] = jnp.zeros_like(dk_ref)
            dv_ref[...] = jnp.zeros_like(dv_ref)
            ds_ref[...] = jnp.zeros_like(ds_ref)

        s0 = pl.multiple_of(jnp.maximum(n - 1, 0) * Q, Q)
        q = q_ref[...]
        do_all = do_ref[...]
        kc = k_ref[pl.ds(s0, 2 * Q), :]
        vc = v_ref[pl.ds(s0, 2 * Q), :]
        hrow = lax.broadcasted_iota(jnp.int32, (HPG * Q, 1), 0) // Q
        orow = lax.broadcasted_iota(jnp.int32, (8, LANES), 0)
        dqs, dks, dvs = [], [], []
        acc = jnp.zeros((8, LANES), F32)
        for g in range(2):
            sl = slice(g * ATTN_HEAD_DIM, (g + 1) * ATTN_HEAD_DIM)
            qg = _heads_to_rows(q, g)
            do = _heads_to_rows(do_all, g)
            p, ps = _attn_probs(qg, kc[:, sl], n, s0, sink_ref, i * ATTN_Q_HEADS + g * HPG)
            dp = _mm(do, vc[:, sl], _NT)
            delta = jnp.sum(p * dp, axis=1, keepdims=True)
            ds = p * (dp - delta)
            dqs.append(_mm(ds, kc[:, sl]) * SCALE)
            dks.append(_mm(ds, qg, _TN) * SCALE)
            dvs.append(_mm(p, do, _TN))
            dsink = -(ps * delta)
            for r in range(HPG):
                tot = jnp.sum(jnp.where(hrow == r, dsink, 0.0), axis=0, keepdims=True)
                acc = acc + jnp.where(orow == g * HPG + r, tot, 0.0)
        dq_ref[...] = _rows_to_heads(dqs)
        dk_ref[pl.ds(s0, 2 * Q), :] += jnp.concatenate(dks, axis=1)
        dv_ref[pl.ds(s0, 2 * Q), :] += jnp.concatenate(dvs, axis=1)
        ds_ref[...] += acc

    return pl.pallas_call(
        body, name="attn_bwd", grid=(L // Q,),
        in_specs=[pl.BlockSpec(memory_space=pltpu.SMEM), _rows(512, Q),
                  pl.BlockSpec((L, LANES), lambda n: (0, 4)), pl.BlockSpec((L, LANES), lambda n: (0, 5)),
                  _rows(512, Q, 2)],
        out_specs=[_rows(512, Q), _full((L, LANES)), _full((L, LANES)), _full((8, LANES))],
        out_shape=[_sds((L, 512)), _sds((L, LANES)), _sds((L, LANES)), _sds((8, LANES))],
        compiler_params=_cparams(("arbitrary",)),
    )(sinks, qk, qk, qkv, dcat)


N_PAIR = SSD_HEADS // 2
P = 64
OFF_B = SSD_WIDTH
OFF_C = SSD_WIDTH + SSD_GROUPS * SSD_STATE


def _expand_matrix():
    i = lax.broadcasted_iota(jnp.int32, (LANES, SSD_WIDTH), 0)
    j = lax.broadcasted_iota(jnp.int32, (LANES, SSD_WIDTH), 1)
    return jnp.where(j // P == i, 1.0, 0.0).astype(F32)


def _ssd_chunk_common(dtr_ref, bias_ref, alog_ref):
    dt = jax.nn.softplus(dtr_ref[...] + bias_ref[0])
    a = -jnp.exp(alog_ref[0])
    adt = dt * a
    ri = lax.broadcasted_iota(jnp.int32, (Q, Q), 0)
    ci = lax.broadcasted_iota(jnp.int32, (Q, Q), 1)
    causal = ri >= ci
    tri = jnp.where(causal, 1.0, 0.0).astype(F32)
    acs = _mmx(tri, adt)
    em = _expand_matrix()
    acs_e = _mmx(acs, em)
    dt_e = _mmx(dt, em)
    alast_e = acs_e[Q - 1:Q, :]
    return dt, a, acs, causal, tri, em, acs_e, dt_e, alast_e


def _ssd_fwd(xbc, dtr, bias, alog, d_e, i):
    L = xbc.shape[0]
    nc = L // Q

    def body(xbc_ref, dtr_ref, bias_ref, alog_ref, de_ref, y_ref, hp_ref, st_ref):
        @pl.when(pl.program_id(0) == 0)
        def _():
            st_ref[...] = jnp.zeros_like(st_ref)

        dt, a, acs, causal, tri, em, acs_e, dt_e, alast_e = _ssd_chunk_common(dtr_ref, bias_ref, alog_ref)
        acs_t = acs.T
        x = xbc_ref[:, :SSD_WIDTH]
        xdt = x * dt_e
        ea_e = jnp.exp(acs_e)
        xds = xdt * jnp.exp(alast_e - acs_e)
        cd_e = jnp.exp(alast_e)
        lane = lax.broadcasted_iota(jnp.int32, (Q, LANES), 1)
        lo = lane < P
        for g in range(SSD_GROUPS):
            bg = xbc_ref[:, OFF_B + g * SSD_STATE:OFF_B + (g + 1) * SSD_STATE]
            cg = xbc_ref[:, OFF_C + g * SSD_STATE:OFF_C + (g + 1) * SSD_STATE]
            cb = _mm(cg, bg, _NT)
            for pp in range(N_PAIR // SSD_GROUPS):
                pr = g * (N_PAIR // SSD_GROUPS) + pp
                sl = slice(pr * LANES, (pr + 1) * LANES)
                xdt_p = xdt[:, sl]
                yd = jnp.zeros((Q, LANES), F32)
                for half in range(2):
                    h = 2 * pr + half
                    rowb = jnp.broadcast_to(acs_t[h:h + 1, :], (Q, Q))
                    lm = jnp.exp(jnp.where(causal, rowb.T - rowb, -jnp.inf))
                    xh = jnp.where(lo if half == 0 else ~lo, xdt_p, 0.0)
                    yd = yd + _mm(cb * lm, xh)
                hp = st_ref[pr]
                hp_ref[0, pr] = hp
                yoff = _mm(cg, hp) * ea_e[:, sl]
                y_ref[:, sl] = yd + yoff + x[:, sl] * de_ref[0, :, sl]
                st_ref[pr] = hp * cd_e[:, sl] + _mm(bg, xds[:, sl], _TN)

    return pl.pallas_call(
        body, name="ssd_fwd", grid=(nc,),
        in_specs=[_rows(SSD_XBC, Q), _rows(LANES, Q), _lp(LANES, i), _lp(LANES, i), _lp(SSD_WIDTH, i)],
        out_specs=[_rows(SSD_WIDTH, Q), pl.BlockSpec((1, N_PAIR, SSD_STATE, LANES), lambda c: (c, 0, 0, 0))],
        out_shape=[_sds((L, SSD_WIDTH)), _sds((nc, N_PAIR, SSD_STATE, LANES))],
        scratch_shapes=[pltpu.VMEM((N_PAIR, SSD_STATE, LANES), F32)],
        compiler_params=_cparams(("arbitrary",)),
    )(xbc, dtr, bias, alog, d_e)


def _ssd_bwd(xbc, dtr, bias, alog, d_e, i, hprev, dy):
    L = xbc.shape[0]
    nc = L // Q
    rev = lambda c: (nc - 1 - c, 0)

    def body(xbc_ref, dtr_ref, bias_ref, alog_ref, de_ref, hp_ref, dy_ref,
             dxbc_ref, ddtr_ref, dbias_ref, dalog_ref, dd_ref, dst_ref):
        @pl.when(pl.program_id(0) == 0)
        def _():
            dst_ref[...] = jnp.zeros_like(dst_ref)
            dbias_ref[...] = jnp.zeros_like(dbias_ref)
            dalog_ref[...] = jnp.zeros_like(dalog_ref)
            dd_ref[...] = jnp.zeros_like(dd_ref)

        dt, a, acs, causal, tri, em, acs_e, dt_e, alast_e = _ssd_chunk_common(dtr_ref, bias_ref, alog_ref)
        acs_t = acs.T
        x = xbc_ref[:, :SSD_WIDTH]
        dy = dy_ref[...]
        xdt = x * dt_e
        ea_e = jnp.exp(acs_e)
        dse = jnp.exp(alast_e - acs_e)
        xds = xdt * dse
        cd_e = jnp.exp(alast_e)
        lane = lax.broadcasted_iota(jnp.int32, (Q, LANES), 1)
        lo = lane < P
        sub = lax.broadcasted_iota(jnp.int32, (Q, Q), 0)
        lan = lax.broadcasted_iota(jnp.int32, (Q, Q), 1)

        da_rows = jnp.zeros((Q, Q), F32)
        da_cols_t = jnp.zeros((Q, Q), F32)
        dxdt_parts = []
        wyoff_parts = []
        dcd_parts = []
        dxds_parts = []
        for g in range(SSD_GROUPS):
            bg = xbc_ref[:, OFF_B + g * SSD_STATE:OFF_B + (g + 1) * SSD_STATE]
            cg = xbc_ref[:, OFF_C + g * SSD_STATE:OFF_C + (g + 1) * SSD_STATE]
            cb = _mm(cg, bg, _NT)
            dcb = jnp.zeros((Q, Q), F32)
            dcg = jnp.zeros((Q, SSD_STATE), F32)
            dbg = jnp.zeros((Q, SSD_STATE), F32)
            for pp in range(N_PAIR // SSD_GROUPS):
                pr = g * (N_PAIR // SSD_GROUPS) + pp
                sl = slice(pr * LANES, (pr + 1) * LANES)
                xdt_p = xdt[:, sl]
                dy_p = dy[:, sl]
                dxdt_p = jnp.zeros((Q, LANES), F32)
                for half in range(2):
                    h = 2 * pr + half
                    hm = lo if half == 0 else ~lo
                    rowb = jnp.broadcast_to(acs_t[h:h + 1, :], (Q, Q))
                    lm = jnp.exp(jnp.where(causal, rowb.T - rowb, -jnp.inf))
                    m = cb * lm
                    dyh = jnp.where(hm, dy_p, 0.0)
                    gmat = _mm(dyh, xdt_p, _NT)
                    w = gmat * m
                    da_rows = da_rows + jnp.where(lan == h, jnp.sum(w, axis=1, keepdims=True), 0.0)
                    da_cols_t = da_cols_t + jnp.where(sub == h, jnp.sum(w, axis=0, keepdims=True), 0.0)
                    dcb = dcb + gmat * lm
                    dxdt_p = dxdt_p + _mm(m, dyh, _TN)
                hp = hp_ref[0, pr]
                dt_off = dy_p * ea_e[:, sl]
                t_off = _mm(cg, hp)
                wyoff_parts.append(dt_off * t_off)
                dcg = dcg + _mm(dt_off, hp, _NT)
                dhp = _mm(cg, dt_off, _TN)
                dS = dst_ref[pr]
                dxds_p = _mm(bg, dS)
                dbg = dbg + _mm(xds[:, sl], dS, _NT)
                dxds_parts.append(dxds_p)
                dxdt_parts.append(dxdt_p + dxds_p * dse[:, sl])
                dcd_parts.append(jnp.sum(dS * hp, axis=0, keepdims=True))
                dst_ref[pr] = dS * cd_e[:, sl] + dhp
            dcg = dcg + _mm(dcb, bg)
            dbg = dbg + _mm(dcb, cg, _TN)
            dxbc_ref[:, OFF_C + g * SSD_STATE:OFF_C + (g + 1) * SSD_STATE] = dcg
            dxbc_ref[:, OFF_B + g * SSD_STATE:OFF_B + (g + 1) * SSD_STATE] = dbg
        dxdt = jnp.concatenate(dxdt_parts, axis=1)
        dxds = jnp.concatenate(dxds_parts, axis=1)
        wyoff = jnp.concatenate(wyoff_parts, axis=1)
        dcd = jnp.concatenate(dcd_parts, axis=1)
        dxbc_ref[:, :SSD_WIDTH] = dy * de_ref[0] + dxdt * dt_e
        zds = dxds * xds
        dacs = _mmx(wyoff - zds, em, _NT) + da_rows - da_cols_t.T
        dalast = _mmx(jnp.broadcast_to(jnp.sum(zds, axis=0, keepdims=True) + dcd * cd_e, (8, SSD_WIDTH)), em, _NT)[0:1, :]
        dacs = dacs + jnp.where(sub == Q - 1, dalast, 0.0)
        dadt = _mmx(tri, dacs, _TN)
        ddt = dadt * a + _mmx(dxdt * x, em, _NT)
        ddtr = ddt * _sigmoid(dtr_ref[...] + bias_ref[0])
        ddtr_ref[...] = ddtr
        row0 = lax.broadcasted_iota(jnp.int32, (8, LANES), 0) == 0
        dbias_ref[...] += jnp.where(row0, jnp.sum(ddtr, axis=0, keepdims=True), 0.0)
        dalog_ref[...] += jnp.where(row0, jnp.sum(dadt * dt, axis=0, keepdims=True) * a, 0.0)
        ddx = _mmx(jnp.broadcast_to(jnp.sum(dy * x, axis=0, keepdims=True), (8, SSD_WIDTH)), em, _NT)
        dd_ref[...] += jnp.where(row0, ddx, 0.0)

    acc = _full((8, LANES))
    return pl.pallas_call(
        body, name="ssd_bwd", grid=(nc,),
        in_specs=[pl.BlockSpec((Q, SSD_XBC), rev), pl.BlockSpec((Q, LANES), rev),
                  _lp(LANES, i), _lp(LANES, i), _lp(SSD_WIDTH, i),
                  pl.BlockSpec((1, N_PAIR, SSD_STATE, LANES), lambda c: (nc - 1 - c, 0, 0, 0)), pl.BlockSpec((Q, SSD_WIDTH), rev)],
        out_specs=[pl.BlockSpec((Q, SSD_XBC), rev), pl.BlockSpec((Q, LANES), rev), acc, acc, acc],
        out_shape=[_sds((L, SSD_XBC)), _sds((L, LANES)), _sds((8, LANES)), _sds((8, LANES)), _sds((8, LANES))],
        scratch_shapes=[pltpu.VMEM((N_PAIR, SSD_STATE, LANES), F32)],
        compiler_params=_cparams(("arbitrary",)),
    )(xbc, dtr, bias, alog, d_e, hprev, dy)


def _rope_tables(L):
    inv_freq = ROPE_THETA ** (-jnp.arange(0, ATTN_HEAD_DIM, 2, dtype=F32) / ATTN_HEAD_DIM)
    ang = jnp.arange(L, dtype=F32)[:, None] * inv_freq[None, :]
    return jnp.tile(jnp.cos(ang), (1, 4)), jnp.tile(jnp.sin(ang), (1, 4))


def _stacked_params(small, conv_w, cm_w):
    row = lambda a: a[:, None, :]
    pad = lambda a: jnp.pad(a, ((0, 0), (0, LANES - a.shape[1])))[:, None, :]
    return dict(
        nw_mix=row(small["norm_mix_w"]), conv_w=conv_w, conv_b=row(small["ssd_conv_b"]),
        dt_bias=pad(small["ssd_dt_bias"]), a_log=pad(small["ssd_a_log"]),
        d_e=row(jnp.repeat(small["ssd_d"], P, axis=1)), snw=row(small["ssd_norm_w"]),
        qw=row(jnp.tile(small["q_norm_w"], (1, 2))), kw=row(jnp.tile(small["k_norm_w"], (1, 2))),
        sinks=small["attn_sinks"].reshape(-1), cm_w=cm_w, cm_b=row(small["cm_dw_b"]),
        ln_w=row(small["cm_ln_w"]), ln_b=row(small["cm_ln_b"]), nw_mlp=row(small["norm_mlp_w"]))


def _layer_fwd(x, p, wts, i, cos, sin, after):
    w_in8, w_out8, w_up8, w_down8 = wts
    w_in = _w_in_regroup(w_in8, after)
    h, z, xbc, qkv, glu, dtr = _in_proj(x, p["nw_mix"], i, w_in)
    xbc_c = _ssd_conv_fwd(xbc, p["conv_w"], p["conv_b"], i)
    y_ssd, hprev = _ssd_fwd(xbc_c, dtr, p["dt_bias"], p["a_log"], p["d_e"], i)
    qk = _qk_prep(qkv, p["qw"], p["kw"], i, cos, sin)
    attn = _attn_fwd(qk, qkv, p["sinks"], i)
    c = _cm_conv_fwd(glu, p["cm_w"], p["cm_b"], i)
    ycat = _mix_post(y_ssd, z, attn, c, p["snw"], p["ln_w"], p["ln_b"], i)
    x1 = _mm_res(ycat, w_out8, i, x, "out_proj")
    hm, up, act = _mlp_up(x1, p["nw_mlp"], i, w_up8)
    x2 = _mm_res(act, w_down8, i, x1, "mlp_down")
    saved = dict(x=x, h=h, z=z, xbc=xbc, qkv=qkv, glu=glu, dtr=dtr, xbc_c=xbc_c, y_ssd=y_ssd, hprev=hprev,
                 qk=qk, c=c, ycat=ycat, x1=x1, hm=hm, up=up, act=act, w_in=w_in)
    return x2, saved


def _layer_bwd(dx2, p, wts, i, s, cos, sin, after):
    _, w_out8, w_up8, w_down8 = wts
    d_up = _mlp_down_bwd(dx2, w_down8, i, s["up"], after)
    g_down = _mm_tn(s["act"], dx2, "dw_down", 512, D_MODEL, "rows")
    g_up = _mm_tn(s["hm"], d_up, "dw_up", D_MODEL, FF_SHARD, "cols")
    dx1, g_nw_mlp = _mlp_up_bwd(d_up, w_up8, i, dx2, s["x1"], p["nw_mlp"])
    dcat = _out_proj_bwd(dx1, w_out8, i)
    g_out = _mm_tn(s["ycat"], dx1, "dw_out", 512, D_MODEL, "rows")
    dy_ssd, dz, dc, g_snw, g_lw, g_lb = _mix_post_bwd(dcat, s["y_ssd"], s["z"], s["c"], p["snw"], p["ln_w"], p["ln_b"], i)
    da, dg, g_cmw, g_cmb = _cm_conv_bwd(s["glu"], p["cm_w"], i, dc)
    dq, dk, dv, dsk = _attn_bwd(s["qk"], s["qkv"], p["sinks"], i, dcat)
    dqk_raw, g_qw, g_kw = _qk_prep_bwd(dq, dk, s["qkv"], p["qw"], p["kw"], i, cos, sin)
    dxbc_c, ddtr, g_bias, g_alog, g_d = _ssd_bwd(s["xbc_c"], s["dtr"], p["dt_bias"], p["a_log"], p["d_e"], i, s["hprev"], dy_ssd)
    dxbc, g_convw, g_convb = _ssd_conv_bwd(s["xbc"], p["conv_w"], p["conv_b"], i, dxbc_c)
    du = _du_pack([dz, dxbc, dqk_raw, dv, da, dg, ddtr])
    g_in = _g_in_split(_mm_tn(s["h"], du, "dw_in", 512, 640, "f32"))
    dx, g_nw_mix = _in_proj_bwd(du, s["w_in"], dx1, s["x"], p["nw_mix"], i)
    big = (g_in, g_out, g_up, g_down)
    half = ATTN_HEAD_DIM
    small = dict(
        norm_mix_w=g_nw_mix[0], ssd_conv_b=g_convb[0], ssd_dt_bias=g_bias[0, :SSD_HEADS], ssd_a_log=g_alog[0, :SSD_HEADS],
        ssd_d=g_d[0, :SSD_HEADS], ssd_norm_w=g_snw[0], q_norm_w=g_qw[0, :half] + g_qw[0, half:],
        k_norm_w=g_kw[0, :half] + g_kw[0, half:], attn_sinks=dsk[:, 0],
        cm_dw_b=g_cmb[0], cm_ln_w=g_lw[0], cm_ln_b=g_lb[0], norm_mlp_w=g_nw_mlp[0],
        ssd_conv_w=g_convw, cm_dw_w=g_cmw)
    return dx, big, small


MESH = pl.DeviceIdType.MESH
_ANY = pl.BlockSpec(memory_space=pl.ANY)


def _coords():
    return lax.axis_index("x"), lax.axis_index("y"), lax.axis_index("c")


def _all_gather(xs, name):
    nt = len(xs)

    def body(*refs):
        x_refs, out_refs = refs[:nt], refs[nt:2 * nt]
        send_sems, recv_sems, local_sems = refs[2 * nt:]
        x, y, c = _coords()
        me, sibling = (x, y, c), (x, y, 1 - c)
        chips = [(1 - x, y), (x, 1 - y), (1 - x, 1 - y)]

        def slot(t, px, py, pc):
            return out_refs[t].at[4 * px + 2 * py + pc]

        def copy(t, k, block, to, src=None):
            return pltpu.make_async_remote_copy(
                src_ref=slot(t, *block) if src is None else src, dst_ref=slot(t, *block),
                send_sem=send_sems.at[7 * t + k], recv_sem=recv_sems.at[7 * t + k], device_id=to, device_id_type=MESH)

        mine = [pltpu.make_async_copy(x_refs[t], slot(t, *me), local_sems.at[t]) for t in range(nt)]
        for cp in mine:
            cp.start()
        first = []
        for t in range(nt):
            first.append(copy(t, 0, me, sibling, src=x_refs[t]))
            first += [copy(t, 1 + j, me, (*chip, c), src=x_refs[t]) for j, chip in enumerate(chips)]
        for cp in first:
            cp.start()
        passed = []
        for j, chip in enumerate(chips):
            for t in range(nt):
                copy(t, 1 + j, (*chip, c), me).wait_recv()
                passed.append(copy(t, 4 + j, (*chip, c), sibling))
                passed[-1].start()
        for t in range(nt):
            copy(t, 0, sibling, me).wait_recv()
            for j, chip in enumerate(chips):
                copy(t, 4 + j, (*chip, 1 - c), me).wait_recv()
        for cp in first + passed:
            cp.wait_send()
        for cp in mine:
            cp.wait()

    return pl.pallas_call(
        body, name=name, out_shape=[_sds((N_DEV,) + a.shape, a.dtype) for a in xs],
        in_specs=[_ANY] * nt, out_specs=[_ANY] * nt,
        scratch_shapes=[pltpu.SemaphoreType.DMA((7 * nt,)), pltpu.SemaphoreType.DMA((7 * nt,)), pltpu.SemaphoreType.DMA((nt,))],
    )(*xs)


def _peer_chips(x, y):
    return [(1 - x, y), (x, 1 - y), (1 - x, 1 - y)]


def _ici_copies(src_refs, land_refs, send_sems, recv_sems, kind):
    x, y, c = _coords()
    sends, recvs = [], []
    for t, d in enumerate(land_refs):
        for j, (px, py) in enumerate(_peer_chips(x, y)):
            if kind == "gather":
                src, dst, got = d.at[4 * x + 2 * y + c], d.at[4 * x + 2 * y + c], d.at[4 * px + 2 * py + c]
            else:
                src, dst, got = src_refs[t].at[2 * px + py], d.at[2 * x + y], d.at[2 * px + py]
            sems = dict(send_sem=send_sems.at[3 * t + j], recv_sem=recv_sems.at[3 * t + j],
                        device_id=(px, py, c), device_id_type=MESH)
            sends.append(pltpu.make_async_remote_copy(src_ref=src, dst_ref=dst, **sems))
            recvs.append(pltpu.make_async_remote_copy(src_ref=src, dst_ref=got, **sems))
    return sends, recvs


_HBM = pl.BlockSpec(memory_space=pltpu.HBM)
_SEMS = pl.BlockSpec(memory_space=pltpu.SEMAPHORE)
_EFFECT = pltpu.SideEffectType.DATAFLOW_SIDE_EFFECTING


def _hbm(a):
    return pltpu.with_memory_space_constraint(a, pltpu.HBM)


def _ici_start(srcs, lands, after, name, kind):
    ns, n = len(srcs), len(lands)
    nt = ns + n

    def body(*refs):
        sends, _ = _ici_copies(refs[:ns], refs[ns:nt], refs[nt + 1], refs[nt + 2], kind)
        for cp in sends:
            cp.start()
        refs[-1][...] = jnp.zeros_like(refs[-1])

    thru = srcs + lands
    out = pl.pallas_call(
        body, name=name,
        out_shape=(pltpu.SemaphoreType.DMA((3 * n,)), pltpu.SemaphoreType.DMA((3 * n,)))
        + tuple(pltpu.HBM(a.shape, a.dtype) for a in thru) + (_sds((8, LANES)),),
        in_specs=[_HBM] * nt + [_ANY],
        out_specs=(_SEMS, _SEMS) + (_HBM,) * nt + (pl.BlockSpec(memory_space=pltpu.VMEM),),
        input_output_aliases={k: 2 + k for k in range(nt)},
        compiler_params=pltpu.CompilerParams(has_side_effects=_EFFECT),
    )(*[_hbm(a) for a in thru], after)
    return out[0], out[1], list(out[2:2 + ns]), list(out[2 + ns:2 + nt]), out[-1]


def _ici_wait(started, after, name, kind):
    send_sems, recv_sems, srcs, lands, _ = started
    ns, n = len(srcs), len(lands)
    nt = ns + n

    def body(*refs):
        sends, recvs = _ici_copies(refs[:ns], refs[ns:nt], refs[nt], refs[nt + 1], kind)
        for s, r in zip(sends, recvs):
            s.wait_send()
            r.wait_recv()

    thru = srcs + lands
    out = pl.pallas_call(
        body, name=name, out_shape=tuple(pltpu.HBM(a.shape, a.dtype) for a in thru),
        in_specs=[_HBM] * nt + [_SEMS, _SEMS, _ANY], out_specs=(_HBM,) * nt,
        input_output_aliases={k: k for k in range(nt)},
        compiler_params=pltpu.CompilerParams(has_side_effects=_EFFECT),
    )(*thru, send_sems, recv_sems, after)
    return list(out[ns:])


def _ag_d2d(lands):
    n = len(lands)

    def body(*refs):
        in_refs, out_refs = refs[:n], refs[n:2 * n]
        send_sems, recv_sems = refs[2 * n:]
        x, y, c = _coords()
        sends, recvs = [], []
        for t in range(n):
            for k, (px, py) in enumerate([(x, y)] + _peer_chips(x, y)):
                sems = dict(send_sem=send_sems.at[4 * t + k], recv_sem=recv_sems.at[4 * t + k],
                            device_id=(x, y, 1 - c), device_id_type=MESH)
                src = in_refs[t].at[4 * px + 2 * py + c]
                sends.append(pltpu.make_async_remote_copy(src_ref=src, dst_ref=out_refs[t].at[4 * px + 2 * py + c], **sems))
                recvs.append(pltpu.make_async_remote_copy(src_ref=src, dst_ref=out_refs[t].at[4 * px + 2 * py + 1 - c], **sems))
        for cp in sends:
            cp.start()
        for cp in recvs:
            cp.wait_recv()
        for cp in sends:
            cp.wait_send()

    return pl.pallas_call(
        body, name="ag_d2d", out_shape=[_sds(a.shape, a.dtype) for a in lands],
        in_specs=[_ANY] * n, out_specs=[_ANY] * n,
        input_output_aliases={k: k for k in range(n)},
        scratch_shapes=[pltpu.SemaphoreType.DMA((4 * n,)), pltpu.SemaphoreType.DMA((4 * n,))],
    )(*lands)


def _rs_sib(grads):
    nt = len(grads)

    def body(*refs):
        s_refs, ra_refs = refs[:nt], refs[nt:2 * nt]
        send_sems, recv_sems = refs[2 * nt:]
        x, y, c = _coords()
        cps = [pltpu.make_async_remote_copy(
            src_ref=s_refs[t].at[:, 1 - c], dst_ref=ra_refs[t], send_sem=send_sems.at[t], recv_sem=recv_sems.at[t],
            device_id=(x, y, 1 - c), device_id_type=MESH) for t in range(nt)]
        for cp in cps:
            cp.start()
        for cp in cps:
            cp.wait()

    return pl.pallas_call(
        body, name="rs_sibling",
        out_shape=[_sds((4,) + g.shape[2:], g.dtype) for g in grads],
        in_specs=[_ANY] * nt, out_specs=[_ANY] * nt,
        scratch_shapes=[pltpu.SemaphoreType.DMA((nt,)), pltpu.SemaphoreType.DMA((nt,))],
    )(*grads)


def _rs_add(grads, ras, core):
    nt = len(grads)

    def body(c_ref, *refs):
        s_refs, ra_refs, q_refs, rb_refs = refs[:nt], refs[nt:2 * nt], refs[2 * nt:3 * nt], refs[3 * nt:]
        for t in range(nt):
            q = (s_refs[t][0, 0].astype(F32) + ra_refs[t][0].astype(F32)).astype(q_refs[t].dtype)
            q_refs[t][0] = q
            rb_refs[t][0] = q

    own = [pl.BlockSpec((1, 1) + g.shape[2:], lambda j, c: (j, c[0], 0, 0)) for g in grads]
    blk = [pl.BlockSpec((1,) + g.shape[2:], lambda j, c: (j, 0, 0)) for g in grads]
    return pl.pallas_call(
        body, name="rs_add", out_shape=[_sds(r.shape, r.dtype) for r in ras] * 2,
        grid_spec=pltpu.PrefetchScalarGridSpec(num_scalar_prefetch=1, grid=(4,), in_specs=own + blk, out_specs=blk * 2),
        compiler_params=_cparams(("parallel",)),
    )(core, *grads, *ras)


def _adamw(w, g, m, v):
    m = ADAM_B1 * m + (1.0 - ADAM_B1) * g
    v = ADAM_B2 * v + (1.0 - ADAM_B2) * jnp.square(g)
    m_hat = m / (1.0 - ADAM_B1 ** ADAM_STEP)
    v_hat = v / (1.0 - ADAM_B2 ** ADAM_STEP)
    delta = -ADAM_LR * (m_hat / (jnp.sqrt(v_hat) + ADAM_EPS) + ADAM_WD * w)
    return delta, m, v


def _rs_final(rb, w, m, v, outs, l):
    _, R, C = w.shape
    cp = rb.shape[2]

    def body(rb_ref, w_ref, m_ref, v_ref, o0, o1, o2, o3, g_ref, d_ref, m2_ref, v2_ref):
        g = rb_ref[0].astype(F32)
        for j in range(1, 4):
            g = g + rb_ref[j].astype(F32)
        g = g[:, :C]
        g_ref[0] = g
        d_ref[0], m2_ref[0], v2_ref[0] = _adamw(w_ref[0], g, m_ref[0], v_ref[0])

    blk = pl.BlockSpec((1, TM, C), lambda r: (l, r, 0))
    return pl.pallas_call(
        body, name="rs_final_adamw", grid=(R // TM,),
        in_specs=[pl.BlockSpec((4, TM, cp), lambda r: (0, r, 0)), blk, blk, blk] + [_ANY] * 4,
        out_specs=[blk] * 4, out_shape=[_sds(w.shape)] * 4,
        input_output_aliases={4 + k: k for k in range(4)},
        compiler_params=_cparams(("parallel",)),
    )(rb, w, m, v, *outs)


def _sum8(g8):
    _, R, C = g8.shape

    def body(g_ref, o_ref):
        acc = g_ref[0]
        for d in range(1, N_DEV):
            acc = acc + g_ref[d]
        o_ref[...] = acc

    return pl.pallas_call(body, name="small_sum", out_shape=_sds((R, C)))(g8)


def _adamw_small(w, g, m, v):
    def body(w_ref, g_ref, m_ref, v_ref, d_ref, m2_ref, v2_ref):
        d_ref[...], m2_ref[...], v2_ref[...] = _adamw(w_ref[...], g_ref[...], m_ref[...], v_ref[...])

    return pl.pallas_call(body, name="small_adamw", out_shape=[_sds(w.shape)] * 3)(w, g, m, v)


REP = (("norm_mix_w", 1024), ("ssd_conv_b", 1536), ("ssd_dt_bias", 16), ("ssd_a_log", 16), ("ssd_d", 16),
       ("ssd_norm_w", 1024), ("q_norm_w", 64), ("k_norm_w", 64), ("attn_sinks", 8), ("cm_dw_b", 512),
       ("cm_ln_w", 512), ("cm_ln_b", 512), ("norm_mlp_w", 1024))
WEIGHTS = ("norm_mix_w", "w_in", "ssd_conv_w", "ssd_conv_b", "ssd_dt_bias", "ssd_a_log", "ssd_d", "ssd_norm_w",
           "q_norm_w", "k_norm_w", "attn_sinks", "cm_dw_w", "cm_dw_b", "cm_ln_w", "cm_ln_b", "w_out", "norm_mlp_w",
           "w_mlp_up", "w_mlp_down")
BIG = ("w_in", "w_out", "w_mlp_up", "w_mlp_down")
N_REP = DEPTH * sum(n for _, n in REP)
CONVW_SHARD = SSD_XBC // N_DEV
CMW_SHARD = CM_CHANNELS // N_DEV


def _to_rows(flat, rows):
    return jnp.pad(flat, (0, rows * LANES - flat.shape[0])).reshape(rows, LANES)


def kernel(x, norm_mix_w, w_in, ssd_conv_w, ssd_conv_b, ssd_dt_bias, ssd_a_log, ssd_d, ssd_norm_w, q_norm_w, k_norm_w, attn_sinks, cm_dw_w, cm_dw_b, cm_ln_w, cm_ln_b, w_out, norm_mlp_w, w_mlp_up, w_mlp_down, loss_target, m_norm_mix_w, m_w_in, m_ssd_conv_w, m_ssd_conv_b, m_ssd_dt_bias, m_ssd_a_log, m_ssd_d, m_ssd_norm_w, m_q_norm_w, m_k_norm_w, m_attn_sinks, m_cm_dw_w, m_cm_dw_b, m_cm_ln_w, m_cm_ln_b, m_w_out, m_norm_mlp_w, m_w_mlp_up, m_w_mlp_down, v_norm_mix_w, v_w_in, v_ssd_conv_w, v_ssd_conv_b, v_ssd_dt_bias, v_ssd_a_log, v_ssd_d, v_ssd_norm_w, v_q_norm_w, v_k_norm_w, v_attn_sinks, v_cm_dw_w, v_cm_dw_b, v_cm_ln_w, v_cm_ln_b, v_w_out, v_norm_mlp_w, v_w_mlp_up, v_w_mlp_down):
    w = dict(norm_mix_w=norm_mix_w, w_in=w_in, ssd_conv_w=ssd_conv_w, ssd_conv_b=ssd_conv_b, ssd_dt_bias=ssd_dt_bias, ssd_a_log=ssd_a_log, ssd_d=ssd_d, ssd_norm_w=ssd_norm_w, q_norm_w=q_norm_w, k_norm_w=k_norm_w, attn_sinks=attn_sinks, cm_dw_w=cm_dw_w, cm_dw_b=cm_dw_b, cm_ln_w=cm_ln_w, cm_ln_b=cm_ln_b, w_out=w_out, norm_mlp_w=norm_mlp_w, w_mlp_up=w_mlp_up, w_mlp_down=w_mlp_down)
    m = dict(norm_mix_w=m_norm_mix_w, w_in=m_w_in, ssd_conv_w=m_ssd_conv_w, ssd_conv_b=m_ssd_conv_b, ssd_dt_bias=m_ssd_dt_bias, ssd_a_log=m_ssd_a_log, ssd_d=m_ssd_d, ssd_norm_w=m_ssd_norm_w, q_norm_w=m_q_norm_w, k_norm_w=m_k_norm_w, attn_sinks=m_attn_sinks, cm_dw_w=m_cm_dw_w, cm_dw_b=m_cm_dw_b, cm_ln_w=m_cm_ln_w, cm_ln_b=m_cm_ln_b, w_out=m_w_out, norm_mlp_w=m_norm_mlp_w, w_mlp_up=m_w_mlp_up, w_mlp_down=m_w_mlp_down)
    v = dict(norm_mix_w=v_norm_mix_w, w_in=v_w_in, ssd_conv_w=v_ssd_conv_w, ssd_conv_b=v_ssd_conv_b, ssd_dt_bias=v_ssd_dt_bias, ssd_a_log=v_ssd_a_log, ssd_d=v_ssd_d, ssd_norm_w=v_ssd_norm_w, q_norm_w=v_q_norm_w, k_norm_w=v_k_norm_w, attn_sinks=v_attn_sinks, cm_dw_w=v_cm_dw_w, cm_dw_b=v_cm_dw_b, cm_ln_w=v_cm_ln_w, cm_ln_b=v_cm_ln_b, w_out=v_w_out, norm_mlp_w=v_norm_mlp_w, w_mlp_up=v_w_mlp_up, w_mlp_down=v_w_mlp_down)
    L = x.shape[1]
    xi, yi, ci = _coords()
    me = 4 * xi + 2 * yi + ci
    n_conv = DEPTH * SSD_CONV * CONVW_SHARD
    n_cm = DEPTH * CM_CONV * CMW_SHARD

    conv_rows = 88
    cw8, = _all_gather([_to_rows(jnp.concatenate([ssd_conv_w.reshape(-1), cm_dw_w.reshape(-1)]), conv_rows)], "ag_conv_w")
    cw8 = cw8.reshape(N_DEV, -1)
    conv_full = cw8[:, :n_conv].reshape(N_DEV, DEPTH, SSD_CONV, CONVW_SHARD).transpose(1, 2, 0, 3).reshape(DEPTH, SSD_CONV, SSD_XBC)
    cm_full = cw8[:, n_conv:n_conv + n_cm].reshape(N_DEV, DEPTH, CM_CONV, CMW_SHARD).transpose(1, 2, 0, 3).reshape(DEPTH, CM_CONV, CM_CHANNELS)
    me1 = jnp.reshape(me, (1,)).astype(jnp.int32)
    casts = [_cast_shard(w_in, me1, W_IN_SHARD_P), _cast_shard(w_out, me1), _cast_shard(w_mlp_up, me1), _cast_shard(w_mlp_down, me1)]
    shards = [[c[l] for c in casts] for l in range(DEPTH)]

    def gather_start(l, after):
        return _ici_start([], shards[l], after, "ag_ici_start", "gather")

    def gather_finish(started, after):
        return _ag_d2d(_ici_wait(started, after, "ag_ici_wait", "gather"))

    cos, sin = _rope_tables(L)
    p = _stacked_params({k: w[k] for k, _ in REP}, conv_full, cm_full)
    saved, wts = [], []
    h = x[0]
    started = gather_start(0, cw8)
    wts.append(gather_finish(started, started[4]))
    for i in range(DEPTH):
        if i + 1 < DEPTH:
            started = gather_start(i + 1, wts[i][0])
        h, s = _layer_fwd(h, p, wts[i], i, cos, sin, started[4])
        saved.append(s)
        if i + 1 < DEPTH:
            wts.append(gather_finish(started, h))
    d, loss_tile = _loss_head(h, loss_target[0])

    core = jnp.reshape(ci, (1,)).astype(jnp.int32)
    smalls = [None] * DEPTH
    big_out = {k: [lax.empty(w[k].shape, F32) for _ in range(4)] for k in BIG}

    def scatter_finish(started, after, l):
        rbs = _ici_wait(started, after, "rs_ici_wait", "scatter")
        for t, k in enumerate(BIG):
            big_out[k] = _rs_final(rbs[t], w[k], m[k], v[k], big_out[k], l)

    token, pending = loss_tile, None
    for i in reversed(range(DEPTH)):
        d, big, smalls[i] = _layer_bwd(d, p, wts[i], i, saved[i], cos, sin, token)
        if pending is not None:
            scatter_finish(pending, d, i + 1)
        grads = [g.reshape((4, 2) + g.shape[1:]) for g in big]
        qs_rbs = _rs_add(grads, _rs_sib(grads), core)
        pending = _ici_start(list(qs_rbs[:4]), list(qs_rbs[4:]), d, "rs_ici_start", "scatter")
        token = pending[4]
    scatter_finish(pending, token, 0)

    gvec = jnp.concatenate(
        [jnp.stack([smalls[i][k] for i in range(DEPTH)]).reshape(-1) for k, _ in REP]
        + [jnp.stack([smalls[i][k] for i in range(DEPTH)]).reshape(-1) for k in ("ssd_conv_w", "cm_dw_w")]
        + [loss_tile[0, :1]])
    g_rows = -(-gvec.shape[0] // (8 * LANES)) * 8
    g8, = _all_gather([_to_rows(gvec, g_rows)], "ag_small_grads")
    gsum = _sum8(g8).reshape(-1)
    o_conv = N_REP
    o_cm = o_conv + DEPTH * SSD_CONV * SSD_XBC
    o_loss = o_cm + DEPTH * CM_CONV * CM_CHANNELS
    g_conv = lax.dynamic_slice_in_dim(gsum[o_conv:o_cm].reshape(DEPTH, SSD_CONV, SSD_XBC), me * CONVW_SHARD, CONVW_SHARD, axis=2)
    g_cm = lax.dynamic_slice_in_dim(gsum[o_cm:o_loss].reshape(DEPTH, CM_CONV, CM_CHANNELS), me * CMW_SHARD, CMW_SHARD, axis=2)
    loss = gsum[o_loss]
    s_rows = -(-(N_REP + n_conv + n_cm) // (8 * LANES)) * 8

    def pack_small(t):
        return _to_rows(jnp.concatenate([t[k].reshape(-1) for k, _ in REP] + [t["ssd_conv_w"].reshape(-1), t["cm_dw_w"].reshape(-1)]), s_rows)

    g_small = _to_rows(jnp.concatenate([gsum[:N_REP], g_conv.reshape(-1), g_cm.reshape(-1)]), s_rows)
    small_out = [g_small] + list(_adamw_small(pack_small(w), g_small, pack_small(m), pack_small(v)))

    def unpack_small(t):
        flat = t.reshape(-1)
        out, off = {}, 0
        for k, n in REP:
            out[k] = flat[off:off + DEPTH * n].reshape(DEPTH, n)
            off += DEPTH * n
        out["ssd_conv_w"] = flat[off:off + n_conv].reshape(DEPTH, SSD_CONV, CONVW_SHARD)
        off += n_conv
        out["cm_dw_w"] = flat[off:off + n_cm].reshape(DEPTH, CM_CONV, CMW_SHARD)
        return out

    outs = [loss, d[None]]
    for j, small_t in enumerate(small_out):
        t = unpack_small(small_t)
        for k in BIG:
            t[k] = big_out[k][j]
        outs += [t[k] for k in WEIGHTS]
    return tuple(outs)
```

```python
import math

import jax
import jax.numpy as jnp
from jax import lax
from jax.experimental import pallas as pl
from jax.experimental.pallas import tpu as pltpu

F32 = jnp.float32
_MM = jnp.bfloat16

D_MODEL = 1024
DEPTH = 4
SSD_WIDTH = 1024
SSD_HEADS = 16
SSD_STATE = 128
SSD_GROUPS = 2
SSD_CONV = 4
SSD_XBC = 1536
Q = 128
ATTN_HEAD_DIM = 64
ATTN_Q_HEADS = 8
CM_CHANNELS = 512
CM_CONV = 31
D_FF = 4096
D_MIX = 2048
N_IN = 4368
RMS_EPS = 1e-6
LN_EPS = 1e-5
ROPE_THETA = 10000.0
ADAM_LR = 0.001
ADAM_B1 = 0.9
ADAM_B2 = 0.999
ADAM_EPS = 1e-08
ADAM_WD = 0.01
ADAM_STEP = 10

N_DEV = 8
LANES = 128
TM = 256
N_IN_P = 4480
U_Z, U_XBC, U_QKV, U_GLU, U_DT = (0, 1024), (1024, 2560), (2560, 3328), (3328, 4352), (4352, 4480)
W_IN_SHARD = N_IN // N_DEV
W_IN_SHARD_P = 640
FF_SHARD = D_FF // N_DEV
OUT_SHARD = D_MIX // N_DEV

_NN = (((1,), (0,)), ((), ()))
_NT = (((1,), (1,)), ((), ()))
_TN = (((0,), (0,)), ((), ()))
_VMEM_LIMIT = 56 * 1024 * 1024


def _mm(a, b, dims=_NN):
    return lax.dot_general(a.astype(_MM), b.astype(_MM), dims, preferred_element_type=F32)


def _mmx(a, b, dims=_NN, exact="b"):
    m, v = (b, a) if exact == "b" else (a, b)
    m = m.astype(jnp.bfloat16)
    acc = None
    for _ in range(3):
        p = v.astype(jnp.bfloat16)
        v = v - p.astype(F32)
        t = lax.dot_general(p, m, dims, preferred_element_type=F32) if exact == "b" else \
            lax.dot_general(m, p, dims, preferred_element_type=F32)
        acc = t if acc is None else acc + t
    return acc


def _sds(shape, dtype=F32):
    return jax.ShapeDtypeStruct(tuple(shape), dtype)


def _full(shape):
    nd = len(shape)
    return pl.BlockSpec(tuple(shape), lambda *_: (0,) * nd)


def _rows(cols, tm=TM, col=0):
    return pl.BlockSpec((tm, cols), lambda i: (i, col))


def _lp(n, i):
    return pl.BlockSpec((1, 1, n), lambda *_: (i, 0, 0))


def _lw(arr):
    return pl.BlockSpec(arr.shape, lambda *_: (0, 0, 0, 0))


_ANY = pl.BlockSpec(memory_space=pl.ANY)


def _cparams(sem=None):
    return pltpu.CompilerParams(dimension_semantics=sem, vmem_limit_bytes=_VMEM_LIMIT)


def _sigmoid(x):
    return 1.0 / (1.0 + jnp.exp(-x))


def _silu(x):
    return x * _sigmoid(x)


def _dsilu(x):
    s = _sigmoid(x)
    return s * (1.0 + x * (1.0 - s))


def _rms_bwd(dy, x, w, inv_n):
    r = lax.rsqrt(jnp.sum(x * x, axis=-1, keepdims=True) * inv_n + RMS_EPS)
    xh = x * r
    dxh = dy * w
    dx = r * (dxh - xh * (jnp.sum(dxh * xh, axis=-1, keepdims=True) * inv_n))
    return dx, dy * xh


def _cast_shard(w, me, cols_p=None):
    _, R, C = w.shape
    cp = C if cols_p is None else cols_p

    def body(me_ref, w_ref, *o_refs):
        v = w_ref[0]
        if cp != C:
            v = jnp.concatenate([v, jnp.zeros((R, cp - C), F32)], axis=1)
        for k in range(DEPTH):
            @pl.when(pl.program_id(0) == k)
            def _():
                o_refs[k][0, 0] = v.astype(_MM)

    return pl.pallas_call(
        body, name="cast_shard", out_shape=[_sds((N_DEV, 1, R, cp), _MM)] * DEPTH,
        grid_spec=pltpu.PrefetchScalarGridSpec(
            num_scalar_prefetch=1, grid=(DEPTH,),
            in_specs=[pl.BlockSpec((1, R, C), lambda l, me: (l, 0, 0))],
            out_specs=[pl.BlockSpec((1, 1, R, cp), lambda l, me: (me[0], 0, 0, 0))] * DEPTH),
        compiler_params=_cparams(("arbitrary",)),
    )(me, w)


def _w_in_regroup(w8, after):
    a, b = U_XBC[1], U_XBC[1] + SSD_HEADS

    def body(w_ref, after_ref, o_ref):
        w = jnp.concatenate([w_ref[j, 0][:, :W_IN_SHARD].astype(F32) for j in range(N_DEV)], axis=1)
        r = jnp.concatenate([w[:, :a], w[:, b:], w[:, a:b], jnp.zeros((TM, N_IN_P - N_IN), F32)], axis=1)
        o_ref[...] = r.astype(_MM)

    return pl.pallas_call(
        body, name="w_in_regroup", grid=(D_MODEL // TM,),
        in_specs=[pl.BlockSpec((N_DEV, 1, TM, W_IN_SHARD_P), lambda r: (0, 0, r, 0)), _ANY],
        out_specs=_rows(N_IN_P), out_shape=_sds((D_MODEL, N_IN_P), _MM),
        compiler_params=_cparams(("parallel",)),
    )(w8, after)


def _g_in_split(g):
    a = U_XBC[1]

    def body(g_ref, o_ref):
        v = g_ref[...]
        w = jnp.concatenate([v[:, :a], v[:, U_DT[0]:U_DT[0] + SSD_HEADS], v[:, a:U_DT[0]]], axis=1)
        pad = jnp.zeros((TM, W_IN_SHARD_P - W_IN_SHARD), F32)
        for j in range(N_DEV):
            o_ref[j] = jnp.concatenate([w[:, j * W_IN_SHARD:(j + 1) * W_IN_SHARD], pad], axis=1).astype(_MM)

    return pl.pallas_call(
        body, name="g_in_split", grid=(D_MODEL // TM,),
        in_specs=[_rows(N_IN_P)],
        out_specs=pl.BlockSpec((N_DEV, TM, W_IN_SHARD_P), lambda r: (0, r, 0)),
        out_shape=_sds((N_DEV, D_MODEL, W_IN_SHARD_P), _MM),
        compiler_params=_cparams(("parallel",)),
    )(g)


def _in_proj(x, nw, i, w):
    L = x.shape[0]
    splits = (U_Z, U_XBC, U_QKV, U_GLU, U_DT)

    def body(x_ref, nw_ref, w_ref, h_ref, *out_refs):
        xf = x_ref[...]
        r = lax.rsqrt(jnp.mean(xf * xf, axis=-1, keepdims=True) + RMS_EPS)
        h = (xf * r * nw_ref[0]).astype(_MM)
        h_ref[...] = h
        for ref, (a, b) in zip(out_refs, splits):
            ref[...] = lax.dot_general(h, w_ref[:, a:b], _NN, preferred_element_type=F32)

    return pl.pallas_call(
        body, name="in_proj", grid=(L // TM,),
        in_specs=[_rows(D_MODEL), _lp(D_MODEL, i), _full(w.shape)],
        out_specs=[_rows(D_MODEL)] + [_rows(b - a) for a, b in splits],
        out_shape=[_sds((L, D_MODEL), _MM)] + [_sds((L, b - a)) for a, b in splits],
        compiler_params=_cparams(("parallel",)),
    )(x, nw, w)


def _mlp_up(x, nw, i, w8):
    L = x.shape[0]

    def body(x_ref, nw_ref, w_ref, h_ref, up_ref, act_ref):
        xf = x_ref[...]
        r = lax.rsqrt(jnp.mean(xf * xf, axis=-1, keepdims=True) + RMS_EPS)
        h = (xf * r * nw_ref[0]).astype(_MM)
        h_ref[...] = h
        for j in range(N_DEV):
            sl = slice(j * FF_SHARD, (j + 1) * FF_SHARD)
            up = lax.dot_general(h, w_ref[j, 0], _NN, preferred_element_type=F32)
            up_ref[:, sl] = up
            act_ref[:, sl] = jnp.square(jnp.maximum(up, 0.0)).astype(_MM)

    return pl.pallas_call(
        body, name="mlp_up", grid=(L // TM,),
        in_specs=[_rows(D_MODEL), _lp(D_MODEL, i), _lw(w8)],
        out_specs=[_rows(D_MODEL), _rows(D_FF), _rows(D_FF)],
        out_shape=[_sds((L, D_MODEL), _MM), _sds((L, D_FF)), _sds((L, D_FF), _MM)],
        compiler_params=_cparams(("parallel",)),
    )(x, nw, w8)


def _mm_res(a, w8, i, res, name):
    L, K = a.shape
    N = w8.shape[3]

    def body(a_ref, w_ref, res_ref, o_ref):
        w = w_ref[:, 0].reshape(K, N)
        o_ref[...] = res_ref[...] + lax.dot_general(a_ref[...], w, _NN, preferred_element_type=F32)

    return pl.pallas_call(
        body, name=name, grid=(L // TM,),
        in_specs=[_rows(K), _lw(w8), _rows(N)],
        out_specs=_rows(N), out_shape=_sds((L, N)),
        compiler_params=_cparams(("parallel",)),
    )(a, w8, res)


def _out_proj_bwd(a, w8, i, after):
    L = a.shape[0]

    def body(a_ref, w_ref, after_ref, o_ref):
        w = w_ref[:, 0].reshape(D_MIX, D_MODEL)
        o_ref[...] = lax.dot_general(a_ref[...].astype(_MM), w, _NT, preferred_element_type=F32)

    return pl.pallas_call(
        body, name="out_proj_bwd", grid=(L // TM,),
        in_specs=[_rows(D_MODEL), _lw(w8), _ANY],
        out_specs=_rows(D_MIX), out_shape=_sds((L, D_MIX)),
        compiler_params=_cparams(("parallel",)),
    )(a, w8, after)


def _mlp_down_bwd(dy, w8, i, up, after):
    L = dy.shape[0]

    def body(dy_ref, w_ref, up_ref, after_ref, o_ref):
        d = dy_ref[...].astype(_MM)
        for j in range(N_DEV):
            sl = slice(j * FF_SHARD, (j + 1) * FF_SHARD)
            da = lax.dot_general(d, w_ref[j, 0], _NT, preferred_element_type=F32)
            o_ref[:, sl] = (da * (2.0 * jnp.maximum(up_ref[:, sl], 0.0))).astype(_MM)

    return pl.pallas_call(
        body, name="mlp_down_bwd", grid=(L // TM,),
        in_specs=[_rows(D_MODEL), _lw(w8), _rows(D_FF), _ANY],
        out_specs=_rows(D_FF), out_shape=_sds((L, D_FF), _MM),
        compiler_params=_cparams(("parallel",)),
    )(dy, w8, up, after)


def _rms_bwd_epilogue(dh, res_ref, x_ref, nw_ref, dx_ref, dnw_ref):
    dx, dwx = _rms_bwd(dh, x_ref[...], nw_ref[0], 1.0 / D_MODEL)
    dx_ref[...] = res_ref[...] + dx

    @pl.when(pl.program_id(0) == 0)
    def _():
        dnw_ref[...] = jnp.zeros_like(dnw_ref)

    dnw_ref[...] += jnp.sum(dwx, axis=0, keepdims=True)


def _mlp_up_bwd(d_up, w8, i, res, x, nw):
    L = d_up.shape[0]

    def body(a_ref, w_ref, res_ref, x_ref, nw_ref, dx_ref, dnw_ref):
        dh = jnp.zeros((TM, D_MODEL), F32)
        for j in range(N_DEV):
            dh = dh + lax.dot_general(a_ref[:, j * FF_SHARD:(j + 1) * FF_SHARD], w_ref[j, 0], _NT, preferred_element_type=F32)
        _rms_bwd_epilogue(dh, res_ref, x_ref, nw_ref, dx_ref, dnw_ref)

    return pl.pallas_call(
        body, name="mlp_up_bwd", grid=(L // TM,),
        in_specs=[_rows(D_FF), _lw(w8), _rows(D_MODEL), _rows(D_MODEL), _lp(D_MODEL, i)],
        out_specs=[_rows(D_MODEL), _full((1, D_MODEL))],
        out_shape=[_sds((L, D_MODEL)), _sds((1, D_MODEL))],
        compiler_params=_cparams(("arbitrary",)),
    )(d_up, w8, res, x, nw)


def _in_proj_bwd(du, w, res, x, nw, i):
    L = du.shape[0]

    def body(a_ref, w_ref, res_ref, x_ref, nw_ref, dx_ref, dnw_ref):
        dh = lax.dot_general(a_ref[...], w_ref[...], _NT, preferred_element_type=F32)
        _rms_bwd_epilogue(dh, res_ref, x_ref, nw_ref, dx_ref, dnw_ref)

    return pl.pallas_call(
        body, name="in_proj_bwd", grid=(L // TM,),
        in_specs=[_rows(N_IN_P), _full(w.shape), _rows(D_MODEL), _rows(D_MODEL), _lp(D_MODEL, i)],
        out_specs=[_rows(D_MODEL), _full((1, D_MODEL))],
        out_shape=[_sds((L, D_MODEL)), _sds((1, D_MODEL))],
        compiler_params=_cparams(("arbitrary",)),
    )(du, w, res, x, nw)


def _mm_tn(a, g, name, tk, tn, out):
    L, K = a.shape
    N = g.shape[1]

    def body(a_ref, g_ref, o_ref):
        r = lax.dot_general(a_ref[...].astype(_MM), g_ref[...].astype(_MM), _TN, preferred_element_type=F32)
        o_ref[...] = r.astype(o_ref.dtype).reshape(o_ref.shape)

    if out == "f32":
        out_spec, out_shape = pl.BlockSpec((tk, tn), lambda i, j: (i, j)), _sds((K, N))
    elif out == "rows":
        assert tn == N and tk % (K // N_DEV) == 0
        nblk = tk // (K // N_DEV)
        out_spec, out_shape = pl.BlockSpec((nblk, K // N_DEV, N), lambda i, j: (i, 0, 0)), _sds((N_DEV, K // N_DEV, N), _MM)
    else:
        assert tk == K and tn == N // N_DEV
        out_spec, out_shape = pl.BlockSpec((1, K, tn), lambda i, j: (j, 0, 0)), _sds((N_DEV, K, tn), _MM)
    return pl.pallas_call(
        body, name=name, grid=(K // tk, N // tn),
        in_specs=[pl.BlockSpec((L, tk), lambda i, j: (0, i)), pl.BlockSpec((L, tn), lambda i, j: (0, j))],
        out_specs=out_spec, out_shape=out_shape,
        compiler_params=_cparams(("parallel", "parallel")),
    )(a, g)


def _loss_head(y, t):
    L = y.shape[0]

    def body(y_ref, t_ref, dy_ref, l_ref):
        e = y_ref[...] - t_ref[...]
        dy_ref[...] = e * (1.0 / D_MODEL)

        @pl.when(pl.program_id(0) == 0)
        def _():
            l_ref[...] = jnp.zeros_like(l_ref)

        l_ref[...] += jnp.sum(jnp.sum(e * e, axis=1, keepdims=True), axis=0, keepdims=True) * (0.5 / D_MODEL)

    return pl.pallas_call(
        body, name="loss_head", grid=(L // TM,),
        in_specs=[_rows(D_MODEL), _rows(D_MODEL)],
        out_specs=[_rows(D_MODEL), _full((8, LANES))],
        out_shape=[_sds((L, D_MODEL)), _sds((8, LANES))],
        compiler_params=_cparams(("arbitrary",)),
    )(y, t)


def _du_pack(pieces):
    L = pieces[0].shape[0]

    def body(*refs):
        o_ref = refs[-1]
        off = 0
        for r in refs[:-1]:
            n = r.shape[1]
            o_ref[:, off:off + n] = r[...].astype(_MM)
            off += n

    return pl.pallas_call(
        body, name="du_pack", grid=(L // TM,),
        in_specs=[_rows(p.shape[1]) for p in pieces],
        out_specs=_rows(N_IN_P), out_shape=_sds((L, N_IN_P), _MM),
        compiler_params=_cparams(("parallel",)),
    )(*pieces)


EDGE = 32


def _roll_rows(x, s):
    s = s % x.shape[0]
    return x if s == 0 else pltpu.roll(x, s, axis=0)


def _conv_taps(x, w_ref, b, k_w):
    def taps(v, zero_fill):
        r = lax.broadcasted_iota(jnp.int32, v.shape, 0)
        acc = jnp.broadcast_to(b, v.shape)
        for k in range(k_w):
            s = k_w - 1 - k
            sh = _roll_rows(v, s)
            if zero_fill and s:
                sh = jnp.where(r >= s, sh, 0.0)
            acc = acc + w_ref[0, k:k + 1, :] * sh
        return acc

    return jnp.concatenate([taps(x[:EDGE], True), taps(x, False)[EDGE:]], axis=0)


def _conv_bwd_taps(x, dc, w_ref, dw_ref, db_ref, k_w):
    n = x.shape[0]
    dc_tail, x_tail, dc_head = dc[n - EDGE:], x[n - EDGE:], dc[:EDGE]
    r = lax.broadcasted_iota(jnp.int32, dc_head.shape, 0)
    dx = jnp.zeros_like(x)
    dx_tail = jnp.zeros_like(dc_tail)
    for k in range(k_w):
        s = k_w - 1 - k
        wk = w_ref[0, k:k + 1, :]
        dx = dx + wk * _roll_rows(dc, n - s)
        up = _roll_rows(dc_tail, EDGE - s)
        dx_tail = dx_tail + wk * (jnp.where(r < EDGE - s, up, 0.0) if s else up)
        dw = jnp.sum(dc * _roll_rows(x, s), axis=0, keepdims=True)
        if s:
            dw = dw - jnp.sum(jnp.where(r < s, dc_head * _roll_rows(x_tail, s), 0.0), axis=0, keepdims=True)
        dw_ref[k:k + 1, :] = dw
    db_ref[...] = jnp.sum(dc, axis=0, keepdims=True)
    return jnp.concatenate([dx[:n - EDGE], dx_tail], axis=0)


def _cols(L, cb, off=0):
    return pl.BlockSpec((L, cb), lambda j: (0, j + off))


def _lcols(k, cb, i):
    return pl.BlockSpec((1, k, cb), lambda j: (i, 0, j))


SSD_CB = 256


def _ssd_conv_fwd(x, w, b, i):
    L, C = x.shape
    cb = SSD_CB

    def body(x_ref, w_ref, b_ref, o_ref):
        o_ref[...] = _silu(_conv_taps(x_ref[...], w_ref, b_ref[0], SSD_CONV))

    return pl.pallas_call(
        body, name="ssd_conv_fwd", grid=(C // cb,),
        in_specs=[_cols(L, cb), _lcols(SSD_CONV, cb, i), _lcols(1, cb, i)],
        out_specs=_cols(L, cb), out_shape=_sds((L, C)),
        compiler_params=_cparams(("parallel",)),
    )(x, w, b)


def _ssd_conv_bwd(x, w, b, i, dy):
    L, C = x.shape
    cb = SSD_CB

    def body(x_ref, w_ref, b_ref, dy_ref, dx_ref, dw_ref, db_ref):
        x_ = x_ref[...]
        c = _conv_taps(x_, w_ref, b_ref[0], SSD_CONV)
        dc = dy_ref[...] * _dsilu(c)
        dx_ref[...] = _conv_bwd_taps(x_, dc, w_ref, dw_ref, db_ref, SSD_CONV)

    return pl.pallas_call(
        body, name="ssd_conv_bwd", grid=(C // cb,),
        in_specs=[_cols(L, cb), _lcols(SSD_CONV, cb, i), _lcols(1, cb, i), _cols(L, cb)],
        out_specs=[_cols(L, cb), _cols(SSD_CONV, cb), _cols(1, cb)],
        out_shape=[_sds((L, C)), _sds((SSD_CONV, C)), _sds((1, C))],
        compiler_params=_cparams(("parallel",)),
    )(x, w, b, dy)


def _cm_conv_fwd(glu, w, b, i):
    L = glu.shape[0]
    cb = LANES
    nb = CM_CHANNELS // cb

    def body(a_ref, g_ref, w_ref, b_ref, o_ref):
        h = a_ref[...] * _sigmoid(g_ref[...])
        o_ref[...] = _conv_taps(h, w_ref, b_ref[0], CM_CONV)

    return pl.pallas_call(
        body, name="cm_conv_fwd", grid=(nb,),
        in_specs=[_cols(L, cb), _cols(L, cb, nb), _lcols(CM_CONV, cb, i), _lcols(1, cb, i)],
        out_specs=_cols(L, cb), out_shape=_sds((L, CM_CHANNELS)),
        compiler_params=_cparams(("parallel",)),
    )(glu, glu, w, b)


def _cm_conv_bwd(glu, w, i, dc):
    L = glu.shape[0]
    cb = LANES
    nb = CM_CHANNELS // cb

    def body(a_ref, g_ref, w_ref, dc_ref, da_ref, dg_ref, dw_ref, db_ref):
        a = a_ref[...]
        sg = _sigmoid(g_ref[...])
        dh = _conv_bwd_taps(a * sg, dc_ref[...], w_ref, dw_ref, db_ref, CM_CONV)
        da_ref[...] = dh * sg
        dg_ref[...] = dh * a * sg * (1.0 - sg)

    return pl.pallas_call(
        body, name="cm_conv_bwd", grid=(nb,),
        in_specs=[_cols(L, cb), _cols(L, cb, nb), _lcols(CM_CONV, cb, i), _cols(L, cb)],
        out_specs=[_cols(L, cb), _cols(L, cb), _cols(CM_CONV, cb), _cols(1, cb)],
        out_shape=[_sds((L, CM_CHANNELS)), _sds((L, CM_CHANNELS)), _sds((CM_CONV, CM_CHANNELS)), _sds((1, CM_CHANNELS))],
        compiler_params=_cparams(("parallel",)),
    )(glu, glu, w, dc)


GRP = SSD_WIDTH // SSD_GROUPS


def _mix_post(y, z, attn, c, snw, lw, lb, i):
    L = y.shape[0]

    def body(y_ref, z_ref, a_ref, c_ref, snw_ref, lw_ref, lb_ref, o_ref):
        g = y_ref[...] * _silu(z_ref[...])
        for k in range(SSD_GROUPS):
            sl = slice(k * GRP, (k + 1) * GRP)
            gg = g[:, sl]
            r = lax.rsqrt(jnp.mean(gg * gg, axis=-1, keepdims=True) + RMS_EPS)
            o_ref[:, sl] = (gg * r * snw_ref[0, :, sl]).astype(_MM)
        o_ref[:, SSD_WIDTH:SSD_WIDTH + 512] = a_ref[...].astype(_MM)
        cv = c_ref[...]
        mu = jnp.mean(cv, axis=-1, keepdims=True)
        xc = cv - mu
        rs = lax.rsqrt(jnp.mean(xc * xc, axis=-1, keepdims=True) + LN_EPS)
        o_ref[:, SSD_WIDTH + 512:] = _silu(xc * rs * lw_ref[0] + lb_ref[0]).astype(_MM)

    return pl.pallas_call(
        body, name="mix_post", grid=(L // TM,),
        in_specs=[_rows(SSD_WIDTH), _rows(SSD_WIDTH), _rows(512), _rows(512),
                  _lp(SSD_WIDTH, i), _lp(512, i), _lp(512, i)],
        out_specs=_rows(D_MIX), out_shape=_sds((L, D_MIX), _MM),
        compiler_params=_cparams(("parallel",)),
    )(y, z, attn, c, snw, lw, lb)


def _mix_post_bwd(dcat, y, z, c, snw, lw, lb, i):
    L = y.shape[0]

    def body(d_ref, y_ref, z_ref, c_ref, snw_ref, lw_ref, lb_ref, dy_ref, dz_ref, dc_ref, dsnw_ref, dlw_ref, dlb_ref):
        @pl.when(pl.program_id(0) == 0)
        def _():
            dsnw_ref[...] = jnp.zeros_like(dsnw_ref)
            dlw_ref[...] = jnp.zeros_like(dlw_ref)
            dlb_ref[...] = jnp.zeros_like(dlb_ref)

        yv = y_ref[...]
        zv = z_ref[...]
        sz = _silu(zv)
        g = yv * sz
        for k in range(SSD_GROUPS):
            sl = slice(k * GRP, (k + 1) * GRP)
            dgg, dwx = _rms_bwd(d_ref[:, sl], g[:, sl], snw_ref[0, :, sl], 1.0 / GRP)
            dsnw_ref[:, sl] += jnp.sum(dwx, axis=0, keepdims=True)
            dy_ref[:, sl] = dgg * sz[:, sl]
            dz_ref[:, sl] = dgg * yv[:, sl] * _dsilu(zv[:, sl])
        cv = c_ref[...]
        mu = jnp.mean(cv, axis=-1, keepdims=True)
        xc = cv - mu
        rs = lax.rsqrt(jnp.mean(xc * xc, axis=-1, keepdims=True) + LN_EPS)
        xh = xc * rs
        ln = xh * lw_ref[0] + lb_ref[0]
        dln = d_ref[:, SSD_WIDTH + 512:] * _dsilu(ln)
        dlb_ref[...] += jnp.sum(dln, axis=0, keepdims=True)
        dlw_ref[...] += jnp.sum(dln * xh, axis=0, keepdims=True)
        dxh = dln * lw_ref[0]
        dc_ref[...] = rs * (dxh - jnp.mean(dxh, axis=-1, keepdims=True)
                            - xh * jnp.mean(dxh * xh, axis=-1, keepdims=True))

    return pl.pallas_call(
        body, name="mix_post_bwd", grid=(L // TM,),
        in_specs=[_rows(D_MIX), _rows(SSD_WIDTH), _rows(SSD_WIDTH), _rows(512),
                  _lp(SSD_WIDTH, i), _lp(512, i), _lp(512, i)],
        out_specs=[_rows(SSD_WIDTH), _rows(SSD_WIDTH), _rows(512), _full((1, SSD_WIDTH)), _full((1, 512)), _full((1, 512))],
        out_shape=[_sds((L, SSD_WIDTH)), _sds((L, SSD_WIDTH)), _sds((L, 512)), _sds((1, SSD_WIDTH)), _sds((1, 512)), _sds((1, 512))],
        compiler_params=_cparams(("arbitrary",)),
    )(dcat, y, z, c, snw, lw, lb)


def _seg_mean_matrix():
    i = lax.broadcasted_iota(jnp.int32, (LANES, LANES), 0)
    j = lax.broadcasted_iota(jnp.int32, (LANES, LANES), 1)
    return jnp.where(i // ATTN_HEAD_DIM == j // ATTN_HEAD_DIM, 1.0 / ATTN_HEAD_DIM, 0.0).astype(F32)


def _rot_matrix():
    i = lax.broadcasted_iota(jnp.int32, (LANES, LANES), 0)
    j = lax.broadcasted_iota(jnp.int32, (LANES, LANES), 1)
    half = ATTN_HEAD_DIM // 2
    lo = (j % ATTN_HEAD_DIM) < half
    return jnp.where(lo & (i == j + half), -1.0, jnp.where((~lo) & (i == j - half), 1.0, 0.0)).astype(F32)


N_QK_TILES = 5
QK_W = N_QK_TILES * LANES


def _qk_prep(qkv, qw, kw, i, cos, sin):
    L = qkv.shape[0]

    def body(x_ref, qw_ref, kw_ref, c_ref, s_ref, o_ref):
        m64 = _seg_mean_matrix()
        rot = _rot_matrix()
        cs, sn = c_ref[...], s_ref[...]
        for t in range(N_QK_TILES):
            sl = slice(t * LANES, (t + 1) * LANES)
            x = x_ref[:, sl]
            w = qw_ref[0] if t < 4 else kw_ref[0]
            xn = x * lax.rsqrt(_mmx(x * x, m64) + RMS_EPS) * w
            o_ref[:, sl] = xn * cs + _mmx(xn, rot) * sn

    return pl.pallas_call(
        body, name="qk_prep", grid=(L // TM,),
        in_specs=[_rows(QK_W), _lp(LANES, i), _lp(LANES, i), _rows(LANES), _rows(LANES)],
        out_specs=_rows(QK_W), out_shape=_sds((L, QK_W)),
        compiler_params=_cparams(("parallel",)),
    )(qkv, qw, kw, cos, sin)


def _qk_prep_bwd(dq, dk, qkv, qw, kw, i, cos, sin):
    L = qkv.shape[0]

    def body(dq_ref, dk_ref, x_ref, qw_ref, kw_ref, c_ref, s_ref, dx_ref, dqw_ref, dkw_ref):
        @pl.when(pl.program_id(0) == 0)
        def _():
            dqw_ref[...] = jnp.zeros_like(dqw_ref)
            dkw_ref[...] = jnp.zeros_like(dkw_ref)

        m64 = _seg_mean_matrix()
        rot = _rot_matrix()
        cs, sn = c_ref[...], s_ref[...]
        for t in range(N_QK_TILES):
            sl = slice(t * LANES, (t + 1) * LANES)
            x = x_ref[:, sl]
            dy = dq_ref[:, sl] if t < 4 else dk_ref[...]
            w = qw_ref[0] if t < 4 else kw_ref[0]
            dxn = dy * cs - _mmx(dy * sn, rot)
            r = lax.rsqrt(_mmx(x * x, m64) + RMS_EPS)
            xh = x * r
            dxh = dxn * w
            dx_ref[:, sl] = r * (dxh - xh * _mmx(dxh * xh, m64))
            dw = jnp.sum(dxn * xh, axis=0, keepdims=True)
            if t < 4:
                dqw_ref[...] += dw
            else:
                dkw_ref[...] += dw

    return pl.pallas_call(
        body, name="qk_prep_bwd", grid=(L // TM,),
        in_specs=[_rows(512), _rows(LANES), _rows(QK_W), _lp(LANES, i), _lp(LANES, i), _rows(LANES), _rows(LANES)],
        out_specs=[_rows(QK_W), _full((1, LANES)), _full((1, LANES))],
        out_shape=[_sds((L, QK_W)), _sds((1, LANES)), _sds((1, LANES))],
        compiler_params=_cparams(("arbitrary",)),
    )(dq, dk, qkv, qw, kw, cos, sin)


HPG = 4
SCALE = 1.0 / math.sqrt(ATTN_HEAD_DIM)


def _heads_to_rows(q, g):
    return jnp.concatenate([q[:, (HPG * g + r) * ATTN_HEAD_DIM:(HPG * g + r + 1) * ATTN_HEAD_DIM] for r in range(HPG)], axis=0)


def _rows_to_heads(parts):
    return jnp.concatenate([p[r * Q:(r + 1) * Q] for p in parts for r in range(HPG)], axis=1)


def _attn_probs(q, kc, n, s0, sink_ref, base):
    s = _mm(q, kc, _NT) * SCALE
    rows = lax.broadcasted_iota(jnp.int32, s.shape, 0)
    ki = lax.broadcasted_iota(jnp.int32, s.shape, 1)
    diff = (n * Q + rows % Q) - (s0 + ki)
    s = jnp.where((diff >= 0) & (diff < Q), s, -jnp.inf)
    hrow = lax.broadcasted_iota(jnp.int32, (HPG * Q, 1), 0) // Q
    sink = jnp.zeros((HPG * Q, 1), F32)
    for r in range(HPG):
        sink = jnp.where(hrow == r, sink_ref[base + r], sink)
    m = jnp.maximum(jnp.max(s, axis=1, keepdims=True), sink)
    p = jnp.exp(s - m)
    es = jnp.exp(sink - m)
    inv = 1.0 / (jnp.sum(p, axis=1, keepdims=True) + es)
    return p * inv, es * inv


def _attn_fwd(qk, qkv, sinks, i):
    L = qk.shape[0]

    def body(sink_ref, q_ref, k_ref, v_ref, o_ref):
        n = pl.program_id(0)
        s0 = pl.multiple_of(jnp.maximum(n - 1, 0) * Q, Q)
        q = q_ref[...]
        kc = k_ref[pl.ds(s0, 2 * Q), :]
        vc = v_ref[pl.ds(s0, 2 * Q), :]
        outs = []
        for g in range(2):
            sl = slice(g * ATTN_HEAD_DIM, (g + 1) * ATTN_HEAD_DIM)
            p, _ = _attn_probs(_heads_to_rows(q, g), kc[:, sl], n, s0, sink_ref, i * ATTN_Q_HEADS + g * HPG)
            outs.append(_mm(p, vc[:, sl]))
        o_ref[...] = _rows_to_heads(outs)

    return pl.pallas_call(
        body, name="attn_fwd", grid=(L // Q,),
        in_specs=[pl.BlockSpec(memory_space=pltpu.SMEM), _rows(512, Q),
                  pl.BlockSpec((L, LANES), lambda n: (0, 4)), pl.BlockSpec((L, LANES), lambda n: (0, 5))],
        out_specs=_rows(512, Q), out_shape=_sds((L, 512)),
        compiler_params=_cparams(("parallel",)),
    )(sinks, qk, qk, qkv)


def _attn_bwd(qk, qkv, sinks, i, dcat):
    L = qk.shape[0]

    def body(sink_ref, q_ref, k_ref, v_ref, do_ref, dq_ref, dk_ref, dv_ref, ds_ref):
        n = pl.program_id(0)

        @pl.when(n == 0)
        def _():
            dk_ref[...] = jnp.zeros_like(dk_ref)
            dv_ref[...] = jnp.zeros_like(dv_ref)
            ds_ref[...] = jnp.zeros_like(ds_ref)

        s0 = pl.multiple_of(jnp.maximum(n - 1, 0) * Q, Q)
        q = q_ref[...]
        do_all = do_ref[...]
        kc = k_ref[pl.ds(s0, 2 * Q), :]
        vc = v_ref[pl.ds(s0, 2 * Q), :]
        hrow = lax.broadcasted_iota(jnp.int32, (HPG * Q, 1), 0) // Q
        orow = lax.broadcasted_iota(jnp.int32, (8, LANES), 0)
        dqs, dks, dvs = [], [], []
        acc = jnp.zeros((8, LANES), F32)
        for g in range(2):
            sl = slice(g * ATTN_HEAD_DIM, (g + 1) * ATTN_HEAD_DIM)
            qg = _heads_to_rows(q, g)
            do = _heads_to_rows(do_all, g)
            p, ps = _attn_probs(qg, kc[:, sl], n, s0, sink_ref, i * ATTN_Q_HEADS + g * HPG)
            dp = _mm(do, vc[:, sl], _NT)
            delta = jnp.sum(p * dp, axis=1, keepdims=True)
            ds = p * (dp - delta)
            dqs.append(_mm(ds, kc[:, sl]) * SCALE)
            dks.append(_mm(ds, qg, _TN) * SCALE)
            dvs.append(_mm(p, do, _TN))
            dsink = -(ps * delta)
            for r in range(HPG):
                tot = jnp.sum(jnp.where(hrow == r, dsink, 0.0), axis=0, keepdims=True)
                acc = acc + jnp.where(orow == g * HPG + r, tot, 0.0)
        dq_ref[...] = _rows_to_heads(dqs)
        dk_ref[pl.ds(s0, 2 * Q), :] += jnp.concatenate(dks, axis=1)
        dv_ref[pl.ds(s0, 2 * Q), :] += jnp.concatenate(dvs, axis=1)
        ds_ref[...] += acc

    return pl.pallas_call(
        body, name="attn_bwd", grid=(L // Q,),
        in_specs=[pl.BlockSpec(memory_space=pltpu.SMEM), _rows(512, Q),
                  pl.BlockSpec((L, LANES), lambda n: (0, 4)), pl.BlockSpec((L, LANES), lambda n: (0, 5)),
                  _rows(512, Q, 2)],
        out_specs=[_rows(512, Q), _full((L, LANES)), _full((L, LANES)), _full((8, LANES))],
        out_shape=[_sds((L, 512)), _sds((L, LANES)), _sds((L, LANES)), _sds((8, LANES))],
        compiler_params=_cparams(("arbitrary",)),
    )(sinks, qk, qk, qkv, dcat)


N_PAIR = SSD_HEADS // 2
P = 64
OFF_B = SSD_WIDTH
OFF_C = SSD_WIDTH + SSD_GROUPS * SSD_STATE


def _expand_matrix():
    i = lax.broadcasted_iota(jnp.int32, (LANES, SSD_WIDTH), 0)
    j = lax.broadcasted_iota(jnp.int32, (LANES, SSD_WIDTH), 1)
    return jnp.where(j // P == i, 1.0, 0.0).astype(F32)


def _ssd_chunk_common(dtr_ref, bias_ref, alog_ref):
    dt = jax.nn.softplus(dtr_ref[...] + bias_ref[0])
    a = -jnp.exp(alog_ref[0])
    adt = dt * a
    ri = lax.broadcasted_iota(jnp.int32, (Q, Q), 0)
    ci = lax.broadcasted_iota(jnp.int32, (Q, Q), 1)
    causal = ri >= ci
    tri = jnp.where(causal, 1.0, 0.0).astype(F32)
    acs = _mmx(tri, adt, exact="a")
    em = _expand_matrix()
    acs_e = _mmx(acs, em)
    dt_e = _mmx(dt, em)
    alast_e = acs_e[Q - 1:Q, :]
    return dt, a, acs, causal, tri, em, acs_e, dt_e, alast_e


def _ssd_fwd(xbc, dtr, bias, alog, d_e, i):
    L = xbc.shape[0]
    nc = L // Q

    def body(xbc_ref, dtr_ref, bias_ref, alog_ref, de_ref, y_ref, hp_ref, st_ref):
        @pl.when(pl.program_id(0) == 0)
        def _():
            st_ref[...] = jnp.zeros_like(st_ref)

        dt, a, acs, causal, tri, em, acs_e, dt_e, alast_e = _ssd_chunk_common(dtr_ref, bias_ref, alog_ref)
        acs_t = acs.T
        x = xbc_ref[:, :SSD_WIDTH]
        xdt = x * dt_e
        ea_e = jnp.exp(acs_e)
        xds = xdt * jnp.exp(alast_e - acs_e)
        cd_e = jnp.exp(alast_e)
        lane = lax.broadcasted_iota(jnp.int32, (Q, LANES), 1)
        lo = lane < P
        for g in range(SSD_GROUPS):
            bg = xbc_ref[:, OFF_B + g * SSD_STATE:OFF_B + (g + 1) * SSD_STATE]
            cg = xbc_ref[:, OFF_C + g * SSD_STATE:OFF_C + (g + 1) * SSD_STATE]
            cb = _mm(cg, bg, _NT)
            for pp in range(N_PAIR // SSD_GROUPS):
                pr = g * (N_PAIR // SSD_GROUPS) + pp
                sl = slice(pr * LANES, (pr + 1) * LANES)
                xdt_p = xdt[:, sl]
                yd = jnp.zeros((Q, LANES), F32)
                for half in range(2):
                    h = 2 * pr + half
                    rowb = jnp.broadcast_to(acs_t[h:h + 1, :], (Q, Q))
                    lm = jnp.exp(jnp.where(causal, rowb.T - rowb, -jnp.inf))
                    xh = jnp.where(lo if half == 0 else ~lo, xdt_p, 0.0)
                    yd = yd + _mm(cb * lm, xh)
                hp = st_ref[pr]
                hp_ref[0, pr] = hp
                yoff = _mm(cg, hp) * ea_e[:, sl]
                y_ref[:, sl] = yd + yoff + x[:, sl] * de_ref[0, :, sl]
                st_ref[pr] = hp * cd_e[:, sl] + _mm(bg, xds[:, sl], _TN)

    return pl.pallas_call(
        body, name="ssd_fwd", grid=(nc,),
        in_specs=[_rows(SSD_XBC, Q), _rows(LANES, Q), _lp(LANES, i), _lp(LANES, i), _lp(SSD_WIDTH, i)],
        out_specs=[_rows(SSD_WIDTH, Q), pl.BlockSpec((1, N_PAIR, SSD_STATE, LANES), lambda c: (c, 0, 0, 0))],
        out_shape=[_sds((L, SSD_WIDTH)), _sds((nc, N_PAIR, SSD_STATE, LANES))],
        scratch_shapes=[pltpu.VMEM((N_PAIR, SSD_STATE, LANES), F32)],
        compiler_params=_cparams(("arbitrary",)),
    )(xbc, dtr, bias, alog, d_e)


def _ssd_bwd(xbc, dtr, bias, alog, d_e, i, hprev, dy):
    L = xbc.shape[0]
    nc = L // Q
    rev = lambda c: (nc - 1 - c, 0)

    def body(xbc_ref, dtr_ref, bias_ref, alog_ref, de_ref, hp_ref, dy_ref,
             dxbc_ref, ddtr_ref, dbias_ref, dalog_ref, dd_ref, dst_ref):
        @pl.when(pl.program_id(0) == 0)
        def _():
            dst_ref[...] = jnp.zeros_like(dst_ref)
            dbias_ref[...] = jnp.zeros_like(dbias_ref)
            dalog_ref[...] = jnp.zeros_like(dalog_ref)
            dd_ref[...] = jnp.zeros_like(dd_ref)

        dt, a, acs, causal, tri, em, acs_e, dt_e, alast_e = _ssd_chunk_common(dtr_ref, bias_ref, alog_ref)
        acs_t = acs.T
        x = xbc_ref[:, :SSD_WIDTH]
        dy = dy_ref[...]
        xdt = x * dt_e
        ea_e = jnp.exp(acs_e)
        dse = jnp.exp(alast_e - acs_e)
        xds = xdt * dse
        cd_e = jnp.exp(alast_e)
        lane = lax.broadcasted_iota(jnp.int32, (Q, LANES), 1)
        lo = lane < P
        sub = lax.broadcasted_iota(jnp.int32, (Q, Q), 0)
        lan = lax.broadcasted_iota(jnp.int32, (Q, Q), 1)

        da_rows = jnp.zeros((Q, Q), F32)
        da_cols_t = jnp.zeros((Q, Q), F32)
        dxdt_parts = []
        wyoff_parts = []
        dcd_parts = []
        dxds_parts = []
        for g in range(SSD_GROUPS):
            bg = xbc_ref[:, OFF_B + g * SSD_STATE:OFF_B + (g + 1) * SSD_STATE]
            cg = xbc_ref[:, OFF_C + g * SSD_STATE:OFF_C + (g + 1) * SSD_STATE]
            cb = _mm(cg, bg, _NT)
            dcb = jnp.zeros((Q, Q), F32)
            dcg = jnp.zeros((Q, SSD_STATE), F32)
            dbg = jnp.zeros((Q, SSD_STATE), F32)
            for pp in range(N_PAIR // SSD_GROUPS):
                pr = g * (N_PAIR // SSD_GROUPS) + pp
                sl = slice(pr * LANES, (pr + 1) * LANES)
                xdt_p = xdt[:, sl]
                dy_p = dy[:, sl]
                dxdt_p = jnp.zeros((Q, LANES), F32)
                for half in range(2):
                    h = 2 * pr + half
                    hm = lo if half == 0 else ~lo
                    rowb = jnp.broadcast_to(acs_t[h:h + 1, :], (Q, Q))
                    lm = jnp.exp(jnp.where(causal, rowb.T - rowb, -jnp.inf))
                    m = cb * lm
                    dyh = jnp.where(hm, dy_p, 0.0)
                    gmat = _mm(dyh, xdt_p, _NT)
                    w = gmat * m
                    da_rows = da_rows + jnp.where(lan == h, jnp.sum(w, axis=1, keepdims=True), 0.0)
                    da_cols_t = da_cols_t + jnp.where(sub == h, jnp.sum(w, axis=0, keepdims=True), 0.0)
                    dcb = dcb + gmat * lm
                    dxdt_p = dxdt_p + _mm(m, dyh, _TN)
                hp = hp_ref[0, pr]
                dt_off = dy_p * ea_e[:, sl]
                t_off = _mm(cg, hp)
                wyoff_parts.append(dt_off * t_off)
                dcg = dcg + _mm(dt_off, hp, _NT)
                dhp = _mm(cg, dt_off, _TN)
                dS = dst_ref[pr]
                dxds_p = _mm(bg, dS)
                dbg = dbg + _mm(xds[:, sl], dS, _NT)
                dxds_parts.append(dxds_p)
                dxdt_parts.append(dxdt_p + dxds_p * dse[:, sl])
                dcd_parts.append(jnp.sum(dS * hp, axis=0, keepdims=True))
                dst_ref[pr] = dS * cd_e[:, sl] + dhp
            dcg = dcg + _mm(dcb, bg)
            dbg = dbg + _mm(dcb, cg, _TN)
            dxbc_ref[:, OFF_C + g * SSD_STATE:OFF_C + (g + 1) * SSD_STATE] = dcg
            dxbc_ref[:, OFF_B + g * SSD_STATE:OFF_B + (g + 1) * SSD_STATE] = dbg
        dxdt = jnp.concatenate(dxdt_parts, axis=1)
        dxds = jnp.concatenate(dxds_parts, axis=1)
        wyoff = jnp.concatenate(wyoff_parts, axis=1)
        dcd = jnp.concatenate(dcd_parts, axis=1)
        dxbc_ref[:, :SSD_WIDTH] = dy * de_ref[0] + dxdt * dt_e
        zds = dxds * xds
        dacs = _mmx(wyoff - zds, em, _NT) + da_rows - da_cols_t.T
        dalast = _mmx(jnp.broadcast_to(jnp.sum(zds, axis=0, keepdims=True) + dcd * cd_e, (8, SSD_WIDTH)), em, _NT)[0:1, :]
        dacs = dacs + jnp.where(sub == Q - 1, dalast, 0.0)
        dadt = _mmx(tri, dacs, _TN, exact="a")
        ddt = dadt * a + _mmx(dxdt * x, em, _NT)
        ddtr = ddt * _sigmoid(dtr_ref[...] + bias_ref[0])
        ddtr_ref[...] = ddtr
        row0 = lax.broadcasted_iota(jnp.int32, (8, LANES), 0) == 0
        dbias_ref[...] += jnp.where(row0, jnp.sum(ddtr, axis=0, keepdims=True), 0.0)
        dalog_ref[...] += jnp.where(row0, jnp.sum(dadt * dt, axis=0, keepdims=True) * a, 0.0)
        ddx = _mmx(jnp.broadcast_to(jnp.sum(dy * x, axis=0, keepdims=True), (8, SSD_WIDTH)), em, _NT)
        dd_ref[...] += jnp.where(row0, ddx, 0.0)

    acc = _full((8, LANES))
    return pl.pallas_call(
        body, name="ssd_bwd", grid=(nc,),
        in_specs=[pl.BlockSpec((Q, SSD_XBC), rev), pl.BlockSpec((Q, LANES), rev),
                  _lp(LANES, i), _lp(LANES, i), _lp(SSD_WIDTH, i),
                  pl.BlockSpec((1, N_PAIR, SSD_STATE, LANES), lambda c: (nc - 1 - c, 0, 0, 0)), pl.BlockSpec((Q, SSD_WIDTH), rev)],
        out_specs=[pl.BlockSpec((Q, SSD_XBC), rev), pl.BlockSpec((Q, LANES), rev), acc, acc, acc],
        out_shape=[_sds((L, SSD_XBC)), _sds((L, LANES)), _sds((8, LANES)), _sds((8, LANES)), _sds((8, LANES))],
        scratch_shapes=[pltpu.VMEM((N_PAIR, SSD_STATE, LANES), F32)],
        compiler_params=_cparams(("arbitrary",)),
    )(xbc, dtr, bias, alog, d_e, hprev, dy)


def _rope_tables(L):
    inv_freq = ROPE_THETA ** (-jnp.arange(0, ATTN_HEAD_DIM, 2, dtype=F32) / ATTN_HEAD_DIM)
    ang = jnp.arange(L, dtype=F32)[:, None] * inv_freq[None, :]
    return jnp.tile(jnp.cos(ang), (1, 4)), jnp.tile(jnp.sin(ang), (1, 4))


def _stacked_params(small, conv_w, cm_w):
    row = lambda a: a[:, None, :]
    pad = lambda a: jnp.pad(a, ((0, 0), (0, LANES - a.shape[1])))[:, None, :]
    return dict(
        nw_mix=row(small["norm_mix_w"]), conv_w=conv_w, conv_b=row(small["ssd_conv_b"]),
        dt_bias=pad(small["ssd_dt_bias"]), a_log=pad(small["ssd_a_log"]),
        d_e=row(jnp.repeat(small["ssd_d"], P, axis=1)), snw=row(small["ssd_norm_w"]),
        qw=row(jnp.tile(small["q_norm_w"], (1, 2))), kw=row(jnp.tile(small["k_norm_w"], (1, 2))),
        sinks=small["attn_sinks"].reshape(-1), cm_w=cm_w, cm_b=row(small["cm_dw_b"]),
        ln_w=row(small["cm_ln_w"]), ln_b=row(small["cm_ln_b"]), nw_mlp=row(small["norm_mlp_w"]))


def _layer_fwd(x, p, w_in8, late_weights, i, cos, sin, after):
    w_in = _w_in_regroup(w_in8, after)
    h, z, xbc, qkv, glu, dtr = _in_proj(x, p["nw_mix"], i, w_in)
    xbc_c = _ssd_conv_fwd(xbc, p["conv_w"], p["conv_b"], i)
    y_ssd, hprev = _ssd_fwd(xbc_c, dtr, p["dt_bias"], p["a_log"], p["d_e"], i)
    qk = _qk_prep(qkv, p["qw"], p["kw"], i, cos, sin)
    attn = _attn_fwd(qk, qkv, p["sinks"], i)
    c = _cm_conv_fwd(glu, p["cm_w"], p["cm_b"], i)
    ycat = _mix_post(y_ssd, z, attn, c, p["snw"], p["ln_w"], p["ln_b"], i)
    w_out8, w_up8, w_down8 = late_weights(ycat)
    x1 = _mm_res(ycat, w_out8, i, x, "out_proj")
    hm, up, act = _mlp_up(x1, p["nw_mlp"], i, w_up8)
    x2 = _mm_res(act, w_down8, i, x1, "mlp_down")
    saved = dict(x=x, h=h, z=z, xbc=xbc, qkv=qkv, glu=glu, dtr=dtr, xbc_c=xbc_c, y_ssd=y_ssd, hprev=hprev,
                 qk=qk, c=c, ycat=ycat, x1=x1, hm=hm, up=up, act=act, w_in=w_in,
                 w_out8=w_out8, w_up8=w_up8, w_down8=w_down8)
    return x2, saved


def _layer_bwd_mlp(dx2, p, i, s, after):
    d_up = _mlp_down_bwd(dx2, s["w_down8"], i, s["up"], after)
    g_down = _mm_tn(s["act"], dx2, "dw_down", 512, D_MODEL, "rows")
    g_up = _mm_tn(s["hm"], d_up, "dw_up", D_MODEL, FF_SHARD, "cols")
    dx1, g_nw_mlp = _mlp_up_bwd(d_up, s["w_up8"], i, dx2, s["x1"], p["nw_mlp"])
    return dx1, g_up, g_down, g_nw_mlp


def _layer_bwd_mix(dx1, g_nw_mlp, p, i, s, cos, sin, after):
    dcat = _out_proj_bwd(dx1, s["w_out8"], i, after)
    g_out = _mm_tn(s["ycat"], dx1, "dw_out", 512, D_MODEL, "rows")
    dy_ssd, dz, dc, g_snw, g_lw, g_lb = _mix_post_bwd(dcat, s["y_ssd"], s["z"], s["c"], p["snw"], p["ln_w"], p["ln_b"], i)
    da, dg, g_cmw, g_cmb = _cm_conv_bwd(s["glu"], p["cm_w"], i, dc)
    dq, dk, dv, dsk = _attn_bwd(s["qk"], s["qkv"], p["sinks"], i, dcat)
    dqk_raw, g_qw, g_kw = _qk_prep_bwd(dq, dk, s["qkv"], p["qw"], p["kw"], i, cos, sin)
    dxbc_c, ddtr, g_bias, g_alog, g_d = _ssd_bwd(s["xbc_c"], s["dtr"], p["dt_bias"], p["a_log"], p["d_e"], i, s["hprev"], dy_ssd)
    dxbc, g_convw, g_convb = _ssd_conv_bwd(s["xbc"], p["conv_w"], p["conv_b"], i, dxbc_c)
    du = _du_pack([dz, dxbc, dqk_raw, dv, da, dg, ddtr])
    g_in = _g_in_split(_mm_tn(s["h"], du, "dw_in", 512, 640, "f32"))
    dx, g_nw_mix = _in_proj_bwd(du, s["w_in"], dx1, s["x"], p["nw_mix"], i)
    half = ATTN_HEAD_DIM
    small = dict(
        norm_mix_w=g_nw_mix[0], ssd_conv_b=g_convb[0], ssd_dt_bias=g_bias[0, :SSD_HEADS], ssd_a_log=g_alog[0, :SSD_HEADS],
        ssd_d=g_d[0, :SSD_HEADS], ssd_norm_w=g_snw[0], q_norm_w=g_qw[0, :half] + g_qw[0, half:],
        k_norm_w=g_kw[0, :half] + g_kw[0, half:], attn_sinks=dsk[:, 0],
        cm_dw_b=g_cmb[0], cm_ln_w=g_lw[0], cm_ln_b=g_lb[0], norm_mlp_w=g_nw_mlp[0],
        ssd_conv_w=g_convw, cm_dw_w=g_cmw)
    return dx, g_in, g_out, small


MESH = pl.DeviceIdType.MESH
_ANY = pl.BlockSpec(memory_space=pl.ANY)


def _coords():
    return lax.axis_index("x"), lax.axis_index("y"), lax.axis_index("c")


def _all_gather(xs, name):
    nt = len(xs)

    def body(*refs):
        x_refs, out_refs = refs[:nt], refs[nt:2 * nt]
        send_sems, recv_sems, local_sems = refs[2 * nt:]
        x, y, c = _coords()
        me, sibling = (x, y, c), (x, y, 1 - c)
        chips = [(1 - x, y), (x, 1 - y), (1 - x, 1 - y)]

        def slot(t, px, py, pc):
            return out_refs[t].at[4 * px + 2 * py + pc]

        def copy(t, k, block, to, src=None):
            return pltpu.make_async_remote_copy(
                src_ref=slot(t, *block) if src is None else src, dst_ref=slot(t, *block),
                send_sem=send_sems.at[7 * t + k], recv_sem=recv_sems.at[7 * t + k], device_id=to, device_id_type=MESH)

        mine = [pltpu.make_async_copy(x_refs[t], slot(t, *me), local_sems.at[t]) for t in range(nt)]
        for cp in mine:
            cp.start()
        first = []
        for t in range(nt):
            first.append(copy(t, 0, me, sibling, src=x_refs[t]))
            first += [copy(t, 1 + j, me, (*chip, c), src=x_refs[t]) for j, chip in enumerate(chips)]
        for cp in first:
            cp.start()
        passed = []
        for j, chip in enumerate(chips):
            for t in range(nt):
                copy(t, 1 + j, (*chip, c), me).wait_recv()
                passed.append(copy(t, 4 + j, (*chip, c), sibling))
                passed[-1].start()
        for t in range(nt):
            copy(t, 0, sibling, me).wait_recv()
            for j, chip in enumerate(chips):
                copy(t, 4 + j, (*chip, 1 - c), me).wait_recv()
        for cp in first + passed:
            cp.wait_send()
        for cp in mine:
            cp.wait()

    return pl.pallas_call(
        body, name=name, out_shape=[_sds((N_DEV,) + a.shape, a.dtype) for a in xs],
        in_specs=[_ANY] * nt, out_specs=[_ANY] * nt,
        scratch_shapes=[pltpu.SemaphoreType.DMA((7 * nt,)), pltpu.SemaphoreType.DMA((7 * nt,)), pltpu.SemaphoreType.DMA((nt,))],
    )(*xs)


def _peer_chips(x, y):
    return [(1 - x, y), (x, 1 - y), (1 - x, 1 - y)]


def _ici_copies(src_refs, land_refs, send_sems, recv_sems, kind):
    x, y, c = _coords()
    sends, recvs = [], []
    for t, d in enumerate(land_refs):
        for j, (px, py) in enumerate(_peer_chips(x, y)):
            if kind == "gather":
                src, dst, got = d.at[4 * x + 2 * y + c], d.at[4 * x + 2 * y + c], d.at[4 * px + 2 * py + c]
            else:
                src, dst, got = src_refs[t].at[2 * px + py], d.at[2 * x + y], d.at[2 * px + py]
            sems = dict(send_sem=send_sems.at[3 * t + j], recv_sem=recv_sems.at[3 * t + j],
                        device_id=(px, py, c), device_id_type=MESH)
            sends.append(pltpu.make_async_remote_copy(src_ref=src, dst_ref=dst, **sems))
            recvs.append(pltpu.make_async_remote_copy(src_ref=src, dst_ref=got, **sems))
    return sends, recvs


_HBM = pl.BlockSpec(memory_space=pltpu.HBM)
_SEMS = pl.BlockSpec(memory_space=pltpu.SEMAPHORE)
_EFFECT = pltpu.SideEffectType.DATAFLOW_SIDE_EFFECTING


def _hbm(a):
    return pltpu.with_memory_space_constraint(a, pltpu.HBM)


def _ici_start(srcs, lands, after, name, kind):
    ns, n = len(srcs), len(lands)
    nt = ns + n

    def body(*refs):
        sends, _ = _ici_copies(refs[:ns], refs[ns:nt], refs[nt + 1], refs[nt + 2], kind)
        for cp in sends:
            cp.start()
        refs[-1][...] = jnp.zeros_like(refs[-1])

    thru = srcs + lands
    out = pl.pallas_call(
        body, name=name,
        out_shape=(pltpu.SemaphoreType.DMA((3 * n,)), pltpu.SemaphoreType.DMA((3 * n,)))
        + tuple(pltpu.HBM(a.shape, a.dtype) for a in thru) + (_sds((8, LANES)),),
        in_specs=[_HBM] * nt + [_ANY],
        out_specs=(_SEMS, _SEMS) + (_HBM,) * nt + (pl.BlockSpec(memory_space=pltpu.VMEM),),
        input_output_aliases={k: 2 + k for k in range(nt)},
        compiler_params=pltpu.CompilerParams(has_side_effects=_EFFECT),
    )(*[_hbm(a) for a in thru], after)
    return out[0], out[1], list(out[2:2 + ns]), list(out[2 + ns:2 + nt]), out[-1]


def _ici_wait(started, after, name, kind):
    send_sems, recv_sems, srcs, lands, _ = started
    ns, n = len(srcs), len(lands)
    nt = ns + n

    def body(*refs):
        sends, recvs = _ici_copies(refs[:ns], refs[ns:nt], refs[nt], refs[nt + 1], kind)
        for s, r in zip(sends, recvs):
            s.wait_send()
            r.wait_recv()

    thru = srcs + lands
    out = pl.pallas_call(
        body, name=name, out_shape=tuple(pltpu.HBM(a.shape, a.dtype) for a in thru),
        in_specs=[_HBM] * nt + [_SEMS, _SEMS, _ANY], out_specs=(_HBM,) * nt,
        input_output_aliases={k: k for k in range(nt)},
        compiler_params=pltpu.CompilerParams(has_side_effects=_EFFECT),
    )(*thru, send_sems, recv_sems, after)
    return list(out[ns:])


def _ag_d2d(lands):
    n = len(lands)

    def body(*refs):
        in_refs, out_refs = refs[:n], refs[n:2 * n]
        send_sems, recv_sems = refs[2 * n:]
        x, y, c = _coords()
        sends, recvs = [], []
        for t in range(n):
            for k, (px, py) in enumerate([(x, y)] + _peer_chips(x, y)):
                sems = dict(send_sem=send_sems.at[4 * t + k], recv_sem=recv_sems.at[4 * t + k],
                            device_id=(x, y, 1 - c), device_id_type=MESH)
                src = in_refs[t].at[4 * px + 2 * py + c]
                sends.append(pltpu.make_async_remote_copy(src_ref=src, dst_ref=out_refs[t].at[4 * px + 2 * py + c], **sems))
                recvs.append(pltpu.make_async_remote_copy(src_ref=src, dst_ref=out_refs[t].at[4 * px + 2 * py + 1 - c], **sems))
        for cp in sends:
            cp.start()
        for cp in recvs:
            cp.wait_recv()
        for cp in sends:
            cp.wait_send()

    return pl.pallas_call(
        body, name="ag_d2d", out_shape=[_sds(a.shape, a.dtype) for a in lands],
        in_specs=[_ANY] * n, out_specs=[_ANY] * n,
        input_output_aliases={k: k for k in range(n)},
        scratch_shapes=[pltpu.SemaphoreType.DMA((4 * n,)), pltpu.SemaphoreType.DMA((4 * n,))],
    )(*lands)


def _rs_sib(grads):
    nt = len(grads)

    def body(*refs):
        s_refs, ra_refs = refs[:nt], refs[nt:2 * nt]
        send_sems, recv_sems = refs[2 * nt:]
        x, y, c = _coords()
        cps = [pltpu.make_async_remote_copy(
            src_ref=s_refs[t].at[:, 1 - c], dst_ref=ra_refs[t], send_sem=send_sems.at[t], recv_sem=recv_sems.at[t],
            device_id=(x, y, 1 - c), device_id_type=MESH) for t in range(nt)]
        for cp in cps:
            cp.start()
        for cp in cps:
            cp.wait()

    return pl.pallas_call(
        body, name="rs_sibling",
        out_shape=[_sds((4,) + g.shape[2:], g.dtype) for g in grads],
        in_specs=[_ANY] * nt, out_specs=[_ANY] * nt,
        scratch_shapes=[pltpu.SemaphoreType.DMA((nt,)), pltpu.SemaphoreType.DMA((nt,))],
    )(*grads)


def _rs_add(grads, ras, core):
    nt = len(grads)

    def body(c_ref, *refs):
        s_refs, ra_refs, q_refs, rb_refs = refs[:nt], refs[nt:2 * nt], refs[2 * nt:3 * nt], refs[3 * nt:]
        for t in range(nt):
            q = (s_refs[t][0, 0].astype(F32) + ra_refs[t][0].astype(F32)).astype(q_refs[t].dtype)
            q_refs[t][0] = q
            rb_refs[t][0] = q

    nr = 4
    own = [pl.BlockSpec((1, 1, g.shape[2] // nr, g.shape[3]), lambda j, r, c: (j, c[0], r, 0)) for g in grads]
    blk = [pl.BlockSpec((1, g.shape[2] // nr, g.shape[3]), lambda j, r, c: (j, r, 0)) for g in grads]
    return pl.pallas_call(
        body, name="rs_add", out_shape=[_sds(r.shape, r.dtype) for r in ras] * 2,
        grid_spec=pltpu.PrefetchScalarGridSpec(num_scalar_prefetch=1, grid=(4, nr), in_specs=own + blk, out_specs=blk * 2),
        compiler_params=_cparams(("parallel", "parallel")),
    )(core, *grads, *ras)


def _adamw(w, g, m, v):
    m = ADAM_B1 * m + (1.0 - ADAM_B1) * g
    v = ADAM_B2 * v + (1.0 - ADAM_B2) * jnp.square(g)
    m_hat = m / (1.0 - ADAM_B1 ** ADAM_STEP)
    v_hat = v / (1.0 - ADAM_B2 ** ADAM_STEP)
    delta = -ADAM_LR * (m_hat / (jnp.sqrt(v_hat) + ADAM_EPS) + ADAM_WD * w)
    return delta, m, v


def _rs_final(rb, w, m, v, outs, l):
    _, R, C = w.shape
    cp = rb.shape[2]

    def body(rb_ref, w_ref, m_ref, v_ref, o0, o1, o2, o3, g_ref, d_ref, m2_ref, v2_ref):
        g = rb_ref[0].astype(F32)
        for j in range(1, 4):
            g = g + rb_ref[j].astype(F32)
        g = g[:, :C]
        g_ref[0] = g
        d_ref[0], m2_ref[0], v2_ref[0] = _adamw(w_ref[0], g, m_ref[0], v_ref[0])

    blk = pl.BlockSpec((1, TM, C), lambda r: (l, r, 0))
    return pl.pallas_call(
        body, name="rs_final_adamw", grid=(R // TM,),
        in_specs=[pl.BlockSpec((4, TM, cp), lambda r: (0, r, 0)), blk, blk, blk] + [_ANY] * 4,
        out_specs=[blk] * 4, out_shape=[_sds(w.shape)] * 4,
        input_output_aliases={4 + k: k for k in range(4)},
        compiler_params=_cparams(("parallel",)),
    )(rb, w, m, v, *outs)


def _sum8(g8):
    _, R, C = g8.shape

    def body(g_ref, o_ref):
        acc = g_ref[0]
        for d in range(1, N_DEV):
            acc = acc + g_ref[d]
        o_ref[...] = acc

    return pl.pallas_call(body, name="small_sum", out_shape=_sds((R, C)))(g8)


def _adamw_small(w, g, m, v):
    def body(w_ref, g_ref, m_ref, v_ref, d_ref, m2_ref, v2_ref):
        d_ref[...], m2_ref[...], v2_ref[...] = _adamw(w_ref[...], g_ref[...], m_ref[...], v_ref[...])

    return pl.pallas_call(body, name="small_adamw", out_shape=[_sds(w.shape)] * 3)(w, g, m, v)


REP = (("norm_mix_w", 1024), ("ssd_conv_b", 1536), ("ssd_dt_bias", 16), ("ssd_a_log", 16), ("ssd_d", 16),
       ("ssd_norm_w", 1024), ("q_norm_w", 64), ("k_norm_w", 64), ("attn_sinks", 8), ("cm_dw_b", 512),
       ("cm_ln_w", 512), ("cm_ln_b", 512), ("norm_mlp_w", 1024))
WEIGHTS = ("norm_mix_w", "w_in", "ssd_conv_w", "ssd_conv_b", "ssd_dt_bias", "ssd_a_log", "ssd_d", "ssd_norm_w",
           "q_norm_w", "k_norm_w", "attn_sinks", "cm_dw_w", "cm_dw_b", "cm_ln_w", "cm_ln_b", "w_out", "norm_mlp_w",
           "w_mlp_up", "w_mlp_down")
BIG = ("w_in", "w_out", "w_mlp_up", "w_mlp_down")
N_REP = DEPTH * sum(n for _, n in REP)
CONVW_SHARD = SSD_XBC // N_DEV
CMW_SHARD = CM_CHANNELS // N_DEV


def _to_rows(flat, rows):
    return jnp.pad(flat, (0, rows * LANES - flat.shape[0])).reshape(rows, LANES)


def kernel(x, norm_mix_w, w_in, ssd_conv_w, ssd_conv_b, ssd_dt_bias, ssd_a_log, ssd_d, ssd_norm_w, q_norm_w, k_norm_w, attn_sinks, cm_dw_w, cm_dw_b, cm_ln_w, cm_ln_b, w_out, norm_mlp_w, w_mlp_up, w_mlp_down, loss_target, m_norm_mix_w, m_w_in, m_ssd_conv_w, m_ssd_conv_b, m_ssd_dt_bias, m_ssd_a_log, m_ssd_d, m_ssd_norm_w, m_q_norm_w, m_k_norm_w, m_attn_sinks, m_cm_dw_w, m_cm_dw_b, m_cm_ln_w, m_cm_ln_b, m_w_out, m_norm_mlp_w, m_w_mlp_up, m_w_mlp_down, v_norm_mix_w, v_w_in, v_ssd_conv_w, v_ssd_conv_b, v_ssd_dt_bias, v_ssd_a_log, v_ssd_d, v_ssd_norm_w, v_q_norm_w, v_k_norm_w, v_attn_sinks, v_cm_dw_w, v_cm_dw_b, v_cm_ln_w, v_cm_ln_b, v_w_out, v_norm_mlp_w, v_w_mlp_up, v_w_mlp_down):
    w = dict(norm_mix_w=norm_mix_w, w_in=w_in, ssd_conv_w=ssd_conv_w, ssd_conv_b=ssd_conv_b, ssd_dt_bias=ssd_dt_bias, ssd_a_log=ssd_a_log, ssd_d=ssd_d, ssd_norm_w=ssd_norm_w, q_norm_w=q_norm_w, k_norm_w=k_norm_w, attn_sinks=attn_sinks, cm_dw_w=cm_dw_w, cm_dw_b=cm_dw_b, cm_ln_w=cm_ln_w, cm_ln_b=cm_ln_b, w_out=w_out, norm_mlp_w=norm_mlp_w, w_mlp_up=w_mlp_up, w_mlp_down=w_mlp_down)
    m = dict(norm_mix_w=m_norm_mix_w, w_in=m_w_in, ssd_conv_w=m_ssd_conv_w, ssd_conv_b=m_ssd_conv_b, ssd_dt_bias=m_ssd_dt_bias, ssd_a_log=m_ssd_a_log, ssd_d=m_ssd_d, ssd_norm_w=m_ssd_norm_w, q_norm_w=m_q_norm_w, k_norm_w=m_k_norm_w, attn_sinks=m_attn_sinks, cm_dw_w=m_cm_dw_w, cm_dw_b=m_cm_dw_b, cm_ln_w=m_cm_ln_w, cm_ln_b=m_cm_ln_b, w_out=m_w_out, norm_mlp_w=m_norm_mlp_w, w_mlp_up=m_w_mlp_up, w_mlp_down=m_w_mlp_down)
    v = dict(norm_mix_w=v_norm_mix_w, w_in=v_w_in, ssd_conv_w=v_ssd_conv_w, ssd_conv_b=v_ssd_conv_b, ssd_dt_bias=v_ssd_dt_bias, ssd_a_log=v_ssd_a_log, ssd_d=v_ssd_d, ssd_norm_w=v_ssd_norm_w, q_norm_w=v_q_norm_w, k_norm_w=v_k_norm_w, attn_sinks=v_attn_sinks, cm_dw_w=v_cm_dw_w, cm_dw_b=v_cm_dw_b, cm_ln_w=v_cm_ln_w, cm_ln_b=v_cm_ln_b, w_out=v_w_out, norm_mlp_w=v_norm_mlp_w, w_mlp_up=v_w_mlp_up, w_mlp_down=v_w_mlp_down)
    L = x.shape[1]
    xi, yi, ci = _coords()
    me = 4 * xi + 2 * yi + ci
    n_conv = DEPTH * SSD_CONV * CONVW_SHARD
    n_cm = DEPTH * CM_CONV * CMW_SHARD

    conv_rows = 88
    cw8, = _all_gather([_to_rows(jnp.concatenate([ssd_conv_w.reshape(-1), cm_dw_w.reshape(-1)]), conv_rows)], "ag_conv_w")
    cw8 = cw8.reshape(N_DEV, -1)
    conv_full = cw8[:, :n_conv].reshape(N_DEV, DEPTH, SSD_CONV, CONVW_SHARD).transpose(1, 2, 0, 3).reshape(DEPTH, SSD_CONV, SSD_XBC)
    cm_full = cw8[:, n_conv:n_conv + n_cm].reshape(N_DEV, DEPTH, CM_CONV, CMW_SHARD).transpose(1, 2, 0, 3).reshape(DEPTH, CM_CONV, CM_CHANNELS)
    me1 = jnp.reshape(me, (1,)).astype(jnp.int32)
    casts = [_cast_shard(w_in, me1, W_IN_SHARD_P), _cast_shard(w_out, me1), _cast_shard(w_mlp_up, me1), _cast_shard(w_mlp_down, me1)]
    shards = [[c[l] for c in casts] for l in range(DEPTH)]

    def gather_start(lands, after):
        return _ici_start([], lands, after, "ag_ici_start", "gather")

    def gather_finish(started, after):
        return _ag_d2d(_ici_wait(started, after, "ag_ici_wait", "gather"))

    cos, sin = _rope_tables(L)
    p = _stacked_params({k: w[k] for k, _ in REP}, conv_full, cm_full)
    saved = []
    h = x[0]
    first = gather_start(shards[0][:1], cw8)
    rest0 = gather_start(shards[0][1:], first[4])
    w_in8, = gather_finish(first, rest0[4])
    token, rest = rest0[4], None
    for i in range(DEPTH):
        if i == 0:
            late = lambda ycat: gather_finish(rest0, ycat)
        else:
            late = lambda ycat, r=rest: r
        nxt = None
        if i + 1 < DEPTH:
            nxt = gather_start(shards[i + 1], w_in8)
            token = nxt[4]
        h, s = _layer_fwd(h, p, w_in8, late, i, cos, sin, token)
        saved.append(s)
        if nxt is not None:
            got = gather_finish(nxt, h)
            w_in8, rest = got[0], got[1:]
    d, loss_tile = _loss_head(h, loss_target[0])

    core = jnp.reshape(ci, (1,)).astype(jnp.int32)
    smalls = [None] * DEPTH
    big_out = {k: [lax.empty(w[k].shape, F32) for _ in range(4)] for k in BIG}

    def scatter_start(grads, after):
        g4 = [g.reshape((4, 2) + g.shape[1:]) for g in grads]
        out = _rs_add(g4, _rs_sib(g4), core)
        return _ici_start(list(out[:len(g4)]), list(out[len(g4):]), after, "rs_ici_start", "scatter")

    def scatter_finish(started, after, l, names):
        for rb, k in zip(_ici_wait(started, after, "rs_ici_wait", "scatter"), names):
            big_out[k] = _rs_final(rb, w[k], m[k], v[k], big_out[k], l)

    token, pending = loss_tile, []
    for i in reversed(range(DEPTH)):
        dx1, g_up, g_down, g_nw_mlp = _layer_bwd_mlp(d, p, i, saved[i], token)
        started = []
        if i == 0:
            started.append((scatter_start([g_up, g_down], dx1), i, BIG[2:]))
        d, g_in, g_out, smalls[i] = _layer_bwd_mix(dx1, g_nw_mlp, p, i, saved[i], cos, sin,
                                                   started[0][0][4] if started else g_nw_mlp)
        started.append((scatter_start([g_in, g_out] + ([] if i == 0 else [g_up, g_down]), d), i, BIG[:2] if i == 0 else BIG))
        token = started[-1][0][4]
        for st, l, names in pending:
            scatter_finish(st, token, l, names)
        pending = started
    for st, l, names in pending:
        scatter_finish(st, token, l, names)

    gvec = jnp.concatenate(
        [jnp.stack([smalls[i][k] for i in range(DEPTH)]).reshape(-1) for k, _ in REP]
        + [jnp.stack([smalls[i][k] for i in range(DEPTH)]).reshape(-1) for k in ("ssd_conv_w", "cm_dw_w")]
        + [loss_tile[0, :1]])
    g_rows = -(-gvec.shape[0] // (8 * LANES)) * 8
    g8, = _all_gather([_to_rows(gvec, g_rows)], "ag_small_grads")
    gsum = _sum8(g8).reshape(-1)
    o_conv = N_REP
    o_cm = o_conv + DEPTH * SSD_CONV * SSD_XBC
    o_loss = o_cm + DEPTH * CM_CONV * CM_CHANNELS
    g_conv = lax.dynamic_slice_in_dim(gsum[o_conv:o_cm].reshape(DEPTH, SSD_CONV, SSD_XBC), me * CONVW_SHARD, CONVW_SHARD, axis=2)
    g_cm = lax.dynamic_slice_in_dim(gsum[o_cm:o_loss].reshape(DEPTH, CM_CONV, CM_CHANNELS), me * CMW_SHARD, CMW_SHARD, axis=2)
    loss = gsum[o_loss]
    s_rows = -(-(N_REP + n_conv + n_cm) // (8 * LANES)) * 8

    def pack_small(t):
        return _to_rows(jnp.concatenate([t[k].reshape(-1) for k, _ in REP] + [t["ssd_conv_w"].reshape(-1), t["cm_dw_w"].reshape(-1)]), s_rows)

    g_small = _to_rows(jnp.concatenate([gsum[:N_REP], g_conv.reshape(-1), g_cm.reshape(-1)]), s_rows)
    small_out = [g_small] + list(_adamw_small(pack_small(w), g_small, pack_small(m), pack_small(v)))

    def unpack_small(t):
        flat = t.reshape(-1)
        out, off = {}, 0
        for k, n in REP:
            out[k] = flat[off:off + DEPTH * n].reshape(DEPTH, n)
            off += DEPTH * n
        out["ssd_conv_w"] = flat[off:off + n_conv].reshape(DEPTH, SSD_CONV, CONVW_SHARD)
        off += n_conv
        out["cm_dw_w"] = flat[off:off + n_cm].reshape(DEPTH, CM_CONV, CMW_SHARD)
        return out

    outs = [loss, d[None]]
    for j, small_t in enumerate(small_out):
        t = unpack_small(small_t)
        for k in BIG:
            t[k] = big_out[k][j]
        outs += [t[k] for k in WEIGHTS]
    return tuple(outs)
```

```python
import math

import jax
import jax.numpy as jnp
from jax import lax
from jax.experimental import pallas as pl
from jax.experimental.pallas import tpu as pltpu

F32 = jnp.float32
_MM = jnp.bfloat16

D_MODEL = 1024
DEPTH = 4
SSD_WIDTH = 1024
SSD_HEADS = 16
SSD_STATE = 128
SSD_GROUPS = 2
SSD_CONV = 4
SSD_XBC = 1536
Q = 128
ATTN_HEAD_DIM = 64
ATTN_Q_HEADS = 8
CM_CHANNELS = 512
CM_CONV = 31
D_FF = 4096
D_MIX = 2048
N_IN = 4368
RMS_EPS = 1e-6
LN_EPS = 1e-5
ROPE_THETA = 10000.0
ADAM_LR = 0.001
ADAM_B1 = 0.9
ADAM_B2 = 0.999
ADAM_EPS = 1e-08
ADAM_WD = 0.01
ADAM_STEP = 10

N_DEV = 8
LANES = 128
TM = 256
N_IN_P = 4480
U_Z, U_XBC, U_QKV, U_GLU, U_DT = (0, 1024), (1024, 2560), (2560, 3328), (3328, 4352), (4352, 4480)
W_IN_SHARD = N_IN // N_DEV
W_IN_SHARD_P = 640
FF_SHARD = D_FF // N_DEV
OUT_SHARD = D_MIX // N_DEV

_NN = (((1,), (0,)), ((), ()))
_NT = (((1,), (1,)), ((), ()))
_TN = (((0,), (0,)), ((), ()))
_VMEM_LIMIT = 56 * 1024 * 1024


def _mm(a, b, dims=_NN):
    return lax.dot_general(a.astype(_MM), b.astype(_MM), dims, preferred_element_type=F32)


def _mmx(a, b, dims=_NN, exact="b"):
    m, v = (b, a) if exact == "b" else (a, b)
    m = m.astype(jnp.bfloat16)
    acc = None
    for _ in range(3):
        p = v.astype(jnp.bfloat16)
        v = v - p.astype(F32)
        t = lax.dot_general(p, m, dims, preferred_element_type=F32) if exact == "b" else \
            lax.dot_general(m, p, dims, preferred_element_type=F32)
        acc = t if acc is None else acc + t
    return acc


def _sds(shape, dtype=F32):
    return jax.ShapeDtypeStruct(tuple(shape), dtype)


def _full(shape):
    nd = len(shape)
    return pl.BlockSpec(tuple(shape), lambda *_: (0,) * nd)


def _rows(cols, tm=TM, col=0):
    return pl.BlockSpec((tm, cols), lambda i: (i, col))


TMM = 512


def _mrows(cols):
    return _rows(cols, TMM)


def _lp(n, i):
    return pl.BlockSpec((1, 1, n), lambda *_: (i, 0, 0))


def _lw(arr):
    return pl.BlockSpec(arr.shape, lambda *_: (0, 0, 0, 0))


_ANY = pl.BlockSpec(memory_space=pl.ANY)


def _cparams(sem=None):
    return pltpu.CompilerParams(dimension_semantics=sem, vmem_limit_bytes=_VMEM_LIMIT)


def _sigmoid(x):
    return 1.0 / (1.0 + jnp.exp(-x))


def _silu(x):
    return x * _sigmoid(x)


def _dsilu(x):
    s = _sigmoid(x)
    return s * (1.0 + x * (1.0 - s))


def _rms_bwd(dy, x, w, inv_n):
    r = lax.rsqrt(jnp.sum(x * x, axis=-1, keepdims=True) * inv_n + RMS_EPS)
    xh = x * r
    dxh = dy * w
    dx = r * (dxh - xh * (jnp.sum(dxh * xh, axis=-1, keepdims=True) * inv_n))
    return dx, dy * xh


def _cast_shard(w, me, cols_p=None):
    _, R, C = w.shape
    cp = C if cols_p is None else cols_p

    def body(me_ref, w_ref, *o_refs):
        v = w_ref[0]
        if cp != C:
            v = jnp.concatenate([v, jnp.zeros((R, cp - C), F32)], axis=1)
        for k in range(DEPTH):
            @pl.when(pl.program_id(0) == k)
            def _():
                o_refs[k][0, 0] = v.astype(_MM)

    return pl.pallas_call(
        body, name="cast_shard", out_shape=[_sds((N_DEV, 1, R, cp), _MM)] * DEPTH,
        grid_spec=pltpu.PrefetchScalarGridSpec(
            num_scalar_prefetch=1, grid=(DEPTH,),
            in_specs=[pl.BlockSpec((1, R, C), lambda l, me: (l, 0, 0))],
            out_specs=[pl.BlockSpec((1, 1, R, cp), lambda l, me: (me[0], 0, 0, 0))] * DEPTH),
        compiler_params=_cparams(("arbitrary",)),
    )(me, w)


def _cast_w_in(w_in, me):
    wt = jnp.transpose(w_in, (2, 0, 1))

    def body(me_ref, wt_ref, *rest):
        o_refs, buf, sem = rest[:DEPTH], rest[DEPTH], rest[DEPTH + 1]
        l = pl.program_id(0)
        cp = pltpu.make_async_copy(wt_ref.at[:, l, :], buf, sem)
        cp.start()
        cp.wait()
        v = jnp.concatenate([buf[...], jnp.zeros((W_IN_SHARD_P - W_IN_SHARD, D_MODEL), F32)], axis=0).T.astype(_MM)
        for k in range(DEPTH):
            @pl.when(l == k)
            def _():
                o_refs[k][0, 0] = v

    return pl.pallas_call(
        body, name="cast_w_in", out_shape=[_sds((N_DEV, 1, D_MODEL, W_IN_SHARD_P), _MM)] * DEPTH,
        grid_spec=pltpu.PrefetchScalarGridSpec(
            num_scalar_prefetch=1, grid=(DEPTH,), in_specs=[_ANY],
            out_specs=[pl.BlockSpec((1, 1, D_MODEL, W_IN_SHARD_P), lambda l, me: (me[0], 0, 0, 0))] * DEPTH,
            scratch_shapes=[pltpu.VMEM((W_IN_SHARD, D_MODEL), F32), pltpu.SemaphoreType.DMA]),
        compiler_params=_cparams(("arbitrary",)),
    )(me, wt)


def _w_in_regroup(w8, after):
    a, b = U_XBC[1], U_XBC[1] + SSD_HEADS

    def body(w_ref, after_ref, o_ref):
        w = jnp.concatenate([w_ref[j, 0][:, :W_IN_SHARD].astype(F32) for j in range(N_DEV)], axis=1)
        r = jnp.concatenate([w[:, :a], w[:, b:], w[:, a:b], jnp.zeros((TM, N_IN_P - N_IN), F32)], axis=1)
        o_ref[...] = r.astype(_MM)

    return pl.pallas_call(
        body, name="w_in_regroup", grid=(D_MODEL // TM,),
        in_specs=[pl.BlockSpec((N_DEV, 1, TM, W_IN_SHARD_P), lambda r: (0, 0, r, 0)), _ANY],
        out_specs=_rows(N_IN_P), out_shape=_sds((D_MODEL, N_IN_P), _MM),
        compiler_params=_cparams(("parallel",)),
    )(w8, after)


def _g_in_split(g):
    a = U_XBC[1]

    def body(g_ref, o_ref):
        v = g_ref[...].astype(F32)
        w = jnp.concatenate([v[:, :a], v[:, U_DT[0]:U_DT[0] + SSD_HEADS], v[:, a:U_DT[0]]], axis=1)
        pad = jnp.zeros((TM, W_IN_SHARD_P - W_IN_SHARD), F32)
        for j in range(N_DEV):
            o_ref[j] = jnp.concatenate([w[:, j * W_IN_SHARD:(j + 1) * W_IN_SHARD], pad], axis=1).astype(_MM)

    return pl.pallas_call(
        body, name="g_in_split", grid=(D_MODEL // TM,),
        in_specs=[_rows(N_IN_P)],
        out_specs=pl.BlockSpec((N_DEV, TM, W_IN_SHARD_P), lambda r: (0, r, 0)),
        out_shape=_sds((N_DEV, D_MODEL, W_IN_SHARD_P), _MM),
        compiler_params=_cparams(("parallel",)),
    )(g)


def _in_proj(x, nw, i, w):
    L = x.shape[0]
    splits = (U_Z, U_XBC, U_QKV, U_GLU, U_DT)

    def body(x_ref, nw_ref, w_ref, h_ref, *out_refs):
        xf = x_ref[...]
        r = lax.rsqrt(jnp.mean(xf * xf, axis=-1, keepdims=True) + RMS_EPS)
        h = (xf * r * nw_ref[0]).astype(_MM)
        h_ref[...] = h
        for ref, (a, b) in zip(out_refs, splits):
            ref[...] = lax.dot_general(h, w_ref[:, a:b], _NN, preferred_element_type=F32)

    return pl.pallas_call(
        body, name="in_proj", grid=(L // TMM,),
        in_specs=[_mrows(D_MODEL), _lp(D_MODEL, i), _full(w.shape)],
        out_specs=[_mrows(D_MODEL)] + [_mrows(b - a) for a, b in splits],
        out_shape=[_sds((L, D_MODEL), _MM)] + [_sds((L, b - a)) for a, b in splits],
        compiler_params=_cparams(("parallel",)),
    )(x, nw, w)


def _mlp_up(x, nw, i, w8):
    L = x.shape[0]

    def body(x_ref, nw_ref, w_ref, h_ref, up_ref, act_ref):
        xf = x_ref[...]
        r = lax.rsqrt(jnp.mean(xf * xf, axis=-1, keepdims=True) + RMS_EPS)
        h = (xf * r * nw_ref[0]).astype(_MM)
        h_ref[...] = h
        for j in range(N_DEV):
            sl = slice(j * FF_SHARD, (j + 1) * FF_SHARD)
            up = lax.dot_general(h, w_ref[j, 0], _NN, preferred_element_type=F32)
            up_ref[:, sl] = up
            act_ref[:, sl] = jnp.square(jnp.maximum(up, 0.0)).astype(_MM)

    return pl.pallas_call(
        body, name="mlp_up", grid=(L // TM,),
        in_specs=[_rows(D_MODEL), _lp(D_MODEL, i), _lw(w8)],
        out_specs=[_rows(D_MODEL), _rows(D_FF), _rows(D_FF)],
        out_shape=[_sds((L, D_MODEL), _MM), _sds((L, D_FF)), _sds((L, D_FF), _MM)],
        compiler_params=_cparams(("parallel",)),
    )(x, nw, w8)


def _mm_res(a, w8, i, res, name):
    L, K = a.shape
    N = w8.shape[3]

    def body(a_ref, w_ref, res_ref, o_ref):
        w = w_ref[:, 0].reshape(K, N)
        o_ref[...] = res_ref[...] + lax.dot_general(a_ref[...], w, _NN, preferred_element_type=F32)

    return pl.pallas_call(
        body, name=name, grid=(L // TMM,),
        in_specs=[_mrows(K), _lw(w8), _mrows(N)],
        out_specs=_mrows(N), out_shape=_sds((L, N)),
        compiler_params=_cparams(("parallel",)),
    )(a, w8, res)


def _out_proj_bwd(a, w8, i, after):
    L = a.shape[0]

    def body(a_ref, w_ref, after_ref, o_ref):
        w = w_ref[:, 0].reshape(D_MIX, D_MODEL)
        o_ref[...] = lax.dot_general(a_ref[...].astype(_MM), w, _NT, preferred_element_type=F32)

    return pl.pallas_call(
        body, name="out_proj_bwd", grid=(L // TMM,),
        in_specs=[_mrows(D_MODEL), _lw(w8), _ANY],
        out_specs=_mrows(D_MIX), out_shape=_sds((L, D_MIX)),
        compiler_params=_cparams(("parallel",)),
    )(a, w8, after)


def _mlp_down_bwd(dy, w8, i, up, after):
    L = dy.shape[0]

    def body(dy_ref, w_ref, up_ref, after_ref, o_ref):
        d = dy_ref[...].astype(_MM)
        for j in range(N_DEV):
            sl = slice(j * FF_SHARD, (j + 1) * FF_SHARD)
            da = lax.dot_general(d, w_ref[j, 0], _NT, preferred_element_type=F32)
            o_ref[:, sl] = (da * (2.0 * jnp.maximum(up_ref[:, sl], 0.0))).astype(_MM)

    return pl.pallas_call(
        body, name="mlp_down_bwd", grid=(L // TMM,),
        in_specs=[_mrows(D_MODEL), _lw(w8), _mrows(D_FF), _ANY],
        out_specs=_mrows(D_FF), out_shape=_sds((L, D_FF), _MM),
        compiler_params=_cparams(("parallel",)),
    )(dy, w8, up, after)


def _rms_bwd_epilogue(dh, res_ref, x_ref, nw_ref, dx_ref, dnw_ref):
    dx, dwx = _rms_bwd(dh, x_ref[...], nw_ref[0], 1.0 / D_MODEL)
    dx_ref[...] = res_ref[...] + dx

    @pl.when(pl.program_id(0) == 0)
    def _():
        dnw_ref[...] = jnp.zeros_like(dnw_ref)

    dnw_ref[...] += jnp.sum(dwx, axis=0, keepdims=True)


def _mlp_up_bwd(d_up, w8, i, res, x, nw):
    L = d_up.shape[0]

    def body(a_ref, w_ref, res_ref, x_ref, nw_ref, dx_ref, dnw_ref):
        dh = jnp.zeros((TMM, D_MODEL), F32)
        for j in range(N_DEV):
            dh = dh + lax.dot_general(a_ref[:, j * FF_SHARD:(j + 1) * FF_SHARD], w_ref[j, 0], _NT, preferred_element_type=F32)
        _rms_bwd_epilogue(dh, res_ref, x_ref, nw_ref, dx_ref, dnw_ref)

    return pl.pallas_call(
        body, name="mlp_up_bwd", grid=(L // TMM,),
        in_specs=[_mrows(D_FF), _lw(w8), _mrows(D_MODEL), _mrows(D_MODEL), _lp(D_MODEL, i)],
        out_specs=[_mrows(D_MODEL), _full((1, D_MODEL))],
        out_shape=[_sds((L, D_MODEL)), _sds((1, D_MODEL))],
        compiler_params=_cparams(("arbitrary",)),
    )(d_up, w8, res, x, nw)


def _in_proj_bwd(pieces, w, res, x, nw, i):
    L = pieces[0].shape[0]
    n = len(pieces)

    def body(*refs):
        w_ref, res_ref, x_ref, nw_ref, dx_ref, dnw_ref, du_ref = refs[n:]
        off = 0
        for r in refs[:n]:
            du_ref[:, off:off + r.shape[1]] = r[...].astype(_MM)
            off += r.shape[1]
        dh = lax.dot_general(du_ref[...], w_ref[...], _NT, preferred_element_type=F32)
        _rms_bwd_epilogue(dh, res_ref, x_ref, nw_ref, dx_ref, dnw_ref)

    return pl.pallas_call(
        body, name="in_proj_bwd", grid=(L // TM,),
        in_specs=[_rows(q.shape[1]) for q in pieces] + [_full(w.shape), _rows(D_MODEL), _rows(D_MODEL), _lp(D_MODEL, i)],
        out_specs=[_rows(D_MODEL), _full((1, D_MODEL)), _rows(N_IN_P)],
        out_shape=[_sds((L, D_MODEL)), _sds((1, D_MODEL)), _sds((L, N_IN_P), _MM)],
        compiler_params=_cparams(("arbitrary",)),
    )(*pieces, w, res, x, nw)


def _mm_tn(a, g, name, tk, tn, out):
    L, K = a.shape
    N = g.shape[1]

    def body(a_ref, g_ref, o_ref):
        r = lax.dot_general(a_ref[...].astype(_MM), g_ref[...].astype(_MM), _TN, preferred_element_type=F32)
        o_ref[...] = r.astype(o_ref.dtype).reshape(o_ref.shape)

    if out == "flat":
        out_spec, out_shape = pl.BlockSpec((tk, tn), lambda i, j: (i, j)), _sds((K, N), _MM)
    elif out == "rows":
        assert tn == N and tk % (K // N_DEV) == 0
        nblk = tk // (K // N_DEV)
        out_spec, out_shape = pl.BlockSpec((nblk, K // N_DEV, N), lambda i, j: (i, 0, 0)), _sds((N_DEV, K // N_DEV, N), _MM)
    else:
        assert tk == K and tn == N // N_DEV
        out_spec, out_shape = pl.BlockSpec((1, K, tn), lambda i, j: (j, 0, 0)), _sds((N_DEV, K, tn), _MM)
    return pl.pallas_call(
        body, name=name, grid=(K // tk, N // tn),
        in_specs=[pl.BlockSpec((L, tk), lambda i, j: (0, i)), pl.BlockSpec((L, tn), lambda i, j: (0, j))],
        out_specs=out_spec, out_shape=out_shape,
        compiler_params=_cparams(("parallel", "parallel")),
    )(a, g)


def _loss_head(y, t):
    L = y.shape[0]

    def body(y_ref, t_ref, dy_ref, l_ref):
        e = y_ref[...] - t_ref[...]
        dy_ref[...] = e * (1.0 / D_MODEL)

        @pl.when(pl.program_id(0) == 0)
        def _():
            l_ref[...] = jnp.zeros_like(l_ref)

        l_ref[...] += jnp.sum(jnp.sum(e * e, axis=1, keepdims=True), axis=0, keepdims=True) * (0.5 / D_MODEL)

    return pl.pallas_call(
        body, name="loss_head", grid=(L // TM,),
        in_specs=[_rows(D_MODEL), _rows(D_MODEL)],
        out_specs=[_rows(D_MODEL), _full((8, LANES))],
        out_shape=[_sds((L, D_MODEL)), _sds((8, LANES))],
        compiler_params=_cparams(("arbitrary",)),
    )(y, t)


EDGE = 32


def _roll_rows(x, s):
    s = s % x.shape[0]
    return x if s == 0 else pltpu.roll(x, s, axis=0)


class _Rolls:
    def __init__(self, x):
        self.x, self.by_phase = x, {}

    def __call__(self, s):
        s = s % self.x.shape[0]
        b = s % 8
        if b not in self.by_phase:
            self.by_phase[b] = _roll_rows(self.x, b)
        return _roll_rows(self.by_phase[b], s - b)


def _conv_taps(x, w_ref, b, k_w):
    def taps(v, zero_fill):
        r = lax.broadcasted_iota(jnp.int32, v.shape, 0)
        acc = jnp.broadcast_to(b, v.shape)
        rolled = _Rolls(v)
        for k in range(k_w):
            s = k_w - 1 - k
            sh = rolled(s)
            if zero_fill and s:
                sh = jnp.where(r >= s, sh, 0.0)
            acc = acc + w_ref[0, k:k + 1, :] * sh
        return acc

    return jnp.concatenate([taps(x[:EDGE], True), taps(x, False)[EDGE:]], axis=0)


def _conv_bwd_taps(x, dc, w_ref, dw_ref, db_ref, k_w):
    n = x.shape[0]
    dc_tail, x_tail, dc_head = dc[n - EDGE:], x[n - EDGE:], dc[:EDGE]
    r = lax.broadcasted_iota(jnp.int32, dc_head.shape, 0)
    dx = jnp.zeros_like(x)
    dx_tail = jnp.zeros_like(dc_tail)
    dc_rolled, x_rolled = _Rolls(dc), _Rolls(x)
    for k in range(k_w):
        s = k_w - 1 - k
        wk = w_ref[0, k:k + 1, :]
        dx = dx + wk * dc_rolled(n - s)
        up = _roll_rows(dc_tail, EDGE - s)
        dx_tail = dx_tail + wk * (jnp.where(r < EDGE - s, up, 0.0) if s else up)
        dw = jnp.sum(dc * x_rolled(s), axis=0, keepdims=True)
        if s:
            dw = dw - jnp.sum(jnp.where(r < s, dc_head * _roll_rows(x_tail, s), 0.0), axis=0, keepdims=True)
        dw_ref[k:k + 1, :] = dw
    db_ref[...] = jnp.sum(dc, axis=0, keepdims=True)
    return jnp.concatenate([dx[:n - EDGE], dx_tail], axis=0)


def _cols(L, cb, off=0):
    return pl.BlockSpec((L, cb), lambda j: (0, j + off))


def _lcols(k, cb, i):
    return pl.BlockSpec((1, k, cb), lambda j: (i, 0, j))


SSD_CB = 256


def _ssd_conv_fwd(x, w, b, i):
    L, C = x.shape
    cb = SSD_CB

    def body(x_ref, w_ref, b_ref, o_ref):
        o_ref[...] = _silu(_conv_taps(x_ref[...], w_ref, b_ref[0], SSD_CONV))

    return pl.pallas_call(
        body, name="ssd_conv_fwd", grid=(C // cb,),
        in_specs=[_cols(L, cb), _lcols(SSD_CONV, cb, i), _lcols(1, cb, i)],
        out_specs=_cols(L, cb), out_shape=_sds((L, C)),
        compiler_params=_cparams(("parallel",)),
    )(x, w, b)


def _ssd_conv_bwd(x, w, b, i, dy):
    L, C = x.shape
    cb = SSD_CB

    def body(x_ref, w_ref, b_ref, dy_ref, dx_ref, dw_ref, db_ref):
        x_ = x_ref[...]
        c = _conv_taps(x_, w_ref, b_ref[0], SSD_CONV)
        dc = dy_ref[...] * _dsilu(c)
        dx_ref[...] = _conv_bwd_taps(x_, dc, w_ref, dw_ref, db_ref, SSD_CONV)

    return pl.pallas_call(
        body, name="ssd_conv_bwd", grid=(C // cb,),
        in_specs=[_cols(L, cb), _lcols(SSD_CONV, cb, i), _lcols(1, cb, i), _cols(L, cb)],
        out_specs=[_cols(L, cb), _cols(SSD_CONV, cb), _cols(1, cb)],
        out_shape=[_sds((L, C)), _sds((SSD_CONV, C)), _sds((1, C))],
        compiler_params=_cparams(("parallel",)),
    )(x, w, b, dy)


def _cm_conv_fwd(glu, w, b, i):
    L = glu.shape[0]
    cb = LANES
    nb = CM_CHANNELS // cb

    def body(a_ref, g_ref, w_ref, b_ref, o_ref):
        h = a_ref[...] * _sigmoid(g_ref[...])
        o_ref[...] = _conv_taps(h, w_ref, b_ref[0], CM_CONV)

    return pl.pallas_call(
        body, name="cm_conv_fwd", grid=(nb,),
        in_specs=[_cols(L, cb), _cols(L, cb, nb), _lcols(CM_CONV, cb, i), _lcols(1, cb, i)],
        out_specs=_cols(L, cb), out_shape=_sds((L, CM_CHANNELS)),
        compiler_params=_cparams(("parallel",)),
    )(glu, glu, w, b)


def _cm_conv_bwd(glu, w, i, dc):
    L = glu.shape[0]
    cb = LANES
    nb = CM_CHANNELS // cb

    def body(a_ref, g_ref, w_ref, dc_ref, da_ref, dg_ref, dw_ref, db_ref):
        a = a_ref[...]
        sg = _sigmoid(g_ref[...])
        dh = _conv_bwd_taps(a * sg, dc_ref[...], w_ref, dw_ref, db_ref, CM_CONV)
        da_ref[...] = dh * sg
        dg_ref[...] = dh * a * sg * (1.0 - sg)

    return pl.pallas_call(
        body, name="cm_conv_bwd", grid=(nb,),
        in_specs=[_cols(L, cb), _cols(L, cb, nb), _lcols(CM_CONV, cb, i), _cols(L, cb)],
        out_specs=[_cols(L, cb), _cols(L, cb), _cols(CM_CONV, cb), _cols(1, cb)],
        out_shape=[_sds((L, CM_CHANNELS)), _sds((L, CM_CHANNELS)), _sds((CM_CONV, CM_CHANNELS)), _sds((1, CM_CHANNELS))],
        compiler_params=_cparams(("parallel",)),
    )(glu, glu, w, dc)


GRP = SSD_WIDTH // SSD_GROUPS


def _mix_post(y, z, attn, c, snw, lw, lb, i):
    L = y.shape[0]

    def body(y_ref, z_ref, a_ref, c_ref, snw_ref, lw_ref, lb_ref, o_ref):
        g = y_ref[...] * _silu(z_ref[...])
        for k in range(SSD_GROUPS):
            sl = slice(k * GRP, (k + 1) * GRP)
            gg = g[:, sl]
            r = lax.rsqrt(jnp.mean(gg * gg, axis=-1, keepdims=True) + RMS_EPS)
            o_ref[:, sl] = (gg * r * snw_ref[0, :, sl]).astype(_MM)
        o_ref[:, SSD_WIDTH:SSD_WIDTH + 512] = a_ref[...].astype(_MM)
        cv = c_ref[...]
        mu = jnp.mean(cv, axis=-1, keepdims=True)
        xc = cv - mu
        rs = lax.rsqrt(jnp.mean(xc * xc, axis=-1, keepdims=True) + LN_EPS)
        o_ref[:, SSD_WIDTH + 512:] = _silu(xc * rs * lw_ref[0] + lb_ref[0]).astype(_MM)

    return pl.pallas_call(
        body, name="mix_post", grid=(L // TM,),
        in_specs=[_rows(SSD_WIDTH), _rows(SSD_WIDTH), _rows(512), _rows(512),
                  _lp(SSD_WIDTH, i), _lp(512, i), _lp(512, i)],
        out_specs=_rows(D_MIX), out_shape=_sds((L, D_MIX), _MM),
        compiler_params=_cparams(("parallel",)),
    )(y, z, attn, c, snw, lw, lb)


def _mix_post_bwd(dcat, y, z, c, snw, lw, lb, i):
    L = y.shape[0]

    def body(d_ref, y_ref, z_ref, c_ref, snw_ref, lw_ref, lb_ref, dy_ref, dz_ref, dc_ref, dsnw_ref, dlw_ref, dlb_ref):
        @pl.when(pl.program_id(0) == 0)
        def _():
            dsnw_ref[...] = jnp.zeros_like(dsnw_ref)
            dlw_ref[...] = jnp.zeros_like(dlw_ref)
            dlb_ref[...] = jnp.zeros_like(dlb_ref)

        yv = y_ref[...]
        zv = z_ref[...]
        sz = _silu(zv)
        g = yv * sz
        for k in range(SSD_GROUPS):
            sl = slice(k * GRP, (k + 1) * GRP)
            dgg, dwx = _rms_bwd(d_ref[:, sl], g[:, sl], snw_ref[0, :, sl], 1.0 / GRP)
            dsnw_ref[:, sl] += jnp.sum(dwx, axis=0, keepdims=True)
            dy_ref[:, sl] = dgg * sz[:, sl]
            dz_ref[:, sl] = dgg * yv[:, sl] * _dsilu(zv[:, sl])
        cv = c_ref[...]
        mu = jnp.mean(cv, axis=-1, keepdims=True)
        xc = cv - mu
        rs = lax.rsqrt(jnp.mean(xc * xc, axis=-1, keepdims=True) + LN_EPS)
        xh = xc * rs
        ln = xh * lw_ref[0] + lb_ref[0]
        dln = d_ref[:, SSD_WIDTH + 512:] * _dsilu(ln)
        dlb_ref[...] += jnp.sum(dln, axis=0, keepdims=True)
        dlw_ref[...] += jnp.sum(dln * xh, axis=0, keepdims=True)
        dxh = dln * lw_ref[0]
        dc_ref[...] = rs * (dxh - jnp.mean(dxh, axis=-1, keepdims=True)
                            - xh * jnp.mean(dxh * xh, axis=-1, keepdims=True))

    return pl.pallas_call(
        body, name="mix_post_bwd", grid=(L // TM,),
        in_specs=[_rows(D_MIX), _rows(SSD_WIDTH), _rows(SSD_WIDTH), _rows(512),
                  _lp(SSD_WIDTH, i), _lp(512, i), _lp(512, i)],
        out_specs=[_rows(SSD_WIDTH), _rows(SSD_WIDTH), _rows(512), _full((1, SSD_WIDTH)), _full((1, 512)), _full((1, 512))],
        out_shape=[_sds((L, SSD_WIDTH)), _sds((L, SSD_WIDTH)), _sds((L, 512)), _sds((1, SSD_WIDTH)), _sds((1, 512)), _sds((1, 512))],
        compiler_params=_cparams(("arbitrary",)),
    )(dcat, y, z, c, snw, lw, lb)


def _seg_mean_matrix():
    i = lax.broadcasted_iota(jnp.int32, (LANES, LANES), 0)
    j = lax.broadcasted_iota(jnp.int32, (LANES, LANES), 1)
    return jnp.where(i // ATTN_HEAD_DIM == j // ATTN_HEAD_DIM, 1.0 / ATTN_HEAD_DIM, 0.0).astype(F32)


def _rot_matrix():
    i = lax.broadcasted_iota(jnp.int32, (LANES, LANES), 0)
    j = lax.broadcasted_iota(jnp.int32, (LANES, LANES), 1)
    half = ATTN_HEAD_DIM // 2
    lo = (j % ATTN_HEAD_DIM) < half
    return jnp.where(lo & (i == j + half), -1.0, jnp.where((~lo) & (i == j - half), 1.0, 0.0)).astype(F32)


N_QK_TILES = 5
QK_W = N_QK_TILES * LANES


def _qk_prep(qkv, qw, kw, i, cos, sin):
    L = qkv.shape[0]

    def body(x_ref, qw_ref, kw_ref, c_ref, s_ref, o_ref):
        m64 = _seg_mean_matrix()
        rot = _rot_matrix()
        cs, sn = c_ref[...], s_ref[...]
        for t in range(N_QK_TILES):
            sl = slice(t * LANES, (t + 1) * LANES)
            x = x_ref[:, sl]
            w = qw_ref[0] if t < 4 else kw_ref[0]
            xn = x * lax.rsqrt(_mmx(x * x, m64) + RMS_EPS) * w
            o_ref[:, sl] = xn * cs + _mmx(xn, rot) * sn

    return pl.pallas_call(
        body, name="qk_prep", grid=(L // TM,),
        in_specs=[_rows(QK_W), _lp(LANES, i), _lp(LANES, i), _rows(LANES), _rows(LANES)],
        out_specs=_rows(QK_W), out_shape=_sds((L, QK_W)),
        compiler_params=_cparams(("parallel",)),
    )(qkv, qw, kw, cos, sin)


def _qk_prep_bwd(dq, dk, qkv, qw, kw, i, cos, sin):
    L = qkv.shape[0]

    def body(dq_ref, dk_ref, x_ref, qw_ref, kw_ref, c_ref, s_ref, dx_ref, dqw_ref, dkw_ref):
        @pl.when(pl.program_id(0) == 0)
        def _():
            dqw_ref[...] = jnp.zeros_like(dqw_ref)
            dkw_ref[...] = jnp.zeros_like(dkw_ref)

        m64 = _seg_mean_matrix()
        rot = _rot_matrix()
        cs, sn = c_ref[...], s_ref[...]
        for t in range(N_QK_TILES):
            sl = slice(t * LANES, (t + 1) * LANES)
            x = x_ref[:, sl]
            dy = dq_ref[:, sl] if t < 4 else dk_ref[...]
            w = qw_ref[0] if t < 4 else kw_ref[0]
            dxn = dy * cs - _mmx(dy * sn, rot)
            r = lax.rsqrt(_mmx(x * x, m64) + RMS_EPS)
            xh = x * r
            dxh = dxn * w
            dx_ref[:, sl] = r * (dxh - xh * _mmx(dxh * xh, m64))
            dw = jnp.sum(dxn * xh, axis=0, keepdims=True)
            if t < 4:
                dqw_ref[...] += dw
            else:
                dkw_ref[...] += dw

    return pl.pallas_call(
        body, name="qk_prep_bwd", grid=(L // TM,),
        in_specs=[_rows(512), _rows(LANES), _rows(QK_W), _lp(LANES, i), _lp(LANES, i), _rows(LANES), _rows(LANES)],
        out_specs=[_rows(QK_W), _full((1, LANES)), _full((1, LANES))],
        out_shape=[_sds((L, QK_W)), _sds((1, LANES)), _sds((1, LANES))],
        compiler_params=_cparams(("arbitrary",)),
    )(dq, dk, qkv, qw, kw, cos, sin)


HPG = 4
SCALE = 1.0 / math.sqrt(ATTN_HEAD_DIM)


def _heads_to_rows(q, g):
    return jnp.concatenate([q[:, (HPG * g + r) * ATTN_HEAD_DIM:(HPG * g + r + 1) * ATTN_HEAD_DIM] for r in range(HPG)], axis=0)


def _rows_to_heads(parts):
    return jnp.concatenate([p[r * Q:(r + 1) * Q] for p in parts for r in range(HPG)], axis=1)


def _attn_probs(q, kc, n, s0, sink_ref, base):
    s = _mm(q, kc, _NT) * SCALE
    rows = lax.broadcasted_iota(jnp.int32, s.shape, 0)
    ki = lax.broadcasted_iota(jnp.int32, s.shape, 1)
    diff = (n * Q + rows % Q) - (s0 + ki)
    s = jnp.where((diff >= 0) & (diff < Q), s, -jnp.inf)
    hrow = lax.broadcasted_iota(jnp.int32, (HPG * Q, 1), 0) // Q
    sink = jnp.zeros((HPG * Q, 1), F32)
    for r in range(HPG):
        sink = jnp.where(hrow == r, sink_ref[base + r], sink)
    m = jnp.maximum(jnp.max(s, axis=1, keepdims=True), sink)
    p = jnp.exp(s - m)
    es = jnp.exp(sink - m)
    inv = 1.0 / (jnp.sum(p, axis=1, keepdims=True) + es)
    return p * inv, es * inv


def _attn_fwd(qk, qkv, sinks, i):
    L = qk.shape[0]

    def body(sink_ref, q_ref, k_ref, v_ref, o_ref):
        n = pl.program_id(0)
        s0 = pl.multiple_of(jnp.maximum(n - 1, 0) * Q, Q)
        q = q_ref[...]
        kc = k_ref[pl.ds(s0, 2 * Q), :]
        vc = v_ref[pl.ds(s0, 2 * Q), :]
        outs = []
        for g in range(2):
            sl = slice(g * ATTN_HEAD_DIM, (g + 1) * ATTN_HEAD_DIM)
            p, _ = _attn_probs(_heads_to_rows(q, g), kc[:, sl], n, s0, sink_ref, i * ATTN_Q_HEADS + g * HPG)
            outs.append(_mm(p, vc[:, sl]))
        o_ref[...] = _rows_to_heads(outs)

    return pl.pallas_call(
        body, name="attn_fwd", grid=(L // Q,),
        in_specs=[pl.BlockSpec(memory_space=pltpu.SMEM), _rows(512, Q),
                  pl.BlockSpec((L, LANES), lambda n: (0, 4)), pl.BlockSpec((L, LANES), lambda n: (0, 5))],
        out_specs=_rows(512, Q), out_shape=_sds((L, 512)),
        compiler_params=_cparams(("parallel",)),
    )(sinks, qk, qk, qkv)


def _attn_bwd(qk, qkv, sinks, i, dcat):
    L = qk.shape[0]

    def body(sink_ref, q_ref, k_ref, v_ref, do_ref, dq_ref, dk_ref, dv_ref, ds_ref):
        n = pl.program_id(0)

        @pl.when(n == 0)
        def _():
            dk_ref[...] = jnp.zeros_like(dk_ref)
            dv_ref[...] = jnp.zeros_like(dv_ref)
            ds_ref[...] = jnp.zeros_like(ds_ref)

        s0 = pl.multiple_of(jnp.maximum(n - 1, 0) * Q, Q)
        q = q_ref[...]
        do_all = do_ref[...]
        kc = k_ref[pl.ds(s0, 2 * Q), :]
        vc = v_ref[pl.ds(s0, 2 * Q), :]
        hrow = lax.broadcasted_iota(jnp.int32, (HPG * Q, 1), 0) // Q
        orow = lax.broadcasted_iota(jnp.int32, (8, LANES), 0)
        dqs, dks, dvs = [], [], []
        acc = jnp.zeros((8, LANES), F32)
        for g in range(2):
            sl = slice(g * ATTN_HEAD_DIM, (g + 1) * ATTN_HEAD_DIM)
            qg = _heads_to_rows(q, g)
            do = _heads_to_rows(do_all, g)
            p, ps = _attn_probs(qg, kc[:, sl], n, s0, sink_ref, i * ATTN_Q_HEADS + g * HPG)
            dp = _mm(do, vc[:, sl], _NT)
            delta = jnp.sum(p * dp, axis=1, keepdims=True)
            ds = p * (dp - delta)
            dqs.append(_mm(ds, kc[:, sl]) * SCALE)
            dks.append(_mm(ds, qg, _TN) * SCALE)
            dvs.append(_mm(p, do, _TN))
            dsink = -(ps * delta)
            for r in range(HPG):
                tot = jnp.sum(jnp.where(hrow == r, dsink, 0.0), axis=0, keepdims=True)
                acc = acc + jnp.where(orow == g * HPG + r, tot, 0.0)
        dq_ref[...] = _rows_to_heads(dqs)
        dk_ref[pl.ds(s0, 2 * Q), :] += jnp.concatenate(dks, axis=1)
        dv_ref[pl.ds(s0, 2 * Q), :] += jnp.concatenate(dvs, axis=1)
        ds_ref[...] += acc

    return pl.pallas_call(
        body, name="attn_bwd", grid=(L // Q,),
        in_specs=[pl.BlockSpec(memory_space=pltpu.SMEM), _rows(512, Q),
                  pl.BlockSpec((L, LANES), lambda n: (0, 4)), pl.BlockSpec((L, LANES), lambda n: (0, 5)),
                  _rows(512, Q, 2)],
        out_specs=[_rows(512, Q), _full((L, LANES)), _full((L, LANES)), _full((8, LANES))],
        out_shape=[_sds((L, 512)), _sds((L, LANES)), _sds((L, LANES)), _sds((8, LANES))],
        compiler_params=_cparams(("arbitrary",)),
    )(sinks, qk, qk, qkv, dcat)


N_PAIR = SSD_HEADS // 2
P = 64
OFF_B = SSD_WIDTH
OFF_C = SSD_WIDTH + SSD_GROUPS * SSD_STATE


def _expand_matrix():
    i = lax.broadcasted_iota(jnp.int32, (LANES, SSD_WIDTH), 0)
    j = lax.broadcasted_iota(jnp.int32, (LANES, SSD_WIDTH), 1)
    return jnp.where(j // P == i, 1.0, 0.0).astype(F32)


def _ssd_chunk_common(dtr_ref, bias_ref, alog_ref):
    dt = jax.nn.softplus(dtr_ref[...] + bias_ref[0])
    a = -jnp.exp(alog_ref[0])
    adt = dt * a
    ri = lax.broadcasted_iota(jnp.int32, (Q, Q), 0)
    ci = lax.broadcasted_iota(jnp.int32, (Q, Q), 1)
    causal = ri >= ci
    tri = jnp.where(causal, 1.0, 0.0).astype(F32)
    acs = _mmx(tri, adt, exact="a")
    em = _expand_matrix()
    acs_e = _mmx(acs, em)
    dt_e = _mmx(dt, em)
    alast_e = acs_e[Q - 1:Q, :]
    return dt, a, acs, causal, tri, em, acs_e, dt_e, alast_e


def _ssd_fwd(xbc, dtr, bias, alog, d_e, i):
    L = xbc.shape[0]
    nc = L // Q

    def body(xbc_ref, dtr_ref, bias_ref, alog_ref, de_ref, y_ref, hp_ref, st_ref):
        @pl.when(pl.program_id(0) == 0)
        def _():
            st_ref[...] = jnp.zeros_like(st_ref)

        dt, a, acs, causal, tri, em, acs_e, dt_e, alast_e = _ssd_chunk_common(dtr_ref, bias_ref, alog_ref)
        acs_t = acs.T
        x = xbc_ref[:, :SSD_WIDTH]
        xdt = x * dt_e
        ea_e = jnp.exp(acs_e)
        xds = xdt * jnp.exp(alast_e - acs_e)
        cd_e = jnp.exp(alast_e)
        lane = lax.broadcasted_iota(jnp.int32, (Q, LANES), 1)
        lo = lane < P
        for g in range(SSD_GROUPS):
            bg = xbc_ref[:, OFF_B + g * SSD_STATE:OFF_B + (g + 1) * SSD_STATE]
            cg = xbc_ref[:, OFF_C + g * SSD_STATE:OFF_C + (g + 1) * SSD_STATE]
            cb = _mm(cg, bg, _NT)
            for pp in range(N_PAIR // SSD_GROUPS):
                pr = g * (N_PAIR // SSD_GROUPS) + pp
                sl = slice(pr * LANES, (pr + 1) * LANES)
                xdt_p = xdt[:, sl]
                yd = jnp.zeros((Q, LANES), F32)
                for half in range(2):
                    h = 2 * pr + half
                    rowb = jnp.broadcast_to(acs_t[h:h + 1, :], (Q, Q))
                    lm = jnp.exp(jnp.where(causal, rowb.T - rowb, -jnp.inf))
                    xh = jnp.where(lo if half == 0 else ~lo, xdt_p, 0.0)
                    yd = yd + _mm(cb * lm, xh)
                hp = st_ref[pr]
                hp_ref[0, pr] = hp
                yoff = _mm(cg, hp) * ea_e[:, sl]
                y_ref[:, sl] = yd + yoff + x[:, sl] * de_ref[0, :, sl]
                st_ref[pr] = hp * cd_e[:, sl] + _mm(bg, xds[:, sl], _TN)

    return pl.pallas_call(
        body, name="ssd_fwd", grid=(nc,),
        in_specs=[_rows(SSD_XBC, Q), _rows(LANES, Q), _lp(LANES, i), _lp(LANES, i), _lp(SSD_WIDTH, i)],
        out_specs=[_rows(SSD_WIDTH, Q), pl.BlockSpec((1, N_PAIR, SSD_STATE, LANES), lambda c: (c, 0, 0, 0))],
        out_shape=[_sds((L, SSD_WIDTH)), _sds((nc, N_PAIR, SSD_STATE, LANES))],
        scratch_shapes=[pltpu.VMEM((N_PAIR, SSD_STATE, LANES), F32)],
        compiler_params=_cparams(("arbitrary",)),
    )(xbc, dtr, bias, alog, d_e)


def _ssd_bwd(xbc, dtr, bias, alog, d_e, i, hprev, dy):
    L = xbc.shape[0]
    nc = L // Q
    rev = lambda c: (nc - 1 - c, 0)

    def body(xbc_ref, dtr_ref, bias_ref, alog_ref, de_ref, hp_ref, dy_ref,
             dxbc_ref, ddtr_ref, dbias_ref, dalog_ref, dd_ref, dst_ref):
        @pl.when(pl.program_id(0) == 0)
        def _():
            dst_ref[...] = jnp.zeros_like(dst_ref)
            dbias_ref[...] = jnp.zeros_like(dbias_ref)
            dalog_ref[...] = jnp.zeros_like(dalog_ref)
            dd_ref[...] = jnp.zeros_like(dd_ref)

        dt, a, acs, causal, tri, em, acs_e, dt_e, alast_e = _ssd_chunk_common(dtr_ref, bias_ref, alog_ref)
        acs_t = acs.T
        x = xbc_ref[:, :SSD_WIDTH]
        dy = dy_ref[...]
        xdt = x * dt_e
        ea_e = jnp.exp(acs_e)
        dse = jnp.exp(alast_e - acs_e)
        xds = xdt * dse
        cd_e = jnp.exp(alast_e)
        lane = lax.broadcasted_iota(jnp.int32, (Q, LANES), 1)
        lo = lane < P
        sub = lax.broadcasted_iota(jnp.int32, (Q, Q), 0)
        lan = lax.broadcasted_iota(jnp.int32, (Q, Q), 1)

        da_rows = jnp.zeros((Q, Q), F32)
        da_cols_t = jnp.zeros((Q, Q), F32)
        dxdt_parts = []
        wyoff_parts = []
        dcd_parts = []
        dxds_parts = []
        for g in range(SSD_GROUPS):
            bg = xbc_ref[:, OFF_B + g * SSD_STATE:OFF_B + (g + 1) * SSD_STATE]
            cg = xbc_ref[:, OFF_C + g * SSD_STATE:OFF_C + (g + 1) * SSD_STATE]
            cb = _mm(cg, bg, _NT)
            dcb = jnp.zeros((Q, Q), F32)
            dcg = jnp.zeros((Q, SSD_STATE), F32)
            dbg = jnp.zeros((Q, SSD_STATE), F32)
            for pp in range(N_PAIR // SSD_GROUPS):
                pr = g * (N_PAIR // SSD_GROUPS) + pp
                sl = slice(pr * LANES, (pr + 1) * LANES)
                xdt_p = xdt[:, sl]
                dy_p = dy[:, sl]
                dxdt_p = jnp.zeros((Q, LANES), F32)
                for half in range(2):
                    h = 2 * pr + half
                    hm = lo if half == 0 else ~lo
                    rowb = jnp.broadcast_to(acs_t[h:h + 1, :], (Q, Q))
                    lm = jnp.exp(jnp.where(causal, rowb.T - rowb, -jnp.inf))
                    m = cb * lm
                    dyh = jnp.where(hm, dy_p, 0.0)
                    gmat = _mm(dyh, xdt_p, _NT)
                    w = gmat * m
                    da_rows = da_rows + jnp.where(lan == h, jnp.sum(w, axis=1, keepdims=True), 0.0)
                    da_cols_t = da_cols_t + jnp.where(sub == h, jnp.sum(w, axis=0, keepdims=True), 0.0)
                    dcb = dcb + gmat * lm
                    dxdt_p = dxdt_p + _mm(m, dyh, _TN)
                hp = hp_ref[0, pr]
                dt_off = dy_p * ea_e[:, sl]
                t_off = _mm(cg, hp)
                wyoff_parts.append(dt_off * t_off)
                dcg = dcg + _mm(dt_off, hp, _NT)
                dhp = _mm(cg, dt_off, _TN)
                dS = dst_ref[pr]
                dxds_p = _mm(bg, dS)
                dbg = dbg + _mm(xds[:, sl], dS, _NT)
                dxds_parts.append(dxds_p)
                dxdt_parts.append(dxdt_p + dxds_p * dse[:, sl])
                dcd_parts.append(jnp.sum(dS * hp, axis=0, keepdims=True))
                dst_ref[pr] = dS * cd_e[:, sl] + dhp
            dcg = dcg + _mm(dcb, bg)
            dbg = dbg + _mm(dcb, cg, _TN)
            dxbc_ref[:, OFF_C + g * SSD_STATE:OFF_C + (g + 1) * SSD_STATE] = dcg
            dxbc_ref[:, OFF_B + g * SSD_STATE:OFF_B + (g + 1) * SSD_STATE] = dbg
        dxdt = jnp.concatenate(dxdt_parts, axis=1)
        dxds = jnp.concatenate(dxds_parts, axis=1)
        wyoff = jnp.concatenate(wyoff_parts, axis=1)
        dcd = jnp.concatenate(dcd_parts, axis=1)
        dxbc_ref[:, :SSD_WIDTH] = dy * de_ref[0] + dxdt * dt_e
        zds = dxds * xds
        dacs = _mmx(wyoff - zds, em, _NT) + da_rows - da_cols_t.T
        dalast = _mmx(jnp.broadcast_to(jnp.sum(zds, axis=0, keepdims=True) + dcd * cd_e, (8, SSD_WIDTH)), em, _NT)[0:1, :]
        dacs = dacs + jnp.where(sub == Q - 1, dalast, 0.0)
        dadt = _mmx(tri, dacs, _TN, exact="a")
        ddt = dadt * a + _mmx(dxdt * x, em, _NT)
        ddtr = ddt * _sigmoid(dtr_ref[...] + bias_ref[0])
        ddtr_ref[...] = ddtr
        row0 = lax.broadcasted_iota(jnp.int32, (8, LANES), 0) == 0
        dbias_ref[...] += jnp.where(row0, jnp.sum(ddtr, axis=0, keepdims=True), 0.0)
        dalog_ref[...] += jnp.where(row0, jnp.sum(dadt * dt, axis=0, keepdims=True) * a, 0.0)
        ddx = _mmx(jnp.broadcast_to(jnp.sum(dy * x, axis=0, keepdims=True), (8, SSD_WIDTH)), em, _NT)
        dd_ref[...] += jnp.where(row0, ddx, 0.0)

    acc = _full((8, LANES))
    return pl.pallas_call(
        body, name="ssd_bwd", grid=(nc,),
        in_specs=[pl.BlockSpec((Q, SSD_XBC), rev), pl.BlockSpec((Q, LANES), rev),
                  _lp(LANES, i), _lp(LANES, i), _lp(SSD_WIDTH, i),
                  pl.BlockSpec((1, N_PAIR, SSD_STATE, LANES), lambda c: (nc - 1 - c, 0, 0, 0)), pl.BlockSpec((Q, SSD_WIDTH), rev)],
        out_specs=[pl.BlockSpec((Q, SSD_XBC), rev), pl.BlockSpec((Q, LANES), rev), acc, acc, acc],
        out_shape=[_sds((L, SSD_XBC)), _sds((L, LANES)), _sds((8, LANES)), _sds((8, LANES)), _sds((8, LANES))],
        scratch_shapes=[pltpu.VMEM((N_PAIR, SSD_STATE, LANES), F32)],
        compiler_params=_cparams(("arbitrary",)),
    )(xbc, dtr, bias, alog, d_e, hprev, dy)


def _rope_tables(L):
    inv_freq = ROPE_THETA ** (-jnp.arange(0, ATTN_HEAD_DIM, 2, dtype=F32) / ATTN_HEAD_DIM)
    ang = jnp.arange(L, dtype=F32)[:, None] * inv_freq[None, :]
    return jnp.tile(jnp.cos(ang), (1, 4)), jnp.tile(jnp.sin(ang), (1, 4))


def _stacked_params(small, conv_w, cm_w):
    row = lambda a: a[:, None, :]
    pad = lambda a: jnp.pad(a, ((0, 0), (0, LANES - a.shape[1])))[:, None, :]
    return dict(
        nw_mix=row(small["norm_mix_w"]), conv_w=conv_w, conv_b=row(small["ssd_conv_b"]),
        dt_bias=pad(small["ssd_dt_bias"]), a_log=pad(small["ssd_a_log"]),
        d_e=row(jnp.repeat(small["ssd_d"], P, axis=1)), snw=row(small["ssd_norm_w"]),
        qw=row(jnp.tile(small["q_norm_w"], (1, 2))), kw=row(jnp.tile(small["k_norm_w"], (1, 2))),
        sinks=small["attn_sinks"].reshape(-1), cm_w=cm_w, cm_b=row(small["cm_dw_b"]),
        ln_w=row(small["cm_ln_w"]), ln_b=row(small["cm_ln_b"]), nw_mlp=row(small["norm_mlp_w"]))


def _layer_fwd(x, p, w_in8, late_weights, i, cos, sin, after):
    w_in = _w_in_regroup(w_in8, after)
    h, z, xbc, qkv, glu, dtr = _in_proj(x, p["nw_mix"], i, w_in)
    xbc_c = _ssd_conv_fwd(xbc, p["conv_w"], p["conv_b"], i)
    y_ssd, hprev = _ssd_fwd(xbc_c, dtr, p["dt_bias"], p["a_log"], p["d_e"], i)
    qk = _qk_prep(qkv, p["qw"], p["kw"], i, cos, sin)
    attn = _attn_fwd(qk, qkv, p["sinks"], i)
    c = _cm_conv_fwd(glu, p["cm_w"], p["cm_b"], i)
    ycat = _mix_post(y_ssd, z, attn, c, p["snw"], p["ln_w"], p["ln_b"], i)
    w_out8, w_up8, w_down8 = late_weights(ycat)
    x1 = _mm_res(ycat, w_out8, i, x, "out_proj")
    hm, up, act = _mlp_up(x1, p["nw_mlp"], i, w_up8)
    x2 = _mm_res(act, w_down8, i, x1, "mlp_down")
    saved = dict(x=x, h=h, z=z, xbc=xbc, qkv=qkv, glu=glu, dtr=dtr, xbc_c=xbc_c, y_ssd=y_ssd, hprev=hprev,
                 qk=qk, c=c, ycat=ycat, x1=x1, hm=hm, up=up, act=act, w_in=w_in,
                 w_out8=w_out8, w_up8=w_up8, w_down8=w_down8)
    return x2, saved


def _layer_bwd_mlp(dx2, p, i, s, after):
    d_up = _mlp_down_bwd(dx2, s["w_down8"], i, s["up"], after)
    g_down = _mm_tn(s["act"], dx2, "dw_down", 512, D_MODEL, "rows")
    g_up = _mm_tn(s["hm"], d_up, "dw_up", D_MODEL, FF_SHARD, "cols")
    dx1, g_nw_mlp = _mlp_up_bwd(d_up, s["w_up8"], i, dx2, s["x1"], p["nw_mlp"])
    return dx1, g_up, g_down, g_nw_mlp


def _layer_bwd_mix(dx1, g_nw_mlp, p, i, s, cos, sin, after):
    dcat = _out_proj_bwd(dx1, s["w_out8"], i, after)
    g_out = _mm_tn(s["ycat"], dx1, "dw_out", 512, D_MODEL, "rows")
    dy_ssd, dz, dc, g_snw, g_lw, g_lb = _mix_post_bwd(dcat, s["y_ssd"], s["z"], s["c"], p["snw"], p["ln_w"], p["ln_b"], i)
    da, dg, g_cmw, g_cmb = _cm_conv_bwd(s["glu"], p["cm_w"], i, dc)
    dq, dk, dv, dsk = _attn_bwd(s["qk"], s["qkv"], p["sinks"], i, dcat)
    dqk_raw, g_qw, g_kw = _qk_prep_bwd(dq, dk, s["qkv"], p["qw"], p["kw"], i, cos, sin)
    dxbc_c, ddtr, g_bias, g_alog, g_d = _ssd_bwd(s["xbc_c"], s["dtr"], p["dt_bias"], p["a_log"], p["d_e"], i, s["hprev"], dy_ssd)
    dxbc, g_convw, g_convb = _ssd_conv_bwd(s["xbc"], p["conv_w"], p["conv_b"], i, dxbc_c)
    dx, g_nw_mix, du = _in_proj_bwd([dz, dxbc, dqk_raw, dv, da, dg, ddtr], s["w_in"], dx1, s["x"], p["nw_mix"], i)
    g_in = _g_in_split(_mm_tn(s["h"], du, "dw_in", 512, 640, "flat"))
    half = ATTN_HEAD_DIM
    small = dict(
        norm_mix_w=g_nw_mix[0], ssd_conv_b=g_convb[0], ssd_dt_bias=g_bias[0, :SSD_HEADS], ssd_a_log=g_alog[0, :SSD_HEADS],
        ssd_d=g_d[0, :SSD_HEADS], ssd_norm_w=g_snw[0], q_norm_w=g_qw[0, :half] + g_qw[0, half:],
        k_norm_w=g_kw[0, :half] + g_kw[0, half:], attn_sinks=dsk[:, 0],
        cm_dw_b=g_cmb[0], cm_ln_w=g_lw[0], cm_ln_b=g_lb[0], norm_mlp_w=g_nw_mlp[0],
        ssd_conv_w=g_convw, cm_dw_w=g_cmw)
    return dx, g_in, g_out, small


MESH = pl.DeviceIdType.MESH
_ANY = pl.BlockSpec(memory_space=pl.ANY)


def _coords():
    return lax.axis_index("x"), lax.axis_index("y"), lax.axis_index("c")


def _all_gather(xs, name):
    nt = len(xs)

    def body(*refs):
        x_refs, out_refs = refs[:nt], refs[nt:2 * nt]
        send_sems, recv_sems, local_sems = refs[2 * nt:]
        x, y, c = _coords()
        me, sibling = (x, y, c), (x, y, 1 - c)
        chips = [(1 - x, y), (x, 1 - y), (1 - x, 1 - y)]

        def slot(t, px, py, pc):
            return out_refs[t].at[4 * px + 2 * py + pc]

        def copy(t, k, block, to, src=None):
            return pltpu.make_async_remote_copy(
                src_ref=slot(t, *block) if src is None else src, dst_ref=slot(t, *block),
                send_sem=send_sems.at[7 * t + k], recv_sem=recv_sems.at[7 * t + k], device_id=to, device_id_type=MESH)

        mine = [pltpu.make_async_copy(x_refs[t], slot(t, *me), local_sems.at[t]) for t in range(nt)]
        for cp in mine:
            cp.start()
        first = []
        for t in range(nt):
            first.append(copy(t, 0, me, sibling, src=x_refs[t]))
            first += [copy(t, 1 + j, me, (*chip, c), src=x_refs[t]) for j, chip in enumerate(chips)]
        for cp in first:
            cp.start()
        passed = []
        for j, chip in enumerate(chips):
            for t in range(nt):
                copy(t, 1 + j, (*chip, c), me).wait_recv()
                passed.append(copy(t, 4 + j, (*chip, c), sibling))
                passed[-1].start()
        for t in range(nt):
            copy(t, 0, sibling, me).wait_recv()
            for j, chip in enumerate(chips):
                copy(t, 4 + j, (*chip, 1 - c), me).wait_recv()
        for cp in first + passed:
            cp.wait_send()
        for cp in mine:
            cp.wait()

    return pl.pallas_call(
        body, name=name, out_shape=[_sds((N_DEV,) + a.shape, a.dtype) for a in xs],
        in_specs=[_ANY] * nt, out_specs=[_ANY] * nt,
        scratch_shapes=[pltpu.SemaphoreType.DMA((7 * nt,)), pltpu.SemaphoreType.DMA((7 * nt,)), pltpu.SemaphoreType.DMA((nt,))],
    )(*xs)


def _peer_chips(x, y):
    return [(1 - x, y), (x, 1 - y), (1 - x, 1 - y)]


def _ici_copies(src_refs, land_refs, send_sems, recv_sems, kind):
    x, y, c = _coords()
    sends, recvs = [], []
    for t, d in enumerate(land_refs):
        for j, (px, py) in enumerate(_peer_chips(x, y)):
            if kind == "gather":
                src, dst, got = d.at[4 * x + 2 * y + c], d.at[4 * x + 2 * y + c], d.at[4 * px + 2 * py + c]
            else:
                src, dst, got = src_refs[t].at[2 * px + py], d.at[2 * x + y], d.at[2 * px + py]
            sems = dict(send_sem=send_sems.at[3 * t + j], recv_sem=recv_sems.at[3 * t + j],
                        device_id=(px, py, c), device_id_type=MESH)
            sends.append(pltpu.make_async_remote_copy(src_ref=src, dst_ref=dst, **sems))
            recvs.append(pltpu.make_async_remote_copy(src_ref=src, dst_ref=got, **sems))
    return sends, recvs


_HBM = pl.BlockSpec(memory_space=pltpu.HBM)
_SEMS = pl.BlockSpec(memory_space=pltpu.SEMAPHORE)
_EFFECT = pltpu.SideEffectType.DATAFLOW_SIDE_EFFECTING


def _hbm(a):
    return pltpu.with_memory_space_constraint(a, pltpu.HBM)


def _ici_start(srcs, lands, after, name, kind):
    ns, n = len(srcs), len(lands)
    nt = ns + n

    def body(*refs):
        sends, _ = _ici_copies(refs[:ns], refs[ns:nt], refs[nt + 1], refs[nt + 2], kind)
        for cp in sends:
            cp.start()
        refs[-1][...] = jnp.zeros_like(refs[-1])

    thru = srcs + lands
    out = pl.pallas_call(
        body, name=name,
        out_shape=(pltpu.SemaphoreType.DMA((3 * n,)), pltpu.SemaphoreType.DMA((3 * n,)))
        + tuple(pltpu.HBM(a.shape, a.dtype) for a in thru) + (_sds((8, LANES)),),
        in_specs=[_HBM] * nt + [_ANY],
        out_specs=(_SEMS, _SEMS) + (_HBM,) * nt + (pl.BlockSpec(memory_space=pltpu.VMEM),),
        input_output_aliases={k: 2 + k for k in range(nt)},
        compiler_params=pltpu.CompilerParams(has_side_effects=_EFFECT),
    )(*[_hbm(a) for a in thru], after)
    return out[0], out[1], list(out[2:2 + ns]), list(out[2 + ns:2 + nt]), out[-1]


def _ici_wait(started, after, name, kind):
    send_sems, recv_sems, srcs, lands, _ = started
    ns, n = len(srcs), len(lands)
    nt = ns + n

    def body(*refs):
        sends, recvs = _ici_copies(refs[:ns], refs[ns:nt], refs[nt], refs[nt + 1], kind)
        for s, r in zip(sends, recvs):
            s.wait_send()
            r.wait_recv()

    thru = srcs + lands
    out = pl.pallas_call(
        body, name=name, out_shape=tuple(pltpu.HBM(a.shape, a.dtype) for a in thru),
        in_specs=[_HBM] * nt + [_SEMS, _SEMS, _ANY], out_specs=(_HBM,) * nt,
        input_output_aliases={k: k for k in range(nt)},
        compiler_params=pltpu.CompilerParams(has_side_effects=_EFFECT),
    )(*thru, send_sems, recv_sems, after)
    return list(out[ns:])


def _ag_d2d(lands):
    n = len(lands)

    def body(*refs):
        in_refs, out_refs = refs[:n], refs[n:2 * n]
        send_sems, recv_sems = refs[2 * n:]
        x, y, c = _coords()
        sends, recvs = [], []
        for t in range(n):
            for k, (px, py) in enumerate([(x, y)] + _peer_chips(x, y)):
                sems = dict(send_sem=send_sems.at[4 * t + k], recv_sem=recv_sems.at[4 * t + k],
                            device_id=(x, y, 1 - c), device_id_type=MESH)
                src = in_refs[t].at[4 * px + 2 * py + c]
                sends.append(pltpu.make_async_remote_copy(src_ref=src, dst_ref=out_refs[t].at[4 * px + 2 * py + c], **sems))
                recvs.append(pltpu.make_async_remote_copy(src_ref=src, dst_ref=out_refs[t].at[4 * px + 2 * py + 1 - c], **sems))
        for cp in sends:
            cp.start()
        for cp in recvs:
            cp.wait_recv()
        for cp in sends:
            cp.wait_send()

    return pl.pallas_call(
        body, name="ag_d2d", out_shape=[_sds(a.shape, a.dtype) for a in lands],
        in_specs=[_ANY] * n, out_specs=[_ANY] * n,
        input_output_aliases={k: k for k in range(n)},
        scratch_shapes=[pltpu.SemaphoreType.DMA((4 * n,)), pltpu.SemaphoreType.DMA((4 * n,))],
    )(*lands)


def _rs_sib(grads):
    nt = len(grads)

    def body(*refs):
        s_refs, ra_refs = refs[:nt], refs[nt:2 * nt]
        send_sems, recv_sems = refs[2 * nt:]
        x, y, c = _coords()
        cps = [pltpu.make_async_remote_copy(
            src_ref=s_refs[t].at[:, 1 - c], dst_ref=ra_refs[t], send_sem=send_sems.at[t], recv_sem=recv_sems.at[t],
            device_id=(x, y, 1 - c), device_id_type=MESH) for t in range(nt)]
        for cp in cps:
            cp.start()
        for cp in cps:
            cp.wait()

    return pl.pallas_call(
        body, name="rs_sibling",
        out_shape=[_sds((4,) + g.shape[2:], g.dtype) for g in grads],
        in_specs=[_ANY] * nt, out_specs=[_ANY] * nt,
        scratch_shapes=[pltpu.SemaphoreType.DMA((nt,)), pltpu.SemaphoreType.DMA((nt,))],
    )(*grads)


def _rs_add(grads, ras, core):
    nt = len(grads)

    def body(c_ref, *refs):
        s_refs, ra_refs, q_refs, rb_refs = refs[:nt], refs[nt:2 * nt], refs[2 * nt:3 * nt], refs[3 * nt:]
        for t in range(nt):
            q = (s_refs[t][0, 0].astype(F32) + ra_refs[t][0].astype(F32)).astype(q_refs[t].dtype)
            q_refs[t][0] = q
            rb_refs[t][0] = q

    nr = 4
    own = [pl.BlockSpec((1, 1, g.shape[2] // nr, g.shape[3]), lambda j, r, c: (j, c[0], r, 0)) for g in grads]
    blk = [pl.BlockSpec((1, g.shape[2] // nr, g.shape[3]), lambda j, r, c: (j, r, 0)) for g in grads]
    return pl.pallas_call(
        body, name="rs_add", out_shape=[_sds(r.shape, r.dtype) for r in ras] * 2,
        grid_spec=pltpu.PrefetchScalarGridSpec(num_scalar_prefetch=1, grid=(4, nr), in_specs=own + blk, out_specs=blk * 2),
        compiler_params=_cparams(("parallel", "parallel")),
    )(core, *grads, *ras)


def _adamw(w, g, m, v):
    m = ADAM_B1 * m + (1.0 - ADAM_B1) * g
    v = ADAM_B2 * v + (1.0 - ADAM_B2) * jnp.square(g)
    m_hat = m / (1.0 - ADAM_B1 ** ADAM_STEP)
    v_hat = v / (1.0 - ADAM_B2 ** ADAM_STEP)
    delta = -ADAM_LR * (m_hat / (jnp.sqrt(v_hat) + ADAM_EPS) + ADAM_WD * w)
    return delta, m, v


def _rs_final(rb, w, m, v, outs, l):
    _, R, C = w.shape
    cp = rb.shape[2]

    def body(rb_ref, w_ref, m_ref, v_ref, o0, o1, o2, o3, g_ref, d_ref, m2_ref, v2_ref):
        g = rb_ref[0].astype(F32)
        for j in range(1, 4):
            g = g + rb_ref[j].astype(F32)
        g = g[:, :C]
        g_ref[0] = g
        d_ref[0], m2_ref[0], v2_ref[0] = _adamw(w_ref[0], g, m_ref[0], v_ref[0])

    blk = pl.BlockSpec((1, TM, C), lambda r: (l, r, 0))
    return pl.pallas_call(
        body, name="rs_final_adamw", grid=(R // TM,),
        in_specs=[pl.BlockSpec((4, TM, cp), lambda r: (0, r, 0)), blk, blk, blk] + [_ANY] * 4,
        out_specs=[blk] * 4, out_shape=[_sds(w.shape)] * 4,
        input_output_aliases={4 + k: k for k in range(4)},
        compiler_params=_cparams(("parallel",)),
    )(rb, w, m, v, *outs)


def _rs_final_w_in(rb, wt, mt, vt, outs, l):
    shard = (W_IN_SHARD, D_MODEL)

    def body(rb_ref, wt_ref, mt_ref, vt_ref, o0, o1, o2, o3, g_ref, d_ref, m2_ref, v2_ref, bufs, obufs, sems):
        loads = [pltpu.make_async_copy(src.at[:, l, :], bufs.at[k], sems.at[k]) for k, src in enumerate((wt_ref, mt_ref, vt_ref))]
        for cp in loads:
            cp.start()
        g = rb_ref[0].astype(F32)
        for j in range(1, 4):
            g = g + rb_ref[j].astype(F32)
        g = g.T[:W_IN_SHARD]
        for cp in loads:
            cp.wait()
        obufs[0] = g
        obufs[1], obufs[2], obufs[3] = _adamw(bufs[0], g, bufs[1], bufs[2])
        stores = [pltpu.make_async_copy(obufs.at[k], dst.at[:, l, :], sems.at[3 + k])
                  for k, dst in enumerate((g_ref, d_ref, m2_ref, v2_ref))]
        for cp in stores:
            cp.start()
        for cp in stores:
            cp.wait()

    return pl.pallas_call(
        body, name="rs_final_adamw_w_in",
        in_specs=[pl.BlockSpec(memory_space=pltpu.VMEM)] + [_ANY] * 7,
        out_specs=[_ANY] * 4, out_shape=[_sds(wt.shape)] * 4,
        input_output_aliases={4 + k: k for k in range(4)},
        scratch_shapes=[pltpu.VMEM((3,) + shard, F32), pltpu.VMEM((4,) + shard, F32), pltpu.SemaphoreType.DMA((7,))],
        compiler_params=_cparams(),
    )(rb, wt, mt, vt, *outs)


def _sum8(g8):
    _, R, C = g8.shape

    def body(g_ref, o_ref):
        acc = g_ref[0]
        for d in range(1, N_DEV):
            acc = acc + g_ref[d]
        o_ref[...] = acc

    return pl.pallas_call(body, name="small_sum", out_shape=_sds((R, C)))(g8)


def _adamw_small(w, g, m, v):
    def body(w_ref, g_ref, m_ref, v_ref, d_ref, m2_ref, v2_ref):
        d_ref[...], m2_ref[...], v2_ref[...] = _adamw(w_ref[...], g_ref[...], m_ref[...], v_ref[...])

    return pl.pallas_call(body, name="small_adamw", out_shape=[_sds(w.shape)] * 3)(w, g, m, v)


REP = (("norm_mix_w", 1024), ("ssd_conv_b", 1536), ("ssd_dt_bias", 16), ("ssd_a_log", 16), ("ssd_d", 16),
       ("ssd_norm_w", 1024), ("q_norm_w", 64), ("k_norm_w", 64), ("attn_sinks", 8), ("cm_dw_b", 512),
       ("cm_ln_w", 512), ("cm_ln_b", 512), ("norm_mlp_w", 1024))
WEIGHTS = ("norm_mix_w", "w_in", "ssd_conv_w", "ssd_conv_b", "ssd_dt_bias", "ssd_a_log", "ssd_d", "ssd_norm_w",
           "q_norm_w", "k_norm_w", "attn_sinks", "cm_dw_w", "cm_dw_b", "cm_ln_w", "cm_ln_b", "w_out", "norm_mlp_w",
           "w_mlp_up", "w_mlp_down")
BIG = ("w_in", "w_out", "w_mlp_up", "w_mlp_down")
N_REP = DEPTH * sum(n for _, n in REP)
CONVW_SHARD = SSD_XBC // N_DEV
CMW_SHARD = CM_CHANNELS // N_DEV


def _to_rows(flat, rows):
    return jnp.pad(flat, (0, rows * LANES - flat.shape[0])).reshape(rows, LANES)


def kernel(x, norm_mix_w, w_in, ssd_conv_w, ssd_conv_b, ssd_dt_bias, ssd_a_log, ssd_d, ssd_norm_w, q_norm_w, k_norm_w, attn_sinks, cm_dw_w, cm_dw_b, cm_ln_w, cm_ln_b, w_out, norm_mlp_w, w_mlp_up, w_mlp_down, loss_target, m_norm_mix_w, m_w_in, m_ssd_conv_w, m_ssd_conv_b, m_ssd_dt_bias, m_ssd_a_log, m_ssd_d, m_ssd_norm_w, m_q_norm_w, m_k_norm_w, m_attn_sinks, m_cm_dw_w, m_cm_dw_b, m_cm_ln_w, m_cm_ln_b, m_w_out, m_norm_mlp_w, m_w_mlp_up, m_w_mlp_down, v_norm_mix_w, v_w_in, v_ssd_conv_w, v_ssd_conv_b, v_ssd_dt_bias, v_ssd_a_log, v_ssd_d, v_ssd_norm_w, v_q_norm_w, v_k_norm_w, v_attn_sinks, v_cm_dw_w, v_cm_dw_b, v_cm_ln_w, v_cm_ln_b, v_w_out, v_norm_mlp_w, v_w_mlp_up, v_w_mlp_down):
    w = dict(norm_mix_w=norm_mix_w, w_in=w_in, ssd_conv_w=ssd_conv_w, ssd_conv_b=ssd_conv_b, ssd_dt_bias=ssd_dt_bias, ssd_a_log=ssd_a_log, ssd_d=ssd_d, ssd_norm_w=ssd_norm_w, q_norm_w=q_norm_w, k_norm_w=k_norm_w, attn_sinks=attn_sinks, cm_dw_w=cm_dw_w, cm_dw_b=cm_dw_b, cm_ln_w=cm_ln_w, cm_ln_b=cm_ln_b, w_out=w_out, norm_mlp_w=norm_mlp_w, w_mlp_up=w_mlp_up, w_mlp_down=w_mlp_down)
    m = dict(norm_mix_w=m_norm_mix_w, w_in=m_w_in, ssd_conv_w=m_ssd_conv_w, ssd_conv_b=m_ssd_conv_b, ssd_dt_bias=m_ssd_dt_bias, ssd_a_log=m_ssd_a_log, ssd_d=m_ssd_d, ssd_norm_w=m_ssd_norm_w, q_norm_w=m_q_norm_w, k_norm_w=m_k_norm_w, attn_sinks=m_attn_sinks, cm_dw_w=m_cm_dw_w, cm_dw_b=m_cm_dw_b, cm_ln_w=m_cm_ln_w, cm_ln_b=m_cm_ln_b, w_out=m_w_out, norm_mlp_w=m_norm_mlp_w, w_mlp_up=m_w_mlp_up, w_mlp_down=m_w_mlp_down)
    v = dict(norm_mix_w=v_norm_mix_w, w_in=v_w_in, ssd_conv_w=v_ssd_conv_w, ssd_conv_b=v_ssd_conv_b, ssd_dt_bias=v_ssd_dt_bias, ssd_a_log=v_ssd_a_log, ssd_d=v_ssd_d, ssd_norm_w=v_ssd_norm_w, q_norm_w=v_q_norm_w, k_norm_w=v_k_norm_w, attn_sinks=v_attn_sinks, cm_dw_w=v_cm_dw_w, cm_dw_b=v_cm_dw_b, cm_ln_w=v_cm_ln_w, cm_ln_b=v_cm_ln_b, w_out=v_w_out, norm_mlp_w=v_norm_mlp_w, w_mlp_up=v_w_mlp_up, w_mlp_down=v_w_mlp_down)
    L = x.shape[1]
    xi, yi, ci = _coords()
    me = 4 * xi + 2 * yi + ci
    n_conv = DEPTH * SSD_CONV * CONVW_SHARD
    n_cm = DEPTH * CM_CONV * CMW_SHARD

    conv_rows = 88
    cw8, = _all_gather([_to_rows(jnp.concatenate([ssd_conv_w.reshape(-1), cm_dw_w.reshape(-1)]), conv_rows)], "ag_conv_w")
    cw8 = cw8.reshape(N_DEV, -1)
    conv_full = cw8[:, :n_conv].reshape(N_DEV, DEPTH, SSD_CONV, CONVW_SHARD).transpose(1, 2, 0, 3).reshape(DEPTH, SSD_CONV, SSD_XBC)
    cm_full = cw8[:, n_conv:n_conv + n_cm].reshape(N_DEV, DEPTH, CM_CONV, CMW_SHARD).transpose(1, 2, 0, 3).reshape(DEPTH, CM_CONV, CM_CHANNELS)
    me1 = jnp.reshape(me, (1,)).astype(jnp.int32)
    casts = [_cast_w_in(w_in, me1), _cast_shard(w_out, me1), _cast_shard(w_mlp_up, me1), _cast_shard(w_mlp_down, me1)]
    shards = [[c[l] for c in casts] for l in range(DEPTH)]

    def gather_start(lands, after):
        return _ici_start([], lands, after, "ag_ici_start", "gather")

    def gather_finish(started, after):
        return _ag_d2d(_ici_wait(started, after, "ag_ici_wait", "gather"))

    cos, sin = _rope_tables(L)
    p = _stacked_params({k: w[k] for k, _ in REP}, conv_full, cm_full)
    saved = []
    h = x[0]
    first = gather_start(shards[0][:1], cw8)
    rest0 = gather_start(shards[0][1:], first[4])
    w_in8, = gather_finish(first, rest0[4])
    token, rest = rest0[4], None
    for i in range(DEPTH):
        if i == 0:
            late = lambda ycat: gather_finish(rest0, ycat)
        else:
            late = lambda ycat, r=rest: r
        nxt = None
        if i + 1 < DEPTH:
            nxt = gather_start(shards[i + 1], w_in8)
            token = nxt[4]
        h, s = _layer_fwd(h, p, w_in8, late, i, cos, sin, token)
        saved.append(s)
        if nxt is not None:
            got = gather_finish(nxt, h)
            w_in8, rest = got[0], got[1:]
    d, loss_tile = _loss_head(h, loss_target[0])

    core = jnp.reshape(ci, (1,)).astype(jnp.int32)
    smalls = [None] * DEPTH
    big_out = {k: [lax.empty(w[k].shape, F32) for _ in range(4)] for k in BIG}
    to_t = lambda a: jnp.transpose(a, (2, 0, 1))
    w_in_t = [to_t(t["w_in"]) for t in (w, m, v)]
    big_out["w_in"] = [lax.empty(w_in_t[0].shape, F32) for _ in range(4)]

    def scatter_start(grads, after):
        g4 = [g.reshape((4, 2) + g.shape[1:]) for g in grads]
        out = _rs_add(g4, _rs_sib(g4), core)
        return _ici_start(list(out[:len(g4)]), list(out[len(g4):]), after, "rs_ici_start", "scatter")

    def scatter_finish(started, after, l, names):
        for rb, k in zip(_ici_wait(started, after, "rs_ici_wait", "scatter"), names):
            if k == "w_in":
                big_out[k] = _rs_final_w_in(rb, *w_in_t, big_out[k], l)
            else:
                big_out[k] = _rs_final(rb, w[k], m[k], v[k], big_out[k], l)

    token, pending = loss_tile, []
    for i in reversed(range(DEPTH)):
        dx1, g_up, g_down, g_nw_mlp = _layer_bwd_mlp(d, p, i, saved[i], token)
        started = []
        if i == 0:
            started.append((scatter_start([g_up, g_down], dx1), i, BIG[2:]))
        d, g_in, g_out, smalls[i] = _layer_bwd_mix(dx1, g_nw_mlp, p, i, saved[i], cos, sin,
                                                   started[0][0][4] if started else g_nw_mlp)
        started.append((scatter_start([g_in, g_out] + ([] if i == 0 else [g_up, g_down]), d), i, BIG[:2] if i == 0 else BIG))
        token = started[-1][0][4]
        for st, l, names in pending:
            scatter_finish(st, token, l, names)
        pending = started
    for st, l, names in pending:
        scatter_finish(st, token, l, names)

    gvec = jnp.concatenate(
        [jnp.stack([smalls[i][k] for i in range(DEPTH)]).reshape(-1) for k, _ in REP]
        + [jnp.stack([smalls[i][k] for i in range(DEPTH)]).reshape(-1) for k in ("ssd_conv_w", "cm_dw_w")]
        + [loss_tile[0, :1]])
    g_rows = -(-gvec.shape[0] // (8 * LANES)) * 8
    g8, = _all_gather([_to_rows(gvec, g_rows)], "ag_small_grads")
    gsum = _sum8(g8).reshape(-1)
    o_conv = N_REP
    o_cm = o_conv + DEPTH * SSD_CONV * SSD_XBC
    o_loss = o_cm + DEPTH * CM_CONV * CM_CHANNELS
    g_conv = lax.dynamic_slice_in_dim(gsum[o_conv:o_cm].reshape(DEPTH, SSD_CONV, SSD_XBC), me * CONVW_SHARD, CONVW_SHARD, axis=2)
    g_cm = lax.dynamic_slice_in_dim(gsum[o_cm:o_loss].reshape(DEPTH, CM_CONV, CM_CHANNELS), me * CMW_SHARD, CMW_SHARD, axis=2)
    loss = gsum[o_loss]
    s_rows = -(-(N_REP + n_conv + n_cm) // (8 * LANES)) * 8

    def pack_small(t):
        return _to_rows(jnp.concatenate([t[k].reshape(-1) for k, _ in REP] + [t["ssd_conv_w"].reshape(-1), t["cm_dw_w"].reshape(-1)]), s_rows)

    g_small = _to_rows(jnp.concatenate([gsum[:N_REP], g_conv.reshape(-1), g_cm.reshape(-1)]), s_rows)
    small_out = [g_small] + list(_adamw_small(pack_small(w), g_small, pack_small(m), pack_small(v)))

    def unpack_small(t):
        flat = t.reshape(-1)
        out, off = {}, 0
        for k, n in REP:
            out[k] = flat[off:off + DEPTH * n].reshape(DEPTH, n)
            off += DEPTH * n
        out["ssd_conv_w"] = flat[off:off + n_conv].reshape(DEPTH, SSD_CONV, CONVW_SHARD)
        off += n_conv
        out["cm_dw_w"] = flat[off:off + n_cm].reshape(DEPTH, CM_CONV, CMW_SHARD)
        return out

    outs = [loss, d[None]]
    for j, small_t in enumerate(small_out):
        t = unpack_small(small_t)
        for k in BIG:
            t[k] = big_out[k][j]
        t["w_in"] = jnp.transpose(t["w_in"], (1, 2, 0))
        outs += [t[k] for k in WEIGHTS]
    return tuple(outs)
```

```python
import math

import jax
import jax.numpy as jnp
from jax import lax
from jax.experimental import pallas as pl
from jax.experimental.pallas import tpu as pltpu

F32 = jnp.float32
_MM = jnp.bfloat16

D_MODEL = 1024
DEPTH = 4
SSD_WIDTH = 1024
SSD_HEADS = 16
SSD_STATE = 128
SSD_GROUPS = 2
SSD_CONV = 4
SSD_XBC = 1536
Q = 128
ATTN_HEAD_DIM = 64
ATTN_Q_HEADS = 8
CM_CHANNELS = 512
CM_CONV = 31
D_FF = 4096
D_MIX = 2048
N_IN = 4368
RMS_EPS = 1e-6
LN_EPS = 1e-5
ROPE_THETA = 10000.0
ADAM_LR = 0.001
ADAM_B1 = 0.9
ADAM_B2 = 0.999
ADAM_EPS = 1e-08
ADAM_WD = 0.01
ADAM_STEP = 10

N_DEV = 8
LANES = 128
TM = 256
N_IN_P = 4480
U_Z, U_XBC, U_QKV, U_GLU, U_DT = (0, 1024), (1024, 2560), (2560, 3328), (3328, 4352), (4352, 4480)
W_IN_SHARD = N_IN // N_DEV
W_IN_SHARD_P = 640
FF_SHARD = D_FF // N_DEV
OUT_SHARD = D_MIX // N_DEV

_NN = (((1,), (0,)), ((), ()))
_NT = (((1,), (1,)), ((), ()))
_TN = (((0,), (0,)), ((), ()))
_VMEM_LIMIT = 56 * 1024 * 1024


def _mm(a, b, dims=_NN):
    return lax.dot_general(a.astype(_MM), b.astype(_MM), dims, preferred_element_type=F32)


def _mmx(a, b, dims=_NN, exact="b"):
    m, v = (b, a) if exact == "b" else (a, b)
    m = m.astype(jnp.bfloat16)
    acc = None
    for _ in range(3):
        p = v.astype(jnp.bfloat16)
        v = v - p.astype(F32)
        t = lax.dot_general(p, m, dims, preferred_element_type=F32) if exact == "b" else \
            lax.dot_general(m, p, dims, preferred_element_type=F32)
        acc = t if acc is None else acc + t
    return acc


def _sds(shape, dtype=F32):
    return jax.ShapeDtypeStruct(tuple(shape), dtype)


def _full(shape):
    nd = len(shape)
    return pl.BlockSpec(tuple(shape), lambda *_: (0,) * nd)


def _rows(cols, tm=TM, col=0):
    return pl.BlockSpec((tm, cols), lambda i: (i, col))


TMM = 512


def _mrows(cols):
    return _rows(cols, TMM)


def _lp(n, i):
    return pl.BlockSpec((1, 1, n), lambda *_: (i, 0, 0))


def _lw(arr):
    return pl.BlockSpec(arr.shape, lambda *_: (0, 0, 0, 0))


_ANY = pl.BlockSpec(memory_space=pl.ANY)


def _cparams(sem=None):
    return pltpu.CompilerParams(dimension_semantics=sem, vmem_limit_bytes=_VMEM_LIMIT)


def _sigmoid(x):
    return 1.0 / (1.0 + jnp.exp(-x))


def _silu(x):
    return x * _sigmoid(x)


def _dsilu(x):
    s = _sigmoid(x)
    return s * (1.0 + x * (1.0 - s))


def _rms_bwd(dy, x, w, inv_n):
    r = lax.rsqrt(jnp.sum(x * x, axis=-1, keepdims=True) * inv_n + RMS_EPS)
    xh = x * r
    dxh = dy * w
    dx = r * (dxh - xh * (jnp.sum(dxh * xh, axis=-1, keepdims=True) * inv_n))
    return dx, dy * xh


def _cast_shard(w, me, cols_p=None):
    _, R, C = w.shape
    cp = C if cols_p is None else cols_p

    def body(me_ref, w_ref, *o_refs):
        v = w_ref[0]
        if cp != C:
            v = jnp.concatenate([v, jnp.zeros((R, cp - C), F32)], axis=1)
        for k in range(DEPTH):
            @pl.when(pl.program_id(0) == k)
            def _():
                o_refs[k][0, 0] = v.astype(_MM)

    return pl.pallas_call(
        body, name="cast_shard", out_shape=[_sds((N_DEV, 1, R, cp), _MM)] * DEPTH,
        grid_spec=pltpu.PrefetchScalarGridSpec(
            num_scalar_prefetch=1, grid=(DEPTH,),
            in_specs=[pl.BlockSpec((1, R, C), lambda l, me: (l, 0, 0))],
            out_specs=[pl.BlockSpec((1, 1, R, cp), lambda l, me: (me[0], 0, 0, 0))] * DEPTH),
        compiler_params=_cparams(("arbitrary",)),
    )(me, w)


def _cast_w_in(w_in, me):
    wt = jnp.transpose(w_in, (2, 0, 1))

    def body(me_ref, wt_ref, *rest):
        o_refs, buf, sem = rest[:DEPTH], rest[DEPTH], rest[DEPTH + 1]
        l = pl.program_id(0)
        cp = pltpu.make_async_copy(wt_ref.at[:, l, :], buf, sem)
        cp.start()
        cp.wait()
        v = jnp.concatenate([buf[...], jnp.zeros((W_IN_SHARD_P - W_IN_SHARD, D_MODEL), F32)], axis=0).T.astype(_MM)
        for k in range(DEPTH):
            @pl.when(l == k)
            def _():
                o_refs[k][0, 0] = v

    return pl.pallas_call(
        body, name="cast_w_in", out_shape=[_sds((N_DEV, 1, D_MODEL, W_IN_SHARD_P), _MM)] * DEPTH,
        grid_spec=pltpu.PrefetchScalarGridSpec(
            num_scalar_prefetch=1, grid=(DEPTH,), in_specs=[_ANY],
            out_specs=[pl.BlockSpec((1, 1, D_MODEL, W_IN_SHARD_P), lambda l, me: (me[0], 0, 0, 0))] * DEPTH,
            scratch_shapes=[pltpu.VMEM((W_IN_SHARD, D_MODEL), F32), pltpu.SemaphoreType.DMA]),
        compiler_params=_cparams(("arbitrary",)),
    )(me, wt)


def _w_in_regroup(w8, after):
    a, b = U_XBC[1], U_XBC[1] + SSD_HEADS

    def body(w_ref, after_ref, o_ref):
        w = jnp.concatenate([w_ref[j, 0][:, :W_IN_SHARD].astype(F32) for j in range(N_DEV)], axis=1)
        r = jnp.concatenate([w[:, :a], w[:, b:], w[:, a:b], jnp.zeros((TM, N_IN_P - N_IN), F32)], axis=1)
        o_ref[...] = r.astype(_MM)

    return pl.pallas_call(
        body, name="w_in_regroup", grid=(D_MODEL // TM,),
        in_specs=[pl.BlockSpec((N_DEV, 1, TM, W_IN_SHARD_P), lambda r: (0, 0, r, 0)), _ANY],
        out_specs=_rows(N_IN_P), out_shape=_sds((D_MODEL, N_IN_P), _MM),
        compiler_params=_cparams(("parallel",)),
    )(w8, after)


def _g_in_split(g):
    a = U_XBC[1]

    def body(g_ref, o_ref):
        v = g_ref[...].astype(F32)
        w = jnp.concatenate([v[:, :a], v[:, U_DT[0]:U_DT[0] + SSD_HEADS], v[:, a:U_DT[0]]], axis=1)
        pad = jnp.zeros((TM, W_IN_SHARD_P - W_IN_SHARD), F32)
        for j in range(N_DEV):
            o_ref[j] = jnp.concatenate([w[:, j * W_IN_SHARD:(j + 1) * W_IN_SHARD], pad], axis=1).astype(_MM)

    return pl.pallas_call(
        body, name="g_in_split", grid=(D_MODEL // TM,),
        in_specs=[_rows(N_IN_P)],
        out_specs=pl.BlockSpec((N_DEV, TM, W_IN_SHARD_P), lambda r: (0, r, 0)),
        out_shape=_sds((N_DEV, D_MODEL, W_IN_SHARD_P), _MM),
        compiler_params=_cparams(("parallel",)),
    )(g)


def _in_proj(x, nw, i, w):
    L = x.shape[0]
    splits = (U_Z, U_XBC, U_QKV, U_GLU, U_DT)

    def body(x_ref, nw_ref, w_ref, h_ref, *out_refs):
        xf = x_ref[...]
        r = lax.rsqrt(jnp.mean(xf * xf, axis=-1, keepdims=True) + RMS_EPS)
        h = (xf * r * nw_ref[0]).astype(_MM)
        h_ref[...] = h
        for ref, (a, b) in zip(out_refs, splits):
            ref[...] = lax.dot_general(h, w_ref[:, a:b], _NN, preferred_element_type=F32)

    return pl.pallas_call(
        body, name="in_proj", grid=(L // TMM,),
        in_specs=[_mrows(D_MODEL), _lp(D_MODEL, i), _full(w.shape)],
        out_specs=[_mrows(D_MODEL)] + [_mrows(b - a) for a, b in splits],
        out_shape=[_sds((L, D_MODEL), _MM)] + [_sds((L, b - a)) for a, b in splits],
        compiler_params=_cparams(("parallel",)),
    )(x, nw, w)


def _mlp_up(x, nw, i, w8):
    L = x.shape[0]

    def body(x_ref, nw_ref, w_ref, h_ref, up_ref, act_ref):
        xf = x_ref[...]
        r = lax.rsqrt(jnp.mean(xf * xf, axis=-1, keepdims=True) + RMS_EPS)
        h = (xf * r * nw_ref[0]).astype(_MM)
        h_ref[...] = h
        for j in range(N_DEV):
            sl = slice(j * FF_SHARD, (j + 1) * FF_SHARD)
            up = lax.dot_general(h, w_ref[j, 0], _NN, preferred_element_type=F32)
            up_ref[:, sl] = up
            act_ref[:, sl] = jnp.square(jnp.maximum(up, 0.0)).astype(_MM)

    return pl.pallas_call(
        body, name="mlp_up", grid=(L // TM,),
        in_specs=[_rows(D_MODEL), _lp(D_MODEL, i), _lw(w8)],
        out_specs=[_rows(D_MODEL), _rows(D_FF), _rows(D_FF)],
        out_shape=[_sds((L, D_MODEL), _MM), _sds((L, D_FF)), _sds((L, D_FF), _MM)],
        compiler_params=_cparams(("parallel",)),
    )(x, nw, w8)


def _mm_res(a, w8, i, res, name):
    L, K = a.shape
    N = w8.shape[3]

    def body(a_ref, w_ref, res_ref, o_ref):
        w = w_ref[:, 0].reshape(K, N)
        o_ref[...] = res_ref[...] + lax.dot_general(a_ref[...], w, _NN, preferred_element_type=F32)

    return pl.pallas_call(
        body, name=name, grid=(L // TMM,),
        in_specs=[_mrows(K), _lw(w8), _mrows(N)],
        out_specs=_mrows(N), out_shape=_sds((L, N)),
        compiler_params=_cparams(("parallel",)),
    )(a, w8, res)


def _out_proj_bwd(a, w8, i, after):
    L = a.shape[0]

    def body(a_ref, w_ref, after_ref, o_ref):
        w = w_ref[:, 0].reshape(D_MIX, D_MODEL)
        o_ref[...] = lax.dot_general(a_ref[...].astype(_MM), w, _NT, preferred_element_type=F32)

    return pl.pallas_call(
        body, name="out_proj_bwd", grid=(L // TMM,),
        in_specs=[_mrows(D_MODEL), _lw(w8), _ANY],
        out_specs=_mrows(D_MIX), out_shape=_sds((L, D_MIX)),
        compiler_params=_cparams(("parallel",)),
    )(a, w8, after)


def _mlp_down_bwd(dy, w8, i, up, after):
    L = dy.shape[0]

    def body(dy_ref, w_ref, up_ref, after_ref, o_ref):
        d = dy_ref[...].astype(_MM)
        for j in range(N_DEV):
            sl = slice(j * FF_SHARD, (j + 1) * FF_SHARD)
            da = lax.dot_general(d, w_ref[j, 0], _NT, preferred_element_type=F32)
            o_ref[:, sl] = (da * (2.0 * jnp.maximum(up_ref[:, sl], 0.0))).astype(_MM)

    return pl.pallas_call(
        body, name="mlp_down_bwd", grid=(L // TMM,),
        in_specs=[_mrows(D_MODEL), _lw(w8), _mrows(D_FF), _ANY],
        out_specs=_mrows(D_FF), out_shape=_sds((L, D_FF), _MM),
        compiler_params=_cparams(("parallel",)),
    )(dy, w8, up, after)


def _rms_bwd_epilogue(dh, res_ref, x_ref, nw_ref, dx_ref, dnw_ref):
    dx, dwx = _rms_bwd(dh, x_ref[...], nw_ref[0], 1.0 / D_MODEL)
    dx_ref[...] = res_ref[...] + dx

    @pl.when(pl.program_id(0) == 0)
    def _():
        dnw_ref[...] = jnp.zeros_like(dnw_ref)

    dnw_ref[...] += jnp.sum(dwx, axis=0, keepdims=True)


def _mlp_up_bwd(d_up, w8, i, res, x, nw):
    L = d_up.shape[0]

    def body(a_ref, w_ref, res_ref, x_ref, nw_ref, dx_ref, dnw_ref):
        dh = jnp.zeros((TMM, D_MODEL), F32)
        for j in range(N_DEV):
            dh = dh + lax.dot_general(a_ref[:, j * FF_SHARD:(j + 1) * FF_SHARD], w_ref[j, 0], _NT, preferred_element_type=F32)
        _rms_bwd_epilogue(dh, res_ref, x_ref, nw_ref, dx_ref, dnw_ref)

    return pl.pallas_call(
        body, name="mlp_up_bwd", grid=(L // TMM,),
        in_specs=[_mrows(D_FF), _lw(w8), _mrows(D_MODEL), _mrows(D_MODEL), _lp(D_MODEL, i)],
        out_specs=[_mrows(D_MODEL), _full((1, D_MODEL))],
        out_shape=[_sds((L, D_MODEL)), _sds((1, D_MODEL))],
        compiler_params=_cparams(("arbitrary",)),
    )(d_up, w8, res, x, nw)


def _in_proj_bwd(pieces, w, res, x, nw, i):
    L = pieces[0].shape[0]
    n = len(pieces)

    def body(*refs):
        w_ref, res_ref, x_ref, nw_ref, dx_ref, dnw_ref, du_ref = refs[n:]
        off = 0
        for r in refs[:n]:
            du_ref[:, off:off + r.shape[1]] = r[...].astype(_MM)
            off += r.shape[1]
        dh = lax.dot_general(du_ref[...], w_ref[...], _NT, preferred_element_type=F32)
        _rms_bwd_epilogue(dh, res_ref, x_ref, nw_ref, dx_ref, dnw_ref)

    return pl.pallas_call(
        body, name="in_proj_bwd", grid=(L // TM,),
        in_specs=[_rows(q.shape[1]) for q in pieces] + [_full(w.shape), _rows(D_MODEL), _rows(D_MODEL), _lp(D_MODEL, i)],
        out_specs=[_rows(D_MODEL), _full((1, D_MODEL)), _rows(N_IN_P)],
        out_shape=[_sds((L, D_MODEL)), _sds((1, D_MODEL)), _sds((L, N_IN_P), _MM)],
        compiler_params=_cparams(("arbitrary",)),
    )(*pieces, w, res, x, nw)


def _mm_tn(a, g, name, tk, tn, out):
    L, K = a.shape
    N = g.shape[1]

    def body(a_ref, g_ref, o_ref):
        r = lax.dot_general(a_ref[...].astype(_MM), g_ref[...].astype(_MM), _TN, preferred_element_type=F32)
        o_ref[...] = r.astype(o_ref.dtype).reshape(o_ref.shape)

    if out == "flat":
        out_spec, out_shape = pl.BlockSpec((tk, tn), lambda i, j: (i, j)), _sds((K, N), _MM)
    elif out == "rows":
        assert tn == N and tk % (K // N_DEV) == 0
        nblk = tk // (K // N_DEV)
        out_spec, out_shape = pl.BlockSpec((nblk, K // N_DEV, N), lambda i, j: (i, 0, 0)), _sds((N_DEV, K // N_DEV, N), _MM)
    else:
        assert tk == K and tn == N // N_DEV
        out_spec, out_shape = pl.BlockSpec((1, K, tn), lambda i, j: (j, 0, 0)), _sds((N_DEV, K, tn), _MM)
    return pl.pallas_call(
        body, name=name, grid=(K // tk, N // tn),
        in_specs=[pl.BlockSpec((L, tk), lambda i, j: (0, i)), pl.BlockSpec((L, tn), lambda i, j: (0, j))],
        out_specs=out_spec, out_shape=out_shape,
        compiler_params=_cparams(("parallel", "parallel")),
    )(a, g)


def _loss_head(y, t):
    L = y.shape[0]

    def body(y_ref, t_ref, dy_ref, l_ref):
        e = y_ref[...] - t_ref[...]
        dy_ref[...] = e * (1.0 / D_MODEL)

        @pl.when(pl.program_id(0) == 0)
        def _():
            l_ref[...] = jnp.zeros_like(l_ref)

        l_ref[...] += jnp.sum(jnp.sum(e * e, axis=1, keepdims=True), axis=0, keepdims=True) * (0.5 / D_MODEL)

    return pl.pallas_call(
        body, name="loss_head", grid=(L // TM,),
        in_specs=[_rows(D_MODEL), _rows(D_MODEL)],
        out_specs=[_rows(D_MODEL), _full((8, LANES))],
        out_shape=[_sds((L, D_MODEL)), _sds((8, LANES))],
        compiler_params=_cparams(("arbitrary",)),
    )(y, t)


EDGE = 32


def _roll_rows(x, s):
    s = s % x.shape[0]
    return x if s == 0 else pltpu.roll(x, s, axis=0)


class _Rolls:
    def __init__(self, x):
        self.x, self.by_phase = x, {}

    def __call__(self, s):
        s = s % self.x.shape[0]
        b = s % 8
        if b not in self.by_phase:
            self.by_phase[b] = _roll_rows(self.x, b)
        return _roll_rows(self.by_phase[b], s - b)


def _conv_taps(x, w_ref, b, k_w):
    def taps(v, zero_fill):
        r = lax.broadcasted_iota(jnp.int32, v.shape, 0)
        acc = jnp.broadcast_to(b, v.shape)
        rolled = _Rolls(v)
        for k in range(k_w):
            s = k_w - 1 - k
            sh = rolled(s)
            if zero_fill and s:
                sh = jnp.where(r >= s, sh, 0.0)
            acc = acc + w_ref[0, k:k + 1, :] * sh
        return acc

    return jnp.concatenate([taps(x[:EDGE], True), taps(x, False)[EDGE:]], axis=0)


def _conv_bwd_taps(x, dc, w_ref, dw_ref, db_ref, k_w):
    n = x.shape[0]
    dc_tail, x_tail, dc_head = dc[n - EDGE:], x[n - EDGE:], dc[:EDGE]
    r = lax.broadcasted_iota(jnp.int32, dc_head.shape, 0)
    dx = jnp.zeros_like(x)
    dx_tail = jnp.zeros_like(dc_tail)
    dc_rolled, x_rolled = _Rolls(dc), _Rolls(x)
    for k in range(k_w):
        s = k_w - 1 - k
        wk = w_ref[0, k:k + 1, :]
        dx = dx + wk * dc_rolled(n - s)
        up = _roll_rows(dc_tail, EDGE - s)
        dx_tail = dx_tail + wk * (jnp.where(r < EDGE - s, up, 0.0) if s else up)
        dw = jnp.sum(dc * x_rolled(s), axis=0, keepdims=True)
        if s:
            dw = dw - jnp.sum(jnp.where(r < s, dc_head * _roll_rows(x_tail, s), 0.0), axis=0, keepdims=True)
        dw_ref[k:k + 1, :] = dw
    db_ref[...] = jnp.sum(dc, axis=0, keepdims=True)
    return jnp.concatenate([dx[:n - EDGE], dx_tail], axis=0)


def _cols(L, cb, off=0):
    return pl.BlockSpec((L, cb), lambda j: (0, j + off))


def _lcols(k, cb, i):
    return pl.BlockSpec((1, k, cb), lambda j: (i, 0, j))


SSD_CB = 256


def _ssd_conv_fwd(x, w, b, i):
    L, C = x.shape
    cb = SSD_CB

    def body(x_ref, w_ref, b_ref, o_ref):
        o_ref[...] = _silu(_conv_taps(x_ref[...], w_ref, b_ref[0], SSD_CONV))

    return pl.pallas_call(
        body, name="ssd_conv_fwd", grid=(C // cb,),
        in_specs=[_cols(L, cb), _lcols(SSD_CONV, cb, i), _lcols(1, cb, i)],
        out_specs=_cols(L, cb), out_shape=_sds((L, C)),
        compiler_params=_cparams(("parallel",)),
    )(x, w, b)


def _ssd_conv_bwd(x, w, b, i, dy):
    L, C = x.shape
    cb = SSD_CB

    def body(x_ref, w_ref, b_ref, dy_ref, dx_ref, dw_ref, db_ref):
        x_ = x_ref[...]
        c = _conv_taps(x_, w_ref, b_ref[0], SSD_CONV)
        dc = dy_ref[...] * _dsilu(c)
        dx_ref[...] = _conv_bwd_taps(x_, dc, w_ref, dw_ref, db_ref, SSD_CONV)

    return pl.pallas_call(
        body, name="ssd_conv_bwd", grid=(C // cb,),
        in_specs=[_cols(L, cb), _lcols(SSD_CONV, cb, i), _lcols(1, cb, i), _cols(L, cb)],
        out_specs=[_cols(L, cb), _cols(SSD_CONV, cb), _cols(1, cb)],
        out_shape=[_sds((L, C)), _sds((SSD_CONV, C)), _sds((1, C))],
        compiler_params=_cparams(("parallel",)),
    )(x, w, b, dy)


def _cm_conv_fwd(glu, w, b, i):
    L = glu.shape[0]
    cb = LANES
    nb = CM_CHANNELS // cb

    def body(a_ref, g_ref, w_ref, b_ref, o_ref):
        h = a_ref[...] * _sigmoid(g_ref[...])
        o_ref[...] = _conv_taps(h, w_ref, b_ref[0], CM_CONV)

    return pl.pallas_call(
        body, name="cm_conv_fwd", grid=(nb,),
        in_specs=[_cols(L, cb), _cols(L, cb, nb), _lcols(CM_CONV, cb, i), _lcols(1, cb, i)],
        out_specs=_cols(L, cb), out_shape=_sds((L, CM_CHANNELS)),
        compiler_params=_cparams(("parallel",)),
    )(glu, glu, w, b)


def _cm_conv_bwd(glu, w, i, dc):
    L = glu.shape[0]
    cb = LANES
    nb = CM_CHANNELS // cb

    def body(a_ref, g_ref, w_ref, dc_ref, da_ref, dg_ref, dw_ref, db_ref):
        a = a_ref[...]
        sg = _sigmoid(g_ref[...])
        dh = _conv_bwd_taps(a * sg, dc_ref[...], w_ref, dw_ref, db_ref, CM_CONV)
        da_ref[...] = dh * sg
        dg_ref[...] = dh * a * sg * (1.0 - sg)

    return pl.pallas_call(
        body, name="cm_conv_bwd", grid=(nb,),
        in_specs=[_cols(L, cb), _cols(L, cb, nb), _lcols(CM_CONV, cb, i), _cols(L, cb)],
        out_specs=[_cols(L, cb), _cols(L, cb), _cols(CM_CONV, cb), _cols(1, cb)],
        out_shape=[_sds((L, CM_CHANNELS)), _sds((L, CM_CHANNELS)), _sds((CM_CONV, CM_CHANNELS)), _sds((1, CM_CHANNELS))],
        compiler_params=_cparams(("parallel",)),
    )(glu, glu, w, dc)


GRP = SSD_WIDTH // SSD_GROUPS


def _mix_post(y, z, attn, c, snw, lw, lb, i):
    L = y.shape[0]

    def body(y_ref, z_ref, a_ref, c_ref, snw_ref, lw_ref, lb_ref, o_ref):
        g = y_ref[...] * _silu(z_ref[...])
        for k in range(SSD_GROUPS):
            sl = slice(k * GRP, (k + 1) * GRP)
            gg = g[:, sl]
            r = lax.rsqrt(jnp.mean(gg * gg, axis=-1, keepdims=True) + RMS_EPS)
            o_ref[:, sl] = (gg * r * snw_ref[0, :, sl]).astype(_MM)
        o_ref[:, SSD_WIDTH:SSD_WIDTH + 512] = a_ref[...].astype(_MM)
        cv = c_ref[...]
        mu = jnp.mean(cv, axis=-1, keepdims=True)
        xc = cv - mu
        rs = lax.rsqrt(jnp.mean(xc * xc, axis=-1, keepdims=True) + LN_EPS)
        o_ref[:, SSD_WIDTH + 512:] = _silu(xc * rs * lw_ref[0] + lb_ref[0]).astype(_MM)

    return pl.pallas_call(
        body, name="mix_post", grid=(L // TM,),
        in_specs=[_rows(SSD_WIDTH), _rows(SSD_WIDTH), _rows(512), _rows(512),
                  _lp(SSD_WIDTH, i), _lp(512, i), _lp(512, i)],
        out_specs=_rows(D_MIX), out_shape=_sds((L, D_MIX), _MM),
        compiler_params=_cparams(("parallel",)),
    )(y, z, attn, c, snw, lw, lb)


def _mix_post_bwd(dcat, y, z, c, snw, lw, lb, i, after):
    L = y.shape[0]

    def body(d_ref, y_ref, z_ref, c_ref, snw_ref, lw_ref, lb_ref, after_ref,
             dy_ref, dz_ref, dc_ref, dsnw_ref, dlw_ref, dlb_ref):
        @pl.when(pl.program_id(0) == 0)
        def _():
            dsnw_ref[...] = jnp.zeros_like(dsnw_ref)
            dlw_ref[...] = jnp.zeros_like(dlw_ref)
            dlb_ref[...] = jnp.zeros_like(dlb_ref)

        yv = y_ref[...]
        zv = z_ref[...]
        sz = _silu(zv)
        g = yv * sz
        for k in range(SSD_GROUPS):
            sl = slice(k * GRP, (k + 1) * GRP)
            dgg, dwx = _rms_bwd(d_ref[:, sl], g[:, sl], snw_ref[0, :, sl], 1.0 / GRP)
            dsnw_ref[:, sl] += jnp.sum(dwx, axis=0, keepdims=True)
            dy_ref[:, sl] = dgg * sz[:, sl]
            dz_ref[:, sl] = dgg * yv[:, sl] * _dsilu(zv[:, sl])
        cv = c_ref[...]
        mu = jnp.mean(cv, axis=-1, keepdims=True)
        xc = cv - mu
        rs = lax.rsqrt(jnp.mean(xc * xc, axis=-1, keepdims=True) + LN_EPS)
        xh = xc * rs
        ln = xh * lw_ref[0] + lb_ref[0]
        dln = d_ref[:, SSD_WIDTH + 512:] * _dsilu(ln)
        dlb_ref[...] += jnp.sum(dln, axis=0, keepdims=True)
        dlw_ref[...] += jnp.sum(dln * xh, axis=0, keepdims=True)
        dxh = dln * lw_ref[0]
        dc_ref[...] = rs * (dxh - jnp.mean(dxh, axis=-1, keepdims=True)
                            - xh * jnp.mean(dxh * xh, axis=-1, keepdims=True))

    return pl.pallas_call(
        body, name="mix_post_bwd", grid=(L // TM,),
        in_specs=[_rows(D_MIX), _rows(SSD_WIDTH), _rows(SSD_WIDTH), _rows(512),
                  _lp(SSD_WIDTH, i), _lp(512, i), _lp(512, i), _ANY],
        out_specs=[_rows(SSD_WIDTH), _rows(SSD_WIDTH), _rows(512), _full((1, SSD_WIDTH)), _full((1, 512)), _full((1, 512))],
        out_shape=[_sds((L, SSD_WIDTH)), _sds((L, SSD_WIDTH)), _sds((L, 512)), _sds((1, SSD_WIDTH)), _sds((1, 512)), _sds((1, 512))],
        compiler_params=_cparams(("arbitrary",)),
    )(dcat, y, z, c, snw, lw, lb, after)


def _seg_mean_matrix():
    i = lax.broadcasted_iota(jnp.int32, (LANES, LANES), 0)
    j = lax.broadcasted_iota(jnp.int32, (LANES, LANES), 1)
    return jnp.where(i // ATTN_HEAD_DIM == j // ATTN_HEAD_DIM, 1.0 / ATTN_HEAD_DIM, 0.0).astype(F32)


def _rot_matrix():
    i = lax.broadcasted_iota(jnp.int32, (LANES, LANES), 0)
    j = lax.broadcasted_iota(jnp.int32, (LANES, LANES), 1)
    half = ATTN_HEAD_DIM // 2
    lo = (j % ATTN_HEAD_DIM) < half
    return jnp.where(lo & (i == j + half), -1.0, jnp.where((~lo) & (i == j - half), 1.0, 0.0)).astype(F32)


N_QK_TILES = 5
QK_W = N_QK_TILES * LANES


def _qk_prep(qkv, qw, kw, i, cos, sin):
    L = qkv.shape[0]

    def body(x_ref, qw_ref, kw_ref, c_ref, s_ref, o_ref):
        m64 = _seg_mean_matrix()
        rot = _rot_matrix()
        cs, sn = c_ref[...], s_ref[...]
        for t in range(N_QK_TILES):
            sl = slice(t * LANES, (t + 1) * LANES)
            x = x_ref[:, sl]
            w = qw_ref[0] if t < 4 else kw_ref[0]
            xn = x * lax.rsqrt(_mmx(x * x, m64) + RMS_EPS) * w
            o_ref[:, sl] = xn * cs + _mmx(xn, rot) * sn

    return pl.pallas_call(
        body, name="qk_prep", grid=(L // TM,),
        in_specs=[_rows(QK_W), _lp(LANES, i), _lp(LANES, i), _rows(LANES), _rows(LANES)],
        out_specs=_rows(QK_W), out_shape=_sds((L, QK_W)),
        compiler_params=_cparams(("parallel",)),
    )(qkv, qw, kw, cos, sin)


def _qk_prep_bwd(dq, dk, qkv, qw, kw, i, cos, sin):
    L = qkv.shape[0]

    def body(dq_ref, dk_ref, x_ref, qw_ref, kw_ref, c_ref, s_ref, dx_ref, dqw_ref, dkw_ref):
        @pl.when(pl.program_id(0) == 0)
        def _():
            dqw_ref[...] = jnp.zeros_like(dqw_ref)
            dkw_ref[...] = jnp.zeros_like(dkw_ref)

        m64 = _seg_mean_matrix()
        rot = _rot_matrix()
        cs, sn = c_ref[...], s_ref[...]
        for t in range(N_QK_TILES):
            sl = slice(t * LANES, (t + 1) * LANES)
            x = x_ref[:, sl]
            dy = dq_ref[:, sl] if t < 4 else dk_ref[...]
            w = qw_ref[0] if t < 4 else kw_ref[0]
            dxn = dy * cs - _mmx(dy * sn, rot)
            r = lax.rsqrt(_mmx(x * x, m64) + RMS_EPS)
            xh = x * r
            dxh = dxn * w
            dx_ref[:, sl] = r * (dxh - xh * _mmx(dxh * xh, m64))
            dw = jnp.sum(dxn * xh, axis=0, keepdims=True)
            if t < 4:
                dqw_ref[...] += dw
            else:
                dkw_ref[...] += dw

    return pl.pallas_call(
        body, name="qk_prep_bwd", grid=(L // TM,),
        in_specs=[_rows(512), _rows(LANES), _rows(QK_W), _lp(LANES, i), _lp(LANES, i), _rows(LANES), _rows(LANES)],
        out_specs=[_rows(QK_W), _full((1, LANES)), _full((1, LANES))],
        out_shape=[_sds((L, QK_W)), _sds((1, LANES)), _sds((1, LANES))],
        compiler_params=_cparams(("arbitrary",)),
    )(dq, dk, qkv, qw, kw, cos, sin)


HPG = 4
SCALE = 1.0 / math.sqrt(ATTN_HEAD_DIM)


def _heads_to_rows(q, g):
    return jnp.concatenate([q[:, (HPG * g + r) * ATTN_HEAD_DIM:(HPG * g + r + 1) * ATTN_HEAD_DIM] for r in range(HPG)], axis=0)


def _rows_to_heads(parts):
    return jnp.concatenate([p[r * Q:(r + 1) * Q] for p in parts for r in range(HPG)], axis=1)


def _attn_probs(q, kc, n, s0, sink_ref, base):
    s = _mm(q, kc, _NT) * SCALE
    rows = lax.broadcasted_iota(jnp.int32, s.shape, 0)
    ki = lax.broadcasted_iota(jnp.int32, s.shape, 1)
    diff = (n * Q + rows % Q) - (s0 + ki)
    s = jnp.where((diff >= 0) & (diff < Q), s, -jnp.inf)
    hrow = lax.broadcasted_iota(jnp.int32, (HPG * Q, 1), 0) // Q
    sink = jnp.zeros((HPG * Q, 1), F32)
    for r in range(HPG):
        sink = jnp.where(hrow == r, sink_ref[base + r], sink)
    m = jnp.maximum(jnp.max(s, axis=1, keepdims=True), sink)
    p = jnp.exp(s - m)
    es = jnp.exp(sink - m)
    inv = 1.0 / (jnp.sum(p, axis=1, keepdims=True) + es)
    return p * inv, es * inv


def _attn_fwd(qk, qkv, sinks, i):
    L = qk.shape[0]

    def body(sink_ref, q_ref, k_ref, v_ref, o_ref):
        n = pl.program_id(0)
        s0 = pl.multiple_of(jnp.maximum(n - 1, 0) * Q, Q)
        q = q_ref[...]
        kc = k_ref[pl.ds(s0, 2 * Q), :]
        vc = v_ref[pl.ds(s0, 2 * Q), :]
        outs = []
        for g in range(2):
            sl = slice(g * ATTN_HEAD_DIM, (g + 1) * ATTN_HEAD_DIM)
            p, _ = _attn_probs(_heads_to_rows(q, g), kc[:, sl], n, s0, sink_ref, i * ATTN_Q_HEADS + g * HPG)
            outs.append(_mm(p, vc[:, sl]))
        o_ref[...] = _rows_to_heads(outs)

    return pl.pallas_call(
        body, name="attn_fwd", grid=(L // Q,),
        in_specs=[pl.BlockSpec(memory_space=pltpu.SMEM), _rows(512, Q),
                  pl.BlockSpec((L, LANES), lambda n: (0, 4)), pl.BlockSpec((L, LANES), lambda n: (0, 5))],
        out_specs=_rows(512, Q), out_shape=_sds((L, 512)),
        compiler_params=_cparams(("parallel",)),
    )(sinks, qk, qk, qkv)


def _attn_bwd(qk, qkv, sinks, i, dcat):
    L = qk.shape[0]

    def body(sink_ref, q_ref, k_ref, v_ref, do_ref, dq_ref, dk_ref, dv_ref, ds_ref):
        n = pl.program_id(0)

        @pl.when(n == 0)
        def _():
            dk_ref[...] = jnp.zeros_like(dk_ref)
            dv_ref[...] = jnp.zeros_like(dv_ref)
            ds_ref[...] = jnp.zeros_like(ds_ref)

        s0 = pl.multiple_of(jnp.maximum(n - 1, 0) * Q, Q)
        q = q_ref[...]
        do_all = do_ref[...]
        kc = k_ref[pl.ds(s0, 2 * Q), :]
        vc = v_ref[pl.ds(s0, 2 * Q), :]
        hrow = lax.broadcasted_iota(jnp.int32, (HPG * Q, 1), 0) // Q
        orow = lax.broadcasted_iota(jnp.int32, (8, LANES), 0)
        dqs, dks, dvs = [], [], []
        acc = jnp.zeros((8, LANES), F32)
        for g in range(2):
            sl = slice(g * ATTN_HEAD_DIM, (g + 1) * ATTN_HEAD_DIM)
            qg = _heads_to_rows(q, g)
            do = _heads_to_rows(do_all, g)
            p, ps = _attn_probs(qg, kc[:, sl], n, s0, sink_ref, i * ATTN_Q_HEADS + g * HPG)
            dp = _mm(do, vc[:, sl], _NT)
            delta = jnp.sum(p * dp, axis=1, keepdims=True)
            ds = p * (dp - delta)
            dqs.append(_mm(ds, kc[:, sl]) * SCALE)
            dks.append(_mm(ds, qg, _TN) * SCALE)
            dvs.append(_mm(p, do, _TN))
            dsink = -(ps * delta)
            for r in range(HPG):
                tot = jnp.sum(jnp.where(hrow == r, dsink, 0.0), axis=0, keepdims=True)
                acc = acc + jnp.where(orow == g * HPG + r, tot, 0.0)
        dq_ref[...] = _rows_to_heads(dqs)
        dk_ref[pl.ds(s0, 2 * Q), :] += jnp.concatenate(dks, axis=1)
        dv_ref[pl.ds(s0, 2 * Q), :] += jnp.concatenate(dvs, axis=1)
        ds_ref[...] += acc

    return pl.pallas_call(
        body, name="attn_bwd", grid=(L // Q,),
        in_specs=[pl.BlockSpec(memory_space=pltpu.SMEM), _rows(512, Q),
                  pl.BlockSpec((L, LANES), lambda n: (0, 4)), pl.BlockSpec((L, LANES), lambda n: (0, 5)),
                  _rows(512, Q, 2)],
        out_specs=[_rows(512, Q), _full((L, LANES)), _full((L, LANES)), _full((8, LANES))],
        out_shape=[_sds((L, 512)), _sds((L, LANES)), _sds((L, LANES)), _sds((8, LANES))],
        compiler_params=_cparams(("arbitrary",)),
    )(sinks, qk, qk, qkv, dcat)


N_PAIR = SSD_HEADS // 2
P = 64
OFF_B = SSD_WIDTH
OFF_C = SSD_WIDTH + SSD_GROUPS * SSD_STATE


def _expand_matrix():
    i = lax.broadcasted_iota(jnp.int32, (LANES, SSD_WIDTH), 0)
    j = lax.broadcasted_iota(jnp.int32, (LANES, SSD_WIDTH), 1)
    return jnp.where(j // P == i, 1.0, 0.0).astype(F32)


def _ssd_chunk_common(dtr_ref, bias_ref, alog_ref):
    dt = jax.nn.softplus(dtr_ref[...] + bias_ref[0])
    a = -jnp.exp(alog_ref[0])
    adt = dt * a
    ri = lax.broadcasted_iota(jnp.int32, (Q, Q), 0)
    ci = lax.broadcasted_iota(jnp.int32, (Q, Q), 1)
    causal = ri >= ci
    tri = jnp.where(causal, 1.0, 0.0).astype(F32)
    acs = _mmx(tri, adt, exact="a")
    em = _expand_matrix()
    acs_e = _mmx(acs, em)
    dt_e = _mmx(dt, em)
    alast_e = acs_e[Q - 1:Q, :]
    return dt, a, acs, causal, tri, em, acs_e, dt_e, alast_e


def _ssd_fwd(xbc, dtr, bias, alog, d_e, i):
    L = xbc.shape[0]
    nc = L // Q

    def body(xbc_ref, dtr_ref, bias_ref, alog_ref, de_ref, y_ref, hp_ref, st_ref):
        @pl.when(pl.program_id(0) == 0)
        def _():
            st_ref[...] = jnp.zeros_like(st_ref)

        dt, a, acs, causal, tri, em, acs_e, dt_e, alast_e = _ssd_chunk_common(dtr_ref, bias_ref, alog_ref)
        acs_t = acs.T
        x = xbc_ref[:, :SSD_WIDTH]
        xdt = x * dt_e
        ea_e = jnp.exp(acs_e)
        xds = xdt * jnp.exp(alast_e - acs_e)
        cd_e = jnp.exp(alast_e)
        lane = lax.broadcasted_iota(jnp.int32, (Q, LANES), 1)
        lo = lane < P
        for g in range(SSD_GROUPS):
            bg = xbc_ref[:, OFF_B + g * SSD_STATE:OFF_B + (g + 1) * SSD_STATE]
            cg = xbc_ref[:, OFF_C + g * SSD_STATE:OFF_C + (g + 1) * SSD_STATE]
            cb = _mm(cg, bg, _NT)
            for pp in range(N_PAIR // SSD_GROUPS):
                pr = g * (N_PAIR // SSD_GROUPS) + pp
                sl = slice(pr * LANES, (pr + 1) * LANES)
                xdt_p = xdt[:, sl]
                yd = jnp.zeros((Q, LANES), F32)
                for half in range(2):
                    h = 2 * pr + half
                    rowb = jnp.broadcast_to(acs_t[h:h + 1, :], (Q, Q))
                    lm = jnp.exp(jnp.where(causal, rowb.T - rowb, -jnp.inf))
                    xh = jnp.where(lo if half == 0 else ~lo, xdt_p, 0.0)
                    yd = yd + _mm(cb * lm, xh)
                hp = st_ref[pr]
                hp_ref[0, pr] = hp
                yoff = _mm(cg, hp) * ea_e[:, sl]
                y_ref[:, sl] = yd + yoff + x[:, sl] * de_ref[0, :, sl]
                st_ref[pr] = hp * cd_e[:, sl] + _mm(bg, xds[:, sl], _TN)

    return pl.pallas_call(
        body, name="ssd_fwd", grid=(nc,),
        in_specs=[_rows(SSD_XBC, Q), _rows(LANES, Q), _lp(LANES, i), _lp(LANES, i), _lp(SSD_WIDTH, i)],
        out_specs=[_rows(SSD_WIDTH, Q), pl.BlockSpec((1, N_PAIR, SSD_STATE, LANES), lambda c: (c, 0, 0, 0))],
        out_shape=[_sds((L, SSD_WIDTH)), _sds((nc, N_PAIR, SSD_STATE, LANES))],
        scratch_shapes=[pltpu.VMEM((N_PAIR, SSD_STATE, LANES), F32)],
        compiler_params=_cparams(("arbitrary",)),
    )(xbc, dtr, bias, alog, d_e)


def _ssd_bwd(xbc, dtr, bias, alog, d_e, i, hprev, dy):
    L = xbc.shape[0]
    nc = L // Q
    rev = lambda c: (nc - 1 - c, 0)

    def body(xbc_ref, dtr_ref, bias_ref, alog_ref, de_ref, hp_ref, dy_ref,
             dxbc_ref, ddtr_ref, dbias_ref, dalog_ref, dd_ref, dst_ref):
        @pl.when(pl.program_id(0) == 0)
        def _():
            dst_ref[...] = jnp.zeros_like(dst_ref)
            dbias_ref[...] = jnp.zeros_like(dbias_ref)
            dalog_ref[...] = jnp.zeros_like(dalog_ref)
            dd_ref[...] = jnp.zeros_like(dd_ref)

        dt, a, acs, causal, tri, em, acs_e, dt_e, alast_e = _ssd_chunk_common(dtr_ref, bias_ref, alog_ref)
        acs_t = acs.T
        x = xbc_ref[:, :SSD_WIDTH]
        dy = dy_ref[...]
        xdt = x * dt_e
        ea_e = jnp.exp(acs_e)
        dse = jnp.exp(alast_e - acs_e)
        xds = xdt * dse
        cd_e = jnp.exp(alast_e)
        lane = lax.broadcasted_iota(jnp.int32, (Q, LANES), 1)
        lo = lane < P
        sub = lax.broadcasted_iota(jnp.int32, (Q, Q), 0)
        lan = lax.broadcasted_iota(jnp.int32, (Q, Q), 1)

        da_rows = jnp.zeros((Q, Q), F32)
        da_cols_t = jnp.zeros((Q, Q), F32)
        dxdt_parts = []
        wyoff_parts = []
        dcd_parts = []
        dxds_parts = []
        for g in range(SSD_GROUPS):
            bg = xbc_ref[:, OFF_B + g * SSD_STATE:OFF_B + (g + 1) * SSD_STATE]
            cg = xbc_ref[:, OFF_C + g * SSD_STATE:OFF_C + (g + 1) * SSD_STATE]
            cb = _mm(cg, bg, _NT)
            dcb = jnp.zeros((Q, Q), F32)
            dcg = jnp.zeros((Q, SSD_STATE), F32)
            dbg = jnp.zeros((Q, SSD_STATE), F32)
            for pp in range(N_PAIR // SSD_GROUPS):
                pr = g * (N_PAIR // SSD_GROUPS) + pp
                sl = slice(pr * LANES, (pr + 1) * LANES)
                xdt_p = xdt[:, sl]
                dy_p = dy[:, sl]
                dxdt_p = jnp.zeros((Q, LANES), F32)
                for half in range(2):
                    h = 2 * pr + half
                    hm = lo if half == 0 else ~lo
                    rowb = jnp.broadcast_to(acs_t[h:h + 1, :], (Q, Q))
                    lm = jnp.exp(jnp.where(causal, rowb.T - rowb, -jnp.inf))
                    m = cb * lm
                    dyh = jnp.where(hm, dy_p, 0.0)
                    gmat = _mm(dyh, xdt_p, _NT)
                    w = gmat * m
                    da_rows = da_rows + jnp.where(lan == h, jnp.sum(w, axis=1, keepdims=True), 0.0)
                    da_cols_t = da_cols_t + jnp.where(sub == h, jnp.sum(w, axis=0, keepdims=True), 0.0)
                    dcb = dcb + gmat * lm
                    dxdt_p = dxdt_p + _mm(m, dyh, _TN)
                hp = hp_ref[0, pr]
                dt_off = dy_p * ea_e[:, sl]
                t_off = _mm(cg, hp)
                wyoff_parts.append(dt_off * t_off)
                dcg = dcg + _mm(dt_off, hp, _NT)
                dhp = _mm(cg, dt_off, _TN)
                dS = dst_ref[pr]
                dxds_p = _mm(bg, dS)
                dbg = dbg + _mm(xds[:, sl], dS, _NT)
                dxds_parts.append(dxds_p)
                dxdt_parts.append(dxdt_p + dxds_p * dse[:, sl])
                dcd_parts.append(jnp.sum(dS * hp, axis=0, keepdims=True))
                dst_ref[pr] = dS * cd_e[:, sl] + dhp
            dcg = dcg + _mm(dcb, bg)
            dbg = dbg + _mm(dcb, cg, _TN)
            dxbc_ref[:, OFF_C + g * SSD_STATE:OFF_C + (g + 1) * SSD_STATE] = dcg
            dxbc_ref[:, OFF_B + g * SSD_STATE:OFF_B + (g + 1) * SSD_STATE] = dbg
        dxdt = jnp.concatenate(dxdt_parts, axis=1)
        dxds = jnp.concatenate(dxds_parts, axis=1)
        wyoff = jnp.concatenate(wyoff_parts, axis=1)
        dcd = jnp.concatenate(dcd_parts, axis=1)
        dxbc_ref[:, :SSD_WIDTH] = dy * de_ref[0] + dxdt * dt_e
        zds = dxds * xds
        dacs = _mmx(wyoff - zds, em, _NT) + da_rows - da_cols_t.T
        dalast = _mmx(jnp.broadcast_to(jnp.sum(zds, axis=0, keepdims=True) + dcd * cd_e, (8, SSD_WIDTH)), em, _NT)[0:1, :]
        dacs = dacs + jnp.where(sub == Q - 1, dalast, 0.0)
        dadt = _mmx(tri, dacs, _TN, exact="a")
        ddt = dadt * a + _mmx(dxdt * x, em, _NT)
        ddtr = ddt * _sigmoid(dtr_ref[...] + bias_ref[0])
        ddtr_ref[...] = ddtr
        row0 = lax.broadcasted_iota(jnp.int32, (8, LANES), 0) == 0
        dbias_ref[...] += jnp.where(row0, jnp.sum(ddtr, axis=0, keepdims=True), 0.0)
        dalog_ref[...] += jnp.where(row0, jnp.sum(dadt * dt, axis=0, keepdims=True) * a, 0.0)
        ddx = _mmx(jnp.broadcast_to(jnp.sum(dy * x, axis=0, keepdims=True), (8, SSD_WIDTH)), em, _NT)
        dd_ref[...] += jnp.where(row0, ddx, 0.0)

    acc = _full((8, LANES))
    return pl.pallas_call(
        body, name="ssd_bwd", grid=(nc,),
        in_specs=[pl.BlockSpec((Q, SSD_XBC), rev), pl.BlockSpec((Q, LANES), rev),
                  _lp(LANES, i), _lp(LANES, i), _lp(SSD_WIDTH, i),
                  pl.BlockSpec((1, N_PAIR, SSD_STATE, LANES), lambda c: (nc - 1 - c, 0, 0, 0)), pl.BlockSpec((Q, SSD_WIDTH), rev)],
        out_specs=[pl.BlockSpec((Q, SSD_XBC), rev), pl.BlockSpec((Q, LANES), rev), acc, acc, acc],
        out_shape=[_sds((L, SSD_XBC)), _sds((L, LANES)), _sds((8, LANES)), _sds((8, LANES)), _sds((8, LANES))],
        scratch_shapes=[pltpu.VMEM((N_PAIR, SSD_STATE, LANES), F32)],
        compiler_params=_cparams(("arbitrary",)),
    )(xbc, dtr, bias, alog, d_e, hprev, dy)


def _rope_tables(L):
    inv_freq = ROPE_THETA ** (-jnp.arange(0, ATTN_HEAD_DIM, 2, dtype=F32) / ATTN_HEAD_DIM)
    ang = jnp.arange(L, dtype=F32)[:, None] * inv_freq[None, :]
    return jnp.tile(jnp.cos(ang), (1, 4)), jnp.tile(jnp.sin(ang), (1, 4))


def _stacked_params(small, conv_w, cm_w):
    row = lambda a: a[:, None, :]
    pad = lambda a: jnp.pad(a, ((0, 0), (0, LANES - a.shape[1])))[:, None, :]
    return dict(
        nw_mix=row(small["norm_mix_w"]), conv_w=conv_w, conv_b=row(small["ssd_conv_b"]),
        dt_bias=pad(small["ssd_dt_bias"]), a_log=pad(small["ssd_a_log"]),
        d_e=row(jnp.repeat(small["ssd_d"], P, axis=1)), snw=row(small["ssd_norm_w"]),
        qw=row(jnp.tile(small["q_norm_w"], (1, 2))), kw=row(jnp.tile(small["k_norm_w"], (1, 2))),
        sinks=small["attn_sinks"].reshape(-1), cm_w=cm_w, cm_b=row(small["cm_dw_b"]),
        ln_w=row(small["cm_ln_w"]), ln_b=row(small["cm_ln_b"]), nw_mlp=row(small["norm_mlp_w"]))


def _layer_fwd(x, p, w_in8, late_weights, i, cos, sin, after):
    w_in = _w_in_regroup(w_in8, after)
    h, z, xbc, qkv, glu, dtr = _in_proj(x, p["nw_mix"], i, w_in)
    xbc_c = _ssd_conv_fwd(xbc, p["conv_w"], p["conv_b"], i)
    y_ssd, hprev = _ssd_fwd(xbc_c, dtr, p["dt_bias"], p["a_log"], p["d_e"], i)
    qk = _qk_prep(qkv, p["qw"], p["kw"], i, cos, sin)
    attn = _attn_fwd(qk, qkv, p["sinks"], i)
    c = _cm_conv_fwd(glu, p["cm_w"], p["cm_b"], i)
    ycat = _mix_post(y_ssd, z, attn, c, p["snw"], p["ln_w"], p["ln_b"], i)
    w_out8, w_up8, w_down8 = late_weights(ycat)
    x1 = _mm_res(ycat, w_out8, i, x, "out_proj")
    hm, up, act = _mlp_up(x1, p["nw_mlp"], i, w_up8)
    x2 = _mm_res(act, w_down8, i, x1, "mlp_down")
    saved = dict(x=x, h=h, z=z, xbc=xbc, qkv=qkv, glu=glu, dtr=dtr, xbc_c=xbc_c, y_ssd=y_ssd, hprev=hprev,
                 qk=qk, c=c, ycat=ycat, x1=x1, hm=hm, up=up, act=act, w_in=w_in,
                 w_out8=w_out8, w_up8=w_up8, w_down8=w_down8)
    return x2, saved


def _layer_bwd_mlp(dx2, p, i, s, after):
    d_up = _mlp_down_bwd(dx2, s["w_down8"], i, s["up"], after)
    g_down = _mm_tn(s["act"], dx2, "dw_down", 512, D_MODEL, "rows")
    g_up = _mm_tn(s["hm"], d_up, "dw_up", D_MODEL, FF_SHARD, "cols")
    dx1, g_nw_mlp = _mlp_up_bwd(d_up, s["w_up8"], i, dx2, s["x1"], p["nw_mlp"])
    dcat = _out_proj_bwd(dx1, s["w_out8"], i, g_nw_mlp)
    g_out = _mm_tn(s["ycat"], dx1, "dw_out", 512, D_MODEL, "rows")
    return dx1, dcat, g_out, g_up, g_down, g_nw_mlp


def _layer_bwd_mix(dx1, dcat, g_nw_mlp, p, i, s, cos, sin, after):
    dy_ssd, dz, dc, g_snw, g_lw, g_lb = _mix_post_bwd(dcat, s["y_ssd"], s["z"], s["c"], p["snw"], p["ln_w"], p["ln_b"], i, after)
    da, dg, g_cmw, g_cmb = _cm_conv_bwd(s["glu"], p["cm_w"], i, dc)
    dq, dk, dv, dsk = _attn_bwd(s["qk"], s["qkv"], p["sinks"], i, dcat)
    dqk_raw, g_qw, g_kw = _qk_prep_bwd(dq, dk, s["qkv"], p["qw"], p["kw"], i, cos, sin)
    dxbc_c, ddtr, g_bias, g_alog, g_d = _ssd_bwd(s["xbc_c"], s["dtr"], p["dt_bias"], p["a_log"], p["d_e"], i, s["hprev"], dy_ssd)
    dxbc, g_convw, g_convb = _ssd_conv_bwd(s["xbc"], p["conv_w"], p["conv_b"], i, dxbc_c)
    dx, g_nw_mix, du = _in_proj_bwd([dz, dxbc, dqk_raw, dv, da, dg, ddtr], s["w_in"], dx1, s["x"], p["nw_mix"], i)
    g_in = _g_in_split(_mm_tn(s["h"], du, "dw_in", 512, 640, "flat"))
    half = ATTN_HEAD_DIM
    small = dict(
        norm_mix_w=g_nw_mix[0], ssd_conv_b=g_convb[0], ssd_dt_bias=g_bias[0, :SSD_HEADS], ssd_a_log=g_alog[0, :SSD_HEADS],
        ssd_d=g_d[0, :SSD_HEADS], ssd_norm_w=g_snw[0], q_norm_w=g_qw[0, :half] + g_qw[0, half:],
        k_norm_w=g_kw[0, :half] + g_kw[0, half:], attn_sinks=dsk[:, 0],
        cm_dw_b=g_cmb[0], cm_ln_w=g_lw[0], cm_ln_b=g_lb[0], norm_mlp_w=g_nw_mlp[0],
        ssd_conv_w=g_convw, cm_dw_w=g_cmw)
    return dx, g_in, small


MESH = pl.DeviceIdType.MESH
_ANY = pl.BlockSpec(memory_space=pl.ANY)


def _coords():
    return lax.axis_index("x"), lax.axis_index("y"), lax.axis_index("c")


def _all_gather(xs, name):
    nt = len(xs)

    def body(*refs):
        x_refs, out_refs = refs[:nt], refs[nt:2 * nt]
        send_sems, recv_sems, local_sems = refs[2 * nt:]
        x, y, c = _coords()
        me, sibling = (x, y, c), (x, y, 1 - c)
        chips = [(1 - x, y), (x, 1 - y), (1 - x, 1 - y)]

        def slot(t, px, py, pc):
            return out_refs[t].at[4 * px + 2 * py + pc]

        def copy(t, k, block, to, src=None):
            return pltpu.make_async_remote_copy(
                src_ref=slot(t, *block) if src is None else src, dst_ref=slot(t, *block),
                send_sem=send_sems.at[7 * t + k], recv_sem=recv_sems.at[7 * t + k], device_id=to, device_id_type=MESH)

        mine = [pltpu.make_async_copy(x_refs[t], slot(t, *me), local_sems.at[t]) for t in range(nt)]
        for cp in mine:
            cp.start()
        first = []
        for t in range(nt):
            first.append(copy(t, 0, me, sibling, src=x_refs[t]))
            first += [copy(t, 1 + j, me, (*chip, c), src=x_refs[t]) for j, chip in enumerate(chips)]
        for cp in first:
            cp.start()
        passed = []
        for j, chip in enumerate(chips):
            for t in range(nt):
                copy(t, 1 + j, (*chip, c), me).wait_recv()
                passed.append(copy(t, 4 + j, (*chip, c), sibling))
                passed[-1].start()
        for t in range(nt):
            copy(t, 0, sibling, me).wait_recv()
            for j, chip in enumerate(chips):
                copy(t, 4 + j, (*chip, 1 - c), me).wait_recv()
        for cp in first + passed:
            cp.wait_send()
        for cp in mine:
            cp.wait()

    return pl.pallas_call(
        body, name=name, out_shape=[_sds((N_DEV,) + a.shape, a.dtype) for a in xs],
        in_specs=[_ANY] * nt, out_specs=[_ANY] * nt,
        scratch_shapes=[pltpu.SemaphoreType.DMA((7 * nt,)), pltpu.SemaphoreType.DMA((7 * nt,)), pltpu.SemaphoreType.DMA((nt,))],
    )(*xs)


def _peer_chips(x, y):
    return [(1 - x, y), (x, 1 - y), (1 - x, 1 - y)]


def _copies_per_tensor(kind):
    return 7 if kind == "scatter" else 3


def _ici_copies(src_refs, land_refs, send_sems, recv_sems, kind):
    x, y, c = _coords()
    flip = lambda v, bit: 1 - v if bit else v
    if kind == "scatter":
        peers = [(flip(x, k & 4), flip(y, k & 2), flip(c, k & 1)) for k in range(1, N_DEV)]
    else:
        peers = [(px, py, c) for px, py in _peer_chips(x, y)]
    index = (lambda px, py, pc: 2 * px + py) if kind == "scatter_chips" else (lambda px, py, pc: 4 * px + 2 * py + pc)
    me = index(x, y, c)
    sends, recvs = [], []
    for t, d in enumerate(land_refs):
        for j, p in enumerate(peers):
            src = d.at[me] if kind == "gather" else src_refs[t].at[index(*p)]
            k = len(peers) * t + j
            sems = dict(send_sem=send_sems.at[k], recv_sem=recv_sems.at[k], device_id=p, device_id_type=MESH)
            sends.append(pltpu.make_async_remote_copy(src_ref=src, dst_ref=d.at[me], **sems))
            recvs.append(pltpu.make_async_remote_copy(src_ref=src, dst_ref=d.at[index(*p)], **sems))
    return sends, recvs


_HBM = pl.BlockSpec(memory_space=pltpu.HBM)
_SEMS = pl.BlockSpec(memory_space=pltpu.SEMAPHORE)
_EFFECT = pltpu.SideEffectType.DATAFLOW_SIDE_EFFECTING


def _hbm(a):
    return pltpu.with_memory_space_constraint(a, pltpu.HBM)


def _ici_start(srcs, lands, after, name, kind):
    ns, n = len(srcs), len(lands)
    nt = ns + n

    def body(*refs):
        sends, _ = _ici_copies(refs[:ns], refs[ns:nt], refs[nt + 1], refs[nt + 2], kind)
        for cp in sends:
            cp.start()
        refs[-1][...] = jnp.zeros_like(refs[-1])

    thru = srcs + lands
    out = pl.pallas_call(
        body, name=name,
        out_shape=(pltpu.SemaphoreType.DMA((_copies_per_tensor(kind) * n,)),) * 2
        + tuple(pltpu.HBM(a.shape, a.dtype) for a in thru) + (_sds((8, LANES)),),
        in_specs=[_HBM] * nt + [_ANY],
        out_specs=(_SEMS, _SEMS) + (_HBM,) * nt + (pl.BlockSpec(memory_space=pltpu.VMEM),),
        input_output_aliases={k: 2 + k for k in range(nt)},
        compiler_params=pltpu.CompilerParams(has_side_effects=_EFFECT),
    )(*[_hbm(a) for a in thru], after)
    return out[0], out[1], list(out[2:2 + ns]), list(out[2 + ns:2 + nt]), out[-1]


def _ici_wait(started, after, name, kind):
    send_sems, recv_sems, srcs, lands, _ = started
    ns, n = len(srcs), len(lands)
    nt = ns + n

    def body(*refs):
        sends, recvs = _ici_copies(refs[:ns], refs[ns:nt], refs[nt], refs[nt + 1], kind)
        for s, r in zip(sends, recvs):
            s.wait_send()
            r.wait_recv()

    thru = srcs + lands
    out = pl.pallas_call(
        body, name=name, out_shape=tuple(pltpu.HBM(a.shape, a.dtype) for a in thru),
        in_specs=[_HBM] * nt + [_SEMS, _SEMS, _ANY], out_specs=(_HBM,) * nt,
        input_output_aliases={k: k for k in range(nt)},
        compiler_params=pltpu.CompilerParams(has_side_effects=_EFFECT),
    )(*thru, send_sems, recv_sems, after)
    return list(out[:ns]), list(out[ns:])


def _ag_d2d(lands):
    n = len(lands)

    def body(*refs):
        in_refs, out_refs = refs[:n], refs[n:2 * n]
        send_sems, recv_sems = refs[2 * n:]
        x, y, c = _coords()
        sends, recvs = [], []
        for t in range(n):
            for k, (px, py) in enumerate([(x, y)] + _peer_chips(x, y)):
                sems = dict(send_sem=send_sems.at[4 * t + k], recv_sem=recv_sems.at[4 * t + k],
                            device_id=(x, y, 1 - c), device_id_type=MESH)
                src = in_refs[t].at[4 * px + 2 * py + c]
                sends.append(pltpu.make_async_remote_copy(src_ref=src, dst_ref=out_refs[t].at[4 * px + 2 * py + c], **sems))
                recvs.append(pltpu.make_async_remote_copy(src_ref=src, dst_ref=out_refs[t].at[4 * px + 2 * py + 1 - c], **sems))
        for cp in sends:
            cp.start()
        for cp in recvs:
            cp.wait_recv()
        for cp in sends:
            cp.wait_send()

    return pl.pallas_call(
        body, name="ag_d2d", out_shape=[_sds(a.shape, a.dtype) for a in lands],
        in_specs=[_ANY] * n, out_specs=[_ANY] * n,
        input_output_aliases={k: k for k in range(n)},
        scratch_shapes=[pltpu.SemaphoreType.DMA((4 * n,)), pltpu.SemaphoreType.DMA((4 * n,))],
    )(*lands)


def _rs_sib(grads):
    nt = len(grads)

    def body(*refs):
        s_refs, ra_refs = refs[:nt], refs[nt:2 * nt]
        send_sems, recv_sems = refs[2 * nt:]
        x, y, c = _coords()
        cps = [pltpu.make_async_remote_copy(
            src_ref=s_refs[t].at[:, 1 - c], dst_ref=ra_refs[t], send_sem=send_sems.at[t], recv_sem=recv_sems.at[t],
            device_id=(x, y, 1 - c), device_id_type=MESH) for t in range(nt)]
        for cp in cps:
            cp.start()
        for cp in cps:
            cp.wait()

    return pl.pallas_call(
        body, name="rs_sibling",
        out_shape=[_sds((4,) + g.shape[2:], g.dtype) for g in grads],
        in_specs=[_ANY] * nt, out_specs=[_ANY] * nt,
        scratch_shapes=[pltpu.SemaphoreType.DMA((nt,)), pltpu.SemaphoreType.DMA((nt,))],
    )(*grads)


def _rs_add(grads, ras, core):
    nt = len(grads)

    def body(c_ref, *refs):
        s_refs, ra_refs, q_refs, rb_refs = refs[:nt], refs[nt:2 * nt], refs[2 * nt:3 * nt], refs[3 * nt:]
        for t in range(nt):
            q = (s_refs[t][0, 0].astype(F32) + ra_refs[t][0].astype(F32)).astype(q_refs[t].dtype)
            q_refs[t][0] = q
            rb_refs[t][0] = q

    nr = 4
    own = [pl.BlockSpec((1, 1, g.shape[2] // nr, g.shape[3]), lambda j, r, c: (j, c[0], r, 0)) for g in grads]
    blk = [pl.BlockSpec((1, g.shape[2] // nr, g.shape[3]), lambda j, r, c: (j, r, 0)) for g in grads]
    return pl.pallas_call(
        body, name="rs_add", out_shape=[_sds(r.shape, r.dtype) for r in ras] * 2,
        grid_spec=pltpu.PrefetchScalarGridSpec(num_scalar_prefetch=1, grid=(4, nr), in_specs=own + blk, out_specs=blk * 2),
        compiler_params=_cparams(("parallel", "parallel")),
    )(core, *grads, *ras)


def _adamw(w, g, m, v):
    m = ADAM_B1 * m + (1.0 - ADAM_B1) * g
    v = ADAM_B2 * v + (1.0 - ADAM_B2) * jnp.square(g)
    m_hat = m / (1.0 - ADAM_B1 ** ADAM_STEP)
    v_hat = v / (1.0 - ADAM_B2 ** ADAM_STEP)
    delta = -ADAM_LR * (m_hat / (jnp.sqrt(v_hat) + ADAM_EPS) + ADAM_WD * w)
    return delta, m, v


def _sum_partials(s_own, rb_ref, me, acc_ref):
    if s_own is None:
        g = rb_ref[0].astype(F32)
        for j in range(1, rb_ref.shape[0]):
            g = g + rb_ref[j].astype(F32)
        acc_ref[...] = g
        return
    for d0 in range(N_DEV):
        @pl.when(me == d0)
        def _():
            g = None
            for d in range(N_DEV):
                term = (s_own if d == d0 else rb_ref[d]).astype(F32)
                g = term if g is None else g + term
            acc_ref[...] = g


def _rs_final(s, rb, me, w, m, v, outs, l):
    _, R, C = w.shape
    cp = rb.shape[2]
    own = [] if s is None else [s]

    def body(me_ref, *refs):
        s_ref = None if s is None else refs[0]
        rb_ref, w_ref, m_ref, v_ref = refs[len(own):len(own) + 4]
        g_ref, d_ref, m2_ref, v2_ref, acc_ref = refs[len(own) + 8:]
        _sum_partials(None if s is None else s_ref[0], rb_ref, me_ref[0], acc_ref)
        g = acc_ref[...][:, :C]
        g_ref[0] = g
        d_ref[0], m2_ref[0], v2_ref[0] = _adamw(w_ref[0], g, m_ref[0], v_ref[0])

    blk = pl.BlockSpec((1, TM, C), lambda r, me: (l, r, 0))
    return pl.pallas_call(
        body, name="rs_final_adamw", out_shape=[_sds(w.shape)] * 4,
        grid_spec=pltpu.PrefetchScalarGridSpec(
            num_scalar_prefetch=1, grid=(R // TM,),
            in_specs=[pl.BlockSpec((1, TM, cp), lambda r, me: (me[0], r, 0))] * len(own)
            + [pl.BlockSpec((rb.shape[0], TM, cp), lambda r, me: (0, r, 0)), blk, blk, blk] + [_ANY] * 4,
            out_specs=[blk] * 4, scratch_shapes=[pltpu.VMEM((TM, cp), F32)]),
        input_output_aliases={5 + len(own) + k: k for k in range(4)},
        compiler_params=_cparams(("parallel",)),
    )(me, *own, rb, w, m, v, *outs)


def _rs_final_w_in(s, rb, me, wt, mt, vt, outs, l):
    shard = (W_IN_SHARD, D_MODEL)
    own = [] if s is None else [s]

    def body(me_ref, rb_ref, *refs):
        wt_ref, mt_ref, vt_ref = refs[len(own):len(own) + 3]
        g_ref, d_ref, m2_ref, v2_ref, sbuf, acc_ref, bufs, obufs, sems = refs[len(own) + 7:]
        me = me_ref[0]
        loads = [pltpu.make_async_copy(src.at[:, l, :], bufs.at[k], sems.at[k]) for k, src in enumerate((wt_ref, mt_ref, vt_ref))]
        if own:
            loads.append(pltpu.make_async_copy(refs[0].at[me], sbuf, sems.at[7]))
        for cp in loads:
            cp.start()
        if own:
            loads[3].wait()
        _sum_partials(sbuf[...] if own else None, rb_ref, me, acc_ref)
        g = acc_ref[...].T[:W_IN_SHARD]
        for cp in loads[:3]:
            cp.wait()
        obufs[0] = g
        obufs[1], obufs[2], obufs[3] = _adamw(bufs[0], g, bufs[1], bufs[2])
        stores = [pltpu.make_async_copy(obufs.at[k], dst.at[:, l, :], sems.at[3 + k])
                  for k, dst in enumerate((g_ref, d_ref, m2_ref, v2_ref))]
        for cp in stores:
            cp.start()
        for cp in stores:
            cp.wait()

    return pl.pallas_call(
        body, name="rs_final_adamw_w_in",
        in_specs=[pl.BlockSpec(memory_space=pltpu.SMEM), pl.BlockSpec(memory_space=pltpu.VMEM)] + [_ANY] * (7 + len(own)),
        out_specs=[_ANY] * 4, out_shape=[_sds(wt.shape)] * 4,
        input_output_aliases={5 + len(own) + k: k for k in range(4)},
        scratch_shapes=[pltpu.VMEM(rb.shape[1:], rb.dtype), pltpu.VMEM(rb.shape[1:], F32),
                        pltpu.VMEM((3,) + shard, F32), pltpu.VMEM((4,) + shard, F32), pltpu.SemaphoreType.DMA((8,))],
        compiler_params=_cparams(),
    )(me, rb, *own, wt, mt, vt, *outs)


def _sum8(g8):
    _, R, C = g8.shape

    def body(g_ref, o_ref):
        acc = g_ref[0]
        for d in range(1, N_DEV):
            acc = acc + g_ref[d]
        o_ref[...] = acc

    return pl.pallas_call(body, name="small_sum", out_shape=_sds((R, C)))(g8)


def _adamw_small(w, g, m, v):
    def body(w_ref, g_ref, m_ref, v_ref, d_ref, m2_ref, v2_ref):
        d_ref[...], m2_ref[...], v2_ref[...] = _adamw(w_ref[...], g_ref[...], m_ref[...], v_ref[...])

    return pl.pallas_call(body, name="small_adamw", out_shape=[_sds(w.shape)] * 3)(w, g, m, v)


REP = (("norm_mix_w", 1024), ("ssd_conv_b", 1536), ("ssd_dt_bias", 16), ("ssd_a_log", 16), ("ssd_d", 16),
       ("ssd_norm_w", 1024), ("q_norm_w", 64), ("k_norm_w", 64), ("attn_sinks", 8), ("cm_dw_b", 512),
       ("cm_ln_w", 512), ("cm_ln_b", 512), ("norm_mlp_w", 1024))
WEIGHTS = ("norm_mix_w", "w_in", "ssd_conv_w", "ssd_conv_b", "ssd_dt_bias", "ssd_a_log", "ssd_d", "ssd_norm_w",
           "q_norm_w", "k_norm_w", "attn_sinks", "cm_dw_w", "cm_dw_b", "cm_ln_w", "cm_ln_b", "w_out", "norm_mlp_w",
           "w_mlp_up", "w_mlp_down")
BIG = ("w_in", "w_out", "w_mlp_up", "w_mlp_down")
N_REP = DEPTH * sum(n for _, n in REP)
CONVW_SHARD = SSD_XBC // N_DEV
CMW_SHARD = CM_CHANNELS // N_DEV


def _to_rows(flat, rows):
    return jnp.pad(flat, (0, rows * LANES - flat.shape[0])).reshape(rows, LANES)


def kernel(x, norm_mix_w, w_in, ssd_conv_w, ssd_conv_b, ssd_dt_bias, ssd_a_log, ssd_d, ssd_norm_w, q_norm_w, k_norm_w, attn_sinks, cm_dw_w, cm_dw_b, cm_ln_w, cm_ln_b, w_out, norm_mlp_w, w_mlp_up, w_mlp_down, loss_target, m_norm_mix_w, m_w_in, m_ssd_conv_w, m_ssd_conv_b, m_ssd_dt_bias, m_ssd_a_log, m_ssd_d, m_ssd_norm_w, m_q_norm_w, m_k_norm_w, m_attn_sinks, m_cm_dw_w, m_cm_dw_b, m_cm_ln_w, m_cm_ln_b, m_w_out, m_norm_mlp_w, m_w_mlp_up, m_w_mlp_down, v_norm_mix_w, v_w_in, v_ssd_conv_w, v_ssd_conv_b, v_ssd_dt_bias, v_ssd_a_log, v_ssd_d, v_ssd_norm_w, v_q_norm_w, v_k_norm_w, v_attn_sinks, v_cm_dw_w, v_cm_dw_b, v_cm_ln_w, v_cm_ln_b, v_w_out, v_norm_mlp_w, v_w_mlp_up, v_w_mlp_down):
    w = dict(norm_mix_w=norm_mix_w, w_in=w_in, ssd_conv_w=ssd_conv_w, ssd_conv_b=ssd_conv_b, ssd_dt_bias=ssd_dt_bias, ssd_a_log=ssd_a_log, ssd_d=ssd_d, ssd_norm_w=ssd_norm_w, q_norm_w=q_norm_w, k_norm_w=k_norm_w, attn_sinks=attn_sinks, cm_dw_w=cm_dw_w, cm_dw_b=cm_dw_b, cm_ln_w=cm_ln_w, cm_ln_b=cm_ln_b, w_out=w_out, norm_mlp_w=norm_mlp_w, w_mlp_up=w_mlp_up, w_mlp_down=w_mlp_down)
    m = dict(norm_mix_w=m_norm_mix_w, w_in=m_w_in, ssd_conv_w=m_ssd_conv_w, ssd_conv_b=m_ssd_conv_b, ssd_dt_bias=m_ssd_dt_bias, ssd_a_log=m_ssd_a_log, ssd_d=m_ssd_d, ssd_norm_w=m_ssd_norm_w, q_norm_w=m_q_norm_w, k_norm_w=m_k_norm_w, attn_sinks=m_attn_sinks, cm_dw_w=m_cm_dw_w, cm_dw_b=m_cm_dw_b, cm_ln_w=m_cm_ln_w, cm_ln_b=m_cm_ln_b, w_out=m_w_out, norm_mlp_w=m_norm_mlp_w, w_mlp_up=m_w_mlp_up, w_mlp_down=m_w_mlp_down)
    v = dict(norm_mix_w=v_norm_mix_w, w_in=v_w_in, ssd_conv_w=v_ssd_conv_w, ssd_conv_b=v_ssd_conv_b, ssd_dt_bias=v_ssd_dt_bias, ssd_a_log=v_ssd_a_log, ssd_d=v_ssd_d, ssd_norm_w=v_ssd_norm_w, q_norm_w=v_q_norm_w, k_norm_w=v_k_norm_w, attn_sinks=v_attn_sinks, cm_dw_w=v_cm_dw_w, cm_dw_b=v_cm_dw_b, cm_ln_w=v_cm_ln_w, cm_ln_b=v_cm_ln_b, w_out=v_w_out, norm_mlp_w=v_norm_mlp_w, w_mlp_up=v_w_mlp_up, w_mlp_down=v_w_mlp_down)
    L = x.shape[1]
    xi, yi, ci = _coords()
    me = 4 * xi + 2 * yi + ci
    n_conv = DEPTH * SSD_CONV * CONVW_SHARD
    n_cm = DEPTH * CM_CONV * CMW_SHARD

    conv_rows = 88
    cw8, = _all_gather([_to_rows(jnp.concatenate([ssd_conv_w.reshape(-1), cm_dw_w.reshape(-1)]), conv_rows)], "ag_conv_w")
    cw8 = cw8.reshape(N_DEV, -1)
    conv_full = cw8[:, :n_conv].reshape(N_DEV, DEPTH, SSD_CONV, CONVW_SHARD).transpose(1, 2, 0, 3).reshape(DEPTH, SSD_CONV, SSD_XBC)
    cm_full = cw8[:, n_conv:n_conv + n_cm].reshape(N_DEV, DEPTH, CM_CONV, CMW_SHARD).transpose(1, 2, 0, 3).reshape(DEPTH, CM_CONV, CM_CHANNELS)
    me1 = jnp.reshape(me, (1,)).astype(jnp.int32)
    casts = [_cast_w_in(w_in, me1), _cast_shard(w_out, me1), _cast_shard(w_mlp_up, me1), _cast_shard(w_mlp_down, me1)]
    shards = [[c[l] for c in casts] for l in range(DEPTH)]

    def gather_start(lands, after):
        return _ici_start([], lands, after, "ag_ici_start", "gather")

    def gather_finish(started, after):
        return _ag_d2d(_ici_wait(started, after, "ag_ici_wait", "gather")[1])

    cos, sin = _rope_tables(L)
    p = _stacked_params({k: w[k] for k, _ in REP}, conv_full, cm_full)
    saved = []
    h = x[0]
    first = gather_start(shards[0][:1], cw8)
    rest0 = gather_start(shards[0][1:], first[4])
    w_in8, = gather_finish(first, rest0[4])
    token, rest = rest0[4], None
    for i in range(DEPTH):
        if i == 0:
            late = lambda ycat: gather_finish(rest0, ycat)
        else:
            late = lambda ycat, r=rest: r
        nxt = None
        if i + 1 < DEPTH:
            nxt = gather_start(shards[i + 1], w_in8)
            token = nxt[4]
        h, s = _layer_fwd(h, p, w_in8, late, i, cos, sin, token)
        saved.append(s)
        if nxt is not None:
            got = gather_finish(nxt, h)
            w_in8, rest = got[0], got[1:]
    d, loss_tile = _loss_head(h, loss_target[0])

    smalls = [None] * DEPTH
    big_out = {k: [lax.empty(w[k].shape, F32) for _ in range(4)] for k in BIG}
    to_t = lambda a: jnp.transpose(a, (2, 0, 1))
    w_in_t = [to_t(t["w_in"]) for t in (w, m, v)]
    big_out["w_in"] = [lax.empty(w_in_t[0].shape, F32) for _ in range(4)]

    core = jnp.reshape(ci, (1,)).astype(jnp.int32)

    def scatter_start(grads, after, l):
        if l > 0:
            lands = [lax.empty(g.shape, g.dtype) for g in grads]
            return _ici_start(list(grads), lands, after, "rs_ici_start", "scatter")
        g4 = [g.reshape((4, 2) + g.shape[1:]) for g in grads]
        out = _rs_add(g4, _rs_sib(g4), core)
        return _ici_start(list(out[:len(g4)]), list(out[len(g4):]), after, "rs_chips_start", "scatter_chips")

    def scatter_finish(started, after, l, names):
        if l > 0:
            srcs, rbs = _ici_wait(started, after, "rs_ici_wait", "scatter")
        else:
            srcs, rbs = [None] * len(names), _ici_wait(started, after, "rs_chips_wait", "scatter_chips")[1]
        for g, rb, k in zip(srcs, rbs, names):
            if k == "w_in":
                big_out[k] = _rs_final_w_in(g, rb, me1, *w_in_t, big_out[k], l)
            else:
                big_out[k] = _rs_final(g, rb, me1, w[k], m[k], v[k], big_out[k], l)

    def gather_small_grads():
        gvec = jnp.concatenate(
            [jnp.stack([smalls[i][k] for i in range(DEPTH)]).reshape(-1) for k, _ in REP]
            + [jnp.stack([smalls[i][k] for i in range(DEPTH)]).reshape(-1) for k in ("ssd_conv_w", "cm_dw_w")]
            + [loss_tile[0, :1]])
        g_rows = -(-gvec.shape[0] // (8 * LANES)) * 8
        return _all_gather([_to_rows(gvec, g_rows)], "ag_small_grads")[0]

    token, pending = loss_tile, []
    for i in reversed(range(DEPTH)):
        dx1, dcat, g_out, g_up, g_down, g_nw_mlp = _layer_bwd_mlp(d, p, i, saved[i], token)
        started = []
        if i == 0:
            started.append((scatter_start([g_out, g_up, g_down], dcat, i), i, BIG[1:]))
        d, g_in, smalls[i] = _layer_bwd_mix(dx1, dcat, g_nw_mlp, p, i, saved[i], cos, sin,
                                            started[0][0][4] if started else g_nw_mlp)
        if i == 0:
            g8 = gather_small_grads()
            started.append((scatter_start([g_in], g8, i), i, BIG[:1]))
        else:
            started.append((scatter_start([g_in, g_out, g_up, g_down], d, i), i, BIG))
        token = started[-1][0][4]
        for st, l, names in pending:
            scatter_finish(st, token, l, names)
        pending = started
    for st, l, names in pending:
        scatter_finish(st, token, l, names)

    gsum = _sum8(g8).reshape(-1)
    o_conv = N_REP
    o_cm = o_conv + DEPTH * SSD_CONV * SSD_XBC
    o_loss = o_cm + DEPTH * CM_CONV * CM_CHANNELS
    g_conv = lax.dynamic_slice_in_dim(gsum[o_conv:o_cm].reshape(DEPTH, SSD_CONV, SSD_XBC), me * CONVW_SHARD, CONVW_SHARD, axis=2)
    g_cm = lax.dynamic_slice_in_dim(gsum[o_cm:o_loss].reshape(DEPTH, CM_CONV, CM_CHANNELS), me * CMW_SHARD, CMW_SHARD, axis=2)
    loss = gsum[o_loss]
    s_rows = -(-(N_REP + n_conv + n_cm) // (8 * LANES)) * 8

    def pack_small(t):
        return _to_rows(jnp.concatenate([t[k].reshape(-1) for k, _ in REP] + [t["ssd_conv_w"].reshape(-1), t["cm_dw_w"].reshape(-1)]), s_rows)

    g_small = _to_rows(jnp.concatenate([gsum[:N_REP], g_conv.reshape(-1), g_cm.reshape(-1)]), s_rows)
    small_out = [g_small] + list(_adamw_small(pack_small(w), g_small, pack_small(m), pack_small(v)))

    def unpack_small(t):
        flat = t.reshape(-1)
        out, off = {}, 0
        for k, n in REP:
            out[k] = flat[off:off + DEPTH * n].reshape(DEPTH, n)
            off += DEPTH * n
        out["ssd_conv_w"] = flat[off:off + n_conv].reshape(DEPTH, SSD_CONV, CONVW_SHARD)
        off += n_conv
        out["cm_dw_w"] = flat[off:off + n_cm].reshape(DEPTH, CM_CONV, CMW_SHARD)
        return out

    outs = [loss, d[None]]
    for j, small_t in enumerate(small_out):
        t = unpack_small(small_t)
        for k in BIG:
            t[k] = big_out[k][j]
        t["w_in"] = jnp.transpose(t["w_in"], (1, 2, 0))
        outs += [t[k] for k in WEIGHTS]
    return tuple(outs)
```

```python
import math

import jax
import jax.numpy as jnp
from jax import lax
from jax.experimental import pallas as pl
from jax.experimental.pallas import tpu as pltpu

F32 = jnp.float32
_MM = jnp.bfloat16

D_MODEL = 1024
DEPTH = 4
SSD_WIDTH = 1024
SSD_HEADS = 16
SSD_STATE = 128
SSD_GROUPS = 2
SSD_CONV = 4
SSD_XBC = 1536
Q = 128
ATTN_HEAD_DIM = 64
ATTN_Q_HEADS = 8
CM_CHANNELS = 512
CM_CONV = 31
D_FF = 4096
D_MIX = 2048
N_IN = 4368
RMS_EPS = 1e-6
LN_EPS = 1e-5
ROPE_THETA = 10000.0
ADAM_LR = 0.001
ADAM_B1 = 0.9
ADAM_B2 = 0.999
ADAM_EPS = 1e-08
ADAM_WD = 0.01
ADAM_STEP = 10

N_DEV = 8
LANES = 128
TM = 256
N_IN_P = 4480
U_Z, U_XBC, U_QKV, U_GLU, U_DT = (0, 1024), (1024, 2560), (2560, 3328), (3328, 4352), (4352, 4480)
W_IN_SHARD = N_IN // N_DEV
W_IN_SHARD_P = 640
FF_SHARD = D_FF // N_DEV
OUT_SHARD = D_MIX // N_DEV

_NN = (((1,), (0,)), ((), ()))
_NT = (((1,), (1,)), ((), ()))
_TN = (((0,), (0,)), ((), ()))
_VMEM_LIMIT = 56 * 1024 * 1024


def _mm(a, b, dims=_NN):
    return lax.dot_general(a.astype(_MM), b.astype(_MM), dims, preferred_element_type=F32)


def _mmx(a, b, dims=_NN, exact="b"):
    m, v = (b, a) if exact == "b" else (a, b)
    m = m.astype(jnp.bfloat16)
    acc = None
    for _ in range(3):
        p = v.astype(jnp.bfloat16)
        v = v - p.astype(F32)
        t = lax.dot_general(p, m, dims, preferred_element_type=F32) if exact == "b" else \
            lax.dot_general(m, p, dims, preferred_element_type=F32)
        acc = t if acc is None else acc + t
    return acc


def _sds(shape, dtype=F32):
    return jax.ShapeDtypeStruct(tuple(shape), dtype)


def _full(shape):
    nd = len(shape)
    return pl.BlockSpec(tuple(shape), lambda *_: (0,) * nd)


def _rows(cols, tm=TM, col=0):
    return pl.BlockSpec((tm, cols), lambda i: (i, col))


TMM = 512


def _mrows(cols):
    return _rows(cols, TMM)


def _lp(n, i):
    return pl.BlockSpec((1, 1, n), lambda *_: (i, 0, 0))


def _lw(arr):
    return pl.BlockSpec(arr.shape, lambda *_: (0, 0, 0, 0))


_ANY = pl.BlockSpec(memory_space=pl.ANY)


def _cparams(sem=None):
    return pltpu.CompilerParams(dimension_semantics=sem, vmem_limit_bytes=_VMEM_LIMIT)


def _sigmoid(x):
    return 1.0 / (1.0 + jnp.exp(-x))


def _silu(x):
    return x * _sigmoid(x)


def _dsilu(x):
    s = _sigmoid(x)
    return s * (1.0 + x * (1.0 - s))


def _rms_bwd(dy, x, w, inv_n):
    r = lax.rsqrt(jnp.sum(x * x, axis=-1, keepdims=True) * inv_n + RMS_EPS)
    xh = x * r
    dxh = dy * w
    dx = r * (dxh - xh * (jnp.sum(dxh * xh, axis=-1, keepdims=True) * inv_n))
    return dx, dy * xh


def _cast_shard(w, me, cols_p=None):
    _, R, C = w.shape
    cp = C if cols_p is None else cols_p

    def body(me_ref, w_ref, *o_refs):
        v = w_ref[0]
        if cp != C:
            v = jnp.concatenate([v, jnp.zeros((R, cp - C), F32)], axis=1)
        for k in range(DEPTH):
            @pl.when(pl.program_id(0) == k)
            def _():
                o_refs[k][0, 0] = v.astype(_MM)

    return pl.pallas_call(
        body, name="cast_shard", out_shape=[_sds((N_DEV, 1, R, cp), _MM)] * DEPTH,
        grid_spec=pltpu.PrefetchScalarGridSpec(
            num_scalar_prefetch=1, grid=(DEPTH,),
            in_specs=[pl.BlockSpec((1, R, C), lambda l, me: (l, 0, 0))],
            out_specs=[pl.BlockSpec((1, 1, R, cp), lambda l, me: (me[0], 0, 0, 0))] * DEPTH),
        compiler_params=_cparams(("arbitrary",)),
    )(me, w)


def _cast_w_in(w_in, me):
    wt = jnp.transpose(w_in, (2, 0, 1))

    def body(me_ref, wt_ref, *rest):
        o_refs, buf, sem = rest[:DEPTH], rest[DEPTH], rest[DEPTH + 1]
        l = pl.program_id(0)
        cp = pltpu.make_async_copy(wt_ref.at[:, l, :], buf, sem)
        cp.start()
        cp.wait()
        v = jnp.concatenate([buf[...], jnp.zeros((W_IN_SHARD_P - W_IN_SHARD, D_MODEL), F32)], axis=0).T.astype(_MM)
        for k in range(DEPTH):
            @pl.when(l == k)
            def _():
                o_refs[k][0, 0] = v

    return pl.pallas_call(
        body, name="cast_w_in", out_shape=[_sds((N_DEV, 1, D_MODEL, W_IN_SHARD_P), _MM)] * DEPTH,
        grid_spec=pltpu.PrefetchScalarGridSpec(
            num_scalar_prefetch=1, grid=(DEPTH,), in_specs=[_ANY],
            out_specs=[pl.BlockSpec((1, 1, D_MODEL, W_IN_SHARD_P), lambda l, me: (me[0], 0, 0, 0))] * DEPTH,
            scratch_shapes=[pltpu.VMEM((W_IN_SHARD, D_MODEL), F32), pltpu.SemaphoreType.DMA]),
        compiler_params=_cparams(("arbitrary",)),
    )(me, wt)


def _w_in_regroup(w8, after):
    a, b = U_XBC[1], U_XBC[1] + SSD_HEADS

    def body(w_ref, after_ref, o_ref):
        w = jnp.concatenate([w_ref[j, 0][:, :W_IN_SHARD].astype(F32) for j in range(N_DEV)], axis=1)
        r = jnp.concatenate([w[:, :a], w[:, b:], w[:, a:b], jnp.zeros((TM, N_IN_P - N_IN), F32)], axis=1)
        o_ref[...] = r.astype(_MM)

    return pl.pallas_call(
        body, name="w_in_regroup", grid=(D_MODEL // TM,),
        in_specs=[pl.BlockSpec((N_DEV, 1, TM, W_IN_SHARD_P), lambda r: (0, 0, r, 0)), _ANY],
        out_specs=_rows(N_IN_P), out_shape=_sds((D_MODEL, N_IN_P), _MM),
        compiler_params=_cparams(("parallel",)),
    )(w8, after)


def _g_in_split(g):
    a = U_XBC[1]

    def body(g_ref, o_ref):
        v = g_ref[...].astype(F32)
        w = jnp.concatenate([v[:, :a], v[:, U_DT[0]:U_DT[0] + SSD_HEADS], v[:, a:U_DT[0]]], axis=1)
        pad = jnp.zeros((TM, W_IN_SHARD_P - W_IN_SHARD), F32)
        for j in range(N_DEV):
            o_ref[j] = jnp.concatenate([w[:, j * W_IN_SHARD:(j + 1) * W_IN_SHARD], pad], axis=1).astype(_MM)

    return pl.pallas_call(
        body, name="g_in_split", grid=(D_MODEL // TM,),
        in_specs=[_rows(N_IN_P)],
        out_specs=pl.BlockSpec((N_DEV, TM, W_IN_SHARD_P), lambda r: (0, r, 0)),
        out_shape=_sds((N_DEV, D_MODEL, W_IN_SHARD_P), _MM),
        compiler_params=_cparams(("parallel",)),
    )(g)


def _in_proj(x, nw, i, w):
    L = x.shape[0]
    splits = (U_Z, U_XBC, U_QKV, U_GLU, U_DT)

    def body(x_ref, nw_ref, w_ref, h_ref, *out_refs):
        xf = x_ref[...]
        r = lax.rsqrt(jnp.mean(xf * xf, axis=-1, keepdims=True) + RMS_EPS)
        h = (xf * r * nw_ref[0]).astype(_MM)
        h_ref[...] = h
        for ref, (a, b) in zip(out_refs, splits):
            ref[...] = lax.dot_general(h, w_ref[:, a:b], _NN, preferred_element_type=F32)

    return pl.pallas_call(
        body, name="in_proj", grid=(L // TMM,),
        in_specs=[_mrows(D_MODEL), _lp(D_MODEL, i), _full(w.shape)],
        out_specs=[_mrows(D_MODEL)] + [_mrows(b - a) for a, b in splits],
        out_shape=[_sds((L, D_MODEL), _MM)] + [_sds((L, b - a)) for a, b in splits],
        compiler_params=_cparams(("parallel",)),
    )(x, nw, w)


def _mlp_up(x, nw, i, w8):
    L = x.shape[0]

    def body(x_ref, nw_ref, w_ref, h_ref, up_ref, act_ref):
        xf = x_ref[...]
        r = lax.rsqrt(jnp.mean(xf * xf, axis=-1, keepdims=True) + RMS_EPS)
        h = (xf * r * nw_ref[0]).astype(_MM)
        h_ref[...] = h
        for j in range(N_DEV):
            sl = slice(j * FF_SHARD, (j + 1) * FF_SHARD)
            up = lax.dot_general(h, w_ref[j, 0], _NN, preferred_element_type=F32)
            up_ref[:, sl] = up
            act_ref[:, sl] = jnp.square(jnp.maximum(up, 0.0)).astype(_MM)

    return pl.pallas_call(
        body, name="mlp_up", grid=(L // TM,),
        in_specs=[_rows(D_MODEL), _lp(D_MODEL, i), _lw(w8)],
        out_specs=[_rows(D_MODEL), _rows(D_FF), _rows(D_FF)],
        out_shape=[_sds((L, D_MODEL), _MM), _sds((L, D_FF)), _sds((L, D_FF), _MM)],
        compiler_params=_cparams(("parallel",)),
    )(x, nw, w8)


def _mm_res(a, w8, i, res, name, after):
    L, K = a.shape
    N = w8.shape[3]

    def body(a_ref, w_ref, res_ref, after_ref, o_ref):
        w = w_ref[:, 0].reshape(K, N)
        o_ref[...] = res_ref[...] + lax.dot_general(a_ref[...], w, _NN, preferred_element_type=F32)

    return pl.pallas_call(
        body, name=name, grid=(L // TMM,),
        in_specs=[_mrows(K), _lw(w8), _mrows(N), _ANY],
        out_specs=_mrows(N), out_shape=_sds((L, N)),
        compiler_params=_cparams(("parallel",)),
    )(a, w8, res, after)


def _out_proj_bwd(a, w8, i, after):
    L = a.shape[0]

    def body(a_ref, w_ref, after_ref, o_ref):
        w = w_ref[:, 0].reshape(D_MIX, D_MODEL)
        o_ref[...] = lax.dot_general(a_ref[...].astype(_MM), w, _NT, preferred_element_type=F32)

    return pl.pallas_call(
        body, name="out_proj_bwd", grid=(L // TMM,),
        in_specs=[_mrows(D_MODEL), _lw(w8), _ANY],
        out_specs=_mrows(D_MIX), out_shape=_sds((L, D_MIX)),
        compiler_params=_cparams(("parallel",)),
    )(a, w8, after)


def _mlp_down_bwd(dy, w8, i, up, after):
    L = dy.shape[0]

    def body(dy_ref, w_ref, up_ref, after_ref, o_ref):
        d = dy_ref[...].astype(_MM)
        for j in range(N_DEV):
            sl = slice(j * FF_SHARD, (j + 1) * FF_SHARD)
            da = lax.dot_general(d, w_ref[j, 0], _NT, preferred_element_type=F32)
            o_ref[:, sl] = (da * (2.0 * jnp.maximum(up_ref[:, sl], 0.0))).astype(_MM)

    return pl.pallas_call(
        body, name="mlp_down_bwd", grid=(L // TMM,),
        in_specs=[_mrows(D_MODEL), _lw(w8), _mrows(D_FF), _ANY],
        out_specs=_mrows(D_FF), out_shape=_sds((L, D_FF), _MM),
        compiler_params=_cparams(("parallel",)),
    )(dy, w8, up, after)


def _rms_bwd_epilogue(dh, res_ref, x_ref, nw_ref, dx_ref, dnw_ref):
    dx, dwx = _rms_bwd(dh, x_ref[...], nw_ref[0], 1.0 / D_MODEL)
    dx_ref[...] = res_ref[...] + dx

    @pl.when(pl.program_id(0) == 0)
    def _():
        dnw_ref[...] = jnp.zeros_like(dnw_ref)

    dnw_ref[...] += jnp.sum(dwx, axis=0, keepdims=True)


def _mlp_up_bwd(d_up, w8, i, res, x, nw):
    L = d_up.shape[0]

    def body(a_ref, w_ref, res_ref, x_ref, nw_ref, dx_ref, dnw_ref):
        dh = jnp.zeros((TMM, D_MODEL), F32)
        for j in range(N_DEV):
            dh = dh + lax.dot_general(a_ref[:, j * FF_SHARD:(j + 1) * FF_SHARD], w_ref[j, 0], _NT, preferred_element_type=F32)
        _rms_bwd_epilogue(dh, res_ref, x_ref, nw_ref, dx_ref, dnw_ref)

    return pl.pallas_call(
        body, name="mlp_up_bwd", grid=(L // TMM,),
        in_specs=[_mrows(D_FF), _lw(w8), _mrows(D_MODEL), _mrows(D_MODEL), _lp(D_MODEL, i)],
        out_specs=[_mrows(D_MODEL), _full((1, D_MODEL))],
        out_shape=[_sds((L, D_MODEL)), _sds((1, D_MODEL))],
        compiler_params=_cparams(("arbitrary",)),
    )(d_up, w8, res, x, nw)


def _in_proj_bwd(pieces, w, res, x, nw, i):
    L = pieces[0].shape[0]
    n = len(pieces)

    def body(*refs):
        w_ref, res_ref, x_ref, nw_ref, dx_ref, dnw_ref, du_ref = refs[n:]
        off = 0
        for r in refs[:n]:
            du_ref[:, off:off + r.shape[1]] = r[...].astype(_MM)
            off += r.shape[1]
        dh = lax.dot_general(du_ref[...], w_ref[...], _NT, preferred_element_type=F32)
        _rms_bwd_epilogue(dh, res_ref, x_ref, nw_ref, dx_ref, dnw_ref)

    return pl.pallas_call(
        body, name="in_proj_bwd", grid=(L // TM,),
        in_specs=[_rows(q.shape[1]) for q in pieces] + [_full(w.shape), _rows(D_MODEL), _rows(D_MODEL), _lp(D_MODEL, i)],
        out_specs=[_rows(D_MODEL), _full((1, D_MODEL)), _rows(N_IN_P)],
        out_shape=[_sds((L, D_MODEL)), _sds((1, D_MODEL)), _sds((L, N_IN_P), _MM)],
        compiler_params=_cparams(("arbitrary",)),
    )(*pieces, w, res, x, nw)


def _mm_tn(a, g, name, tk, tn, out):
    L, K = a.shape
    N = g.shape[1]

    def body(a_ref, g_ref, o_ref):
        r = lax.dot_general(a_ref[...].astype(_MM), g_ref[...].astype(_MM), _TN, preferred_element_type=F32)
        o_ref[...] = r.astype(o_ref.dtype).reshape(o_ref.shape)

    if out == "flat":
        out_spec, out_shape = pl.BlockSpec((tk, tn), lambda i, j: (i, j)), _sds((K, N), _MM)
    elif out == "rows":
        assert tn == N and tk % (K // N_DEV) == 0
        nblk = tk // (K // N_DEV)
        out_spec, out_shape = pl.BlockSpec((nblk, K // N_DEV, N), lambda i, j: (i, 0, 0)), _sds((N_DEV, K // N_DEV, N), _MM)
    else:
        assert tk == K and tn == N // N_DEV
        out_spec, out_shape = pl.BlockSpec((1, K, tn), lambda i, j: (j, 0, 0)), _sds((N_DEV, K, tn), _MM)
    return pl.pallas_call(
        body, name=name, grid=(K // tk, N // tn),
        in_specs=[pl.BlockSpec((L, tk), lambda i, j: (0, i)), pl.BlockSpec((L, tn), lambda i, j: (0, j))],
        out_specs=out_spec, out_shape=out_shape,
        compiler_params=_cparams(("parallel", "parallel")),
    )(a, g)


def _loss_head(y, t):
    L = y.shape[0]

    def body(y_ref, t_ref, dy_ref, l_ref):
        e = y_ref[...] - t_ref[...]
        dy_ref[...] = e * (1.0 / D_MODEL)

        @pl.when(pl.program_id(0) == 0)
        def _():
            l_ref[...] = jnp.zeros_like(l_ref)

        l_ref[...] += jnp.sum(jnp.sum(e * e, axis=1, keepdims=True), axis=0, keepdims=True) * (0.5 / D_MODEL)

    return pl.pallas_call(
        body, name="loss_head", grid=(L // TM,),
        in_specs=[_rows(D_MODEL), _rows(D_MODEL)],
        out_specs=[_rows(D_MODEL), _full((8, LANES))],
        out_shape=[_sds((L, D_MODEL)), _sds((8, LANES))],
        compiler_params=_cparams(("arbitrary",)),
    )(y, t)


EDGE = 32


def _roll_rows(x, s):
    s = s % x.shape[0]
    return x if s == 0 else pltpu.roll(x, s, axis=0)


class _Rolls:
    def __init__(self, x):
        self.x, self.by_phase = x, {}

    def __call__(self, s):
        s = s % self.x.shape[0]
        b = s % 8
        if b not in self.by_phase:
            self.by_phase[b] = _roll_rows(self.x, b)
        return _roll_rows(self.by_phase[b], s - b)


def _conv_taps(x, w_ref, b, k_w):
    def taps(v, zero_fill):
        r = lax.broadcasted_iota(jnp.int32, v.shape, 0)
        acc = jnp.broadcast_to(b, v.shape)
        rolled = _Rolls(v)
        for k in range(k_w):
            s = k_w - 1 - k
            sh = rolled(s)
            if zero_fill and s:
                sh = jnp.where(r >= s, sh, 0.0)
            acc = acc + w_ref[0, k:k + 1, :] * sh
        return acc

    return jnp.concatenate([taps(x[:EDGE], True), taps(x, False)[EDGE:]], axis=0)


def _conv_bwd_taps(x, dc, w_ref, dw_ref, db_ref, k_w):
    n = x.shape[0]
    dc_tail, x_tail, dc_head = dc[n - EDGE:], x[n - EDGE:], dc[:EDGE]
    r = lax.broadcasted_iota(jnp.int32, dc_head.shape, 0)
    dx = jnp.zeros_like(x)
    dx_tail = jnp.zeros_like(dc_tail)
    dc_rolled, x_rolled = _Rolls(dc), _Rolls(x)
    for k in range(k_w):
        s = k_w - 1 - k
        wk = w_ref[0, k:k + 1, :]
        dx = dx + wk * dc_rolled(n - s)
        up = _roll_rows(dc_tail, EDGE - s)
        dx_tail = dx_tail + wk * (jnp.where(r < EDGE - s, up, 0.0) if s else up)
        dw = jnp.sum(dc * x_rolled(s), axis=0, keepdims=True)
        if s:
            dw = dw - jnp.sum(jnp.where(r < s, dc_head * _roll_rows(x_tail, s), 0.0), axis=0, keepdims=True)
        dw_ref[k:k + 1, :] = dw
    db_ref[...] = jnp.sum(dc, axis=0, keepdims=True)
    return jnp.concatenate([dx[:n - EDGE], dx_tail], axis=0)


def _cols(L, cb, off=0):
    return pl.BlockSpec((L, cb), lambda j: (0, j + off))


def _lcols(k, cb, i):
    return pl.BlockSpec((1, k, cb), lambda j: (i, 0, j))


SSD_CB = 256


def _ssd_conv_fwd(x, w, b, i):
    L, C = x.shape
    cb = SSD_CB

    def body(x_ref, w_ref, b_ref, o_ref):
        o_ref[...] = _silu(_conv_taps(x_ref[...], w_ref, b_ref[0], SSD_CONV))

    return pl.pallas_call(
        body, name="ssd_conv_fwd", grid=(C // cb,),
        in_specs=[_cols(L, cb), _lcols(SSD_CONV, cb, i), _lcols(1, cb, i)],
        out_specs=_cols(L, cb), out_shape=_sds((L, C)),
        compiler_params=_cparams(("parallel",)),
    )(x, w, b)


def _ssd_conv_bwd(x, w, b, i, dy):
    L, C = x.shape
    cb = SSD_CB

    def body(x_ref, w_ref, b_ref, dy_ref, dx_ref, dw_ref, db_ref):
        x_ = x_ref[...]
        c = _conv_taps(x_, w_ref, b_ref[0], SSD_CONV)
        dc = dy_ref[...] * _dsilu(c)
        dx_ref[...] = _conv_bwd_taps(x_, dc, w_ref, dw_ref, db_ref, SSD_CONV)

    return pl.pallas_call(
        body, name="ssd_conv_bwd", grid=(C // cb,),
        in_specs=[_cols(L, cb), _lcols(SSD_CONV, cb, i), _lcols(1, cb, i), _cols(L, cb)],
        out_specs=[_cols(L, cb), _cols(SSD_CONV, cb), _cols(1, cb)],
        out_shape=[_sds((L, C)), _sds((SSD_CONV, C)), _sds((1, C))],
        compiler_params=_cparams(("parallel",)),
    )(x, w, b, dy)


def _cm_conv_fwd(glu, w, b, i):
    L = glu.shape[0]
    cb = LANES
    nb = CM_CHANNELS // cb

    def body(a_ref, g_ref, w_ref, b_ref, o_ref):
        h = a_ref[...] * _sigmoid(g_ref[...])
        o_ref[...] = _conv_taps(h, w_ref, b_ref[0], CM_CONV)

    return pl.pallas_call(
        body, name="cm_conv_fwd", grid=(nb,),
        in_specs=[_cols(L, cb), _cols(L, cb, nb), _lcols(CM_CONV, cb, i), _lcols(1, cb, i)],
        out_specs=_cols(L, cb), out_shape=_sds((L, CM_CHANNELS)),
        compiler_params=_cparams(("parallel",)),
    )(glu, glu, w, b)


def _cm_conv_bwd(glu, w, i, dc):
    L = glu.shape[0]
    cb = LANES
    nb = CM_CHANNELS // cb

    def body(a_ref, g_ref, w_ref, dc_ref, da_ref, dg_ref, dw_ref, db_ref):
        a = a_ref[...]
        sg = _sigmoid(g_ref[...])
        dh = _conv_bwd_taps(a * sg, dc_ref[...], w_ref, dw_ref, db_ref, CM_CONV)
        da_ref[...] = dh * sg
        dg_ref[...] = dh * a * sg * (1.0 - sg)

    return pl.pallas_call(
        body, name="cm_conv_bwd", grid=(nb,),
        in_specs=[_cols(L, cb), _cols(L, cb, nb), _lcols(CM_CONV, cb, i), _cols(L, cb)],
        out_specs=[_cols(L, cb), _cols(L, cb), _cols(CM_CONV, cb), _cols(1, cb)],
        out_shape=[_sds((L, CM_CHANNELS)), _sds((L, CM_CHANNELS)), _sds((CM_CONV, CM_CHANNELS)), _sds((1, CM_CHANNELS))],
        compiler_params=_cparams(("parallel",)),
    )(glu, glu, w, dc)


GRP = SSD_WIDTH // SSD_GROUPS


def _mix_post(y, z, attn, c, snw, lw, lb, i):
    L = y.shape[0]

    def body(y_ref, z_ref, a_ref, c_ref, snw_ref, lw_ref, lb_ref, o_ref):
        g = y_ref[...] * _silu(z_ref[...])
        for k in range(SSD_GROUPS):
            sl = slice(k * GRP, (k + 1) * GRP)
            gg = g[:, sl]
            r = lax.rsqrt(jnp.mean(gg * gg, axis=-1, keepdims=True) + RMS_EPS)
            o_ref[:, sl] = (gg * r * snw_ref[0, :, sl]).astype(_MM)
        o_ref[:, SSD_WIDTH:SSD_WIDTH + 512] = a_ref[...].astype(_MM)
        cv = c_ref[...]
        mu = jnp.mean(cv, axis=-1, keepdims=True)
        xc = cv - mu
        rs = lax.rsqrt(jnp.mean(xc * xc, axis=-1, keepdims=True) + LN_EPS)
        o_ref[:, SSD_WIDTH + 512:] = _silu(xc * rs * lw_ref[0] + lb_ref[0]).astype(_MM)

    return pl.pallas_call(
        body, name="mix_post", grid=(L // TM,),
        in_specs=[_rows(SSD_WIDTH), _rows(SSD_WIDTH), _rows(512), _rows(512),
                  _lp(SSD_WIDTH, i), _lp(512, i), _lp(512, i)],
        out_specs=_rows(D_MIX), out_shape=_sds((L, D_MIX), _MM),
        compiler_params=_cparams(("parallel",)),
    )(y, z, attn, c, snw, lw, lb)


def _mix_post_bwd(dcat, y, z, c, snw, lw, lb, i, after):
    L = y.shape[0]

    def body(d_ref, y_ref, z_ref, c_ref, snw_ref, lw_ref, lb_ref, after_ref,
             dy_ref, dz_ref, dc_ref, dsnw_ref, dlw_ref, dlb_ref):
        @pl.when(pl.program_id(0) == 0)
        def _():
            dsnw_ref[...] = jnp.zeros_like(dsnw_ref)
            dlw_ref[...] = jnp.zeros_like(dlw_ref)
            dlb_ref[...] = jnp.zeros_like(dlb_ref)

        yv = y_ref[...]
        zv = z_ref[...]
        sz = _silu(zv)
        g = yv * sz
        for k in range(SSD_GROUPS):
            sl = slice(k * GRP, (k + 1) * GRP)
            dgg, dwx = _rms_bwd(d_ref[:, sl], g[:, sl], snw_ref[0, :, sl], 1.0 / GRP)
            dsnw_ref[:, sl] += jnp.sum(dwx, axis=0, keepdims=True)
            dy_ref[:, sl] = dgg * sz[:, sl]
            dz_ref[:, sl] = dgg * yv[:, sl] * _dsilu(zv[:, sl])
        cv = c_ref[...]
        mu = jnp.mean(cv, axis=-1, keepdims=True)
        xc = cv - mu
        rs = lax.rsqrt(jnp.mean(xc * xc, axis=-1, keepdims=True) + LN_EPS)
        xh = xc * rs
        ln = xh * lw_ref[0] + lb_ref[0]
        dln = d_ref[:, SSD_WIDTH + 512:] * _dsilu(ln)
        dlb_ref[...] += jnp.sum(dln, axis=0, keepdims=True)
        dlw_ref[...] += jnp.sum(dln * xh, axis=0, keepdims=True)
        dxh = dln * lw_ref[0]
        dc_ref[...] = rs * (dxh - jnp.mean(dxh, axis=-1, keepdims=True)
                            - xh * jnp.mean(dxh * xh, axis=-1, keepdims=True))

    return pl.pallas_call(
        body, name="mix_post_bwd", grid=(L // TM,),
        in_specs=[_rows(D_MIX), _rows(SSD_WIDTH), _rows(SSD_WIDTH), _rows(512),
                  _lp(SSD_WIDTH, i), _lp(512, i), _lp(512, i), _ANY],
        out_specs=[_rows(SSD_WIDTH), _rows(SSD_WIDTH), _rows(512), _full((1, SSD_WIDTH)), _full((1, 512)), _full((1, 512))],
        out_shape=[_sds((L, SSD_WIDTH)), _sds((L, SSD_WIDTH)), _sds((L, 512)), _sds((1, SSD_WIDTH)), _sds((1, 512)), _sds((1, 512))],
        compiler_params=_cparams(("arbitrary",)),
    )(dcat, y, z, c, snw, lw, lb, after)


def _seg_mean_matrix():
    i = lax.broadcasted_iota(jnp.int32, (LANES, LANES), 0)
    j = lax.broadcasted_iota(jnp.int32, (LANES, LANES), 1)
    return jnp.where(i // ATTN_HEAD_DIM == j // ATTN_HEAD_DIM, 1.0 / ATTN_HEAD_DIM, 0.0).astype(F32)


def _rot_matrix():
    i = lax.broadcasted_iota(jnp.int32, (LANES, LANES), 0)
    j = lax.broadcasted_iota(jnp.int32, (LANES, LANES), 1)
    half = ATTN_HEAD_DIM // 2
    lo = (j % ATTN_HEAD_DIM) < half
    return jnp.where(lo & (i == j + half), -1.0, jnp.where((~lo) & (i == j - half), 1.0, 0.0)).astype(F32)


N_QK_TILES = 5
QK_W = N_QK_TILES * LANES


def _qk_prep(qkv, qw, kw, i, cos, sin):
    L = qkv.shape[0]

    def body(x_ref, qw_ref, kw_ref, c_ref, s_ref, o_ref):
        m64 = _seg_mean_matrix()
        rot = _rot_matrix()
        cs, sn = c_ref[...], s_ref[...]
        for t in range(N_QK_TILES):
            sl = slice(t * LANES, (t + 1) * LANES)
            x = x_ref[:, sl]
            w = qw_ref[0] if t < 4 else kw_ref[0]
            xn = x * lax.rsqrt(_mmx(x * x, m64) + RMS_EPS) * w
            o_ref[:, sl] = xn * cs + _mmx(xn, rot) * sn

    return pl.pallas_call(
        body, name="qk_prep", grid=(L // TM,),
        in_specs=[_rows(QK_W), _lp(LANES, i), _lp(LANES, i), _rows(LANES), _rows(LANES)],
        out_specs=_rows(QK_W), out_shape=_sds((L, QK_W)),
        compiler_params=_cparams(("parallel",)),
    )(qkv, qw, kw, cos, sin)


def _qk_prep_bwd(dq, dk, qkv, qw, kw, i, cos, sin):
    L = qkv.shape[0]

    def body(dq_ref, dk_ref, x_ref, qw_ref, kw_ref, c_ref, s_ref, dx_ref, dqw_ref, dkw_ref):
        @pl.when(pl.program_id(0) == 0)
        def _():
            dqw_ref[...] = jnp.zeros_like(dqw_ref)
            dkw_ref[...] = jnp.zeros_like(dkw_ref)

        m64 = _seg_mean_matrix()
        rot = _rot_matrix()
        cs, sn = c_ref[...], s_ref[...]
        for t in range(N_QK_TILES):
            sl = slice(t * LANES, (t + 1) * LANES)
            x = x_ref[:, sl]
            dy = dq_ref[:, sl] if t < 4 else dk_ref[...]
            w = qw_ref[0] if t < 4 else kw_ref[0]
            dxn = dy * cs - _mmx(dy * sn, rot)
            r = lax.rsqrt(_mmx(x * x, m64) + RMS_EPS)
            xh = x * r
            dxh = dxn * w
            dx_ref[:, sl] = r * (dxh - xh * _mmx(dxh * xh, m64))
            dw = jnp.sum(dxn * xh, axis=0, keepdims=True)
            if t < 4:
                dqw_ref[...] += dw
            else:
                dkw_ref[...] += dw

    return pl.pallas_call(
        body, name="qk_prep_bwd", grid=(L // TM,),
        in_specs=[_rows(512), _rows(LANES), _rows(QK_W), _lp(LANES, i), _lp(LANES, i), _rows(LANES), _rows(LANES)],
        out_specs=[_rows(QK_W), _full((1, LANES)), _full((1, LANES))],
        out_shape=[_sds((L, QK_W)), _sds((1, LANES)), _sds((1, LANES))],
        compiler_params=_cparams(("arbitrary",)),
    )(dq, dk, qkv, qw, kw, cos, sin)


HPG = 4
SCALE = 1.0 / math.sqrt(ATTN_HEAD_DIM)


def _heads_to_rows(q, g):
    return jnp.concatenate([q[:, (HPG * g + r) * ATTN_HEAD_DIM:(HPG * g + r + 1) * ATTN_HEAD_DIM] for r in range(HPG)], axis=0)


def _rows_to_heads(parts):
    return jnp.concatenate([p[r * Q:(r + 1) * Q] for p in parts for r in range(HPG)], axis=1)


def _attn_probs(q, k_own, k_prev, n, sink_ref, base):
    s_own = _mm(q, k_own, _NT) * SCALE
    s_prev = _mm(q, k_prev, _NT) * SCALE
    own = lax.broadcasted_iota(jnp.int32, s_own.shape, 1) <= lax.broadcasted_iota(jnp.int32, s_own.shape, 0) % Q
    s = jnp.where(own, s_own, jnp.where(n >= 1, s_prev, -jnp.inf))
    hrow = lax.broadcasted_iota(jnp.int32, (HPG * Q, 1), 0) // Q
    sink = jnp.zeros((HPG * Q, 1), F32)
    for r in range(HPG):
        sink = jnp.where(hrow == r, sink_ref[base + r], sink)
    m = jnp.maximum(jnp.max(s, axis=1, keepdims=True), sink)
    p = jnp.exp(s - m)
    es = jnp.exp(sink - m)
    inv = 1.0 / (jnp.sum(p, axis=1, keepdims=True) + es)
    return p * inv, own, es * inv


def _kv_blocks(ref, n):
    own = ref[pl.ds(pl.multiple_of(n * Q, Q), Q), :]
    prev = ref[pl.ds(pl.multiple_of(jnp.maximum(n - 1, 0) * Q, Q), Q), :]
    return own, prev


def _attn_fwd(qk, qkv, sinks, i):
    L = qk.shape[0]

    def body(sink_ref, q_ref, k_ref, v_ref, o_ref):
        n = pl.program_id(0)
        q = q_ref[...]
        k_own, k_prev = _kv_blocks(k_ref, n)
        v_own, v_prev = _kv_blocks(v_ref, n)
        outs = []
        for g in range(2):
            sl = slice(g * ATTN_HEAD_DIM, (g + 1) * ATTN_HEAD_DIM)
            p, own, _ = _attn_probs(_heads_to_rows(q, g), k_own[:, sl], k_prev[:, sl], n, sink_ref, i * ATTN_Q_HEADS + g * HPG)
            outs.append(_mm(jnp.where(own, p, 0.0), v_own[:, sl]) + _mm(jnp.where(own, 0.0, p), v_prev[:, sl]))
        o_ref[...] = _rows_to_heads(outs)

    return pl.pallas_call(
        body, name="attn_fwd", grid=(L // Q,),
        in_specs=[pl.BlockSpec(memory_space=pltpu.SMEM), _rows(512, Q),
                  pl.BlockSpec((L, LANES), lambda n: (0, 4)), pl.BlockSpec((L, LANES), lambda n: (0, 5))],
        out_specs=_rows(512, Q), out_shape=_sds((L, 512)),
        compiler_params=_cparams(("parallel",)),
    )(sinks, qk, qk, qkv)


def _attn_bwd(qk, qkv, sinks, i, dcat):
    L = qk.shape[0]

    def body(sink_ref, q_ref, k_ref, v_ref, do_ref, dq_ref, dk_ref, dv_ref, ds_ref):
        n = pl.program_id(0)

        @pl.when(n == 0)
        def _():
            dk_ref[...] = jnp.zeros_like(dk_ref)
            dv_ref[...] = jnp.zeros_like(dv_ref)
            ds_ref[...] = jnp.zeros_like(ds_ref)

        q = q_ref[...]
        do_all = do_ref[...]
        k_own, k_prev = _kv_blocks(k_ref, n)
        v_own, v_prev = _kv_blocks(v_ref, n)
        hrow = lax.broadcasted_iota(jnp.int32, (HPG * Q, 1), 0) // Q
        orow = lax.broadcasted_iota(jnp.int32, (8, LANES), 0)
        dqs, dks, dvs = [], [[], []], [[], []]
        acc = jnp.zeros((8, LANES), F32)
        for g in range(2):
            sl = slice(g * ATTN_HEAD_DIM, (g + 1) * ATTN_HEAD_DIM)
            qg = _heads_to_rows(q, g)
            do = _heads_to_rows(do_all, g)
            p, own, ps = _attn_probs(qg, k_own[:, sl], k_prev[:, sl], n, sink_ref, i * ATTN_Q_HEADS + g * HPG)
            dp = jnp.where(own, _mm(do, v_own[:, sl], _NT), _mm(do, v_prev[:, sl], _NT))
            delta = jnp.sum(p * dp, axis=1, keepdims=True)
            ds = p * (dp - delta)
            parts = ((jnp.where(own, ds, 0.0), jnp.where(own, p, 0.0), k_own), (jnp.where(own, 0.0, ds), jnp.where(own, 0.0, p), k_prev))
            dqs.append((_mm(parts[0][0], k_own[:, sl]) + _mm(parts[1][0], k_prev[:, sl])) * SCALE)
            for b, (ds_b, p_b, _) in enumerate(parts):
                dks[b].append(_mm(ds_b, qg, _TN) * SCALE)
                dvs[b].append(_mm(p_b, do, _TN))
            dsink = -(ps * delta)
            for r in range(HPG):
                tot = jnp.sum(jnp.where(hrow == r, dsink, 0.0), axis=0, keepdims=True)
                acc = acc + jnp.where(orow == g * HPG + r, tot, 0.0)
        dq_ref[...] = _rows_to_heads(dqs)
        so = pl.multiple_of(n * Q, Q)
        sp = pl.multiple_of(jnp.maximum(n - 1, 0) * Q, Q)
        dk_ref[pl.ds(so, Q), :] += jnp.concatenate(dks[0], axis=1)
        dv_ref[pl.ds(so, Q), :] += jnp.concatenate(dvs[0], axis=1)
        dk_ref[pl.ds(sp, Q), :] += jnp.concatenate(dks[1], axis=1)
        dv_ref[pl.ds(sp, Q), :] += jnp.concatenate(dvs[1], axis=1)
        ds_ref[...] += acc

    return pl.pallas_call(
        body, name="attn_bwd", grid=(L // Q,),
        in_specs=[pl.BlockSpec(memory_space=pltpu.SMEM), _rows(512, Q),
                  pl.BlockSpec((L, LANES), lambda n: (0, 4)), pl.BlockSpec((L, LANES), lambda n: (0, 5)),
                  _rows(512, Q, 2)],
        out_specs=[_rows(512, Q), _full((L, LANES)), _full((L, LANES)), _full((8, LANES))],
        out_shape=[_sds((L, 512)), _sds((L, LANES)), _sds((L, LANES)), _sds((8, LANES))],
        compiler_params=_cparams(("arbitrary",)),
    )(sinks, qk, qk, qkv, dcat)


N_PAIR = SSD_HEADS // 2
P = 64
OFF_B = SSD_WIDTH
OFF_C = SSD_WIDTH + SSD_GROUPS * SSD_STATE


def _expand_matrix():
    i = lax.broadcasted_iota(jnp.int32, (LANES, SSD_WIDTH), 0)
    j = lax.broadcasted_iota(jnp.int32, (LANES, SSD_WIDTH), 1)
    return jnp.where(j // P == i, 1.0, 0.0).astype(F32)


def _ssd_chunk_common(dtr_ref, bias_ref, alog_ref):
    dt = jax.nn.softplus(dtr_ref[...] + bias_ref[0])
    a = -jnp.exp(alog_ref[0])
    adt = dt * a
    ri = lax.broadcasted_iota(jnp.int32, (Q, Q), 0)
    ci = lax.broadcasted_iota(jnp.int32, (Q, Q), 1)
    causal = ri >= ci
    tri = jnp.where(causal, 1.0, 0.0).astype(F32)
    acs = _mmx(tri, adt, exact="a")
    em = _expand_matrix()
    acs_e = _mmx(acs, em)
    dt_e = _mmx(dt, em)
    alast_e = acs_e[Q - 1:Q, :]
    return dt, a, acs, causal, tri, em, acs_e, dt_e, alast_e


def _ssd_fwd(xbc, dtr, bias, alog, d_e, i):
    L = xbc.shape[0]
    nc = L // Q

    def body(xbc_ref, dtr_ref, bias_ref, alog_ref, de_ref, y_ref, hp_ref, st_ref):
        @pl.when(pl.program_id(0) == 0)
        def _():
            st_ref[...] = jnp.zeros_like(st_ref)

        dt, a, acs, causal, tri, em, acs_e, dt_e, alast_e = _ssd_chunk_common(dtr_ref, bias_ref, alog_ref)
        acs_t = acs.T
        x = xbc_ref[:, :SSD_WIDTH]
        xdt = x * dt_e
        ea_e = jnp.exp(acs_e)
        xds = xdt * jnp.exp(alast_e - acs_e)
        cd_e = jnp.exp(alast_e)
        lane = lax.broadcasted_iota(jnp.int32, (Q, LANES), 1)
        lo = lane < P
        for g in range(SSD_GROUPS):
            bg = xbc_ref[:, OFF_B + g * SSD_STATE:OFF_B + (g + 1) * SSD_STATE]
            cg = xbc_ref[:, OFF_C + g * SSD_STATE:OFF_C + (g + 1) * SSD_STATE]
            cb = _mm(cg, bg, _NT)
            for pp in range(N_PAIR // SSD_GROUPS):
                pr = g * (N_PAIR // SSD_GROUPS) + pp
                sl = slice(pr * LANES, (pr + 1) * LANES)
                xdt_p = xdt[:, sl]
                yd = jnp.zeros((Q, LANES), F32)
                for half in range(2):
                    h = 2 * pr + half
                    rowb = jnp.broadcast_to(acs_t[h:h + 1, :], (Q, Q))
                    lm = jnp.exp(jnp.where(causal, rowb.T - rowb, -jnp.inf))
                    xh = jnp.where(lo if half == 0 else ~lo, xdt_p, 0.0)
                    yd = yd + _mm(cb * lm, xh)
                hp = st_ref[pr]
                hp_ref[0, pr] = hp
                yoff = _mm(cg, hp) * ea_e[:, sl]
                y_ref[:, sl] = yd + yoff + x[:, sl] * de_ref[0, :, sl]
                st_ref[pr] = hp * cd_e[:, sl] + _mm(bg, xds[:, sl], _TN)

    return pl.pallas_call(
        body, name="ssd_fwd", grid=(nc,),
        in_specs=[_rows(SSD_XBC, Q), _rows(LANES, Q), _lp(LANES, i), _lp(LANES, i), _lp(SSD_WIDTH, i)],
        out_specs=[_rows(SSD_WIDTH, Q), pl.BlockSpec((1, N_PAIR, SSD_STATE, LANES), lambda c: (c, 0, 0, 0))],
        out_shape=[_sds((L, SSD_WIDTH)), _sds((nc, N_PAIR, SSD_STATE, LANES))],
        scratch_shapes=[pltpu.VMEM((N_PAIR, SSD_STATE, LANES), F32)],
        compiler_params=_cparams(("arbitrary",)),
    )(xbc, dtr, bias, alog, d_e)


def _ssd_bwd(xbc, dtr, bias, alog, d_e, i, hprev, dy):
    L = xbc.shape[0]
    nc = L // Q
    rev = lambda c: (nc - 1 - c, 0)

    def body(xbc_ref, dtr_ref, bias_ref, alog_ref, de_ref, hp_ref, dy_ref,
             dxbc_ref, ddtr_ref, dbias_ref, dalog_ref, dd_ref, dst_ref):
        @pl.when(pl.program_id(0) == 0)
        def _():
            dst_ref[...] = jnp.zeros_like(dst_ref)
            dbias_ref[...] = jnp.zeros_like(dbias_ref)
            dalog_ref[...] = jnp.zeros_like(dalog_ref)
            dd_ref[...] = jnp.zeros_like(dd_ref)

        dt, a, acs, causal, tri, em, acs_e, dt_e, alast_e = _ssd_chunk_common(dtr_ref, bias_ref, alog_ref)
        acs_t = acs.T
        x = xbc_ref[:, :SSD_WIDTH]
        dy = dy_ref[...]
        xdt = x * dt_e
        ea_e = jnp.exp(acs_e)
        dse = jnp.exp(alast_e - acs_e)
        xds = xdt * dse
        cd_e = jnp.exp(alast_e)
        lane = lax.broadcasted_iota(jnp.int32, (Q, LANES), 1)
        lo = lane < P
        sub = lax.broadcasted_iota(jnp.int32, (Q, Q), 0)
        lan = lax.broadcasted_iota(jnp.int32, (Q, Q), 1)

        da_rows = jnp.zeros((Q, Q), F32)
        da_cols_t = jnp.zeros((Q, Q), F32)
        dxdt_parts = []
        wyoff_parts = []
        dcd_parts = []
        dxds_parts = []
        for g in range(SSD_GROUPS):
            bg = xbc_ref[:, OFF_B + g * SSD_STATE:OFF_B + (g + 1) * SSD_STATE]
            cg = xbc_ref[:, OFF_C + g * SSD_STATE:OFF_C + (g + 1) * SSD_STATE]
            cb = _mm(cg, bg, _NT)
            dcb = jnp.zeros((Q, Q), F32)
            dcg = jnp.zeros((Q, SSD_STATE), F32)
            dbg = jnp.zeros((Q, SSD_STATE), F32)
            for pp in range(N_PAIR // SSD_GROUPS):
                pr = g * (N_PAIR // SSD_GROUPS) + pp
                sl = slice(pr * LANES, (pr + 1) * LANES)
                xdt_p = xdt[:, sl]
                dy_p = dy[:, sl]
                dxdt_p = jnp.zeros((Q, LANES), F32)
                for half in range(2):
                    h = 2 * pr + half
                    hm = lo if half == 0 else ~lo
                    rowb = jnp.broadcast_to(acs_t[h:h + 1, :], (Q, Q))
                    lm = jnp.exp(jnp.where(causal, rowb.T - rowb, -jnp.inf))
                    m = cb * lm
                    dyh = jnp.where(hm, dy_p, 0.0)
                    gmat = _mm(dyh, xdt_p, _NT)
                    w = gmat * m
                    da_rows = da_rows + jnp.where(lan == h, jnp.sum(w, axis=1, keepdims=True), 0.0)
                    da_cols_t = da_cols_t + jnp.where(sub == h, jnp.sum(w, axis=0, keepdims=True), 0.0)
                    dcb = dcb + gmat * lm
                    dxdt_p = dxdt_p + _mm(m, dyh, _TN)
                hp = hp_ref[0, pr]
                dt_off = dy_p * ea_e[:, sl]
                t_off = _mm(cg, hp)
                wyoff_parts.append(dt_off * t_off)
                dcg = dcg + _mm(dt_off, hp, _NT)
                dhp = _mm(cg, dt_off, _TN)
                dS = dst_ref[pr]
                dxds_p = _mm(bg, dS)
                dbg = dbg + _mm(xds[:, sl], dS, _NT)
                dxds_parts.append(dxds_p)
                dxdt_parts.append(dxdt_p + dxds_p * dse[:, sl])
                dcd_parts.append(jnp.sum(dS * hp, axis=0, keepdims=True))
                dst_ref[pr] = dS * cd_e[:, sl] + dhp
            dcg = dcg + _mm(dcb, bg)
            dbg = dbg + _mm(dcb, cg, _TN)
            dxbc_ref[:, OFF_C + g * SSD_STATE:OFF_C + (g + 1) * SSD_STATE] = dcg
            dxbc_ref[:, OFF_B + g * SSD_STATE:OFF_B + (g + 1) * SSD_STATE] = dbg
        dxdt = jnp.concatenate(dxdt_parts, axis=1)
        dxds = jnp.concatenate(dxds_parts, axis=1)
        wyoff = jnp.concatenate(wyoff_parts, axis=1)
        dcd = jnp.concatenate(dcd_parts, axis=1)
        dxbc_ref[:, :SSD_WIDTH] = dy * de_ref[0] + dxdt * dt_e
        zds = dxds * xds
        dacs = _mmx(wyoff - zds, em, _NT) + da_rows - da_cols_t.T
        dalast = _mmx(jnp.broadcast_to(jnp.sum(zds, axis=0, keepdims=True) + dcd * cd_e, (8, SSD_WIDTH)), em, _NT)[0:1, :]
        dacs = dacs + jnp.where(sub == Q - 1, dalast, 0.0)
        dadt = _mmx(tri, dacs, _TN, exact="a")
        ddt = dadt * a + _mmx(dxdt * x, em, _NT)
        ddtr = ddt * _sigmoid(dtr_ref[...] + bias_ref[0])
        ddtr_ref[...] = ddtr
        row0 = lax.broadcasted_iota(jnp.int32, (8, LANES), 0) == 0
        dbias_ref[...] += jnp.where(row0, jnp.sum(ddtr, axis=0, keepdims=True), 0.0)
        dalog_ref[...] += jnp.where(row0, jnp.sum(dadt * dt, axis=0, keepdims=True) * a, 0.0)
        ddx = _mmx(jnp.broadcast_to(jnp.sum(dy * x, axis=0, keepdims=True), (8, SSD_WIDTH)), em, _NT)
        dd_ref[...] += jnp.where(row0, ddx, 0.0)

    acc = _full((8, LANES))
    return pl.pallas_call(
        body, name="ssd_bwd", grid=(nc,),
        in_specs=[pl.BlockSpec((Q, SSD_XBC), rev), pl.BlockSpec((Q, LANES), rev),
                  _lp(LANES, i), _lp(LANES, i), _lp(SSD_WIDTH, i),
                  pl.BlockSpec((1, N_PAIR, SSD_STATE, LANES), lambda c: (nc - 1 - c, 0, 0, 0)), pl.BlockSpec((Q, SSD_WIDTH), rev)],
        out_specs=[pl.BlockSpec((Q, SSD_XBC), rev), pl.BlockSpec((Q, LANES), rev), acc, acc, acc],
        out_shape=[_sds((L, SSD_XBC)), _sds((L, LANES)), _sds((8, LANES)), _sds((8, LANES)), _sds((8, LANES))],
        scratch_shapes=[pltpu.VMEM((N_PAIR, SSD_STATE, LANES), F32)],
        compiler_params=_cparams(("arbitrary",)),
    )(xbc, dtr, bias, alog, d_e, hprev, dy)


def _rope_tables(L):
    inv_freq = ROPE_THETA ** (-jnp.arange(0, ATTN_HEAD_DIM, 2, dtype=F32) / ATTN_HEAD_DIM)
    ang = jnp.arange(L, dtype=F32)[:, None] * inv_freq[None, :]
    return jnp.tile(jnp.cos(ang), (1, 4)), jnp.tile(jnp.sin(ang), (1, 4))


def _stacked_params(small, conv_w, cm_w):
    row = lambda a: a[:, None, :]
    pad = lambda a: jnp.pad(a, ((0, 0), (0, LANES - a.shape[1])))[:, None, :]
    return dict(
        nw_mix=row(small["norm_mix_w"]), conv_w=conv_w, conv_b=row(small["ssd_conv_b"]),
        dt_bias=pad(small["ssd_dt_bias"]), a_log=pad(small["ssd_a_log"]),
        d_e=row(jnp.repeat(small["ssd_d"], P, axis=1)), snw=row(small["ssd_norm_w"]),
        qw=row(jnp.tile(small["q_norm_w"], (1, 2))), kw=row(jnp.tile(small["k_norm_w"], (1, 2))),
        sinks=small["attn_sinks"].reshape(-1), cm_w=cm_w, cm_b=row(small["cm_dw_b"]),
        ln_w=row(small["cm_ln_w"]), ln_b=row(small["cm_ln_b"]), nw_mlp=row(small["norm_mlp_w"]))


def _layer_fwd(x, p, w_in8, late_weights, before_down, i, cos, sin, after):
    w_in = _w_in_regroup(w_in8, after)
    h, z, xbc, qkv, glu, dtr = _in_proj(x, p["nw_mix"], i, w_in)
    xbc_c = _ssd_conv_fwd(xbc, p["conv_w"], p["conv_b"], i)
    y_ssd, hprev = _ssd_fwd(xbc_c, dtr, p["dt_bias"], p["a_log"], p["d_e"], i)
    qk = _qk_prep(qkv, p["qw"], p["kw"], i, cos, sin)
    attn = _attn_fwd(qk, qkv, p["sinks"], i)
    c = _cm_conv_fwd(glu, p["cm_w"], p["cm_b"], i)
    ycat = _mix_post(y_ssd, z, attn, c, p["snw"], p["ln_w"], p["ln_b"], i)
    w_out8, w_up8, w_down8 = late_weights(ycat)
    x1 = _mm_res(ycat, w_out8, i, x, "out_proj", ycat)
    hm, up, act = _mlp_up(x1, p["nw_mlp"], i, w_up8)
    x2 = _mm_res(act, w_down8, i, x1, "mlp_down", before_down(act))
    saved = dict(x=x, h=h, z=z, xbc=xbc, qkv=qkv, glu=glu, dtr=dtr, xbc_c=xbc_c, y_ssd=y_ssd, hprev=hprev,
                 qk=qk, c=c, ycat=ycat, x1=x1, hm=hm, up=up, act=act, w_in=w_in,
                 w_out8=w_out8, w_up8=w_up8, w_down8=w_down8)
    return x2, saved


def _layer_bwd_mlp(dx2, p, i, s, after):
    d_up = _mlp_down_bwd(dx2, s["w_down8"], i, s["up"], after)
    g_down = _mm_tn(s["act"], dx2, "dw_down", 512, D_MODEL, "rows")
    g_up = _mm_tn(s["hm"], d_up, "dw_up", D_MODEL, FF_SHARD, "cols")
    dx1, g_nw_mlp = _mlp_up_bwd(d_up, s["w_up8"], i, dx2, s["x1"], p["nw_mlp"])
    dcat = _out_proj_bwd(dx1, s["w_out8"], i, g_nw_mlp)
    g_out = _mm_tn(s["ycat"], dx1, "dw_out", 512, D_MODEL, "rows")
    return dx1, dcat, g_out, g_up, g_down, g_nw_mlp


def _layer_bwd_mix(dx1, dcat, g_nw_mlp, p, i, s, cos, sin, after):
    dy_ssd, dz, dc, g_snw, g_lw, g_lb = _mix_post_bwd(dcat, s["y_ssd"], s["z"], s["c"], p["snw"], p["ln_w"], p["ln_b"], i, after)
    da, dg, g_cmw, g_cmb = _cm_conv_bwd(s["glu"], p["cm_w"], i, dc)
    dq, dk, dv, dsk = _attn_bwd(s["qk"], s["qkv"], p["sinks"], i, dcat)
    dqk_raw, g_qw, g_kw = _qk_prep_bwd(dq, dk, s["qkv"], p["qw"], p["kw"], i, cos, sin)
    dxbc_c, ddtr, g_bias, g_alog, g_d = _ssd_bwd(s["xbc_c"], s["dtr"], p["dt_bias"], p["a_log"], p["d_e"], i, s["hprev"], dy_ssd)
    dxbc, g_convw, g_convb = _ssd_conv_bwd(s["xbc"], p["conv_w"], p["conv_b"], i, dxbc_c)
    dx, g_nw_mix, du = _in_proj_bwd([dz, dxbc, dqk_raw, dv, da, dg, ddtr], s["w_in"], dx1, s["x"], p["nw_mix"], i)
    g_in = _g_in_split(_mm_tn(s["h"], du, "dw_in", 512, 640, "flat"))
    half = ATTN_HEAD_DIM
    small = dict(
        norm_mix_w=g_nw_mix[0], ssd_conv_b=g_convb[0], ssd_dt_bias=g_bias[0, :SSD_HEADS], ssd_a_log=g_alog[0, :SSD_HEADS],
        ssd_d=g_d[0, :SSD_HEADS], ssd_norm_w=g_snw[0], q_norm_w=g_qw[0, :half] + g_qw[0, half:],
        k_norm_w=g_kw[0, :half] + g_kw[0, half:], attn_sinks=dsk[:, 0],
        cm_dw_b=g_cmb[0], cm_ln_w=g_lw[0], cm_ln_b=g_lb[0], norm_mlp_w=g_nw_mlp[0],
        ssd_conv_w=g_convw, cm_dw_w=g_cmw)
    return dx, g_in, small


MESH = pl.DeviceIdType.MESH
_ANY = pl.BlockSpec(memory_space=pl.ANY)


def _coords():
    return lax.axis_index("x"), lax.axis_index("y"), lax.axis_index("c")


def _all_gather(xs, name):
    nt = len(xs)

    def body(*refs):
        x_refs, out_refs = refs[:nt], refs[nt:2 * nt]
        send_sems, recv_sems, local_sems = refs[2 * nt:]
        x, y, c = _coords()
        me, sibling = (x, y, c), (x, y, 1 - c)
        chips = [(1 - x, y), (x, 1 - y), (1 - x, 1 - y)]

        def slot(t, px, py, pc):
            return out_refs[t].at[4 * px + 2 * py + pc]

        def copy(t, k, block, to, src=None):
            return pltpu.make_async_remote_copy(
                src_ref=slot(t, *block) if src is None else src, dst_ref=slot(t, *block),
                send_sem=send_sems.at[7 * t + k], recv_sem=recv_sems.at[7 * t + k], device_id=to, device_id_type=MESH)

        mine = [pltpu.make_async_copy(x_refs[t], slot(t, *me), local_sems.at[t]) for t in range(nt)]
        for cp in mine:
            cp.start()
        first = []
        for t in range(nt):
            first.append(copy(t, 0, me, sibling, src=x_refs[t]))
            first += [copy(t, 1 + j, me, (*chip, c), src=x_refs[t]) for j, chip in enumerate(chips)]
        for cp in first:
            cp.start()
        passed = []
        for j, chip in enumerate(chips):
            for t in range(nt):
                copy(t, 1 + j, (*chip, c), me).wait_recv()
                passed.append(copy(t, 4 + j, (*chip, c), sibling))
                passed[-1].start()
        for t in range(nt):
            copy(t, 0, sibling, me).wait_recv()
            for j, chip in enumerate(chips):
                copy(t, 4 + j, (*chip, 1 - c), me).wait_recv()
        for cp in first + passed:
            cp.wait_send()
        for cp in mine:
            cp.wait()

    return pl.pallas_call(
        body, name=name, out_shape=[_sds((N_DEV,) + a.shape, a.dtype) for a in xs],
        in_specs=[_ANY] * nt, out_specs=[_ANY] * nt,
        scratch_shapes=[pltpu.SemaphoreType.DMA((7 * nt,)), pltpu.SemaphoreType.DMA((7 * nt,)), pltpu.SemaphoreType.DMA((nt,))],
    )(*xs)


def _peer_chips(x, y):
    return [(1 - x, y), (x, 1 - y), (1 - x, 1 - y)]


def _copies_per_tensor(kind):
    return {"gather": 3, "forward": 4, "scatter": 7, "scatter_chips": 3}[kind]


def _ici_copies(src_refs, land_refs, send_sems, recv_sems, kind):
    x, y, c = _coords()
    if kind == "forward":
        sends, recvs = [], []
        for t, d in enumerate(land_refs):
            for j, (px, py) in enumerate([(x, y)] + _peer_chips(x, y)):
                sems = dict(send_sem=send_sems.at[4 * t + j], recv_sem=recv_sems.at[4 * t + j],
                            device_id=(x, y, 1 - c), device_id_type=MESH)
                src = d.at[4 * px + 2 * py + c]
                sends.append(pltpu.make_async_remote_copy(src_ref=src, dst_ref=src, **sems))
                recvs.append(pltpu.make_async_remote_copy(src_ref=src, dst_ref=d.at[4 * px + 2 * py + 1 - c], **sems))
        return sends, recvs
    flip = lambda v, bit: 1 - v if bit else v
    if kind == "scatter":
        peers = [(flip(x, k & 4), flip(y, k & 2), flip(c, k & 1)) for k in range(1, N_DEV)]
    else:
        peers = [(px, py, c) for px, py in _peer_chips(x, y)]
    index = (lambda px, py, pc: 2 * px + py) if kind == "scatter_chips" else (lambda px, py, pc: 4 * px + 2 * py + pc)
    me = index(x, y, c)
    sends, recvs = [], []
    for t, d in enumerate(land_refs):
        for j, p in enumerate(peers):
            src = d.at[me] if kind == "gather" else src_refs[t].at[index(*p)]
            k = len(peers) * t + j
            sems = dict(send_sem=send_sems.at[k], recv_sem=recv_sems.at[k], device_id=p, device_id_type=MESH)
            sends.append(pltpu.make_async_remote_copy(src_ref=src, dst_ref=d.at[me], **sems))
            recvs.append(pltpu.make_async_remote_copy(src_ref=src, dst_ref=d.at[index(*p)], **sems))
    return sends, recvs


_HBM = pl.BlockSpec(memory_space=pltpu.HBM)
_SEMS = pl.BlockSpec(memory_space=pltpu.SEMAPHORE)
_EFFECT = pltpu.SideEffectType.DATAFLOW_SIDE_EFFECTING


def _hbm(a):
    return pltpu.with_memory_space_constraint(a, pltpu.HBM)


def _ici_start(srcs, lands, after, name, kind):
    ns, n = len(srcs), len(lands)
    nt = ns + n

    def body(*refs):
        sends, _ = _ici_copies(refs[:ns], refs[ns:nt], refs[nt + 1], refs[nt + 2], kind)
        for cp in sends:
            cp.start()
        refs[-1][...] = jnp.zeros_like(refs[-1])

    thru = srcs + lands
    out = pl.pallas_call(
        body, name=name,
        out_shape=(pltpu.SemaphoreType.DMA((_copies_per_tensor(kind) * n,)),) * 2
        + tuple(pltpu.HBM(a.shape, a.dtype) for a in thru) + (_sds((8, LANES)),),
        in_specs=[_HBM] * nt + [_ANY],
        out_specs=(_SEMS, _SEMS) + (_HBM,) * nt + (pl.BlockSpec(memory_space=pltpu.VMEM),),
        input_output_aliases={k: 2 + k for k in range(nt)},
        compiler_params=pltpu.CompilerParams(has_side_effects=_EFFECT),
    )(*[_hbm(a) for a in thru], after)
    return out[0], out[1], list(out[2:2 + ns]), list(out[2 + ns:2 + nt]), out[-1]


def _ici_wait(started, after, name, kind):
    send_sems, recv_sems, srcs, lands, _ = started
    ns, n = len(srcs), len(lands)
    nt = ns + n

    def body(*refs):
        sends, recvs = _ici_copies(refs[:ns], refs[ns:nt], refs[nt], refs[nt + 1], kind)
        for s, r in zip(sends, recvs):
            s.wait_send()
            r.wait_recv()

    thru = srcs + lands
    out = pl.pallas_call(
        body, name=name, out_shape=tuple(pltpu.HBM(a.shape, a.dtype) for a in thru),
        in_specs=[_HBM] * nt + [_SEMS, _SEMS, _ANY], out_specs=(_HBM,) * nt,
        input_output_aliases={k: k for k in range(nt)},
        compiler_params=pltpu.CompilerParams(has_side_effects=_EFFECT),
    )(*thru, send_sems, recv_sems, after)
    return list(out[:ns]), list(out[ns:])


def _ag_d2d(lands):
    n = len(lands)

    def body(*refs):
        in_refs, out_refs = refs[:n], refs[n:2 * n]
        send_sems, recv_sems = refs[2 * n:]
        x, y, c = _coords()
        sends, recvs = [], []
        for t in range(n):
            for k, (px, py) in enumerate([(x, y)] + _peer_chips(x, y)):
                sems = dict(send_sem=send_sems.at[4 * t + k], recv_sem=recv_sems.at[4 * t + k],
                            device_id=(x, y, 1 - c), device_id_type=MESH)
                src = in_refs[t].at[4 * px + 2 * py + c]
                sends.append(pltpu.make_async_remote_copy(src_ref=src, dst_ref=out_refs[t].at[4 * px + 2 * py + c], **sems))
                recvs.append(pltpu.make_async_remote_copy(src_ref=src, dst_ref=out_refs[t].at[4 * px + 2 * py + 1 - c], **sems))
        for cp in sends:
            cp.start()
        for cp in recvs:
            cp.wait_recv()
        for cp in sends:
            cp.wait_send()

    return pl.pallas_call(
        body, name="ag_d2d", out_shape=[_sds(a.shape, a.dtype) for a in lands],
        in_specs=[_ANY] * n, out_specs=[_ANY] * n,
        input_output_aliases={k: k for k in range(n)},
        scratch_shapes=[pltpu.SemaphoreType.DMA((4 * n,)), pltpu.SemaphoreType.DMA((4 * n,))],
    )(*lands)


def _rs_sib(grads):
    nt = len(grads)

    def body(*refs):
        s_refs, ra_refs = refs[:nt], refs[nt:2 * nt]
        send_sems, recv_sems = refs[2 * nt:]
        x, y, c = _coords()
        cps = [pltpu.make_async_remote_copy(
            src_ref=s_refs[t].at[:, 1 - c], dst_ref=ra_refs[t], send_sem=send_sems.at[t], recv_sem=recv_sems.at[t],
            device_id=(x, y, 1 - c), device_id_type=MESH) for t in range(nt)]
        for cp in cps:
            cp.start()
        for cp in cps:
            cp.wait()

    return pl.pallas_call(
        body, name="rs_sibling",
        out_shape=[_sds((4,) + g.shape[2:], g.dtype) for g in grads],
        in_specs=[_ANY] * nt, out_specs=[_ANY] * nt,
        scratch_shapes=[pltpu.SemaphoreType.DMA((nt,)), pltpu.SemaphoreType.DMA((nt,))],
    )(*grads)


def _rs_add(grads, ras, core):
    nt = len(grads)

    def body(c_ref, *refs):
        s_refs, ra_refs, q_refs, rb_refs = refs[:nt], refs[nt:2 * nt], refs[2 * nt:3 * nt], refs[3 * nt:]
        for t in range(nt):
            q = (s_refs[t][0, 0].astype(F32) + ra_refs[t][0].astype(F32)).astype(q_refs[t].dtype)
            q_refs[t][0] = q
            rb_refs[t][0] = q

    nr = 4
    own = [pl.BlockSpec((1, 1, g.shape[2] // nr, g.shape[3]), lambda j, r, c: (j, c[0], r, 0)) for g in grads]
    blk = [pl.BlockSpec((1, g.shape[2] // nr, g.shape[3]), lambda j, r, c: (j, r, 0)) for g in grads]
    return pl.pallas_call(
        body, name="rs_add", out_shape=[_sds(r.shape, r.dtype) for r in ras] * 2,
        grid_spec=pltpu.PrefetchScalarGridSpec(num_scalar_prefetch=1, grid=(4, nr), in_specs=own + blk, out_specs=blk * 2),
        compiler_params=_cparams(("parallel", "parallel")),
    )(core, *grads, *ras)


def _adamw(w, g, m, v):
    m = ADAM_B1 * m + (1.0 - ADAM_B1) * g
    v = ADAM_B2 * v + (1.0 - ADAM_B2) * jnp.square(g)
    m_hat = m / (1.0 - ADAM_B1 ** ADAM_STEP)
    v_hat = v / (1.0 - ADAM_B2 ** ADAM_STEP)
    delta = -ADAM_LR * (m_hat / (jnp.sqrt(v_hat) + ADAM_EPS) + ADAM_WD * w)
    return delta, m, v


def _sum_partials(s_own, rb_ref, me, acc_ref):
    if s_own is None:
        g = rb_ref[0].astype(F32)
        for j in range(1, rb_ref.shape[0]):
            g = g + rb_ref[j].astype(F32)
        acc_ref[...] = g
        return
    for d0 in range(N_DEV):
        @pl.when(me == d0)
        def _():
            g = None
            for d in range(N_DEV):
                term = (s_own if d == d0 else rb_ref[d]).astype(F32)
                g = term if g is None else g + term
            acc_ref[...] = g


def _rs_final(s, rb, me, w, m, v, outs, l):
    _, R, C = w.shape
    cp = rb.shape[2]
    own = [] if s is None else [s]

    def body(me_ref, *refs):
        s_ref = None if s is None else refs[0]
        rb_ref, w_ref, m_ref, v_ref = refs[len(own):len(own) + 4]
        g_ref, d_ref, m2_ref, v2_ref, acc_ref = refs[len(own) + 8:]
        _sum_partials(None if s is None else s_ref[0], rb_ref, me_ref[0], acc_ref)
        g = acc_ref[...][:, :C]
        g_ref[0] = g
        d_ref[0], m2_ref[0], v2_ref[0] = _adamw(w_ref[0], g, m_ref[0], v_ref[0])

    blk = pl.BlockSpec((1, TM, C), lambda r, me: (l, r, 0))
    return pl.pallas_call(
        body, name="rs_final_adamw", out_shape=[_sds(w.shape)] * 4,
        grid_spec=pltpu.PrefetchScalarGridSpec(
            num_scalar_prefetch=1, grid=(R // TM,),
            in_specs=[pl.BlockSpec((1, TM, cp), lambda r, me: (me[0], r, 0))] * len(own)
            + [pl.BlockSpec((rb.shape[0], TM, cp), lambda r, me: (0, r, 0)), blk, blk, blk] + [_ANY] * 4,
            out_specs=[blk] * 4, scratch_shapes=[pltpu.VMEM((TM, cp), F32)]),
        input_output_aliases={5 + len(own) + k: k for k in range(4)},
        compiler_params=_cparams(("parallel",)),
    )(me, *own, rb, w, m, v, *outs)


def _rs_final_w_in(s, rb, me, wt, mt, vt, outs, l):
    shard = (W_IN_SHARD, D_MODEL)
    own = [] if s is None else [s]

    def body(me_ref, rb_ref, *refs):
        wt_ref, mt_ref, vt_ref = refs[len(own):len(own) + 3]
        g_ref, d_ref, m2_ref, v2_ref, sbuf, acc_ref, bufs, obufs, sems = refs[len(own) + 7:]
        me = me_ref[0]
        loads = [pltpu.make_async_copy(src.at[:, l, :], bufs.at[k], sems.at[k]) for k, src in enumerate((wt_ref, mt_ref, vt_ref))]
        if own:
            loads.append(pltpu.make_async_copy(refs[0].at[me], sbuf, sems.at[7]))
        for cp in loads:
            cp.start()
        if own:
            loads[3].wait()
        _sum_partials(sbuf[...] if own else None, rb_ref, me, acc_ref)
        g = acc_ref[...].T[:W_IN_SHARD]
        for cp in loads[:3]:
            cp.wait()
        obufs[0] = g
        obufs[1], obufs[2], obufs[3] = _adamw(bufs[0], g, bufs[1], bufs[2])
        stores = [pltpu.make_async_copy(obufs.at[k], dst.at[:, l, :], sems.at[3 + k])
                  for k, dst in enumerate((g_ref, d_ref, m2_ref, v2_ref))]
        for cp in stores:
            cp.start()
        for cp in stores:
            cp.wait()

    return pl.pallas_call(
        body, name="rs_final_adamw_w_in",
        in_specs=[pl.BlockSpec(memory_space=pltpu.SMEM), pl.BlockSpec(memory_space=pltpu.VMEM)] + [_ANY] * (7 + len(own)),
        out_specs=[_ANY] * 4, out_shape=[_sds(wt.shape)] * 4,
        input_output_aliases={5 + len(own) + k: k for k in range(4)},
        scratch_shapes=[pltpu.VMEM(rb.shape[1:], rb.dtype), pltpu.VMEM(rb.shape[1:], F32),
                        pltpu.VMEM((3,) + shard, F32), pltpu.VMEM((4,) + shard, F32), pltpu.SemaphoreType.DMA((8,))],
        compiler_params=_cparams(),
    )(me, rb, *own, wt, mt, vt, *outs)


def _sum8(g8):
    _, R, C = g8.shape

    def body(g_ref, o_ref):
        acc = g_ref[0]
        for d in range(1, N_DEV):
            acc = acc + g_ref[d]
        o_ref[...] = acc

    return pl.pallas_call(body, name="small_sum", out_shape=_sds((R, C)))(g8)


def _adamw_small(w, g, m, v):
    def body(w_ref, g_ref, m_ref, v_ref, d_ref, m2_ref, v2_ref):
        d_ref[...], m2_ref[...], v2_ref[...] = _adamw(w_ref[...], g_ref[...], m_ref[...], v_ref[...])

    return pl.pallas_call(body, name="small_adamw", out_shape=[_sds(w.shape)] * 3)(w, g, m, v)


REP = (("norm_mix_w", 1024), ("ssd_conv_b", 1536), ("ssd_dt_bias", 16), ("ssd_a_log", 16), ("ssd_d", 16),
       ("ssd_norm_w", 1024), ("q_norm_w", 64), ("k_norm_w", 64), ("attn_sinks", 8), ("cm_dw_b", 512),
       ("cm_ln_w", 512), ("cm_ln_b", 512), ("norm_mlp_w", 1024))
WEIGHTS = ("norm_mix_w", "w_in", "ssd_conv_w", "ssd_conv_b", "ssd_dt_bias", "ssd_a_log", "ssd_d", "ssd_norm_w",
           "q_norm_w", "k_norm_w", "attn_sinks", "cm_dw_w", "cm_dw_b", "cm_ln_w", "cm_ln_b", "w_out", "norm_mlp_w",
           "w_mlp_up", "w_mlp_down")
BIG = ("w_in", "w_out", "w_mlp_up", "w_mlp_down")
N_REP = DEPTH * sum(n for _, n in REP)
CONVW_SHARD = SSD_XBC // N_DEV
CMW_SHARD = CM_CHANNELS // N_DEV


def _to_rows(flat, rows):
    return jnp.pad(flat, (0, rows * LANES - flat.shape[0])).reshape(rows, LANES)


def kernel(x, norm_mix_w, w_in, ssd_conv_w, ssd_conv_b, ssd_dt_bias, ssd_a_log, ssd_d, ssd_norm_w, q_norm_w, k_norm_w, attn_sinks, cm_dw_w, cm_dw_b, cm_ln_w, cm_ln_b, w_out, norm_mlp_w, w_mlp_up, w_mlp_down, loss_target, m_norm_mix_w, m_w_in, m_ssd_conv_w, m_ssd_conv_b, m_ssd_dt_bias, m_ssd_a_log, m_ssd_d, m_ssd_norm_w, m_q_norm_w, m_k_norm_w, m_attn_sinks, m_cm_dw_w, m_cm_dw_b, m_cm_ln_w, m_cm_ln_b, m_w_out, m_norm_mlp_w, m_w_mlp_up, m_w_mlp_down, v_norm_mix_w, v_w_in, v_ssd_conv_w, v_ssd_conv_b, v_ssd_dt_bias, v_ssd_a_log, v_ssd_d, v_ssd_norm_w, v_q_norm_w, v_k_norm_w, v_attn_sinks, v_cm_dw_w, v_cm_dw_b, v_cm_ln_w, v_cm_ln_b, v_w_out, v_norm_mlp_w, v_w_mlp_up, v_w_mlp_down):
    w = dict(norm_mix_w=norm_mix_w, w_in=w_in, ssd_conv_w=ssd_conv_w, ssd_conv_b=ssd_conv_b, ssd_dt_bias=ssd_dt_bias, ssd_a_log=ssd_a_log, ssd_d=ssd_d, ssd_norm_w=ssd_norm_w, q_norm_w=q_norm_w, k_norm_w=k_norm_w, attn_sinks=attn_sinks, cm_dw_w=cm_dw_w, cm_dw_b=cm_dw_b, cm_ln_w=cm_ln_w, cm_ln_b=cm_ln_b, w_out=w_out, norm_mlp_w=norm_mlp_w, w_mlp_up=w_mlp_up, w_mlp_down=w_mlp_down)
    m = dict(norm_mix_w=m_norm_mix_w, w_in=m_w_in, ssd_conv_w=m_ssd_conv_w, ssd_conv_b=m_ssd_conv_b, ssd_dt_bias=m_ssd_dt_bias, ssd_a_log=m_ssd_a_log, ssd_d=m_ssd_d, ssd_norm_w=m_ssd_norm_w, q_norm_w=m_q_norm_w, k_norm_w=m_k_norm_w, attn_sinks=m_attn_sinks, cm_dw_w=m_cm_dw_w, cm_dw_b=m_cm_dw_b, cm_ln_w=m_cm_ln_w, cm_ln_b=m_cm_ln_b, w_out=m_w_out, norm_mlp_w=m_norm_mlp_w, w_mlp_up=m_w_mlp_up, w_mlp_down=m_w_mlp_down)
    v = dict(norm_mix_w=v_norm_mix_w, w_in=v_w_in, ssd_conv_w=v_ssd_conv_w, ssd_conv_b=v_ssd_conv_b, ssd_dt_bias=v_ssd_dt_bias, ssd_a_log=v_ssd_a_log, ssd_d=v_ssd_d, ssd_norm_w=v_ssd_norm_w, q_norm_w=v_q_norm_w, k_norm_w=v_k_norm_w, attn_sinks=v_attn_sinks, cm_dw_w=v_cm_dw_w, cm_dw_b=v_cm_dw_b, cm_ln_w=v_cm_ln_w, cm_ln_b=v_cm_ln_b, w_out=v_w_out, norm_mlp_w=v_norm_mlp_w, w_mlp_up=v_w_mlp_up, w_mlp_down=v_w_mlp_down)
    L = x.shape[1]
    xi, yi, ci = _coords()
    me = 4 * xi + 2 * yi + ci
    n_conv = DEPTH * SSD_CONV * CONVW_SHARD
    n_cm = DEPTH * CM_CONV * CMW_SHARD

    conv_rows = 88
    cw8, = _all_gather([_to_rows(jnp.concatenate([ssd_conv_w.reshape(-1), cm_dw_w.reshape(-1)]), conv_rows)], "ag_conv_w")
    cw8 = cw8.reshape(N_DEV, -1)
    conv_full = cw8[:, :n_conv].reshape(N_DEV, DEPTH, SSD_CONV, CONVW_SHARD).transpose(1, 2, 0, 3).reshape(DEPTH, SSD_CONV, SSD_XBC)
    cm_full = cw8[:, n_conv:n_conv + n_cm].reshape(N_DEV, DEPTH, CM_CONV, CMW_SHARD).transpose(1, 2, 0, 3).reshape(DEPTH, CM_CONV, CM_CHANNELS)
    me1 = jnp.reshape(me, (1,)).astype(jnp.int32)
    casts = [_cast_w_in(w_in, me1), _cast_shard(w_out, me1), _cast_shard(w_mlp_up, me1), _cast_shard(w_mlp_down, me1)]
    shards = [[c[l] for c in casts] for l in range(DEPTH)]

    def gather_start(lands, after):
        return _ici_start([], lands, after, "ag_ici_start", "gather")

    def gather_finish(started, after):
        return _ag_d2d(_ici_wait(started, after, "ag_ici_wait", "gather")[1])

    cos, sin = _rope_tables(L)
    p = _stacked_params({k: w[k] for k, _ in REP}, conv_full, cm_full)
    saved = []
    h = x[0]
    first = gather_start(shards[0][:1], cw8)
    rest0 = gather_start(shards[0][1:], first[4])
    w_in8, = gather_finish(first, rest0[4])
    token, rest = rest0[4], None
    for i in range(DEPTH):
        if i == 0:
            late = lambda ycat: gather_finish(rest0, ycat)
        else:
            late = lambda ycat, r=rest: r
        nxt, fwd = None, []
        if i + 1 < DEPTH:
            nxt = gather_start(shards[i + 1], w_in8)
            token = nxt[4]

        def before_down(act, nxt=nxt, fwd=fwd):
            if nxt is None:
                return act
            lands = _ici_wait(nxt, act, "ag_ici_wait", "gather")[1]
            fwd.append(_ici_start([], lands, act, "ag_d2d_start", "forward"))
            return fwd[0][4]

        h, s = _layer_fwd(h, p, w_in8, late, before_down, i, cos, sin, token)
        saved.append(s)
        if nxt is not None:
            got = _ici_wait(fwd[0], h, "ag_d2d_wait", "forward")[1]
            w_in8, rest = got[0], got[1:]
    d, loss_tile = _loss_head(h, loss_target[0])

    smalls = [None] * DEPTH
    big_out = {k: [lax.empty(w[k].shape, F32) for _ in range(4)] for k in BIG}
    to_t = lambda a: jnp.transpose(a, (2, 0, 1))
    w_in_t = [to_t(t["w_in"]) for t in (w, m, v)]
    big_out["w_in"] = [lax.empty(w_in_t[0].shape, F32) for _ in range(4)]

    core = jnp.reshape(ci, (1,)).astype(jnp.int32)

    def scatter_start(grads, after, l):
        if l > 0:
            lands = [lax.empty(g.shape, g.dtype) for g in grads]
            return _ici_start(list(grads), lands, after, "rs_ici_start", "scatter")
        g4 = [g.reshape((4, 2) + g.shape[1:]) for g in grads]
        out = _rs_add(g4, _rs_sib(g4), core)
        return _ici_start(list(out[:len(g4)]), list(out[len(g4):]), after, "rs_chips_start", "scatter_chips")

    def scatter_finish(started, after, l, names):
        if l > 0:
            srcs, rbs = _ici_wait(started, after, "rs_ici_wait", "scatter")
        else:
            srcs, rbs = [None] * len(names), _ici_wait(started, after, "rs_chips_wait", "scatter_chips")[1]
        for g, rb, k in zip(srcs, rbs, names):
            if k == "w_in":
                big_out[k] = _rs_final_w_in(g, rb, me1, *w_in_t, big_out[k], l)
            else:
                big_out[k] = _rs_final(g, rb, me1, w[k], m[k], v[k], big_out[k], l)

    def gather_small_grads():
        gvec = jnp.concatenate(
            [jnp.stack([smalls[i][k] for i in range(DEPTH)]).reshape(-1) for k, _ in REP]
            + [jnp.stack([smalls[i][k] for i in range(DEPTH)]).reshape(-1) for k in ("ssd_conv_w", "cm_dw_w")]
            + [loss_tile[0, :1]])
        g_rows = -(-gvec.shape[0] // (8 * LANES)) * 8
        return _all_gather([_to_rows(gvec, g_rows)], "ag_small_grads")[0]

    token, pending = loss_tile, []
    for i in reversed(range(DEPTH)):
        dx1, dcat, g_out, g_up, g_down, g_nw_mlp = _layer_bwd_mlp(d, p, i, saved[i], token)
        started = []
        if i == 0:
            started.append((scatter_start([g_out, g_up, g_down], dcat, i), i, BIG[1:]))
        d, g_in, smalls[i] = _layer_bwd_mix(dx1, dcat, g_nw_mlp, p, i, saved[i], cos, sin,
                                            started[0][0][4] if started else g_nw_mlp)
        if i == 0:
            g8 = gather_small_grads()
            started.append((scatter_start([g_in], g8, i), i, BIG[:1]))
        else:
            started.append((scatter_start([g_in, g_out, g_up, g_down], d, i), i, BIG))
        token = started[-1][0][4]
        for st, l, names in pending:
            scatter_finish(st, token, l, names)
        pending = started
    for st, l, names in pending:
        scatter_finish(st, token, l, names)

    gsum = _sum8(g8).reshape(-1)
    o_conv = N_REP
    o_cm = o_conv + DEPTH * SSD_CONV * SSD_XBC
    o_loss = o_cm + DEPTH * CM_CONV * CM_CHANNELS
    g_conv = lax.dynamic_slice_in_dim(gsum[o_conv:o_cm].reshape(DEPTH, SSD_CONV, SSD_XBC), me * CONVW_SHARD, CONVW_SHARD, axis=2)
    g_cm = lax.dynamic_slice_in_dim(gsum[o_cm:o_loss].reshape(DEPTH, CM_CONV, CM_CHANNELS), me * CMW_SHARD, CMW_SHARD, axis=2)
    loss = gsum[o_loss]
    s_rows = -(-(N_REP + n_conv + n_cm) // (8 * LANES)) * 8

    def pack_small(t):
        return _to_rows(jnp.concatenate([t[k].reshape(-1) for k, _ in REP] + [t["ssd_conv_w"].reshape(-1), t["cm_dw_w"].reshape(-1)]), s_rows)

    g_small = _to_rows(jnp.concatenate([gsum[:N_REP], g_conv.reshape(-1), g_cm.reshape(-1)]), s_rows)
    small_out = [g_small] + list(_adamw_small(pack_small(w), g_small, pack_small(m), pack_small(v)))

    def unpack_small(t):
        flat = t.reshape(-1)
        out, off = {}, 0
        for k, n in REP:
            out[k] = flat[off:off + DEPTH * n].reshape(DEPTH, n)
            off += DEPTH * n
        out["ssd_conv_w"] = flat[off:off + n_conv].reshape(DEPTH, SSD_CONV, CONVW_SHARD)
        off += n_conv
        out["cm_dw_w"] = flat[off:off + n_cm].reshape(DEPTH, CM_CONV, CMW_SHARD)
        return out

    outs = [loss, d[None]]
    for j, small_t in enumerate(small_out):
        t = unpack_small(small_t)
        for k in BIG:
            t[k] = big_out[k][j]
        t["w_in"] = jnp.transpose(t["w_in"], (1, 2, 0))
        outs += [t[k] for k in WEIGHTS]
    return tuple(outs)
```

```python
import math

import jax
import jax.numpy as jnp
from jax import lax
from jax.experimental import pallas as pl
from jax.experimental.pallas import tpu as pltpu

F32 = jnp.float32
_MM = jnp.bfloat16

D_MODEL = 1024
DEPTH = 4
SSD_WIDTH = 1024
SSD_HEADS = 16
SSD_STATE = 128
SSD_GROUPS = 2
SSD_CONV = 4
SSD_XBC = 1536
Q = 128
ATTN_HEAD_DIM = 64
ATTN_Q_HEADS = 8
CM_CHANNELS = 512
CM_CONV = 31
D_FF = 4096
D_MIX = 2048
N_IN = 4368
RMS_EPS = 1e-6
LN_EPS = 1e-5
ROPE_THETA = 10000.0
ADAM_LR = 0.001
ADAM_B1 = 0.9
ADAM_B2 = 0.999
ADAM_EPS = 1e-08
ADAM_WD = 0.01
ADAM_STEP = 10

N_DEV = 8
LANES = 128
TM = 256
N_IN_P = 4480
U_Z, U_XBC, U_QKV, U_GLU, U_DT = (0, 1024), (1024, 2560), (2560, 3328), (3328, 4352), (4352, 4480)
W_IN_SHARD = N_IN // N_DEV
W_IN_SHARD_P = 640
FF_SHARD = D_FF // N_DEV
OUT_SHARD = D_MIX // N_DEV

_NN = (((1,), (0,)), ((), ()))
_NT = (((1,), (1,)), ((), ()))
_TN = (((0,), (0,)), ((), ()))
_VMEM_LIMIT = 56 * 1024 * 1024


def _mm(a, b, dims=_NN):
    return lax.dot_general(a.astype(_MM), b.astype(_MM), dims, preferred_element_type=F32)


def _mmx(a, b, dims=_NN, exact="b"):
    m, v = (b, a) if exact == "b" else (a, b)
    m = m.astype(jnp.bfloat16)
    acc = None
    for _ in range(3):
        p = v.astype(jnp.bfloat16)
        v = v - p.astype(F32)
        t = lax.dot_general(p, m, dims, preferred_element_type=F32) if exact == "b" else \
            lax.dot_general(m, p, dims, preferred_element_type=F32)
        acc = t if acc is None else acc + t
    return acc


def _sds(shape, dtype=F32):
    return jax.ShapeDtypeStruct(tuple(shape), dtype)


def _full(shape):
    nd = len(shape)
    return pl.BlockSpec(tuple(shape), lambda *_: (0,) * nd)


def _rows(cols, tm=TM, col=0):
    return pl.BlockSpec((tm, cols), lambda i: (i, col))


TMM = 512


def _mrows(cols):
    return _rows(cols, TMM)


def _lp(n, i):
    return pl.BlockSpec((1, 1, n), lambda *_: (i, 0, 0))


def _lw(arr):
    return pl.BlockSpec(arr.shape, lambda *_: (0, 0, 0, 0))


_ANY = pl.BlockSpec(memory_space=pl.ANY)


def _cparams(sem=None):
    return pltpu.CompilerParams(dimension_semantics=sem, vmem_limit_bytes=_VMEM_LIMIT)


def _sigmoid(x):
    return 1.0 / (1.0 + jnp.exp(-x))


def _silu(x):
    return x * _sigmoid(x)


def _dsilu(x):
    s = _sigmoid(x)
    return s * (1.0 + x * (1.0 - s))


def _rms_bwd(dy, x, w, inv_n):
    r = lax.rsqrt(jnp.sum(x * x, axis=-1, keepdims=True) * inv_n + RMS_EPS)
    xh = x * r
    dxh = dy * w
    dx = r * (dxh - xh * (jnp.sum(dxh * xh, axis=-1, keepdims=True) * inv_n))
    return dx, dy * xh


def _cast_shard(w, me, cols_p=None):
    _, R, C = w.shape
    cp = C if cols_p is None else cols_p

    def body(me_ref, w_ref, *o_refs):
        v = w_ref[0]
        if cp != C:
            v = jnp.concatenate([v, jnp.zeros((R, cp - C), F32)], axis=1)
        for k in range(DEPTH):
            @pl.when(pl.program_id(0) == k)
            def _():
                o_refs[k][0, 0] = v.astype(_MM)

    return pl.pallas_call(
        body, name="cast_shard", out_shape=[_sds((N_DEV, 1, R, cp), _MM)] * DEPTH,
        grid_spec=pltpu.PrefetchScalarGridSpec(
            num_scalar_prefetch=1, grid=(DEPTH,),
            in_specs=[pl.BlockSpec((1, R, C), lambda l, me: (l, 0, 0))],
            out_specs=[pl.BlockSpec((1, 1, R, cp), lambda l, me: (me[0], 0, 0, 0))] * DEPTH),
        compiler_params=_cparams(("arbitrary",)),
    )(me, w)


def _cast_w_in(w_in, me):
    wt = jnp.transpose(w_in, (2, 0, 1))

    def body(me_ref, wt_ref, *rest):
        o_refs, buf, sem = rest[:DEPTH], rest[DEPTH], rest[DEPTH + 1]
        l = pl.program_id(0)
        cp = pltpu.make_async_copy(wt_ref.at[:, l, :], buf, sem)
        cp.start()
        cp.wait()
        v = jnp.concatenate([buf[...], jnp.zeros((W_IN_SHARD_P - W_IN_SHARD, D_MODEL), F32)], axis=0).T.astype(_MM)
        for k in range(DEPTH):
            @pl.when(l == k)
            def _():
                o_refs[k][0, 0] = v

    return pl.pallas_call(
        body, name="cast_w_in", out_shape=[_sds((N_DEV, 1, D_MODEL, W_IN_SHARD_P), _MM)] * DEPTH,
        grid_spec=pltpu.PrefetchScalarGridSpec(
            num_scalar_prefetch=1, grid=(DEPTH,), in_specs=[_ANY],
            out_specs=[pl.BlockSpec((1, 1, D_MODEL, W_IN_SHARD_P), lambda l, me: (me[0], 0, 0, 0))] * DEPTH,
            scratch_shapes=[pltpu.VMEM((W_IN_SHARD, D_MODEL), F32), pltpu.SemaphoreType.DMA]),
        compiler_params=_cparams(("arbitrary",)),
    )(me, wt)


def _w_in_regroup(w8, after):
    a, b = U_XBC[1], U_XBC[1] + SSD_HEADS

    def body(w_ref, after_ref, o_ref):
        w = jnp.concatenate([w_ref[j, 0][:, :W_IN_SHARD].astype(F32) for j in range(N_DEV)], axis=1)
        r = jnp.concatenate([w[:, :a], w[:, b:], w[:, a:b], jnp.zeros((TM, N_IN_P - N_IN), F32)], axis=1)
        o_ref[...] = r.astype(_MM)

    return pl.pallas_call(
        body, name="w_in_regroup", grid=(D_MODEL // TM,),
        in_specs=[pl.BlockSpec((N_DEV, 1, TM, W_IN_SHARD_P), lambda r: (0, 0, r, 0)), _ANY],
        out_specs=_rows(N_IN_P), out_shape=_sds((D_MODEL, N_IN_P), _MM),
        compiler_params=_cparams(("parallel",)),
    )(w8, after)


def _g_in_split(g):
    a = U_XBC[1]

    def body(g_ref, o_ref):
        v = g_ref[...].astype(F32)
        w = jnp.concatenate([v[:, :a], v[:, U_DT[0]:U_DT[0] + SSD_HEADS], v[:, a:U_DT[0]]], axis=1)
        pad = jnp.zeros((TM, W_IN_SHARD_P - W_IN_SHARD), F32)
        for j in range(N_DEV):
            o_ref[j] = jnp.concatenate([w[:, j * W_IN_SHARD:(j + 1) * W_IN_SHARD], pad], axis=1).astype(_MM)

    return pl.pallas_call(
        body, name="g_in_split", grid=(D_MODEL // TM,),
        in_specs=[_rows(N_IN_P)],
        out_specs=pl.BlockSpec((N_DEV, TM, W_IN_SHARD_P), lambda r: (0, r, 0)),
        out_shape=_sds((N_DEV, D_MODEL, W_IN_SHARD_P), _MM),
        compiler_params=_cparams(("parallel",)),
    )(g)


def _in_proj(x, nw, i, w):
    L = x.shape[0]
    splits = (U_Z, U_XBC, U_QKV, U_GLU, U_DT)

    def body(x_ref, nw_ref, w_ref, h_ref, *out_refs):
        xf = x_ref[...]
        r = lax.rsqrt(jnp.mean(xf * xf, axis=-1, keepdims=True) + RMS_EPS)
        h = (xf * r * nw_ref[0]).astype(_MM)
        h_ref[...] = h
        for ref, (a, b) in zip(out_refs, splits):
            ref[...] = lax.dot_general(h, w_ref[:, a:b], _NN, preferred_element_type=F32)

    return pl.pallas_call(
        body, name="in_proj", grid=(L // TMM,),
        in_specs=[_mrows(D_MODEL), _lp(D_MODEL, i), _full(w.shape)],
        out_specs=[_mrows(D_MODEL)] + [_mrows(b - a) for a, b in splits],
        out_shape=[_sds((L, D_MODEL), _MM)] + [_sds((L, b - a)) for a, b in splits],
        compiler_params=_cparams(("parallel",)),
    )(x, nw, w)


def _mlp_up(x, nw, i, w8):
    L = x.shape[0]

    def body(x_ref, nw_ref, w_ref, h_ref, up_ref, act_ref):
        xf = x_ref[...]
        r = lax.rsqrt(jnp.mean(xf * xf, axis=-1, keepdims=True) + RMS_EPS)
        h = (xf * r * nw_ref[0]).astype(_MM)
        h_ref[...] = h
        for j in range(N_DEV):
            sl = slice(j * FF_SHARD, (j + 1) * FF_SHARD)
            up = lax.dot_general(h, w_ref[j, 0], _NN, preferred_element_type=F32)
            up_ref[:, sl] = up
            act_ref[:, sl] = jnp.square(jnp.maximum(up, 0.0)).astype(_MM)

    return pl.pallas_call(
        body, name="mlp_up", grid=(L // TM,),
        in_specs=[_rows(D_MODEL), _lp(D_MODEL, i), _lw(w8)],
        out_specs=[_rows(D_MODEL), _rows(D_FF), _rows(D_FF)],
        out_shape=[_sds((L, D_MODEL), _MM), _sds((L, D_FF)), _sds((L, D_FF), _MM)],
        compiler_params=_cparams(("parallel",)),
    )(x, nw, w8)


def _mm_res(a, w8, i, res, name, after):
    L, K = a.shape
    N = w8.shape[3]

    def body(a_ref, w_ref, res_ref, after_ref, o_ref):
        w = w_ref[:, 0].reshape(K, N)
        o_ref[...] = res_ref[...] + lax.dot_general(a_ref[...], w, _NN, preferred_element_type=F32)

    return pl.pallas_call(
        body, name=name, grid=(L // TMM,),
        in_specs=[_mrows(K), _lw(w8), _mrows(N), _ANY],
        out_specs=_mrows(N), out_shape=_sds((L, N)),
        compiler_params=_cparams(("parallel",)),
    )(a, w8, res, after)


def _out_proj_bwd(a, w8, i, after):
    L = a.shape[0]

    def body(a_ref, w_ref, after_ref, o_ref):
        w = w_ref[:, 0].reshape(D_MIX, D_MODEL)
        o_ref[...] = lax.dot_general(a_ref[...].astype(_MM), w, _NT, preferred_element_type=F32)

    return pl.pallas_call(
        body, name="out_proj_bwd", grid=(L // TMM,),
        in_specs=[_mrows(D_MODEL), _lw(w8), _ANY],
        out_specs=_mrows(D_MIX), out_shape=_sds((L, D_MIX)),
        compiler_params=_cparams(("parallel",)),
    )(a, w8, after)


def _mlp_down_bwd(dy, w8, i, up, after):
    L = dy.shape[0]

    def body(dy_ref, w_ref, up_ref, after_ref, o_ref):
        d = dy_ref[...].astype(_MM)
        for j in range(N_DEV):
            sl = slice(j * FF_SHARD, (j + 1) * FF_SHARD)
            da = lax.dot_general(d, w_ref[j, 0], _NT, preferred_element_type=F32)
            o_ref[:, sl] = (da * (2.0 * jnp.maximum(up_ref[:, sl], 0.0))).astype(_MM)

    return pl.pallas_call(
        body, name="mlp_down_bwd", grid=(L // TMM,),
        in_specs=[_mrows(D_MODEL), _lw(w8), _mrows(D_FF), _ANY],
        out_specs=_mrows(D_FF), out_shape=_sds((L, D_FF), _MM),
        compiler_params=_cparams(("parallel",)),
    )(dy, w8, up, after)


def _rms_bwd_epilogue(dh, res_ref, x_ref, nw_ref, dx_ref, dnw_ref):
    dx, dwx = _rms_bwd(dh, x_ref[...], nw_ref[0], 1.0 / D_MODEL)
    dx_ref[...] = res_ref[...] + dx

    @pl.when(pl.program_id(0) == 0)
    def _():
        dnw_ref[...] = jnp.zeros_like(dnw_ref)

    dnw_ref[...] += jnp.sum(dwx, axis=0, keepdims=True)


def _mlp_up_bwd(d_up, w8, i, res, x, nw):
    L = d_up.shape[0]

    def body(a_ref, w_ref, res_ref, x_ref, nw_ref, dx_ref, dnw_ref):
        dh = jnp.zeros((TMM, D_MODEL), F32)
        for j in range(N_DEV):
            dh = dh + lax.dot_general(a_ref[:, j * FF_SHARD:(j + 1) * FF_SHARD], w_ref[j, 0], _NT, preferred_element_type=F32)
        _rms_bwd_epilogue(dh, res_ref, x_ref, nw_ref, dx_ref, dnw_ref)

    return pl.pallas_call(
        body, name="mlp_up_bwd", grid=(L // TMM,),
        in_specs=[_mrows(D_FF), _lw(w8), _mrows(D_MODEL), _mrows(D_MODEL), _lp(D_MODEL, i)],
        out_specs=[_mrows(D_MODEL), _full((1, D_MODEL))],
        out_shape=[_sds((L, D_MODEL)), _sds((1, D_MODEL))],
        compiler_params=_cparams(("arbitrary",)),
    )(d_up, w8, res, x, nw)


def _in_proj_bwd(pieces, w, res, x, nw, i):
    L = pieces[0].shape[0]
    n = len(pieces)

    def body(*refs):
        w_ref, res_ref, x_ref, nw_ref, dx_ref, dnw_ref, du_ref = refs[n:]
        off = 0
        for r in refs[:n]:
            du_ref[:, off:off + r.shape[1]] = r[...].astype(_MM)
            off += r.shape[1]
        dh = lax.dot_general(du_ref[...], w_ref[...], _NT, preferred_element_type=F32)
        _rms_bwd_epilogue(dh, res_ref, x_ref, nw_ref, dx_ref, dnw_ref)

    return pl.pallas_call(
        body, name="in_proj_bwd", grid=(L // TM,),
        in_specs=[_rows(q.shape[1]) for q in pieces] + [_full(w.shape), _rows(D_MODEL), _rows(D_MODEL), _lp(D_MODEL, i)],
        out_specs=[_rows(D_MODEL), _full((1, D_MODEL)), _rows(N_IN_P)],
        out_shape=[_sds((L, D_MODEL)), _sds((1, D_MODEL)), _sds((L, N_IN_P), _MM)],
        compiler_params=_cparams(("arbitrary",)),
    )(*pieces, w, res, x, nw)


def _mm_tn(a, g, name, tk, tn, out):
    L, K = a.shape
    N = g.shape[1]

    def body(a_ref, g_ref, o_ref):
        r = lax.dot_general(a_ref[...].astype(_MM), g_ref[...].astype(_MM), _TN, preferred_element_type=F32)
        o_ref[...] = r.astype(o_ref.dtype).reshape(o_ref.shape)

    if out == "flat":
        out_spec, out_shape = pl.BlockSpec((tk, tn), lambda i, j: (i, j)), _sds((K, N), _MM)
    elif out == "rows":
        assert tn == N and tk % (K // N_DEV) == 0
        nblk = tk // (K // N_DEV)
        out_spec, out_shape = pl.BlockSpec((nblk, K // N_DEV, N), lambda i, j: (i, 0, 0)), _sds((N_DEV, K // N_DEV, N), _MM)
    else:
        assert tk == K and tn == N // N_DEV
        out_spec, out_shape = pl.BlockSpec((1, K, tn), lambda i, j: (j, 0, 0)), _sds((N_DEV, K, tn), _MM)
    return pl.pallas_call(
        body, name=name, grid=(K // tk, N // tn),
        in_specs=[pl.BlockSpec((L, tk), lambda i, j: (0, i)), pl.BlockSpec((L, tn), lambda i, j: (0, j))],
        out_specs=out_spec, out_shape=out_shape,
        compiler_params=_cparams(("parallel", "parallel")),
    )(a, g)


def _loss_head(y, t):
    L = y.shape[0]

    def body(y_ref, t_ref, dy_ref, l_ref):
        e = y_ref[...] - t_ref[...]
        dy_ref[...] = e * (1.0 / D_MODEL)

        @pl.when(pl.program_id(0) == 0)
        def _():
            l_ref[...] = jnp.zeros_like(l_ref)

        l_ref[...] += jnp.sum(jnp.sum(e * e, axis=1, keepdims=True), axis=0, keepdims=True) * (0.5 / D_MODEL)

    return pl.pallas_call(
        body, name="loss_head", grid=(L // TM,),
        in_specs=[_rows(D_MODEL), _rows(D_MODEL)],
        out_specs=[_rows(D_MODEL), _full((8, LANES))],
        out_shape=[_sds((L, D_MODEL)), _sds((8, LANES))],
        compiler_params=_cparams(("arbitrary",)),
    )(y, t)


EDGE = 32


def _roll_rows(x, s):
    s = s % x.shape[0]
    return x if s == 0 else pltpu.roll(x, s, axis=0)


class _Rolls:
    def __init__(self, x):
        self.x, self.by_phase = x, {}

    def __call__(self, s):
        s = s % self.x.shape[0]
        b = s % 8
        if b not in self.by_phase:
            self.by_phase[b] = _roll_rows(self.x, b)
        return _roll_rows(self.by_phase[b], s - b)


def _conv_taps(x, w_ref, b, k_w):
    def taps(v, zero_fill):
        r = lax.broadcasted_iota(jnp.int32, v.shape, 0)
        acc = jnp.broadcast_to(b, v.shape)
        rolled = _Rolls(v)
        for k in range(k_w):
            s = k_w - 1 - k
            sh = rolled(s)
            if zero_fill and s:
                sh = jnp.where(r >= s, sh, 0.0)
            acc = acc + w_ref[0, k:k + 1, :] * sh
        return acc

    return jnp.concatenate([taps(x[:EDGE], True), taps(x, False)[EDGE:]], axis=0)


def _conv_bwd_taps(x, dc, w_ref, dw_ref, db_ref, k_w):
    n = x.shape[0]
    dc_tail, x_tail, dc_head = dc[n - EDGE:], x[n - EDGE:], dc[:EDGE]
    r = lax.broadcasted_iota(jnp.int32, dc_head.shape, 0)
    dx = jnp.zeros_like(x)
    dx_tail = jnp.zeros_like(dc_tail)
    dc_rolled, x_rolled = _Rolls(dc), _Rolls(x)
    for k in range(k_w):
        s = k_w - 1 - k
        wk = w_ref[0, k:k + 1, :]
        dx = dx + wk * dc_rolled(n - s)
        up = _roll_rows(dc_tail, EDGE - s)
        dx_tail = dx_tail + wk * (jnp.where(r < EDGE - s, up, 0.0) if s else up)
        dw = jnp.sum(dc * x_rolled(s), axis=0, keepdims=True)
        if s:
            dw = dw - jnp.sum(jnp.where(r < s, dc_head * _roll_rows(x_tail, s), 0.0), axis=0, keepdims=True)
        dw_ref[k:k + 1, :] = dw
    db_ref[...] = jnp.sum(dc, axis=0, keepdims=True)
    return jnp.concatenate([dx[:n - EDGE], dx_tail], axis=0)


def _cols(L, cb, off=0):
    return pl.BlockSpec((L, cb), lambda j: (0, j + off))


def _lcols(k, cb, i):
    return pl.BlockSpec((1, k, cb), lambda j: (i, 0, j))


SSD_CB = 256


def _ssd_conv_fwd(x, w, b, i):
    L, C = x.shape
    cb = SSD_CB

    def body(x_ref, w_ref, b_ref, o_ref):
        o_ref[...] = _silu(_conv_taps(x_ref[...], w_ref, b_ref[0], SSD_CONV))

    return pl.pallas_call(
        body, name="ssd_conv_fwd", grid=(C // cb,),
        in_specs=[_cols(L, cb), _lcols(SSD_CONV, cb, i), _lcols(1, cb, i)],
        out_specs=_cols(L, cb), out_shape=_sds((L, C)),
        compiler_params=_cparams(("parallel",)),
    )(x, w, b)


def _ssd_conv_bwd(x, w, b, i, dy):
    L, C = x.shape
    cb = SSD_CB

    def body(x_ref, w_ref, b_ref, dy_ref, dx_ref, dw_ref, db_ref):
        x_ = x_ref[...]
        c = _conv_taps(x_, w_ref, b_ref[0], SSD_CONV)
        dc = dy_ref[...] * _dsilu(c)
        dx_ref[...] = _conv_bwd_taps(x_, dc, w_ref, dw_ref, db_ref, SSD_CONV).astype(dx_ref.dtype)

    return pl.pallas_call(
        body, name="ssd_conv_bwd", grid=(C // cb,),
        in_specs=[_cols(L, cb), _lcols(SSD_CONV, cb, i), _lcols(1, cb, i), _cols(L, cb)],
        out_specs=[_cols(L, cb), _cols(SSD_CONV, cb), _cols(1, cb)],
        out_shape=[_sds((L, C), _MM), _sds((SSD_CONV, C)), _sds((1, C))],
        compiler_params=_cparams(("parallel",)),
    )(x, w, b, dy)


def _cm_conv_fwd(glu, w, b, i):
    L = glu.shape[0]
    cb = LANES
    nb = CM_CHANNELS // cb

    def body(a_ref, g_ref, w_ref, b_ref, o_ref):
        h = a_ref[...] * _sigmoid(g_ref[...])
        o_ref[...] = _conv_taps(h, w_ref, b_ref[0], CM_CONV)

    return pl.pallas_call(
        body, name="cm_conv_fwd", grid=(nb,),
        in_specs=[_cols(L, cb), _cols(L, cb, nb), _lcols(CM_CONV, cb, i), _lcols(1, cb, i)],
        out_specs=_cols(L, cb), out_shape=_sds((L, CM_CHANNELS)),
        compiler_params=_cparams(("parallel",)),
    )(glu, glu, w, b)


def _cm_conv_bwd(glu, w, i, dc):
    L = glu.shape[0]
    cb = LANES
    nb = CM_CHANNELS // cb

    def body(a_ref, g_ref, w_ref, dc_ref, da_ref, dg_ref, dw_ref, db_ref):
        a = a_ref[...]
        sg = _sigmoid(g_ref[...])
        dh = _conv_bwd_taps(a * sg, dc_ref[...], w_ref, dw_ref, db_ref, CM_CONV)
        da_ref[...] = (dh * sg).astype(da_ref.dtype)
        dg_ref[...] = (dh * a * sg * (1.0 - sg)).astype(dg_ref.dtype)

    return pl.pallas_call(
        body, name="cm_conv_bwd", grid=(nb,),
        in_specs=[_cols(L, cb), _cols(L, cb, nb), _lcols(CM_CONV, cb, i), _cols(L, cb)],
        out_specs=[_cols(L, cb), _cols(L, cb), _cols(CM_CONV, cb), _cols(1, cb)],
        out_shape=[_sds((L, CM_CHANNELS), _MM), _sds((L, CM_CHANNELS), _MM), _sds((CM_CONV, CM_CHANNELS)), _sds((1, CM_CHANNELS))],
        compiler_params=_cparams(("parallel",)),
    )(glu, glu, w, dc)


GRP = SSD_WIDTH // SSD_GROUPS


def _mix_post(y, z, attn, c, snw, lw, lb, i):
    L = y.shape[0]

    def body(y_ref, z_ref, a_ref, c_ref, snw_ref, lw_ref, lb_ref, o_ref):
        g = y_ref[...] * _silu(z_ref[...])
        for k in range(SSD_GROUPS):
            sl = slice(k * GRP, (k + 1) * GRP)
            gg = g[:, sl]
            r = lax.rsqrt(jnp.mean(gg * gg, axis=-1, keepdims=True) + RMS_EPS)
            o_ref[:, sl] = (gg * r * snw_ref[0, :, sl]).astype(_MM)
        o_ref[:, SSD_WIDTH:SSD_WIDTH + 512] = a_ref[...].astype(_MM)
        cv = c_ref[...]
        mu = jnp.mean(cv, axis=-1, keepdims=True)
        xc = cv - mu
        rs = lax.rsqrt(jnp.mean(xc * xc, axis=-1, keepdims=True) + LN_EPS)
        o_ref[:, SSD_WIDTH + 512:] = _silu(xc * rs * lw_ref[0] + lb_ref[0]).astype(_MM)

    return pl.pallas_call(
        body, name="mix_post", grid=(L // TM,),
        in_specs=[_rows(SSD_WIDTH), _rows(SSD_WIDTH), _rows(512), _rows(512),
                  _lp(SSD_WIDTH, i), _lp(512, i), _lp(512, i)],
        out_specs=_rows(D_MIX), out_shape=_sds((L, D_MIX), _MM),
        compiler_params=_cparams(("parallel",)),
    )(y, z, attn, c, snw, lw, lb)


def _mix_post_bwd(dcat, y, z, c, snw, lw, lb, i, after):
    L = y.shape[0]

    def body(d_ref, y_ref, z_ref, c_ref, snw_ref, lw_ref, lb_ref, after_ref,
             dy_ref, dz_ref, dc_ref, dsnw_ref, dlw_ref, dlb_ref):
        @pl.when(pl.program_id(0) == 0)
        def _():
            dsnw_ref[...] = jnp.zeros_like(dsnw_ref)
            dlw_ref[...] = jnp.zeros_like(dlw_ref)
            dlb_ref[...] = jnp.zeros_like(dlb_ref)

        yv = y_ref[...]
        zv = z_ref[...]
        sz = _silu(zv)
        g = yv * sz
        for k in range(SSD_GROUPS):
            sl = slice(k * GRP, (k + 1) * GRP)
            dgg, dwx = _rms_bwd(d_ref[:, sl], g[:, sl], snw_ref[0, :, sl], 1.0 / GRP)
            dsnw_ref[:, sl] += jnp.sum(dwx, axis=0, keepdims=True)
            dy_ref[:, sl] = dgg * sz[:, sl]
            dz_ref[:, sl] = (dgg * yv[:, sl] * _dsilu(zv[:, sl])).astype(dz_ref.dtype)
        cv = c_ref[...]
        mu = jnp.mean(cv, axis=-1, keepdims=True)
        xc = cv - mu
        rs = lax.rsqrt(jnp.mean(xc * xc, axis=-1, keepdims=True) + LN_EPS)
        xh = xc * rs
        ln = xh * lw_ref[0] + lb_ref[0]
        dln = d_ref[:, SSD_WIDTH + 512:] * _dsilu(ln)
        dlb_ref[...] += jnp.sum(dln, axis=0, keepdims=True)
        dlw_ref[...] += jnp.sum(dln * xh, axis=0, keepdims=True)
        dxh = dln * lw_ref[0]
        dc_ref[...] = rs * (dxh - jnp.mean(dxh, axis=-1, keepdims=True)
                            - xh * jnp.mean(dxh * xh, axis=-1, keepdims=True))

    return pl.pallas_call(
        body, name="mix_post_bwd", grid=(L // TM,),
        in_specs=[_rows(D_MIX), _rows(SSD_WIDTH), _rows(SSD_WIDTH), _rows(512),
                  _lp(SSD_WIDTH, i), _lp(512, i), _lp(512, i), _ANY],
        out_specs=[_rows(SSD_WIDTH), _rows(SSD_WIDTH), _rows(512), _full((1, SSD_WIDTH)), _full((1, 512)), _full((1, 512))],
        out_shape=[_sds((L, SSD_WIDTH)), _sds((L, SSD_WIDTH), _MM), _sds((L, 512)), _sds((1, SSD_WIDTH)), _sds((1, 512)), _sds((1, 512))],
        compiler_params=_cparams(("arbitrary",)),
    )(dcat, y, z, c, snw, lw, lb, after)


def _seg_mean_matrix():
    i = lax.broadcasted_iota(jnp.int32, (LANES, LANES), 0)
    j = lax.broadcasted_iota(jnp.int32, (LANES, LANES), 1)
    return jnp.where(i // ATTN_HEAD_DIM == j // ATTN_HEAD_DIM, 1.0 / ATTN_HEAD_DIM, 0.0).astype(F32)


def _rot_matrix():
    i = lax.broadcasted_iota(jnp.int32, (LANES, LANES), 0)
    j = lax.broadcasted_iota(jnp.int32, (LANES, LANES), 1)
    half = ATTN_HEAD_DIM // 2
    lo = (j % ATTN_HEAD_DIM) < half
    return jnp.where(lo & (i == j + half), -1.0, jnp.where((~lo) & (i == j - half), 1.0, 0.0)).astype(F32)


N_QK_TILES = 5
QK_W = N_QK_TILES * LANES


def _qk_prep(qkv, qw, kw, i, cos, sin):
    L = qkv.shape[0]

    def body(x_ref, qw_ref, kw_ref, c_ref, s_ref, o_ref):
        m64 = _seg_mean_matrix()
        rot = _rot_matrix()
        cs, sn = c_ref[...], s_ref[...]
        for t in range(N_QK_TILES):
            sl = slice(t * LANES, (t + 1) * LANES)
            x = x_ref[:, sl]
            w = qw_ref[0] if t < 4 else kw_ref[0]
            xn = x * lax.rsqrt(_mmx(x * x, m64) + RMS_EPS) * w
            o_ref[:, sl] = xn * cs + _mmx(xn, rot) * sn

    return pl.pallas_call(
        body, name="qk_prep", grid=(L // TM,),
        in_specs=[_rows(QK_W), _lp(LANES, i), _lp(LANES, i), _rows(LANES), _rows(LANES)],
        out_specs=_rows(QK_W), out_shape=_sds((L, QK_W)),
        compiler_params=_cparams(("parallel",)),
    )(qkv, qw, kw, cos, sin)


def _qk_prep_bwd(dq, dk, qkv, qw, kw, i, cos, sin):
    L = qkv.shape[0]

    def body(dq_ref, dk_ref, x_ref, qw_ref, kw_ref, c_ref, s_ref, dx_ref, dqw_ref, dkw_ref):
        @pl.when(pl.program_id(0) == 0)
        def _():
            dqw_ref[...] = jnp.zeros_like(dqw_ref)
            dkw_ref[...] = jnp.zeros_like(dkw_ref)

        m64 = _seg_mean_matrix()
        rot = _rot_matrix()
        cs, sn = c_ref[...], s_ref[...]
        for t in range(N_QK_TILES):
            sl = slice(t * LANES, (t + 1) * LANES)
            x = x_ref[:, sl]
            dy = dq_ref[:, sl] if t < 4 else dk_ref[...]
            w = qw_ref[0] if t < 4 else kw_ref[0]
            dxn = dy * cs - _mmx(dy * sn, rot)
            r = lax.rsqrt(_mmx(x * x, m64) + RMS_EPS)
            xh = x * r
            dxh = dxn * w
            dx_ref[:, sl] = (r * (dxh - xh * _mmx(dxh * xh, m64))).astype(dx_ref.dtype)
            dw = jnp.sum(dxn * xh, axis=0, keepdims=True)
            if t < 4:
                dqw_ref[...] += dw
            else:
                dkw_ref[...] += dw

    return pl.pallas_call(
        body, name="qk_prep_bwd", grid=(L // TM,),
        in_specs=[_rows(512), _rows(LANES), _rows(QK_W), _lp(LANES, i), _lp(LANES, i), _rows(LANES), _rows(LANES)],
        out_specs=[_rows(QK_W), _full((1, LANES)), _full((1, LANES))],
        out_shape=[_sds((L, QK_W), _MM), _sds((1, LANES)), _sds((1, LANES))],
        compiler_params=_cparams(("arbitrary",)),
    )(dq, dk, qkv, qw, kw, cos, sin)


HPG = 4
SCALE = 1.0 / math.sqrt(ATTN_HEAD_DIM)


def _heads_to_rows(q, g):
    return jnp.concatenate([q[:, (HPG * g + r) * ATTN_HEAD_DIM:(HPG * g + r + 1) * ATTN_HEAD_DIM] for r in range(HPG)], axis=0)


def _rows_to_heads(parts):
    return jnp.concatenate([p[r * Q:(r + 1) * Q] for p in parts for r in range(HPG)], axis=1)


def _attn_probs(q, k_own, k_prev, n, sink_ref, base):
    s_own = _mm(q, k_own, _NT) * SCALE
    s_prev = _mm(q, k_prev, _NT) * SCALE
    own = lax.broadcasted_iota(jnp.int32, s_own.shape, 1) <= lax.broadcasted_iota(jnp.int32, s_own.shape, 0) % Q
    s = jnp.where(own, s_own, jnp.where(n >= 1, s_prev, -jnp.inf))
    hrow = lax.broadcasted_iota(jnp.int32, (HPG * Q, 1), 0) // Q
    sink = jnp.zeros((HPG * Q, 1), F32)
    for r in range(HPG):
        sink = jnp.where(hrow == r, sink_ref[base + r], sink)
    m = jnp.maximum(jnp.max(s, axis=1, keepdims=True), sink)
    p = jnp.exp(s - m)
    es = jnp.exp(sink - m)
    inv = 1.0 / (jnp.sum(p, axis=1, keepdims=True) + es)
    return p * inv, own, es * inv


def _kv_blocks(ref, n):
    own = ref[pl.ds(pl.multiple_of(n * Q, Q), Q), :]
    prev = ref[pl.ds(pl.multiple_of(jnp.maximum(n - 1, 0) * Q, Q), Q), :]
    return own, prev


def _attn_fwd(qk, qkv, sinks, i):
    L = qk.shape[0]

    def body(sink_ref, q_ref, k_ref, v_ref, o_ref):
        n = pl.program_id(0)
        q = q_ref[...]
        k_own, k_prev = _kv_blocks(k_ref, n)
        v_own, v_prev = _kv_blocks(v_ref, n)
        outs = []
        for g in range(2):
            sl = slice(g * ATTN_HEAD_DIM, (g + 1) * ATTN_HEAD_DIM)
            p, own, _ = _attn_probs(_heads_to_rows(q, g), k_own[:, sl], k_prev[:, sl], n, sink_ref, i * ATTN_Q_HEADS + g * HPG)
            outs.append(_mm(jnp.where(own, p, 0.0), v_own[:, sl]) + _mm(jnp.where(own, 0.0, p), v_prev[:, sl]))
        o_ref[...] = _rows_to_heads(outs)

    return pl.pallas_call(
        body, name="attn_fwd", grid=(L // Q,),
        in_specs=[pl.BlockSpec(memory_space=pltpu.SMEM), _rows(512, Q),
                  pl.BlockSpec((L, LANES), lambda n: (0, 4)), pl.BlockSpec((L, LANES), lambda n: (0, 5))],
        out_specs=_rows(512, Q), out_shape=_sds((L, 512)),
        compiler_params=_cparams(("parallel",)),
    )(sinks, qk, qk, qkv)


def _attn_bwd(qk, qkv, sinks, i, dcat):
    L = qk.shape[0]

    def body(sink_ref, q_ref, k_ref, v_ref, do_ref, dq_ref, dk_ref, dv_ref, ds_ref):
        n = pl.program_id(0)

        @pl.when(n == 0)
        def _():
            dk_ref[...] = jnp.zeros_like(dk_ref)
            dv_ref[...] = jnp.zeros_like(dv_ref)
            ds_ref[...] = jnp.zeros_like(ds_ref)

        q = q_ref[...]
        do_all = do_ref[...]
        k_own, k_prev = _kv_blocks(k_ref, n)
        v_own, v_prev = _kv_blocks(v_ref, n)
        hrow = lax.broadcasted_iota(jnp.int32, (HPG * Q, 1), 0) // Q
        orow = lax.broadcasted_iota(jnp.int32, (8, LANES), 0)
        dqs, dks, dvs = [], [[], []], [[], []]
        acc = jnp.zeros((8, LANES), F32)
        for g in range(2):
            sl = slice(g * ATTN_HEAD_DIM, (g + 1) * ATTN_HEAD_DIM)
            qg = _heads_to_rows(q, g)
            do = _heads_to_rows(do_all, g)
            p, own, ps = _attn_probs(qg, k_own[:, sl], k_prev[:, sl], n, sink_ref, i * ATTN_Q_HEADS + g * HPG)
            dp = jnp.where(own, _mm(do, v_own[:, sl], _NT), _mm(do, v_prev[:, sl], _NT))
            delta = jnp.sum(p * dp, axis=1, keepdims=True)
            ds = p * (dp - delta)
            parts = ((jnp.where(own, ds, 0.0), jnp.where(own, p, 0.0)), (jnp.where(own, 0.0, ds), jnp.where(own, 0.0, p)))
            dqs.append((_mm(parts[0][0], k_own[:, sl]) + _mm(parts[1][0], k_prev[:, sl])) * SCALE)
            for b, (ds_b, p_b) in enumerate(parts):
                dks[b].append(_mm(ds_b, qg, _TN) * SCALE)
                dvs[b].append(_mm(p_b, do, _TN))
            dsink = -(ps * delta)
            for r in range(HPG):
                tot = jnp.sum(jnp.where(hrow == r, dsink, 0.0), axis=0, keepdims=True)
                acc = acc + jnp.where(orow == g * HPG + r, tot, 0.0)
        dq_ref[...] = _rows_to_heads(dqs)
        so = pl.multiple_of(n * Q, Q)
        sp = pl.multiple_of(jnp.maximum(n - 1, 0) * Q, Q)
        dk_ref[pl.ds(so, Q), :] += jnp.concatenate(dks[0], axis=1)
        dv_ref[pl.ds(so, Q), :] += jnp.concatenate(dvs[0], axis=1)
        dk_ref[pl.ds(sp, Q), :] += jnp.concatenate(dks[1], axis=1)
        dv_ref[pl.ds(sp, Q), :] += jnp.concatenate(dvs[1], axis=1)
        ds_ref[...] += acc

    return pl.pallas_call(
        body, name="attn_bwd", grid=(L // Q,),
        in_specs=[pl.BlockSpec(memory_space=pltpu.SMEM), _rows(512, Q),
                  pl.BlockSpec((L, LANES), lambda n: (0, 4)), pl.BlockSpec((L, LANES), lambda n: (0, 5)),
                  _rows(512, Q, 2)],
        out_specs=[_rows(512, Q), _full((L, LANES)), _full((L, LANES)), _full((8, LANES))],
        out_shape=[_sds((L, 512)), _sds((L, LANES)), _sds((L, LANES)), _sds((8, LANES))],
        compiler_params=_cparams(("arbitrary",)),
    )(sinks, qk, qk, qkv, dcat)


N_PAIR = SSD_HEADS // 2
P = 64
OFF_B = SSD_WIDTH
OFF_C = SSD_WIDTH + SSD_GROUPS * SSD_STATE


def _expand_matrix():
    i = lax.broadcasted_iota(jnp.int32, (LANES, SSD_WIDTH), 0)
    j = lax.broadcasted_iota(jnp.int32, (LANES, SSD_WIDTH), 1)
    return jnp.where(j // P == i, 1.0, 0.0).astype(F32)


def _ssd_chunk_common(dtr_ref, bias_ref, alog_ref):
    dt = jax.nn.softplus(dtr_ref[...] + bias_ref[0])
    a = -jnp.exp(alog_ref[0])
    adt = dt * a
    ri = lax.broadcasted_iota(jnp.int32, (Q, Q), 0)
    ci = lax.broadcasted_iota(jnp.int32, (Q, Q), 1)
    causal = ri >= ci
    tri = jnp.where(causal, 1.0, 0.0).astype(F32)
    acs = _mmx(tri, adt, exact="a")
    em = _expand_matrix()
    acs_e = _mmx(acs, em)
    dt_e = _mmx(dt, em)
    alast_e = acs_e[Q - 1:Q, :]
    return dt, a, acs, causal, tri, em, acs_e, dt_e, alast_e


def _ssd_fwd(xbc, dtr, bias, alog, d_e, i):
    L = xbc.shape[0]
    nc = L // Q

    def body(xbc_ref, dtr_ref, bias_ref, alog_ref, de_ref, y_ref, hp_ref, st_ref):
        @pl.when(pl.program_id(0) == 0)
        def _():
            st_ref[...] = jnp.zeros_like(st_ref)

        dt, a, acs, causal, tri, em, acs_e, dt_e, alast_e = _ssd_chunk_common(dtr_ref, bias_ref, alog_ref)
        acs_t = acs.T
        x = xbc_ref[:, :SSD_WIDTH]
        xdt = x * dt_e
        ea_e = jnp.exp(acs_e)
        xds = xdt * jnp.exp(alast_e - acs_e)
        cd_e = jnp.exp(alast_e)
        lane = lax.broadcasted_iota(jnp.int32, (Q, LANES), 1)
        lo = lane < P
        for g in range(SSD_GROUPS):
            bg = xbc_ref[:, OFF_B + g * SSD_STATE:OFF_B + (g + 1) * SSD_STATE]
            cg = xbc_ref[:, OFF_C + g * SSD_STATE:OFF_C + (g + 1) * SSD_STATE]
            cb = _mm(cg, bg, _NT)
            for pp in range(N_PAIR // SSD_GROUPS):
                pr = g * (N_PAIR // SSD_GROUPS) + pp
                sl = slice(pr * LANES, (pr + 1) * LANES)
                xdt_p = xdt[:, sl]
                yd = jnp.zeros((Q, LANES), F32)
                for half in range(2):
                    h = 2 * pr + half
                    rowb = jnp.broadcast_to(acs_t[h:h + 1, :], (Q, Q))
                    lm = jnp.exp(jnp.where(causal, rowb.T - rowb, -jnp.inf))
                    xh = jnp.where(lo if half == 0 else ~lo, xdt_p, 0.0)
                    yd = yd + _mm(cb * lm, xh)
                hp = st_ref[pr]
                hp_ref[0, pr] = hp
                yoff = _mm(cg, hp) * ea_e[:, sl]
                y_ref[:, sl] = yd + yoff + x[:, sl] * de_ref[0, :, sl]
                st_ref[pr] = hp * cd_e[:, sl] + _mm(bg, xds[:, sl], _TN)

    return pl.pallas_call(
        body, name="ssd_fwd", grid=(nc,),
        in_specs=[_rows(SSD_XBC, Q), _rows(LANES, Q), _lp(LANES, i), _lp(LANES, i), _lp(SSD_WIDTH, i)],
        out_specs=[_rows(SSD_WIDTH, Q), pl.BlockSpec((1, N_PAIR, SSD_STATE, LANES), lambda c: (c, 0, 0, 0))],
        out_shape=[_sds((L, SSD_WIDTH)), _sds((nc, N_PAIR, SSD_STATE, LANES))],
        scratch_shapes=[pltpu.VMEM((N_PAIR, SSD_STATE, LANES), F32)],
        compiler_params=_cparams(("arbitrary",)),
    )(xbc, dtr, bias, alog, d_e)


def _ssd_bwd(xbc, dtr, bias, alog, d_e, i, hprev, dy):
    L = xbc.shape[0]
    nc = L // Q
    rev = lambda c: (nc - 1 - c, 0)

    def body(xbc_ref, dtr_ref, bias_ref, alog_ref, de_ref, hp_ref, dy_ref,
             dxbc_ref, ddtr_ref, dbias_ref, dalog_ref, dd_ref, dst_ref):
        @pl.when(pl.program_id(0) == 0)
        def _():
            dst_ref[...] = jnp.zeros_like(dst_ref)
            dbias_ref[...] = jnp.zeros_like(dbias_ref)
            dalog_ref[...] = jnp.zeros_like(dalog_ref)
            dd_ref[...] = jnp.zeros_like(dd_ref)

        dt, a, acs, causal, tri, em, acs_e, dt_e, alast_e = _ssd_chunk_common(dtr_ref, bias_ref, alog_ref)
        acs_t = acs.T
        x = xbc_ref[:, :SSD_WIDTH]
        dy = dy_ref[...]
        xdt = x * dt_e
        ea_e = jnp.exp(acs_e)
        dse = jnp.exp(alast_e - acs_e)
        xds = xdt * dse
        cd_e = jnp.exp(alast_e)
        lane = lax.broadcasted_iota(jnp.int32, (Q, LANES), 1)
        lo = lane < P
        sub = lax.broadcasted_iota(jnp.int32, (Q, Q), 0)
        lan = lax.broadcasted_iota(jnp.int32, (Q, Q), 1)

        da_rows = jnp.zeros((Q, Q), F32)
        da_cols_t = jnp.zeros((Q, Q), F32)
        dxdt_parts = []
        wyoff_parts = []
        dcd_parts = []
        dxds_parts = []
        for g in range(SSD_GROUPS):
            bg = xbc_ref[:, OFF_B + g * SSD_STATE:OFF_B + (g + 1) * SSD_STATE]
            cg = xbc_ref[:, OFF_C + g * SSD_STATE:OFF_C + (g + 1) * SSD_STATE]
            cb = _mm(cg, bg, _NT)
            dcb = jnp.zeros((Q, Q), F32)
            dcg = jnp.zeros((Q, SSD_STATE), F32)
            dbg = jnp.zeros((Q, SSD_STATE), F32)
            for pp in range(N_PAIR // SSD_GROUPS):
                pr = g * (N_PAIR // SSD_GROUPS) + pp
                sl = slice(pr * LANES, (pr + 1) * LANES)
                xdt_p = xdt[:, sl]
                dy_p = dy[:, sl]
                dxdt_p = jnp.zeros((Q, LANES), F32)
                for half in range(2):
                    h = 2 * pr + half
                    hm = lo if half == 0 else ~lo
                    rowb = jnp.broadcast_to(acs_t[h:h + 1, :], (Q, Q))
                    lm = jnp.exp(jnp.where(causal, rowb.T - rowb, -jnp.inf))
                    m = cb * lm
                    dyh = jnp.where(hm, dy_p, 0.0)
                    gmat = _mm(dyh, xdt_p, _NT)
                    w = gmat * m
                    da_rows = da_rows + jnp.where(lan == h, jnp.sum(w, axis=1, keepdims=True), 0.0)
                    da_cols_t = da_cols_t + jnp.where(sub == h, jnp.sum(w, axis=0, keepdims=True), 0.0)
                    dcb = dcb + gmat * lm
                    dxdt_p = dxdt_p + _mm(m, dyh, _TN)
                hp = hp_ref[0, pr]
                dt_off = dy_p * ea_e[:, sl]
                t_off = _mm(cg, hp)
                wyoff_parts.append(dt_off * t_off)
                dcg = dcg + _mm(dt_off, hp, _NT)
                dhp = _mm(cg, dt_off, _TN)
                dS = dst_ref[pr]
                dxds_p = _mm(bg, dS)
                dbg = dbg + _mm(xds[:, sl], dS, _NT)
                dxds_parts.append(dxds_p)
                dxdt_parts.append(dxdt_p + dxds_p * dse[:, sl])
                dcd_parts.append(jnp.sum(dS * hp, axis=0, keepdims=True))
                dst_ref[pr] = dS * cd_e[:, sl] + dhp
            dcg = dcg + _mm(dcb, bg)
            dbg = dbg + _mm(dcb, cg, _TN)
            dxbc_ref[:, OFF_C + g * SSD_STATE:OFF_C + (g + 1) * SSD_STATE] = dcg
            dxbc_ref[:, OFF_B + g * SSD_STATE:OFF_B + (g + 1) * SSD_STATE] = dbg
        dxdt = jnp.concatenate(dxdt_parts, axis=1)
        dxds = jnp.concatenate(dxds_parts, axis=1)
        wyoff = jnp.concatenate(wyoff_parts, axis=1)
        dcd = jnp.concatenate(dcd_parts, axis=1)
        dxbc_ref[:, :SSD_WIDTH] = dy * de_ref[0] + dxdt * dt_e
        zds = dxds * xds
        dacs = _mmx(wyoff - zds, em, _NT) + da_rows - da_cols_t.T
        dalast = _mmx(jnp.broadcast_to(jnp.sum(zds, axis=0, keepdims=True) + dcd * cd_e, (8, SSD_WIDTH)), em, _NT)[0:1, :]
        dacs = dacs + jnp.where(sub == Q - 1, dalast, 0.0)
        dadt = _mmx(tri, dacs, _TN, exact="a")
        ddt = dadt * a + _mmx(dxdt * x, em, _NT)
        ddtr = ddt * _sigmoid(dtr_ref[...] + bias_ref[0])
        ddtr_ref[...] = ddtr.astype(ddtr_ref.dtype)
        row0 = lax.broadcasted_iota(jnp.int32, (8, LANES), 0) == 0
        dbias_ref[...] += jnp.where(row0, jnp.sum(ddtr, axis=0, keepdims=True), 0.0)
        dalog_ref[...] += jnp.where(row0, jnp.sum(dadt * dt, axis=0, keepdims=True) * a, 0.0)
        ddx = _mmx(jnp.broadcast_to(jnp.sum(dy * x, axis=0, keepdims=True), (8, SSD_WIDTH)), em, _NT)
        dd_ref[...] += jnp.where(row0, ddx, 0.0)

    acc = _full((8, LANES))
    return pl.pallas_call(
        body, name="ssd_bwd", grid=(nc,),
        in_specs=[pl.BlockSpec((Q, SSD_XBC), rev), pl.BlockSpec((Q, LANES), rev),
                  _lp(LANES, i), _lp(LANES, i), _lp(SSD_WIDTH, i),
                  pl.BlockSpec((1, N_PAIR, SSD_STATE, LANES), lambda c: (nc - 1 - c, 0, 0, 0)), pl.BlockSpec((Q, SSD_WIDTH), rev)],
        out_specs=[pl.BlockSpec((Q, SSD_XBC), rev), pl.BlockSpec((Q, LANES), rev), acc, acc, acc],
        out_shape=[_sds((L, SSD_XBC)), _sds((L, LANES), _MM), _sds((8, LANES)), _sds((8, LANES)), _sds((8, LANES))],
        scratch_shapes=[pltpu.VMEM((N_PAIR, SSD_STATE, LANES), F32)],
        compiler_params=_cparams(("arbitrary",)),
    )(xbc, dtr, bias, alog, d_e, hprev, dy)


def _rope_tables(L):
    inv_freq = ROPE_THETA ** (-jnp.arange(0, ATTN_HEAD_DIM, 2, dtype=F32) / ATTN_HEAD_DIM)
    ang = jnp.arange(L, dtype=F32)[:, None] * inv_freq[None, :]
    return jnp.tile(jnp.cos(ang), (1, 4)), jnp.tile(jnp.sin(ang), (1, 4))


def _stacked_params(small, conv_w, cm_w):
    row = lambda a: a[:, None, :]
    pad = lambda a: jnp.pad(a, ((0, 0), (0, LANES - a.shape[1])))[:, None, :]
    return dict(
        nw_mix=row(small["norm_mix_w"]), conv_w=conv_w, conv_b=row(small["ssd_conv_b"]),
        dt_bias=pad(small["ssd_dt_bias"]), a_log=pad(small["ssd_a_log"]),
        d_e=row(jnp.repeat(small["ssd_d"], P, axis=1)), snw=row(small["ssd_norm_w"]),
        qw=row(jnp.tile(small["q_norm_w"], (1, 2))), kw=row(jnp.tile(small["k_norm_w"], (1, 2))),
        sinks=small["attn_sinks"].reshape(-1), cm_w=cm_w, cm_b=row(small["cm_dw_b"]),
        ln_w=row(small["cm_ln_w"]), ln_b=row(small["cm_ln_b"]), nw_mlp=row(small["norm_mlp_w"]))


def _layer_fwd(x, p, w_in8, late_weights, before_down, i, cos, sin, after):
    w_in = _w_in_regroup(w_in8, after)
    h, z, xbc, qkv, glu, dtr = _in_proj(x, p["nw_mix"], i, w_in)
    xbc_c = _ssd_conv_fwd(xbc, p["conv_w"], p["conv_b"], i)
    y_ssd, hprev = _ssd_fwd(xbc_c, dtr, p["dt_bias"], p["a_log"], p["d_e"], i)
    qk = _qk_prep(qkv, p["qw"], p["kw"], i, cos, sin)
    attn = _attn_fwd(qk, qkv, p["sinks"], i)
    c = _cm_conv_fwd(glu, p["cm_w"], p["cm_b"], i)
    ycat = _mix_post(y_ssd, z, attn, c, p["snw"], p["ln_w"], p["ln_b"], i)
    w_out8, w_up8, w_down8 = late_weights(ycat)
    x1 = _mm_res(ycat, w_out8, i, x, "out_proj", ycat)
    hm, up, act = _mlp_up(x1, p["nw_mlp"], i, w_up8)
    x2 = _mm_res(act, w_down8, i, x1, "mlp_down", before_down(act))
    saved = dict(x=x, h=h, z=z, xbc=xbc, qkv=qkv, glu=glu, dtr=dtr, xbc_c=xbc_c, y_ssd=y_ssd, hprev=hprev,
                 qk=qk, c=c, ycat=ycat, x1=x1, hm=hm, up=up, act=act, w_in=w_in,
                 w_out8=w_out8, w_up8=w_up8, w_down8=w_down8)
    return x2, saved


def _layer_bwd_mlp(dx2, p, i, s, after):
    d_up = _mlp_down_bwd(dx2, s["w_down8"], i, s["up"], after)
    g_down = _mm_tn(s["act"], dx2, "dw_down", 512, D_MODEL, "rows")
    g_up = _mm_tn(s["hm"], d_up, "dw_up", D_MODEL, FF_SHARD, "cols")
    dx1, g_nw_mlp = _mlp_up_bwd(d_up, s["w_up8"], i, dx2, s["x1"], p["nw_mlp"])
    dcat = _out_proj_bwd(dx1, s["w_out8"], i, g_nw_mlp)
    g_out = _mm_tn(s["ycat"], dx1, "dw_out", 512, D_MODEL, "rows")
    return dx1, dcat, g_out, g_up, g_down, g_nw_mlp


def _layer_bwd_mix(dx1, dcat, g_nw_mlp, p, i, s, cos, sin, after):
    dy_ssd, dz, dc, g_snw, g_lw, g_lb = _mix_post_bwd(dcat, s["y_ssd"], s["z"], s["c"], p["snw"], p["ln_w"], p["ln_b"], i, after)
    da, dg, g_cmw, g_cmb = _cm_conv_bwd(s["glu"], p["cm_w"], i, dc)
    dq, dk, dv, dsk = _attn_bwd(s["qk"], s["qkv"], p["sinks"], i, dcat)
    dqk_raw, g_qw, g_kw = _qk_prep_bwd(dq, dk, s["qkv"], p["qw"], p["kw"], i, cos, sin)
    dxbc_c, ddtr, g_bias, g_alog, g_d = _ssd_bwd(s["xbc_c"], s["dtr"], p["dt_bias"], p["a_log"], p["d_e"], i, s["hprev"], dy_ssd)
    dxbc, g_convw, g_convb = _ssd_conv_bwd(s["xbc"], p["conv_w"], p["conv_b"], i, dxbc_c)
    dx, g_nw_mix, du = _in_proj_bwd([dz, dxbc, dqk_raw, dv, da, dg, ddtr], s["w_in"], dx1, s["x"], p["nw_mix"], i)
    g_in = _g_in_split(_mm_tn(s["h"], du, "dw_in", 512, 640, "flat"))
    half = ATTN_HEAD_DIM
    small = dict(
        norm_mix_w=g_nw_mix[0], ssd_conv_b=g_convb[0], ssd_dt_bias=g_bias[0, :SSD_HEADS], ssd_a_log=g_alog[0, :SSD_HEADS],
        ssd_d=g_d[0, :SSD_HEADS], ssd_norm_w=g_snw[0], q_norm_w=g_qw[0, :half] + g_qw[0, half:],
        k_norm_w=g_kw[0, :half] + g_kw[0, half:], attn_sinks=dsk[:, 0],
        cm_dw_b=g_cmb[0], cm_ln_w=g_lw[0], cm_ln_b=g_lb[0], norm_mlp_w=g_nw_mlp[0],
        ssd_conv_w=g_convw, cm_dw_w=g_cmw)
    return dx, g_in, small


MESH = pl.DeviceIdType.MESH
_ANY = pl.BlockSpec(memory_space=pl.ANY)


def _coords():
    return lax.axis_index("x"), lax.axis_index("y"), lax.axis_index("c")


def _all_gather(xs, name):
    nt = len(xs)

    def body(*refs):
        x_refs, out_refs = refs[:nt], refs[nt:2 * nt]
        send_sems, recv_sems, local_sems = refs[2 * nt:]
        x, y, c = _coords()
        me, sibling = (x, y, c), (x, y, 1 - c)
        chips = [(1 - x, y), (x, 1 - y), (1 - x, 1 - y)]

        def slot(t, px, py, pc):
            return out_refs[t].at[4 * px + 2 * py + pc]

        def copy(t, k, block, to, src=None):
            return pltpu.make_async_remote_copy(
                src_ref=slot(t, *block) if src is None else src, dst_ref=slot(t, *block),
                send_sem=send_sems.at[7 * t + k], recv_sem=recv_sems.at[7 * t + k], device_id=to, device_id_type=MESH)

        mine = [pltpu.make_async_copy(x_refs[t], slot(t, *me), local_sems.at[t]) for t in range(nt)]
        for cp in mine:
            cp.start()
        first = []
        for t in range(nt):
            first.append(copy(t, 0, me, sibling, src=x_refs[t]))
            first += [copy(t, 1 + j, me, (*chip, c), src=x_refs[t]) for j, chip in enumerate(chips)]
        for cp in first:
            cp.start()
        passed = []
        for j, chip in enumerate(chips):
            for t in range(nt):
                copy(t, 1 + j, (*chip, c), me).wait_recv()
                passed.append(copy(t, 4 + j, (*chip, c), sibling))
                passed[-1].start()
        for t in range(nt):
            copy(t, 0, sibling, me).wait_recv()
            for j, chip in enumerate(chips):
                copy(t, 4 + j, (*chip, 1 - c), me).wait_recv()
        for cp in first + passed:
            cp.wait_send()
        for cp in mine:
            cp.wait()

    return pl.pallas_call(
        body, name=name, out_shape=[_sds((N_DEV,) + a.shape, a.dtype) for a in xs],
        in_specs=[_ANY] * nt, out_specs=[_ANY] * nt,
        scratch_shapes=[pltpu.SemaphoreType.DMA((7 * nt,)), pltpu.SemaphoreType.DMA((7 * nt,)), pltpu.SemaphoreType.DMA((nt,))],
    )(*xs)


def _peer_chips(x, y):
    return [(1 - x, y), (x, 1 - y), (1 - x, 1 - y)]


def _copies_per_tensor(kind):
    return {"gather": 3, "forward": 4, "scatter": 7, "scatter_chips": 3}[kind]


def _ici_copies(src_refs, land_refs, send_sems, recv_sems, kind):
    x, y, c = _coords()
    if kind == "forward":
        sends, recvs = [], []
        for t, d in enumerate(land_refs):
            for j, (px, py) in enumerate([(x, y)] + _peer_chips(x, y)):
                sems = dict(send_sem=send_sems.at[4 * t + j], recv_sem=recv_sems.at[4 * t + j],
                            device_id=(x, y, 1 - c), device_id_type=MESH)
                src = d.at[4 * px + 2 * py + c]
                sends.append(pltpu.make_async_remote_copy(src_ref=src, dst_ref=src, **sems))
                recvs.append(pltpu.make_async_remote_copy(src_ref=src, dst_ref=d.at[4 * px + 2 * py + 1 - c], **sems))
        return sends, recvs
    flip = lambda v, bit: 1 - v if bit else v
    if kind == "scatter":
        peers = [(flip(x, k & 4), flip(y, k & 2), flip(c, k & 1)) for k in range(1, N_DEV)]
    else:
        peers = [(px, py, c) for px, py in _peer_chips(x, y)]
    index = (lambda px, py, pc: 2 * px + py) if kind == "scatter_chips" else (lambda px, py, pc: 4 * px + 2 * py + pc)
    me = index(x, y, c)
    sends, recvs = [], []
    for t, d in enumerate(land_refs):
        for j, p in enumerate(peers):
            src = d.at[me] if kind == "gather" else src_refs[t].at[index(*p)]
            k = len(peers) * t + j
            sems = dict(send_sem=send_sems.at[k], recv_sem=recv_sems.at[k], device_id=p, device_id_type=MESH)
            sends.append(pltpu.make_async_remote_copy(src_ref=src, dst_ref=d.at[me], **sems))
            recvs.append(pltpu.make_async_remote_copy(src_ref=src, dst_ref=d.at[index(*p)], **sems))
    return sends, recvs


_HBM = pl.BlockSpec(memory_space=pltpu.HBM)
_SEMS = pl.BlockSpec(memory_space=pltpu.SEMAPHORE)
_EFFECT = pltpu.SideEffectType.DATAFLOW_SIDE_EFFECTING


def _hbm(a):
    return pltpu.with_memory_space_constraint(a, pltpu.HBM)


def _ici_start(srcs, lands, after, name, kind):
    ns, n = len(srcs), len(lands)
    nt = ns + n

    def body(*refs):
        sends, _ = _ici_copies(refs[:ns], refs[ns:nt], refs[nt + 1], refs[nt + 2], kind)
        for cp in sends:
            cp.start()
        refs[-1][...] = jnp.zeros_like(refs[-1])

    thru = srcs + lands
    out = pl.pallas_call(
        body, name=name,
        out_shape=(pltpu.SemaphoreType.DMA((_copies_per_tensor(kind) * n,)),) * 2
        + tuple(pltpu.HBM(a.shape, a.dtype) for a in thru) + (_sds((8, LANES)),),
        in_specs=[_HBM] * nt + [_ANY],
        out_specs=(_SEMS, _SEMS) + (_HBM,) * nt + (pl.BlockSpec(memory_space=pltpu.VMEM),),
        input_output_aliases={k: 2 + k for k in range(nt)},
        compiler_params=pltpu.CompilerParams(has_side_effects=_EFFECT),
    )(*[_hbm(a) for a in thru], after)
    return out[0], out[1], list(out[2:2 + ns]), list(out[2 + ns:2 + nt]), out[-1]


def _ici_wait(started, after, name, kind):
    send_sems, recv_sems, srcs, lands, _ = started
    ns, n = len(srcs), len(lands)
    nt = ns + n

    def body(*refs):
        sends, recvs = _ici_copies(refs[:ns], refs[ns:nt], refs[nt], refs[nt + 1], kind)
        for s, r in zip(sends, recvs):
            s.wait_send()
            r.wait_recv()

    thru = srcs + lands
    out = pl.pallas_call(
        body, name=name, out_shape=tuple(pltpu.HBM(a.shape, a.dtype) for a in thru),
        in_specs=[_HBM] * nt + [_SEMS, _SEMS, _ANY], out_specs=(_HBM,) * nt,
        input_output_aliases={k: k for k in range(nt)},
        compiler_params=pltpu.CompilerParams(has_side_effects=_EFFECT),
    )(*thru, send_sems, recv_sems, after)
    return list(out[:ns]), list(out[ns:])


def _ag_d2d(lands):
    n = len(lands)

    def body(*refs):
        in_refs, out_refs = refs[:n], refs[n:2 * n]
        send_sems, recv_sems = refs[2 * n:]
        x, y, c = _coords()
        sends, recvs = [], []
        for t in range(n):
            for k, (px, py) in enumerate([(x, y)] + _peer_chips(x, y)):
                sems = dict(send_sem=send_sems.at[4 * t + k], recv_sem=recv_sems.at[4 * t + k],
                            device_id=(x, y, 1 - c), device_id_type=MESH)
                src = in_refs[t].at[4 * px + 2 * py + c]
                sends.append(pltpu.make_async_remote_copy(src_ref=src, dst_ref=out_refs[t].at[4 * px + 2 * py + c], **sems))
                recvs.append(pltpu.make_async_remote_copy(src_ref=src, dst_ref=out_refs[t].at[4 * px + 2 * py + 1 - c], **sems))
        for cp in sends:
            cp.start()
        for cp in recvs:
            cp.wait_recv()
        for cp in sends:
            cp.wait_send()

    return pl.pallas_call(
        body, name="ag_d2d", out_shape=[_sds(a.shape, a.dtype) for a in lands],
        in_specs=[_ANY] * n, out_specs=[_ANY] * n,
        input_output_aliases={k: k for k in range(n)},
        scratch_shapes=[pltpu.SemaphoreType.DMA((4 * n,)), pltpu.SemaphoreType.DMA((4 * n,))],
    )(*lands)


def _rs_sib(grads):
    nt = len(grads)

    def body(*refs):
        s_refs, ra_refs = refs[:nt], refs[nt:2 * nt]
        send_sems, recv_sems = refs[2 * nt:]
        x, y, c = _coords()
        cps = [pltpu.make_async_remote_copy(
            src_ref=s_refs[t].at[:, 1 - c], dst_ref=ra_refs[t], send_sem=send_sems.at[t], recv_sem=recv_sems.at[t],
            device_id=(x, y, 1 - c), device_id_type=MESH) for t in range(nt)]
        for cp in cps:
            cp.start()
        for cp in cps:
            cp.wait()

    return pl.pallas_call(
        body, name="rs_sibling",
        out_shape=[_sds((4,) + g.shape[2:], g.dtype) for g in grads],
        in_specs=[_ANY] * nt, out_specs=[_ANY] * nt,
        scratch_shapes=[pltpu.SemaphoreType.DMA((nt,)), pltpu.SemaphoreType.DMA((nt,))],
    )(*grads)


def _rs_add(grads, ras, core):
    nt = len(grads)

    def body(c_ref, *refs):
        s_refs, ra_refs, q_refs, rb_refs = refs[:nt], refs[nt:2 * nt], refs[2 * nt:3 * nt], refs[3 * nt:]
        for t in range(nt):
            q = (s_refs[t][0, 0].astype(F32) + ra_refs[t][0].astype(F32)).astype(q_refs[t].dtype)
            q_refs[t][0] = q
            rb_refs[t][0] = q

    nr = 4
    own = [pl.BlockSpec((1, 1, g.shape[2] // nr, g.shape[3]), lambda j, r, c: (j, c[0], r, 0)) for g in grads]
    blk = [pl.BlockSpec((1, g.shape[2] // nr, g.shape[3]), lambda j, r, c: (j, r, 0)) for g in grads]
    return pl.pallas_call(
        body, name="rs_add", out_shape=[_sds(r.shape, r.dtype) for r in ras] * 2,
        grid_spec=pltpu.PrefetchScalarGridSpec(num_scalar_prefetch=1, grid=(4, nr), in_specs=own + blk, out_specs=blk * 2),
        compiler_params=_cparams(("parallel", "parallel")),
    )(core, *grads, *ras)


def _adamw(w, g, m, v):
    m = ADAM_B1 * m + (1.0 - ADAM_B1) * g
    v = ADAM_B2 * v + (1.0 - ADAM_B2) * jnp.square(g)
    m_hat = m / (1.0 - ADAM_B1 ** ADAM_STEP)
    v_hat = v / (1.0 - ADAM_B2 ** ADAM_STEP)
    delta = -ADAM_LR * (m_hat / (jnp.sqrt(v_hat) + ADAM_EPS) + ADAM_WD * w)
    return delta, m, v


def _sum_partials(s_own, rb_ref, me, acc_ref):
    if s_own is None:
        g = rb_ref[0].astype(F32)
        for j in range(1, rb_ref.shape[0]):
            g = g + rb_ref[j].astype(F32)
        acc_ref[...] = g
        return
    for d0 in range(N_DEV):
        @pl.when(me == d0)
        def _():
            g = None
            for d in range(N_DEV):
                term = (s_own if d == d0 else rb_ref[d]).astype(F32)
                g = term if g is None else g + term
            acc_ref[...] = g


def _rs_final(s, rb, me, w, m, v, outs, l):
    _, R, C = w.shape
    cp = rb.shape[2]
    own = [] if s is None else [s]

    def body(me_ref, *refs):
        s_ref = None if s is None else refs[0]
        rb_ref, w_ref, m_ref, v_ref = refs[len(own):len(own) + 4]
        g_ref, d_ref, m2_ref, v2_ref, acc_ref = refs[len(own) + 8:]
        _sum_partials(None if s is None else s_ref[0], rb_ref, me_ref[0], acc_ref)
        g = acc_ref[...][:, :C]
        g_ref[0] = g
        d_ref[0], m2_ref[0], v2_ref[0] = _adamw(w_ref[0], g, m_ref[0], v_ref[0])

    blk = pl.BlockSpec((1, TM, C), lambda r, me: (l, r, 0))
    return pl.pallas_call(
        body, name="rs_final_adamw", out_shape=[_sds(w.shape)] * 4,
        grid_spec=pltpu.PrefetchScalarGridSpec(
            num_scalar_prefetch=1, grid=(R // TM,),
            in_specs=[pl.BlockSpec((1, TM, cp), lambda r, me: (me[0], r, 0))] * len(own)
            + [pl.BlockSpec((rb.shape[0], TM, cp), lambda r, me: (0, r, 0)), blk, blk, blk] + [_ANY] * 4,
            out_specs=[blk] * 4, scratch_shapes=[pltpu.VMEM((TM, cp), F32)]),
        input_output_aliases={5 + len(own) + k: k for k in range(4)},
        compiler_params=_cparams(("parallel",)),
    )(me, *own, rb, w, m, v, *outs)


def _rs_final_w_in(s, rb, me, wt, mt, vt, outs, l):
    shard = (W_IN_SHARD, D_MODEL)
    own = [] if s is None else [s]

    def body(me_ref, rb_ref, *refs):
        wt_ref, mt_ref, vt_ref = refs[len(own):len(own) + 3]
        g_ref, d_ref, m2_ref, v2_ref, sbuf, acc_ref, bufs, obufs, sems = refs[len(own) + 7:]
        me = me_ref[0]
        loads = [pltpu.make_async_copy(src.at[:, l, :], bufs.at[k], sems.at[k]) for k, src in enumerate((wt_ref, mt_ref, vt_ref))]
        if own:
            loads.append(pltpu.make_async_copy(refs[0].at[me], sbuf, sems.at[7]))
        for cp in loads:
            cp.start()
        if own:
            loads[3].wait()
        _sum_partials(sbuf[...] if own else None, rb_ref, me, acc_ref)
        g = acc_ref[...].T[:W_IN_SHARD]
        for cp in loads[:3]:
            cp.wait()
        obufs[0] = g
        obufs[1], obufs[2], obufs[3] = _adamw(bufs[0], g, bufs[1], bufs[2])
        stores = [pltpu.make_async_copy(obufs.at[k], dst.at[:, l, :], sems.at[3 + k])
                  for k, dst in enumerate((g_ref, d_ref, m2_ref, v2_ref))]
        for cp in stores:
            cp.start()
        for cp in stores:
            cp.wait()

    return pl.pallas_call(
        body, name="rs_final_adamw_w_in",
        in_specs=[pl.BlockSpec(memory_space=pltpu.SMEM), pl.BlockSpec(memory_space=pltpu.VMEM)] + [_ANY] * (7 + len(own)),
        out_specs=[_ANY] * 4, out_shape=[_sds(wt.shape)] * 4,
        input_output_aliases={5 + len(own) + k: k for k in range(4)},
        scratch_shapes=[pltpu.VMEM(rb.shape[1:], rb.dtype), pltpu.VMEM(rb.shape[1:], F32),
                        pltpu.VMEM((3,) + shard, F32), pltpu.VMEM((4,) + shard, F32), pltpu.SemaphoreType.DMA((8,))],
        compiler_params=_cparams(),
    )(me, rb, *own, wt, mt, vt, *outs)


def _sum8(g8):
    _, R, C = g8.shape

    def body(g_ref, o_ref):
        acc = g_ref[0]
        for d in range(1, N_DEV):
            acc = acc + g_ref[d]
        o_ref[...] = acc

    return pl.pallas_call(body, name="small_sum", out_shape=_sds((R, C)))(g8)


def _adamw_small(w, g, m, v):
    def body(w_ref, g_ref, m_ref, v_ref, d_ref, m2_ref, v2_ref):
        d_ref[...], m2_ref[...], v2_ref[...] = _adamw(w_ref[...], g_ref[...], m_ref[...], v_ref[...])

    return pl.pallas_call(body, name="small_adamw", out_shape=[_sds(w.shape)] * 3)(w, g, m, v)


REP = (("norm_mix_w", 1024), ("ssd_conv_b", 1536), ("ssd_dt_bias", 16), ("ssd_a_log", 16), ("ssd_d", 16),
       ("ssd_norm_w", 1024), ("q_norm_w", 64), ("k_norm_w", 64), ("attn_sinks", 8), ("cm_dw_b", 512),
       ("cm_ln_w", 512), ("cm_ln_b", 512), ("norm_mlp_w", 1024))
WEIGHTS = ("norm_mix_w", "w_in", "ssd_conv_w", "ssd_conv_b", "ssd_dt_bias", "ssd_a_log", "ssd_d", "ssd_norm_w",
           "q_norm_w", "k_norm_w", "attn_sinks", "cm_dw_w", "cm_dw_b", "cm_ln_w", "cm_ln_b", "w_out", "norm_mlp_w",
           "w_mlp_up", "w_mlp_down")
BIG = ("w_in", "w_out", "w_mlp_up", "w_mlp_down")
N_REP = DEPTH * sum(n for _, n in REP)
CONVW_SHARD = SSD_XBC // N_DEV
CMW_SHARD = CM_CHANNELS // N_DEV


def _to_rows(flat, rows):
    return jnp.pad(flat, (0, rows * LANES - flat.shape[0])).reshape(rows, LANES)


def kernel(x, norm_mix_w, w_in, ssd_conv_w, ssd_conv_b, ssd_dt_bias, ssd_a_log, ssd_d, ssd_norm_w, q_norm_w, k_norm_w, attn_sinks, cm_dw_w, cm_dw_b, cm_ln_w, cm_ln_b, w_out, norm_mlp_w, w_mlp_up, w_mlp_down, loss_target, m_norm_mix_w, m_w_in, m_ssd_conv_w, m_ssd_conv_b, m_ssd_dt_bias, m_ssd_a_log, m_ssd_d, m_ssd_norm_w, m_q_norm_w, m_k_norm_w, m_attn_sinks, m_cm_dw_w, m_cm_dw_b, m_cm_ln_w, m_cm_ln_b, m_w_out, m_norm_mlp_w, m_w_mlp_up, m_w_mlp_down, v_norm_mix_w, v_w_in, v_ssd_conv_w, v_ssd_conv_b, v_ssd_dt_bias, v_ssd_a_log, v_ssd_d, v_ssd_norm_w, v_q_norm_w, v_k_norm_w, v_attn_sinks, v_cm_dw_w, v_cm_dw_b, v_cm_ln_w, v_cm_ln_b, v_w_out, v_norm_mlp_w, v_w_mlp_up, v_w_mlp_down):
    w = dict(norm_mix_w=norm_mix_w, w_in=w_in, ssd_conv_w=ssd_conv_w, ssd_conv_b=ssd_conv_b, ssd_dt_bias=ssd_dt_bias, ssd_a_log=ssd_a_log, ssd_d=ssd_d, ssd_norm_w=ssd_norm_w, q_norm_w=q_norm_w, k_norm_w=k_norm_w, attn_sinks=attn_sinks, cm_dw_w=cm_dw_w, cm_dw_b=cm_dw_b, cm_ln_w=cm_ln_w, cm_ln_b=cm_ln_b, w_out=w_out, norm_mlp_w=norm_mlp_w, w_mlp_up=w_mlp_up, w_mlp_down=w_mlp_down)
    m = dict(norm_mix_w=m_norm_mix_w, w_in=m_w_in, ssd_conv_w=m_ssd_conv_w, ssd_conv_b=m_ssd_conv_b, ssd_dt_bias=m_ssd_dt_bias, ssd_a_log=m_ssd_a_log, ssd_d=m_ssd_d, ssd_norm_w=m_ssd_norm_w, q_norm_w=m_q_norm_w, k_norm_w=m_k_norm_w, attn_sinks=m_attn_sinks, cm_dw_w=m_cm_dw_w, cm_dw_b=m_cm_dw_b, cm_ln_w=m_cm_ln_w, cm_ln_b=m_cm_ln_b, w_out=m_w_out, norm_mlp_w=m_norm_mlp_w, w_mlp_up=m_w_mlp_up, w_mlp_down=m_w_mlp_down)
    v = dict(norm_mix_w=v_norm_mix_w, w_in=v_w_in, ssd_conv_w=v_ssd_conv_w, ssd_conv_b=v_ssd_conv_b, ssd_dt_bias=v_ssd_dt_bias, ssd_a_log=v_ssd_a_log, ssd_d=v_ssd_d, ssd_norm_w=v_ssd_norm_w, q_norm_w=v_q_norm_w, k_norm_w=v_k_norm_w, attn_sinks=v_attn_sinks, cm_dw_w=v_cm_dw_w, cm_dw_b=v_cm_dw_b, cm_ln_w=v_cm_ln_w, cm_ln_b=v_cm_ln_b, w_out=v_w_out, norm_mlp_w=v_norm_mlp_w, w_mlp_up=v_w_mlp_up, w_mlp_down=v_w_mlp_down)
    L = x.shape[1]
    xi, yi, ci = _coords()
    me = 4 * xi + 2 * yi + ci
    n_conv = DEPTH * SSD_CONV * CONVW_SHARD
    n_cm = DEPTH * CM_CONV * CMW_SHARD

    conv_rows = 88
    cw8, = _all_gather([_to_rows(jnp.concatenate([ssd_conv_w.reshape(-1), cm_dw_w.reshape(-1)]), conv_rows)], "ag_conv_w")
    cw8 = cw8.reshape(N_DEV, -1)
    conv_full = cw8[:, :n_conv].reshape(N_DEV, DEPTH, SSD_CONV, CONVW_SHARD).transpose(1, 2, 0, 3).reshape(DEPTH, SSD_CONV, SSD_XBC)
    cm_full = cw8[:, n_conv:n_conv + n_cm].reshape(N_DEV, DEPTH, CM_CONV, CMW_SHARD).transpose(1, 2, 0, 3).reshape(DEPTH, CM_CONV, CM_CHANNELS)
    me1 = jnp.reshape(me, (1,)).astype(jnp.int32)
    casts = [_cast_w_in(w_in, me1), _cast_shard(w_out, me1), _cast_shard(w_mlp_up, me1), _cast_shard(w_mlp_down, me1)]
    shards = [[c[l] for c in casts] for l in range(DEPTH)]

    def gather_start(lands, after):
        return _ici_start([], lands, after, "ag_ici_start", "gather")

    def gather_finish(started, after):
        return _ag_d2d(_ici_wait(started, after, "ag_ici_wait", "gather")[1])

    cos, sin = _rope_tables(L)
    p = _stacked_params({k: w[k] for k, _ in REP}, conv_full, cm_full)
    saved = []
    h = x[0]
    first = gather_start(shards[0][:1], cw8)
    rest0 = gather_start(shards[0][1:], first[4])
    w_in8, = gather_finish(first, rest0[4])
    token, rest = rest0[4], None
    for i in range(DEPTH):
        if i == 0:
            late = lambda ycat: gather_finish(rest0, ycat)
        else:
            late = lambda ycat, r=rest: r
        nxt, fwd = None, []
        if i + 1 < DEPTH:
            nxt = gather_start(shards[i + 1], w_in8)
            token = nxt[4]

        def before_down(act, nxt=nxt, fwd=fwd):
            if nxt is None:
                return act
            lands = _ici_wait(nxt, act, "ag_ici_wait", "gather")[1]
            fwd.append(_ici_start([], lands, act, "ag_d2d_start", "forward"))
            return fwd[0][4]

        h, s = _layer_fwd(h, p, w_in8, late, before_down, i, cos, sin, token)
        saved.append(s)
        if nxt is not None:
            got = _ici_wait(fwd[0], h, "ag_d2d_wait", "forward")[1]
            w_in8, rest = got[0], got[1:]
    d, loss_tile = _loss_head(h, loss_target[0])

    smalls = [None] * DEPTH
    big_out = {k: [lax.empty(w[k].shape, F32) for _ in range(4)] for k in BIG}
    to_t = lambda a: jnp.transpose(a, (2, 0, 1))
    w_in_t = [to_t(t["w_in"]) for t in (w, m, v)]
    big_out["w_in"] = [lax.empty(w_in_t[0].shape, F32) for _ in range(4)]

    core = jnp.reshape(ci, (1,)).astype(jnp.int32)

    def scatter_start(grads, after, l):
        if l > 0:
            lands = [lax.empty(g.shape, g.dtype) for g in grads]
            return _ici_start(list(grads), lands, after, "rs_ici_start", "scatter")
        g4 = [g.reshape((4, 2) + g.shape[1:]) for g in grads]
        out = _rs_add(g4, _rs_sib(g4), core)
        return _ici_start(list(out[:len(g4)]), list(out[len(g4):]), after, "rs_chips_start", "scatter_chips")

    def scatter_finish(started, after, l, names):
        if l > 0:
            srcs, rbs = _ici_wait(started, after, "rs_ici_wait", "scatter")
        else:
            srcs, rbs = [None] * len(names), _ici_wait(started, after, "rs_chips_wait", "scatter_chips")[1]
        for g, rb, k in zip(srcs, rbs, names):
            if k == "w_in":
                big_out[k] = _rs_final_w_in(g, rb, me1, *w_in_t, big_out[k], l)
            else:
                big_out[k] = _rs_final(g, rb, me1, w[k], m[k], v[k], big_out[k], l)

    def gather_small_grads():
        gvec = jnp.concatenate(
            [jnp.stack([smalls[i][k] for i in range(DEPTH)]).reshape(-1) for k, _ in REP]
            + [jnp.stack([smalls[i][k] for i in range(DEPTH)]).reshape(-1) for k in ("ssd_conv_w", "cm_dw_w")]
            + [loss_tile[0, :1]])
        g_rows = -(-gvec.shape[0] // (8 * LANES)) * 8
        return _all_gather([_to_rows(gvec, g_rows)], "ag_small_grads")[0]

    token, pending = loss_tile, []
    for i in reversed(range(DEPTH)):
        dx1, dcat, g_out, g_up, g_down, g_nw_mlp = _layer_bwd_mlp(d, p, i, saved[i], token)
        started = []
        if i == 0:
            started.append((scatter_start([g_out, g_up, g_down], dcat, i), i, BIG[1:]))
        d, g_in, smalls[i] = _layer_bwd_mix(dx1, dcat, g_nw_mlp, p, i, saved[i], cos, sin,
                                            started[0][0][4] if started else g_nw_mlp)
        if i == 0:
            g8 = gather_small_grads()
            started.append((scatter_start([g_in], g8, i), i, BIG[:1]))
        else:
            started.append((scatter_start([g_in, g_out, g_up, g_down], d, i), i, BIG))
        token = started[-1][0][4]
        for st, l, names in pending:
            scatter_finish(st, token, l, names)
        pending = started
    for st, l, names in pending:
        scatter_finish(st, token, l, names)

    gsum = _sum8(g8).reshape(-1)
    o_conv = N_REP
    o_cm = o_conv + DEPTH * SSD_CONV * SSD_XBC
    o_loss = o_cm + DEPTH * CM_CONV * CM_CHANNELS
    g_conv = lax.dynamic_slice_in_dim(gsum[o_conv:o_cm].reshape(DEPTH, SSD_CONV, SSD_XBC), me * CONVW_SHARD, CONVW_SHARD, axis=2)
    g_cm = lax.dynamic_slice_in_dim(gsum[o_cm:o_loss].reshape(DEPTH, CM_CONV, CM_CHANNELS), me * CMW_SHARD, CMW_SHARD, axis=2)
    loss = gsum[o_loss]
    s_rows = -(-(N_REP + n_conv + n_cm) // (8 * LANES)) * 8

    def pack_small(t):
        return _to_rows(jnp.concatenate([t[k].reshape(-1) for k, _ in REP] + [t["ssd_conv_w"].reshape(-1), t["cm_dw_w"].reshape(-1)]), s_rows)

    g_small = _to_rows(jnp.concatenate([gsum[:N_REP], g_conv.reshape(-1), g_cm.reshape(-1)]), s_rows)
    small_out = [g_small] + list(_adamw_small(pack_small(w), g_small, pack_small(m), pack_small(v)))

    def unpack_small(t):
        flat = t.reshape(-1)
        out, off = {}, 0
        for k, n in REP:
            out[k] = flat[off:off + DEPTH * n].reshape(DEPTH, n)
            off += DEPTH * n
        out["ssd_conv_w"] = flat[off:off + n_conv].reshape(DEPTH, SSD_CONV, CONVW_SHARD)
        off += n_conv
        out["cm_dw_w"] = flat[off:off + n_cm].reshape(DEPTH, CM_CONV, CMW_SHARD)
        return out

    outs = [loss, d[None]]
    for j, small_t in enumerate(small_out):
        t = unpack_small(small_t)
        for k in BIG:
            t[k] = big_out[k][j]
        t["w_in"] = jnp.transpose(t["w_in"], (1, 2, 0))
        outs += [t[k] for k in WEIGHTS]
    return tuple(outs)
```

```python
import math

import jax
import jax.numpy as jnp
from jax import lax
from jax.experimental import pallas as pl
from jax.experimental.pallas import tpu as pltpu

F32 = jnp.float32
_MM = jnp.bfloat16

D_MODEL = 1024
DEPTH = 4
SSD_WIDTH = 1024
SSD_HEADS = 16
SSD_STATE = 128
SSD_GROUPS = 2
SSD_CONV = 4
SSD_XBC = 1536
Q = 128
ATTN_HEAD_DIM = 64
ATTN_Q_HEADS = 8
CM_CHANNELS = 512
CM_CONV = 31
D_FF = 4096
D_MIX = 2048
N_IN = 4368
RMS_EPS = 1e-6
LN_EPS = 1e-5
ROPE_THETA = 10000.0
ADAM_LR = 0.001
ADAM_B1 = 0.9
ADAM_B2 = 0.999
ADAM_EPS = 1e-08
ADAM_WD = 0.01
ADAM_STEP = 10

N_DEV = 8
LANES = 128
TM = 256
N_IN_P = 4480
U_Z, U_XBC, U_QKV, U_GLU, U_DT = (0, 1024), (1024, 2560), (2560, 3328), (3328, 4352), (4352, 4480)
W_IN_SHARD = N_IN // N_DEV
W_IN_SHARD_P = 640
FF_SHARD = D_FF // N_DEV
OUT_SHARD = D_MIX // N_DEV

_NN = (((1,), (0,)), ((), ()))
_NT = (((1,), (1,)), ((), ()))
_TN = (((0,), (0,)), ((), ()))
_VMEM_LIMIT = 56 * 1024 * 1024


def _mm(a, b, dims=_NN):
    return lax.dot_general(a.astype(_MM), b.astype(_MM), dims, preferred_element_type=F32)


def _mmx(a, b, dims=_NN, exact="b"):
    m, v = (b, a) if exact == "b" else (a, b)
    m = m.astype(jnp.bfloat16)
    acc = None
    for _ in range(3):
        p = v.astype(jnp.bfloat16)
        v = v - p.astype(F32)
        t = lax.dot_general(p, m, dims, preferred_element_type=F32) if exact == "b" else \
            lax.dot_general(m, p, dims, preferred_element_type=F32)
        acc = t if acc is None else acc + t
    return acc


def _sds(shape, dtype=F32):
    return jax.ShapeDtypeStruct(tuple(shape), dtype)


def _full(shape):
    nd = len(shape)
    return pl.BlockSpec(tuple(shape), lambda *_: (0,) * nd)


def _rows(cols, tm=TM, col=0):
    return pl.BlockSpec((tm, cols), lambda i: (i, col))


TMM = 512


def _mrows(cols):
    return _rows(cols, TMM)


def _lp(n, i):
    return pl.BlockSpec((1, 1, n), lambda *_: (i, 0, 0))


def _lw(arr):
    return pl.BlockSpec(arr.shape, lambda *_: (0, 0, 0, 0))


_ANY = pl.BlockSpec(memory_space=pl.ANY)


def _cparams(sem=None):
    return pltpu.CompilerParams(dimension_semantics=sem, vmem_limit_bytes=_VMEM_LIMIT)


def _sigmoid(x):
    return 1.0 / (1.0 + jnp.exp(-x))


def _silu(x):
    return x * _sigmoid(x)


def _dsilu(x):
    s = _sigmoid(x)
    return s * (1.0 + x * (1.0 - s))


def _rms_bwd(dy, x, w, inv_n):
    r = lax.rsqrt(jnp.sum(x * x, axis=-1, keepdims=True) * inv_n + RMS_EPS)
    xh = x * r
    dxh = dy * w
    dx = r * (dxh - xh * (jnp.sum(dxh * xh, axis=-1, keepdims=True) * inv_n))
    return dx, dy * xh


def _cast_shard(w, me, cols_p=None):
    _, R, C = w.shape
    cp = C if cols_p is None else cols_p

    def body(me_ref, w_ref, *o_refs):
        v = w_ref[0]
        if cp != C:
            v = jnp.concatenate([v, jnp.zeros((R, cp - C), F32)], axis=1)
        for k in range(DEPTH):
            @pl.when(pl.program_id(0) == k)
            def _():
                o_refs[k][0, 0] = v.astype(_MM)

    return pl.pallas_call(
        body, name="cast_shard", out_shape=[_sds((N_DEV, 1, R, cp), _MM)] * DEPTH,
        grid_spec=pltpu.PrefetchScalarGridSpec(
            num_scalar_prefetch=1, grid=(DEPTH,),
            in_specs=[pl.BlockSpec((1, R, C), lambda l, me: (l, 0, 0))],
            out_specs=[pl.BlockSpec((1, 1, R, cp), lambda l, me: (me[0], 0, 0, 0))] * DEPTH),
        compiler_params=_cparams(("arbitrary",)),
    )(me, w)


def _cast_w_in(w_in, me):
    wt = jnp.transpose(w_in, (2, 0, 1))

    def body(me_ref, wt_ref, *rest):
        o_refs, buf, sem = rest[:DEPTH], rest[DEPTH], rest[DEPTH + 1]
        l = pl.program_id(0)
        cp = pltpu.make_async_copy(wt_ref.at[:, l, :], buf, sem)
        cp.start()
        cp.wait()
        v = jnp.concatenate([buf[...], jnp.zeros((W_IN_SHARD_P - W_IN_SHARD, D_MODEL), F32)], axis=0).T.astype(_MM)
        for k in range(DEPTH):
            @pl.when(l == k)
            def _():
                o_refs[k][0, 0] = v

    return pl.pallas_call(
        body, name="cast_w_in", out_shape=[_sds((N_DEV, 1, D_MODEL, W_IN_SHARD_P), _MM)] * DEPTH,
        grid_spec=pltpu.PrefetchScalarGridSpec(
            num_scalar_prefetch=1, grid=(DEPTH,), in_specs=[_ANY],
            out_specs=[pl.BlockSpec((1, 1, D_MODEL, W_IN_SHARD_P), lambda l, me: (me[0], 0, 0, 0))] * DEPTH,
            scratch_shapes=[pltpu.VMEM((W_IN_SHARD, D_MODEL), F32), pltpu.SemaphoreType.DMA]),
        compiler_params=_cparams(("arbitrary",)),
    )(me, wt)


def _w_in_regroup(w8, after):
    a, b = U_XBC[1], U_XBC[1] + SSD_HEADS

    def body(w_ref, after_ref, o_ref):
        w = jnp.concatenate([w_ref[j, 0][:, :W_IN_SHARD].astype(F32) for j in range(N_DEV)], axis=1)
        r = jnp.concatenate([w[:, :a], w[:, b:], w[:, a:b], jnp.zeros((TM, N_IN_P - N_IN), F32)], axis=1)
        o_ref[...] = r.astype(_MM)

    return pl.pallas_call(
        body, name="w_in_regroup", grid=(D_MODEL // TM,),
        in_specs=[pl.BlockSpec((N_DEV, 1, TM, W_IN_SHARD_P), lambda r: (0, 0, r, 0)), _ANY],
        out_specs=_rows(N_IN_P), out_shape=_sds((D_MODEL, N_IN_P), _MM),
        compiler_params=_cparams(("parallel",)),
    )(w8, after)


def _g_in_split(g):
    a = U_XBC[1]

    def body(g_ref, o_ref):
        v = g_ref[...].astype(F32)
        w = jnp.concatenate([v[:, :a], v[:, U_DT[0]:U_DT[0] + SSD_HEADS], v[:, a:U_DT[0]]], axis=1)
        pad = jnp.zeros((TM, W_IN_SHARD_P - W_IN_SHARD), F32)
        for j in range(N_DEV):
            o_ref[j] = jnp.concatenate([w[:, j * W_IN_SHARD:(j + 1) * W_IN_SHARD], pad], axis=1).astype(_MM)

    return pl.pallas_call(
        body, name="g_in_split", grid=(D_MODEL // TM,),
        in_specs=[_rows(N_IN_P)],
        out_specs=pl.BlockSpec((N_DEV, TM, W_IN_SHARD_P), lambda r: (0, r, 0)),
        out_shape=_sds((N_DEV, D_MODEL, W_IN_SHARD_P), _MM),
        compiler_params=_cparams(("parallel",)),
    )(g)


def _in_proj(x, nw, i, w):
    L = x.shape[0]
    splits = (U_Z, U_XBC, U_QKV, U_GLU, U_DT)

    def body(x_ref, nw_ref, w_ref, h_ref, *out_refs):
        xf = x_ref[...]
        r = lax.rsqrt(jnp.mean(xf * xf, axis=-1, keepdims=True) + RMS_EPS)
        h = (xf * r * nw_ref[0]).astype(_MM)
        h_ref[...] = h
        for ref, (a, b) in zip(out_refs, splits):
            ref[...] = lax.dot_general(h, w_ref[:, a:b], _NN, preferred_element_type=F32)

    return pl.pallas_call(
        body, name="in_proj", grid=(L // TMM,),
        in_specs=[_mrows(D_MODEL), _lp(D_MODEL, i), _full(w.shape)],
        out_specs=[_mrows(D_MODEL)] + [_mrows(b - a) for a, b in splits],
        out_shape=[_sds((L, D_MODEL), _MM)] + [_sds((L, b - a)) for a, b in splits],
        compiler_params=_cparams(("parallel",)),
    )(x, nw, w)


def _mlp_up(x, nw, i, w8):
    L = x.shape[0]

    def body(x_ref, nw_ref, w_ref, h_ref, up_ref, act_ref):
        xf = x_ref[...]
        r = lax.rsqrt(jnp.mean(xf * xf, axis=-1, keepdims=True) + RMS_EPS)
        h = (xf * r * nw_ref[0]).astype(_MM)
        h_ref[...] = h
        for j in range(N_DEV):
            sl = slice(j * FF_SHARD, (j + 1) * FF_SHARD)
            up = lax.dot_general(h, w_ref[j, 0], _NN, preferred_element_type=F32)
            up_ref[:, sl] = up
            act_ref[:, sl] = jnp.square(jnp.maximum(up, 0.0)).astype(_MM)

    return pl.pallas_call(
        body, name="mlp_up", grid=(L // TM,),
        in_specs=[_rows(D_MODEL), _lp(D_MODEL, i), _lw(w8)],
        out_specs=[_rows(D_MODEL), _rows(D_FF), _rows(D_FF)],
        out_shape=[_sds((L, D_MODEL), _MM), _sds((L, D_FF)), _sds((L, D_FF), _MM)],
        compiler_params=_cparams(("parallel",)),
    )(x, nw, w8)


def _mm_res(a, w8, i, res, name, after):
    L, K = a.shape
    N = w8.shape[3]

    def body(a_ref, w_ref, res_ref, after_ref, o_ref):
        w = w_ref[:, 0].reshape(K, N)
        o_ref[...] = res_ref[...] + lax.dot_general(a_ref[...], w, _NN, preferred_element_type=F32)

    return pl.pallas_call(
        body, name=name, grid=(L // TMM,),
        in_specs=[_mrows(K), _lw(w8), _mrows(N), _ANY],
        out_specs=_mrows(N), out_shape=_sds((L, N)),
        compiler_params=_cparams(("parallel",)),
    )(a, w8, res, after)


def _out_proj_bwd(a, w8, i, after):
    L = a.shape[0]

    def body(a_ref, w_ref, after_ref, o_ref):
        w = w_ref[:, 0].reshape(D_MIX, D_MODEL)
        o_ref[...] = lax.dot_general(a_ref[...].astype(_MM), w, _NT, preferred_element_type=F32)

    return pl.pallas_call(
        body, name="out_proj_bwd", grid=(L // TMM,),
        in_specs=[_mrows(D_MODEL), _lw(w8), _ANY],
        out_specs=_mrows(D_MIX), out_shape=_sds((L, D_MIX)),
        compiler_params=_cparams(("parallel",)),
    )(a, w8, after)


def _mlp_down_bwd(dy, w8, i, up, after):
    L = dy.shape[0]

    def body(dy_ref, w_ref, up_ref, after_ref, o_ref):
        d = dy_ref[...].astype(_MM)
        for j in range(N_DEV):
            sl = slice(j * FF_SHARD, (j + 1) * FF_SHARD)
            da = lax.dot_general(d, w_ref[j, 0], _NT, preferred_element_type=F32)
            o_ref[:, sl] = (da * (2.0 * jnp.maximum(up_ref[:, sl], 0.0))).astype(_MM)

    return pl.pallas_call(
        body, name="mlp_down_bwd", grid=(L // TMM,),
        in_specs=[_mrows(D_MODEL), _lw(w8), _mrows(D_FF), _ANY],
        out_specs=_mrows(D_FF), out_shape=_sds((L, D_FF), _MM),
        compiler_params=_cparams(("parallel",)),
    )(dy, w8, up, after)


def _rms_bwd_epilogue(dh, res_ref, x_ref, nw_ref, dx_ref, dnw_ref):
    dx, dwx = _rms_bwd(dh, x_ref[...], nw_ref[0], 1.0 / D_MODEL)
    dx_ref[...] = res_ref[...] + dx

    @pl.when(pl.program_id(0) == 0)
    def _():
        dnw_ref[...] = jnp.zeros_like(dnw_ref)

    dnw_ref[...] += jnp.sum(dwx, axis=0, keepdims=True)


def _mlp_up_bwd(d_up, w8, i, res, x, nw):
    L = d_up.shape[0]

    def body(a_ref, w_ref, res_ref, x_ref, nw_ref, dx_ref, dnw_ref):
        dh = jnp.zeros((TMM, D_MODEL), F32)
        for j in range(N_DEV):
            dh = dh + lax.dot_general(a_ref[:, j * FF_SHARD:(j + 1) * FF_SHARD], w_ref[j, 0], _NT, preferred_element_type=F32)
        _rms_bwd_epilogue(dh, res_ref, x_ref, nw_ref, dx_ref, dnw_ref)

    return pl.pallas_call(
        body, name="mlp_up_bwd", grid=(L // TMM,),
        in_specs=[_mrows(D_FF), _lw(w8), _mrows(D_MODEL), _mrows(D_MODEL), _lp(D_MODEL, i)],
        out_specs=[_mrows(D_MODEL), _full((1, D_MODEL))],
        out_shape=[_sds((L, D_MODEL)), _sds((1, D_MODEL))],
        compiler_params=_cparams(("arbitrary",)),
    )(d_up, w8, res, x, nw)


def _in_proj_bwd(pieces, w, res, x, nw, i):
    L = pieces[0].shape[0]
    n = len(pieces)

    def body(*refs):
        w_ref, res_ref, x_ref, nw_ref, dx_ref, dnw_ref, du_ref = refs[n:]
        off = 0
        for r in refs[:n]:
            du_ref[:, off:off + r.shape[1]] = r[...].astype(_MM)
            off += r.shape[1]
        dh = lax.dot_general(du_ref[...], w_ref[...], _NT, preferred_element_type=F32)
        _rms_bwd_epilogue(dh, res_ref, x_ref, nw_ref, dx_ref, dnw_ref)

    return pl.pallas_call(
        body, name="in_proj_bwd", grid=(L // TM,),
        in_specs=[_rows(q.shape[1]) for q in pieces] + [_full(w.shape), _rows(D_MODEL), _rows(D_MODEL), _lp(D_MODEL, i)],
        out_specs=[_rows(D_MODEL), _full((1, D_MODEL)), _rows(N_IN_P)],
        out_shape=[_sds((L, D_MODEL)), _sds((1, D_MODEL)), _sds((L, N_IN_P), _MM)],
        compiler_params=_cparams(("arbitrary",)),
    )(*pieces, w, res, x, nw)


def _mm_tn(a, g, name, tk, tn, out):
    L, K = a.shape
    N = g.shape[1]

    def body(a_ref, g_ref, o_ref):
        r = lax.dot_general(a_ref[...].astype(_MM), g_ref[...].astype(_MM), _TN, preferred_element_type=F32)
        o_ref[...] = r.astype(o_ref.dtype).reshape(o_ref.shape)

    if out == "flat":
        out_spec, out_shape = pl.BlockSpec((tk, tn), lambda i, j: (i, j)), _sds((K, N), _MM)
    elif out == "rows":
        assert tn == N and tk % (K // N_DEV) == 0
        nblk = tk // (K // N_DEV)
        out_spec, out_shape = pl.BlockSpec((nblk, K // N_DEV, N), lambda i, j: (i, 0, 0)), _sds((N_DEV, K // N_DEV, N), _MM)
    else:
        assert tk == K and tn == N // N_DEV
        out_spec, out_shape = pl.BlockSpec((1, K, tn), lambda i, j: (j, 0, 0)), _sds((N_DEV, K, tn), _MM)
    return pl.pallas_call(
        body, name=name, grid=(K // tk, N // tn),
        in_specs=[pl.BlockSpec((L, tk), lambda i, j: (0, i)), pl.BlockSpec((L, tn), lambda i, j: (0, j))],
        out_specs=out_spec, out_shape=out_shape,
        compiler_params=_cparams(("parallel", "parallel")),
    )(a, g)


def _loss_head(y, t):
    L = y.shape[0]

    def body(y_ref, t_ref, dy_ref, l_ref):
        e = y_ref[...] - t_ref[...]
        dy_ref[...] = e * (1.0 / D_MODEL)

        @pl.when(pl.program_id(0) == 0)
        def _():
            l_ref[...] = jnp.zeros_like(l_ref)

        l_ref[...] += jnp.sum(jnp.sum(e * e, axis=1, keepdims=True), axis=0, keepdims=True) * (0.5 / D_MODEL)

    return pl.pallas_call(
        body, name="loss_head", grid=(L // TM,),
        in_specs=[_rows(D_MODEL), _rows(D_MODEL)],
        out_specs=[_rows(D_MODEL), _full((8, LANES))],
        out_shape=[_sds((L, D_MODEL)), _sds((8, LANES))],
        compiler_params=_cparams(("arbitrary",)),
    )(y, t)


EDGE = 32


def _roll_rows(x, s):
    s = s % x.shape[0]
    return x if s == 0 else pltpu.roll(x, s, axis=0)


class _Rolls:
    def __init__(self, x):
        self.x, self.by_phase = x, {}

    def __call__(self, s):
        s = s % self.x.shape[0]
        b = s % 8
        if b not in self.by_phase:
            self.by_phase[b] = _roll_rows(self.x, b)
        return _roll_rows(self.by_phase[b], s - b)


def _conv_taps(x, w_ref, b, k_w):
    def taps(v, zero_fill):
        r = lax.broadcasted_iota(jnp.int32, v.shape, 0)
        acc = jnp.broadcast_to(b, v.shape)
        rolled = _Rolls(v)
        for k in range(k_w):
            s = k_w - 1 - k
            sh = rolled(s)
            if zero_fill and s:
                sh = jnp.where(r >= s, sh, 0.0)
            acc = acc + w_ref[0, k:k + 1, :] * sh
        return acc

    return jnp.concatenate([taps(x[:EDGE], True), taps(x, False)[EDGE:]], axis=0)


def _conv_bwd_taps(x, dc, w_ref, dw_ref, db_ref, k_w):
    n = x.shape[0]
    dc_tail, x_tail, dc_head = dc[n - EDGE:], x[n - EDGE:], dc[:EDGE]
    r = lax.broadcasted_iota(jnp.int32, dc_head.shape, 0)
    dx = jnp.zeros_like(x)
    dx_tail = jnp.zeros_like(dc_tail)
    dc_rolled, x_rolled = _Rolls(dc), _Rolls(x)
    for k in range(k_w):
        s = k_w - 1 - k
        wk = w_ref[0, k:k + 1, :]
        dx = dx + wk * dc_rolled(n - s)
        up = _roll_rows(dc_tail, EDGE - s)
        dx_tail = dx_tail + wk * (jnp.where(r < EDGE - s, up, 0.0) if s else up)
        dw = jnp.sum(dc * x_rolled(s), axis=0, keepdims=True)
        if s:
            dw = dw - jnp.sum(jnp.where(r < s, dc_head * _roll_rows(x_tail, s), 0.0), axis=0, keepdims=True)
        dw_ref[k:k + 1, :] = dw
    db_ref[...] = jnp.sum(dc, axis=0, keepdims=True)
    return jnp.concatenate([dx[:n - EDGE], dx_tail], axis=0)


def _cols(L, cb, off=0):
    return pl.BlockSpec((L, cb), lambda j: (0, j + off))


def _lcols(k, cb, i):
    return pl.BlockSpec((1, k, cb), lambda j: (i, 0, j))


SSD_CB = 256


def _ssd_conv_fwd(x, w, b, i, after):
    L, C = x.shape
    cb = SSD_CB

    def body(x_ref, w_ref, b_ref, after_ref, o_ref):
        o_ref[...] = _silu(_conv_taps(x_ref[...], w_ref, b_ref[0], SSD_CONV))

    return pl.pallas_call(
        body, name="ssd_conv_fwd", grid=(C // cb,),
        in_specs=[_cols(L, cb), _lcols(SSD_CONV, cb, i), _lcols(1, cb, i), _ANY],
        out_specs=_cols(L, cb), out_shape=_sds((L, C)),
        compiler_params=_cparams(("parallel",)),
    )(x, w, b, after)


def _ssd_conv_bwd(x, w, b, i, dy):
    L, C = x.shape
    cb = SSD_CB

    def body(x_ref, w_ref, b_ref, dy_ref, dx_ref, dw_ref, db_ref):
        x_ = x_ref[...]
        c = _conv_taps(x_, w_ref, b_ref[0], SSD_CONV)
        dc = dy_ref[...] * _dsilu(c)
        dx_ref[...] = _conv_bwd_taps(x_, dc, w_ref, dw_ref, db_ref, SSD_CONV).astype(dx_ref.dtype)

    return pl.pallas_call(
        body, name="ssd_conv_bwd", grid=(C // cb,),
        in_specs=[_cols(L, cb), _lcols(SSD_CONV, cb, i), _lcols(1, cb, i), _cols(L, cb)],
        out_specs=[_cols(L, cb), _cols(SSD_CONV, cb), _cols(1, cb)],
        out_shape=[_sds((L, C), _MM), _sds((SSD_CONV, C)), _sds((1, C))],
        compiler_params=_cparams(("parallel",)),
    )(x, w, b, dy)


def _cm_conv_fwd(glu, w, b, i):
    L = glu.shape[0]
    cb = LANES
    nb = CM_CHANNELS // cb

    def body(a_ref, g_ref, w_ref, b_ref, o_ref):
        h = a_ref[...] * _sigmoid(g_ref[...])
        o_ref[...] = _conv_taps(h, w_ref, b_ref[0], CM_CONV)

    return pl.pallas_call(
        body, name="cm_conv_fwd", grid=(nb,),
        in_specs=[_cols(L, cb), _cols(L, cb, nb), _lcols(CM_CONV, cb, i), _lcols(1, cb, i)],
        out_specs=_cols(L, cb), out_shape=_sds((L, CM_CHANNELS)),
        compiler_params=_cparams(("parallel",)),
    )(glu, glu, w, b)


def _cm_conv_bwd(glu, w, i, dc):
    L = glu.shape[0]
    cb = LANES
    nb = CM_CHANNELS // cb

    def body(a_ref, g_ref, w_ref, dc_ref, da_ref, dg_ref, dw_ref, db_ref):
        a = a_ref[...]
        sg = _sigmoid(g_ref[...])
        dh = _conv_bwd_taps(a * sg, dc_ref[...], w_ref, dw_ref, db_ref, CM_CONV)
        da_ref[...] = (dh * sg).astype(da_ref.dtype)
        dg_ref[...] = (dh * a * sg * (1.0 - sg)).astype(dg_ref.dtype)

    return pl.pallas_call(
        body, name="cm_conv_bwd", grid=(nb,),
        in_specs=[_cols(L, cb), _cols(L, cb, nb), _lcols(CM_CONV, cb, i), _cols(L, cb)],
        out_specs=[_cols(L, cb), _cols(L, cb), _cols(CM_CONV, cb), _cols(1, cb)],
        out_shape=[_sds((L, CM_CHANNELS), _MM), _sds((L, CM_CHANNELS), _MM), _sds((CM_CONV, CM_CHANNELS)), _sds((1, CM_CHANNELS))],
        compiler_params=_cparams(("parallel",)),
    )(glu, glu, w, dc)


GRP = SSD_WIDTH // SSD_GROUPS


def _mix_post(y, z, attn, c, snw, lw, lb, i):
    L = y.shape[0]

    def body(y_ref, z_ref, a_ref, c_ref, snw_ref, lw_ref, lb_ref, o_ref):
        g = y_ref[...] * _silu(z_ref[...])
        for k in range(SSD_GROUPS):
            sl = slice(k * GRP, (k + 1) * GRP)
            gg = g[:, sl]
            r = lax.rsqrt(jnp.mean(gg * gg, axis=-1, keepdims=True) + RMS_EPS)
            o_ref[:, sl] = (gg * r * snw_ref[0, :, sl]).astype(_MM)
        o_ref[:, SSD_WIDTH:SSD_WIDTH + 512] = a_ref[...].astype(_MM)
        cv = c_ref[...]
        mu = jnp.mean(cv, axis=-1, keepdims=True)
        xc = cv - mu
        rs = lax.rsqrt(jnp.mean(xc * xc, axis=-1, keepdims=True) + LN_EPS)
        o_ref[:, SSD_WIDTH + 512:] = _silu(xc * rs * lw_ref[0] + lb_ref[0]).astype(_MM)

    return pl.pallas_call(
        body, name="mix_post", grid=(L // TM,),
        in_specs=[_rows(SSD_WIDTH), _rows(SSD_WIDTH), _rows(512), _rows(512),
                  _lp(SSD_WIDTH, i), _lp(512, i), _lp(512, i)],
        out_specs=_rows(D_MIX), out_shape=_sds((L, D_MIX), _MM),
        compiler_params=_cparams(("parallel",)),
    )(y, z, attn, c, snw, lw, lb)


def _mix_post_bwd(dcat, y, z, c, snw, lw, lb, i, after):
    L = y.shape[0]

    def body(d_ref, y_ref, z_ref, c_ref, snw_ref, lw_ref, lb_ref, after_ref,
             dy_ref, dz_ref, dc_ref, dsnw_ref, dlw_ref, dlb_ref):
        @pl.when(pl.program_id(0) == 0)
        def _():
            dsnw_ref[...] = jnp.zeros_like(dsnw_ref)
            dlw_ref[...] = jnp.zeros_like(dlw_ref)
            dlb_ref[...] = jnp.zeros_like(dlb_ref)

        yv = y_ref[...]
        zv = z_ref[...]
        sz = _silu(zv)
        g = yv * sz
        for k in range(SSD_GROUPS):
            sl = slice(k * GRP, (k + 1) * GRP)
            dgg, dwx = _rms_bwd(d_ref[:, sl], g[:, sl], snw_ref[0, :, sl], 1.0 / GRP)
            dsnw_ref[:, sl] += jnp.sum(dwx, axis=0, keepdims=True)
            dy_ref[:, sl] = dgg * sz[:, sl]
            dz_ref[:, sl] = (dgg * yv[:, sl] * _dsilu(zv[:, sl])).astype(dz_ref.dtype)
        cv = c_ref[...]
        mu = jnp.mean(cv, axis=-1, keepdims=True)
        xc = cv - mu
        rs = lax.rsqrt(jnp.mean(xc * xc, axis=-1, keepdims=True) + LN_EPS)
        xh = xc * rs
        ln = xh * lw_ref[0] + lb_ref[0]
        dln = d_ref[:, SSD_WIDTH + 512:] * _dsilu(ln)
        dlb_ref[...] += jnp.sum(dln, axis=0, keepdims=True)
        dlw_ref[...] += jnp.sum(dln * xh, axis=0, keepdims=True)
        dxh = dln * lw_ref[0]
        dc_ref[...] = rs * (dxh - jnp.mean(dxh, axis=-1, keepdims=True)
                            - xh * jnp.mean(dxh * xh, axis=-1, keepdims=True))

    return pl.pallas_call(
        body, name="mix_post_bwd", grid=(L // TM,),
        in_specs=[_rows(D_MIX), _rows(SSD_WIDTH), _rows(SSD_WIDTH), _rows(512),
                  _lp(SSD_WIDTH, i), _lp(512, i), _lp(512, i), _ANY],
        out_specs=[_rows(SSD_WIDTH), _rows(SSD_WIDTH), _rows(512), _full((1, SSD_WIDTH)), _full((1, 512)), _full((1, 512))],
        out_shape=[_sds((L, SSD_WIDTH)), _sds((L, SSD_WIDTH), _MM), _sds((L, 512)), _sds((1, SSD_WIDTH)), _sds((1, 512)), _sds((1, 512))],
        compiler_params=_cparams(("arbitrary",)),
    )(dcat, y, z, c, snw, lw, lb, after)


def _seg_mean_matrix():
    i = lax.broadcasted_iota(jnp.int32, (LANES, LANES), 0)
    j = lax.broadcasted_iota(jnp.int32, (LANES, LANES), 1)
    return jnp.where(i // ATTN_HEAD_DIM == j // ATTN_HEAD_DIM, 1.0 / ATTN_HEAD_DIM, 0.0).astype(F32)


def _rot_matrix():
    i = lax.broadcasted_iota(jnp.int32, (LANES, LANES), 0)
    j = lax.broadcasted_iota(jnp.int32, (LANES, LANES), 1)
    half = ATTN_HEAD_DIM // 2
    lo = (j % ATTN_HEAD_DIM) < half
    return jnp.where(lo & (i == j + half), -1.0, jnp.where((~lo) & (i == j - half), 1.0, 0.0)).astype(F32)


N_QK_TILES = 5
QK_W = N_QK_TILES * LANES


def _qk_prep(qkv, qw, kw, i, cos, sin):
    L = qkv.shape[0]

    def body(x_ref, qw_ref, kw_ref, c_ref, s_ref, o_ref):
        m64 = _seg_mean_matrix()
        rot = _rot_matrix()
        cs, sn = c_ref[...], s_ref[...]
        for t in range(N_QK_TILES):
            sl = slice(t * LANES, (t + 1) * LANES)
            x = x_ref[:, sl]
            w = qw_ref[0] if t < 4 else kw_ref[0]
            xn = x * lax.rsqrt(_mmx(x * x, m64) + RMS_EPS) * w
            o_ref[:, sl] = xn * cs + _mmx(xn, rot) * sn

    return pl.pallas_call(
        body, name="qk_prep", grid=(L // TM,),
        in_specs=[_rows(QK_W), _lp(LANES, i), _lp(LANES, i), _rows(LANES), _rows(LANES)],
        out_specs=_rows(QK_W), out_shape=_sds((L, QK_W)),
        compiler_params=_cparams(("parallel",)),
    )(qkv, qw, kw, cos, sin)


def _qk_prep_bwd(dq, dk, qkv, qw, kw, i, cos, sin):
    L = qkv.shape[0]

    def body(dq_ref, dk_ref, x_ref, qw_ref, kw_ref, c_ref, s_ref, dx_ref, dqw_ref, dkw_ref):
        @pl.when(pl.program_id(0) == 0)
        def _():
            dqw_ref[...] = jnp.zeros_like(dqw_ref)
            dkw_ref[...] = jnp.zeros_like(dkw_ref)

        m64 = _seg_mean_matrix()
        rot = _rot_matrix()
        cs, sn = c_ref[...], s_ref[...]
        for t in range(N_QK_TILES):
            sl = slice(t * LANES, (t + 1) * LANES)
            x = x_ref[:, sl]
            dy = dq_ref[:, sl] if t < 4 else dk_ref[...]
            w = qw_ref[0] if t < 4 else kw_ref[0]
            dxn = dy * cs - _mmx(dy * sn, rot)
            r = lax.rsqrt(_mmx(x * x, m64) + RMS_EPS)
            xh = x * r
            dxh = dxn * w
            dx_ref[:, sl] = (r * (dxh - xh * _mmx(dxh * xh, m64))).astype(dx_ref.dtype)
            dw = jnp.sum(dxn * xh, axis=0, keepdims=True)
            if t < 4:
                dqw_ref[...] += dw
            else:
                dkw_ref[...] += dw

    return pl.pallas_call(
        body, name="qk_prep_bwd", grid=(L // TM,),
        in_specs=[_rows(512), _rows(LANES), _rows(QK_W), _lp(LANES, i), _lp(LANES, i), _rows(LANES), _rows(LANES)],
        out_specs=[_rows(QK_W), _full((1, LANES)), _full((1, LANES))],
        out_shape=[_sds((L, QK_W), _MM), _sds((1, LANES)), _sds((1, LANES))],
        compiler_params=_cparams(("arbitrary",)),
    )(dq, dk, qkv, qw, kw, cos, sin)


HPG = 4
SCALE = 1.0 / math.sqrt(ATTN_HEAD_DIM)


def _heads_to_rows(q, g):
    return jnp.concatenate([q[:, (HPG * g + r) * ATTN_HEAD_DIM:(HPG * g + r + 1) * ATTN_HEAD_DIM] for r in range(HPG)], axis=0)


def _rows_to_heads(parts):
    return jnp.concatenate([p[r * Q:(r + 1) * Q] for p in parts for r in range(HPG)], axis=1)


def _attn_probs(q, k_own, k_prev, n, sink_ref, base):
    s_own = _mm(q, k_own, _NT) * SCALE
    s_prev = _mm(q, k_prev, _NT) * SCALE
    own = lax.broadcasted_iota(jnp.int32, s_own.shape, 1) <= lax.broadcasted_iota(jnp.int32, s_own.shape, 0) % Q
    s = jnp.where(own, s_own, jnp.where(n >= 1, s_prev, -jnp.inf))
    hrow = lax.broadcasted_iota(jnp.int32, (HPG * Q, 1), 0) // Q
    sink = jnp.zeros((HPG * Q, 1), F32)
    for r in range(HPG):
        sink = jnp.where(hrow == r, sink_ref[base + r], sink)
    m = jnp.maximum(jnp.max(s, axis=1, keepdims=True), sink)
    p = jnp.exp(s - m)
    es = jnp.exp(sink - m)
    inv = 1.0 / (jnp.sum(p, axis=1, keepdims=True) + es)
    return p * inv, own, es * inv


def _kv_blocks(ref, n):
    own = ref[pl.ds(pl.multiple_of(n * Q, Q), Q), :]
    prev = ref[pl.ds(pl.multiple_of(jnp.maximum(n - 1, 0) * Q, Q), Q), :]
    return own, prev


def _attn_fwd(qk, qkv, sinks, i):
    L = qk.shape[0]

    def body(sink_ref, q_ref, k_ref, v_ref, o_ref):
        n = pl.program_id(0)
        q = q_ref[...]
        k_own, k_prev = _kv_blocks(k_ref, n)
        v_own, v_prev = _kv_blocks(v_ref, n)
        outs = []
        for g in range(2):
            sl = slice(g * ATTN_HEAD_DIM, (g + 1) * ATTN_HEAD_DIM)
            p, own, _ = _attn_probs(_heads_to_rows(q, g), k_own[:, sl], k_prev[:, sl], n, sink_ref, i * ATTN_Q_HEADS + g * HPG)
            outs.append(_mm(jnp.where(own, p, 0.0), v_own[:, sl]) + _mm(jnp.where(own, 0.0, p), v_prev[:, sl]))
        o_ref[...] = _rows_to_heads(outs)

    return pl.pallas_call(
        body, name="attn_fwd", grid=(L // Q,),
        in_specs=[pl.BlockSpec(memory_space=pltpu.SMEM), _rows(512, Q),
                  pl.BlockSpec((L, LANES), lambda n: (0, 4)), pl.BlockSpec((L, LANES), lambda n: (0, 5))],
        out_specs=_rows(512, Q), out_shape=_sds((L, 512)),
        compiler_params=_cparams(("parallel",)),
    )(sinks, qk, qk, qkv)


def _attn_bwd(qk, qkv, sinks, i, dcat):
    L = qk.shape[0]

    def body(sink_ref, q_ref, k_ref, v_ref, do_ref, dq_ref, dk_ref, dv_ref, ds_ref):
        n = pl.program_id(0)

        @pl.when(n == 0)
        def _():
            dk_ref[...] = jnp.zeros_like(dk_ref)
            dv_ref[...] = jnp.zeros_like(dv_ref)
            ds_ref[...] = jnp.zeros_like(ds_ref)

        q = q_ref[...]
        do_all = do_ref[...]
        k_own, k_prev = _kv_blocks(k_ref, n)
        v_own, v_prev = _kv_blocks(v_ref, n)
        hrow = lax.broadcasted_iota(jnp.int32, (HPG * Q, 1), 0) // Q
        orow = lax.broadcasted_iota(jnp.int32, (8, LANES), 0)
        dqs, dks, dvs = [], [[], []], [[], []]
        acc = jnp.zeros((8, LANES), F32)
        for g in range(2):
            sl = slice(g * ATTN_HEAD_DIM, (g + 1) * ATTN_HEAD_DIM)
            qg = _heads_to_rows(q, g)
            do = _heads_to_rows(do_all, g)
            p, own, ps = _attn_probs(qg, k_own[:, sl], k_prev[:, sl], n, sink_ref, i * ATTN_Q_HEADS + g * HPG)
            dp = jnp.where(own, _mm(do, v_own[:, sl], _NT), _mm(do, v_prev[:, sl], _NT))
            delta = jnp.sum(p * dp, axis=1, keepdims=True)
            ds = p * (dp - delta)
            parts = ((jnp.where(own, ds, 0.0), jnp.where(own, p, 0.0)), (jnp.where(own, 0.0, ds), jnp.where(own, 0.0, p)))
            dqs.append((_mm(parts[0][0], k_own[:, sl]) + _mm(parts[1][0], k_prev[:, sl])) * SCALE)
            for b, (ds_b, p_b) in enumerate(parts):
                dks[b].append(_mm(ds_b, qg, _TN) * SCALE)
                dvs[b].append(_mm(p_b, do, _TN))
            dsink = -(ps * delta)
            for r in range(HPG):
                tot = jnp.sum(jnp.where(hrow == r, dsink, 0.0), axis=0, keepdims=True)
                acc = acc + jnp.where(orow == g * HPG + r, tot, 0.0)
        dq_ref[...] = _rows_to_heads(dqs)
        so = pl.multiple_of(n * Q, Q)
        sp = pl.multiple_of(jnp.maximum(n - 1, 0) * Q, Q)
        dk_ref[pl.ds(so, Q), :] += jnp.concatenate(dks[0], axis=1)
        dv_ref[pl.ds(so, Q), :] += jnp.concatenate(dvs[0], axis=1)
        dk_ref[pl.ds(sp, Q), :] += jnp.concatenate(dks[1], axis=1)
        dv_ref[pl.ds(sp, Q), :] += jnp.concatenate(dvs[1], axis=1)
        ds_ref[...] += acc

    return pl.pallas_call(
        body, name="attn_bwd", grid=(L // Q,),
        in_specs=[pl.BlockSpec(memory_space=pltpu.SMEM), _rows(512, Q),
                  pl.BlockSpec((L, LANES), lambda n: (0, 4)), pl.BlockSpec((L, LANES), lambda n: (0, 5)),
                  _rows(512, Q, 2)],
        out_specs=[_rows(512, Q), _full((L, LANES)), _full((L, LANES)), _full((8, LANES))],
        out_shape=[_sds((L, 512)), _sds((L, LANES)), _sds((L, LANES)), _sds((8, LANES))],
        compiler_params=_cparams(("arbitrary",)),
    )(sinks, qk, qk, qkv, dcat)


N_PAIR = SSD_HEADS // 2
P = 64
OFF_B = SSD_WIDTH
OFF_C = SSD_WIDTH + SSD_GROUPS * SSD_STATE


def _expand_matrix():
    i = lax.broadcasted_iota(jnp.int32, (LANES, SSD_WIDTH), 0)
    j = lax.broadcasted_iota(jnp.int32, (LANES, SSD_WIDTH), 1)
    return jnp.where(j // P == i, 1.0, 0.0).astype(F32)


def _ssd_chunk_common(dtr_ref, bias_ref, alog_ref):
    dt = jax.nn.softplus(dtr_ref[...] + bias_ref[0])
    a = -jnp.exp(alog_ref[0])
    adt = dt * a
    ri = lax.broadcasted_iota(jnp.int32, (Q, Q), 0)
    ci = lax.broadcasted_iota(jnp.int32, (Q, Q), 1)
    causal = ri >= ci
    tri = jnp.where(causal, 1.0, 0.0).astype(F32)
    acs = _mmx(tri, adt, exact="a")
    em = _expand_matrix()
    acs_e = _mmx(acs, em)
    dt_e = _mmx(dt, em)
    alast_e = acs_e[Q - 1:Q, :]
    return dt, a, acs, causal, tri, em, acs_e, dt_e, alast_e


def _ssd_fwd(xbc, dtr, bias, alog, d_e, i):
    L = xbc.shape[0]
    nc = L // Q

    def body(xbc_ref, dtr_ref, bias_ref, alog_ref, de_ref, y_ref, hp_ref, st_ref):
        @pl.when(pl.program_id(0) == 0)
        def _():
            st_ref[...] = jnp.zeros_like(st_ref)

        dt, a, acs, causal, tri, em, acs_e, dt_e, alast_e = _ssd_chunk_common(dtr_ref, bias_ref, alog_ref)
        acs_t = acs.T
        x = xbc_ref[:, :SSD_WIDTH]
        xdt = x * dt_e
        ea_e = jnp.exp(acs_e)
        xds = xdt * jnp.exp(alast_e - acs_e)
        cd_e = jnp.exp(alast_e)
        lane = lax.broadcasted_iota(jnp.int32, (Q, LANES), 1)
        lo = lane < P
        for g in range(SSD_GROUPS):
            bg = xbc_ref[:, OFF_B + g * SSD_STATE:OFF_B + (g + 1) * SSD_STATE]
            cg = xbc_ref[:, OFF_C + g * SSD_STATE:OFF_C + (g + 1) * SSD_STATE]
            cb = _mm(cg, bg, _NT)
            for pp in range(N_PAIR // SSD_GROUPS):
                pr = g * (N_PAIR // SSD_GROUPS) + pp
                sl = slice(pr * LANES, (pr + 1) * LANES)
                xdt_p = xdt[:, sl]
                yd = jnp.zeros((Q, LANES), F32)
                for half in range(2):
                    h = 2 * pr + half
                    rowb = jnp.broadcast_to(acs_t[h:h + 1, :], (Q, Q))
                    lm = jnp.exp(jnp.where(causal, rowb.T - rowb, -jnp.inf))
                    xh = jnp.where(lo if half == 0 else ~lo, xdt_p, 0.0)
                    yd = yd + _mm(cb * lm, xh)
                hp = st_ref[pr]
                hp_ref[0, pr] = hp
                yoff = _mm(cg, hp) * ea_e[:, sl]
                y_ref[:, sl] = yd + yoff + x[:, sl] * de_ref[0, :, sl]
                st_ref[pr] = hp * cd_e[:, sl] + _mm(bg, xds[:, sl], _TN)

    return pl.pallas_call(
        body, name="ssd_fwd", grid=(nc,),
        in_specs=[_rows(SSD_XBC, Q), _rows(LANES, Q), _lp(LANES, i), _lp(LANES, i), _lp(SSD_WIDTH, i)],
        out_specs=[_rows(SSD_WIDTH, Q), pl.BlockSpec((1, N_PAIR, SSD_STATE, LANES), lambda c: (c, 0, 0, 0))],
        out_shape=[_sds((L, SSD_WIDTH)), _sds((nc, N_PAIR, SSD_STATE, LANES))],
        scratch_shapes=[pltpu.VMEM((N_PAIR, SSD_STATE, LANES), F32)],
        compiler_params=_cparams(("arbitrary",)),
    )(xbc, dtr, bias, alog, d_e)


def _ssd_bwd(xbc, dtr, bias, alog, d_e, i, hprev, dy):
    L = xbc.shape[0]
    nc = L // Q
    rev = lambda c: (nc - 1 - c, 0)

    def body(xbc_ref, dtr_ref, bias_ref, alog_ref, de_ref, hp_ref, dy_ref,
             dxbc_ref, ddtr_ref, dbias_ref, dalog_ref, dd_ref, dst_ref):
        @pl.when(pl.program_id(0) == 0)
        def _():
            dst_ref[...] = jnp.zeros_like(dst_ref)
            dbias_ref[...] = jnp.zeros_like(dbias_ref)
            dalog_ref[...] = jnp.zeros_like(dalog_ref)
            dd_ref[...] = jnp.zeros_like(dd_ref)

        dt, a, acs, causal, tri, em, acs_e, dt_e, alast_e = _ssd_chunk_common(dtr_ref, bias_ref, alog_ref)
        acs_t = acs.T
        x = xbc_ref[:, :SSD_WIDTH]
        dy = dy_ref[...]
        xdt = x * dt_e
        ea_e = jnp.exp(acs_e)
        dse = jnp.exp(alast_e - acs_e)
        xds = xdt * dse
        cd_e = jnp.exp(alast_e)
        lane = lax.broadcasted_iota(jnp.int32, (Q, LANES), 1)
        lo = lane < P
        sub = lax.broadcasted_iota(jnp.int32, (Q, Q), 0)
        lan = lax.broadcasted_iota(jnp.int32, (Q, Q), 1)

        da_rows = jnp.zeros((Q, Q), F32)
        da_cols_t = jnp.zeros((Q, Q), F32)
        dxdt_parts = []
        wyoff_parts = []
        dcd_parts = []
        dxds_parts = []
        for g in range(SSD_GROUPS):
            bg = xbc_ref[:, OFF_B + g * SSD_STATE:OFF_B + (g + 1) * SSD_STATE]
            cg = xbc_ref[:, OFF_C + g * SSD_STATE:OFF_C + (g + 1) * SSD_STATE]
            cb = _mm(cg, bg, _NT)
            dcb = jnp.zeros((Q, Q), F32)
            dcg = jnp.zeros((Q, SSD_STATE), F32)
            dbg = jnp.zeros((Q, SSD_STATE), F32)
            for pp in range(N_PAIR // SSD_GROUPS):
                pr = g * (N_PAIR // SSD_GROUPS) + pp
                sl = slice(pr * LANES, (pr + 1) * LANES)
                xdt_p = xdt[:, sl]
                dy_p = dy[:, sl]
                dxdt_p = jnp.zeros((Q, LANES), F32)
                for half in range(2):
                    h = 2 * pr + half
                    hm = lo if half == 0 else ~lo
                    rowb = jnp.broadcast_to(acs_t[h:h + 1, :], (Q, Q))
                    lm = jnp.exp(jnp.where(causal, rowb.T - rowb, -jnp.inf))
                    m = cb * lm
                    dyh = jnp.where(hm, dy_p, 0.0)
                    gmat = _mm(dyh, xdt_p, _NT)
                    w = gmat * m
                    da_rows = da_rows + jnp.where(lan == h, jnp.sum(w, axis=1, keepdims=True), 0.0)
                    da_cols_t = da_cols_t + jnp.where(sub == h, jnp.sum(w, axis=0, keepdims=True), 0.0)
                    dcb = dcb + gmat * lm
                    dxdt_p = dxdt_p + _mm(m, dyh, _TN)
                hp = hp_ref[0, pr]
                dt_off = dy_p * ea_e[:, sl]
                t_off = _mm(cg, hp)
                wyoff_parts.append(dt_off * t_off)
                dcg = dcg + _mm(dt_off, hp, _NT)
                dhp = _mm(cg, dt_off, _TN)
                dS = dst_ref[pr]
                dxds_p = _mm(bg, dS)
                dbg = dbg + _mm(xds[:, sl], dS, _NT)
                dxds_parts.append(dxds_p)
                dxdt_parts.append(dxdt_p + dxds_p * dse[:, sl])
                dcd_parts.append(jnp.sum(dS * hp, axis=0, keepdims=True))
                dst_ref[pr] = dS * cd_e[:, sl] + dhp
            dcg = dcg + _mm(dcb, bg)
            dbg = dbg + _mm(dcb, cg, _TN)
            dxbc_ref[:, OFF_C + g * SSD_STATE:OFF_C + (g + 1) * SSD_STATE] = dcg
            dxbc_ref[:, OFF_B + g * SSD_STATE:OFF_B + (g + 1) * SSD_STATE] = dbg
        dxdt = jnp.concatenate(dxdt_parts, axis=1)
        dxds = jnp.concatenate(dxds_parts, axis=1)
        wyoff = jnp.concatenate(wyoff_parts, axis=1)
        dcd = jnp.concatenate(dcd_parts, axis=1)
        dxbc_ref[:, :SSD_WIDTH] = dy * de_ref[0] + dxdt * dt_e
        zds = dxds * xds
        dacs = _mmx(wyoff - zds, em, _NT) + da_rows - da_cols_t.T
        dalast = _mmx(jnp.broadcast_to(jnp.sum(zds, axis=0, keepdims=True) + dcd * cd_e, (8, SSD_WIDTH)), em, _NT)[0:1, :]
        dacs = dacs + jnp.where(sub == Q - 1, dalast, 0.0)
        dadt = _mmx(tri, dacs, _TN, exact="a")
        ddt = dadt * a + _mmx(dxdt * x, em, _NT)
        ddtr = ddt * _sigmoid(dtr_ref[...] + bias_ref[0])
        ddtr_ref[...] = ddtr.astype(ddtr_ref.dtype)
        row0 = lax.broadcasted_iota(jnp.int32, (8, LANES), 0) == 0
        dbias_ref[...] += jnp.where(row0, jnp.sum(ddtr, axis=0, keepdims=True), 0.0)
        dalog_ref[...] += jnp.where(row0, jnp.sum(dadt * dt, axis=0, keepdims=True) * a, 0.0)
        ddx = _mmx(jnp.broadcast_to(jnp.sum(dy * x, axis=0, keepdims=True), (8, SSD_WIDTH)), em, _NT)
        dd_ref[...] += jnp.where(row0, ddx, 0.0)

    acc = _full((8, LANES))
    return pl.pallas_call(
        body, name="ssd_bwd", grid=(nc,),
        in_specs=[pl.BlockSpec((Q, SSD_XBC), rev), pl.BlockSpec((Q, LANES), rev),
                  _lp(LANES, i), _lp(LANES, i), _lp(SSD_WIDTH, i),
                  pl.BlockSpec((1, N_PAIR, SSD_STATE, LANES), lambda c: (nc - 1 - c, 0, 0, 0)), pl.BlockSpec((Q, SSD_WIDTH), rev)],
        out_specs=[pl.BlockSpec((Q, SSD_XBC), rev), pl.BlockSpec((Q, LANES), rev), acc, acc, acc],
        out_shape=[_sds((L, SSD_XBC)), _sds((L, LANES), _MM), _sds((8, LANES)), _sds((8, LANES)), _sds((8, LANES))],
        scratch_shapes=[pltpu.VMEM((N_PAIR, SSD_STATE, LANES), F32)],
        compiler_params=_cparams(("arbitrary",)),
    )(xbc, dtr, bias, alog, d_e, hprev, dy)


def _rope_tables(L):
    inv_freq = ROPE_THETA ** (-jnp.arange(0, ATTN_HEAD_DIM, 2, dtype=F32) / ATTN_HEAD_DIM)
    ang = jnp.arange(L, dtype=F32)[:, None] * inv_freq[None, :]
    return jnp.tile(jnp.cos(ang), (1, 4)), jnp.tile(jnp.sin(ang), (1, 4))


def _stacked_params(small, conv_w, cm_w):
    row = lambda a: a[:, None, :]
    pad = lambda a: jnp.pad(a, ((0, 0), (0, LANES - a.shape[1])))[:, None, :]
    return dict(
        nw_mix=row(small["norm_mix_w"]), conv_w=conv_w, conv_b=row(small["ssd_conv_b"]),
        dt_bias=pad(small["ssd_dt_bias"]), a_log=pad(small["ssd_a_log"]),
        d_e=row(jnp.repeat(small["ssd_d"], P, axis=1)), snw=row(small["ssd_norm_w"]),
        qw=row(jnp.tile(small["q_norm_w"], (1, 2))), kw=row(jnp.tile(small["k_norm_w"], (1, 2))),
        sinks=small["attn_sinks"].reshape(-1), cm_w=cm_w, cm_b=row(small["cm_dw_b"]),
        ln_w=row(small["cm_ln_w"]), ln_b=row(small["cm_ln_b"]), nw_mlp=row(small["norm_mlp_w"]))


def _layer_fwd(x, p, w_in8, after_in_proj, late_weights, before_down, i, cos, sin, after):
    w_in = _w_in_regroup(w_in8, after)
    h, z, xbc, qkv, glu, dtr = _in_proj(x, p["nw_mix"], i, w_in)
    xbc_c = _ssd_conv_fwd(xbc, p["conv_w"], p["conv_b"], i, after_in_proj(z))
    y_ssd, hprev = _ssd_fwd(xbc_c, dtr, p["dt_bias"], p["a_log"], p["d_e"], i)
    qk = _qk_prep(qkv, p["qw"], p["kw"], i, cos, sin)
    attn = _attn_fwd(qk, qkv, p["sinks"], i)
    c = _cm_conv_fwd(glu, p["cm_w"], p["cm_b"], i)
    ycat = _mix_post(y_ssd, z, attn, c, p["snw"], p["ln_w"], p["ln_b"], i)
    w_out8, w_up8, w_down8 = late_weights(ycat)
    x1 = _mm_res(ycat, w_out8, i, x, "out_proj", ycat)
    hm, up, act = _mlp_up(x1, p["nw_mlp"], i, w_up8)
    x2 = _mm_res(act, w_down8, i, x1, "mlp_down", before_down(act))
    saved = dict(x=x, h=h, z=z, xbc=xbc, qkv=qkv, glu=glu, dtr=dtr, xbc_c=xbc_c, y_ssd=y_ssd, hprev=hprev,
                 qk=qk, c=c, ycat=ycat, x1=x1, hm=hm, up=up, act=act, w_in=w_in,
                 w_out8=w_out8, w_up8=w_up8, w_down8=w_down8)
    return x2, saved


def _layer_bwd_mlp(dx2, p, i, s, after):
    d_up = _mlp_down_bwd(dx2, s["w_down8"], i, s["up"], after)
    g_down = _mm_tn(s["act"], dx2, "dw_down", 512, D_MODEL, "rows")
    g_up = _mm_tn(s["hm"], d_up, "dw_up", D_MODEL, FF_SHARD, "cols")
    dx1, g_nw_mlp = _mlp_up_bwd(d_up, s["w_up8"], i, dx2, s["x1"], p["nw_mlp"])
    dcat = _out_proj_bwd(dx1, s["w_out8"], i, g_nw_mlp)
    g_out = _mm_tn(s["ycat"], dx1, "dw_out", 512, D_MODEL, "rows")
    return dx1, dcat, g_out, g_up, g_down, g_nw_mlp


def _layer_bwd_mix(dx1, dcat, g_nw_mlp, p, i, s, cos, sin, after):
    dy_ssd, dz, dc, g_snw, g_lw, g_lb = _mix_post_bwd(dcat, s["y_ssd"], s["z"], s["c"], p["snw"], p["ln_w"], p["ln_b"], i, after)
    da, dg, g_cmw, g_cmb = _cm_conv_bwd(s["glu"], p["cm_w"], i, dc)
    dq, dk, dv, dsk = _attn_bwd(s["qk"], s["qkv"], p["sinks"], i, dcat)
    dqk_raw, g_qw, g_kw = _qk_prep_bwd(dq, dk, s["qkv"], p["qw"], p["kw"], i, cos, sin)
    dxbc_c, ddtr, g_bias, g_alog, g_d = _ssd_bwd(s["xbc_c"], s["dtr"], p["dt_bias"], p["a_log"], p["d_e"], i, s["hprev"], dy_ssd)
    dxbc, g_convw, g_convb = _ssd_conv_bwd(s["xbc"], p["conv_w"], p["conv_b"], i, dxbc_c)
    dx, g_nw_mix, du = _in_proj_bwd([dz, dxbc, dqk_raw, dv, da, dg, ddtr], s["w_in"], dx1, s["x"], p["nw_mix"], i)
    g_in = _g_in_split(_mm_tn(s["h"], du, "dw_in", 512, 640, "flat"))
    half = ATTN_HEAD_DIM
    small = dict(
        norm_mix_w=g_nw_mix[0], ssd_conv_b=g_convb[0], ssd_dt_bias=g_bias[0, :SSD_HEADS], ssd_a_log=g_alog[0, :SSD_HEADS],
        ssd_d=g_d[0, :SSD_HEADS], ssd_norm_w=g_snw[0], q_norm_w=g_qw[0, :half] + g_qw[0, half:],
        k_norm_w=g_kw[0, :half] + g_kw[0, half:], attn_sinks=dsk[:, 0],
        cm_dw_b=g_cmb[0], cm_ln_w=g_lw[0], cm_ln_b=g_lb[0], norm_mlp_w=g_nw_mlp[0],
        ssd_conv_w=g_convw, cm_dw_w=g_cmw)
    return dx, g_in, small


MESH = pl.DeviceIdType.MESH
_ANY = pl.BlockSpec(memory_space=pl.ANY)


def _coords():
    return lax.axis_index("x"), lax.axis_index("y"), lax.axis_index("c")


def _all_gather(xs, name):
    nt = len(xs)

    def body(*refs):
        x_refs, out_refs = refs[:nt], refs[nt:2 * nt]
        send_sems, recv_sems, local_sems = refs[2 * nt:]
        x, y, c = _coords()
        me, sibling = (x, y, c), (x, y, 1 - c)
        chips = [(1 - x, y), (x, 1 - y), (1 - x, 1 - y)]

        def slot(t, px, py, pc):
            return out_refs[t].at[4 * px + 2 * py + pc]

        def copy(t, k, block, to, src=None):
            return pltpu.make_async_remote_copy(
                src_ref=slot(t, *block) if src is None else src, dst_ref=slot(t, *block),
                send_sem=send_sems.at[7 * t + k], recv_sem=recv_sems.at[7 * t + k], device_id=to, device_id_type=MESH)

        mine = [pltpu.make_async_copy(x_refs[t], slot(t, *me), local_sems.at[t]) for t in range(nt)]
        for cp in mine:
            cp.start()
        first = []
        for t in range(nt):
            first.append(copy(t, 0, me, sibling, src=x_refs[t]))
            first += [copy(t, 1 + j, me, (*chip, c), src=x_refs[t]) for j, chip in enumerate(chips)]
        for cp in first:
            cp.start()
        passed = []
        for j, chip in enumerate(chips):
            for t in range(nt):
                copy(t, 1 + j, (*chip, c), me).wait_recv()
                passed.append(copy(t, 4 + j, (*chip, c), sibling))
                passed[-1].start()
        for t in range(nt):
            copy(t, 0, sibling, me).wait_recv()
            for j, chip in enumerate(chips):
                copy(t, 4 + j, (*chip, 1 - c), me).wait_recv()
        for cp in first + passed:
            cp.wait_send()
        for cp in mine:
            cp.wait()

    return pl.pallas_call(
        body, name=name, out_shape=[_sds((N_DEV,) + a.shape, a.dtype) for a in xs],
        in_specs=[_ANY] * nt, out_specs=[_ANY] * nt,
        scratch_shapes=[pltpu.SemaphoreType.DMA((7 * nt,)), pltpu.SemaphoreType.DMA((7 * nt,)), pltpu.SemaphoreType.DMA((nt,))],
    )(*xs)


def _peer_chips(x, y):
    return [(1 - x, y), (x, 1 - y), (1 - x, 1 - y)]


def _copies_per_tensor(kind):
    return {"gather": 3, "forward": 4, "scatter": 7, "scatter_chips": 3}[kind]


def _ici_copies(src_refs, land_refs, send_sems, recv_sems, kind):
    x, y, c = _coords()
    if kind == "forward":
        sends, recvs = [], []
        for t, d in enumerate(land_refs):
            for j, (px, py) in enumerate([(x, y)] + _peer_chips(x, y)):
                sems = dict(send_sem=send_sems.at[4 * t + j], recv_sem=recv_sems.at[4 * t + j],
                            device_id=(x, y, 1 - c), device_id_type=MESH)
                src = d.at[4 * px + 2 * py + c]
                sends.append(pltpu.make_async_remote_copy(src_ref=src, dst_ref=src, **sems))
                recvs.append(pltpu.make_async_remote_copy(src_ref=src, dst_ref=d.at[4 * px + 2 * py + 1 - c], **sems))
        return sends, recvs
    flip = lambda v, bit: 1 - v if bit else v
    if kind == "scatter":
        peers = [(flip(x, k & 4), flip(y, k & 2), flip(c, k & 1)) for k in range(1, N_DEV)]
    else:
        peers = [(px, py, c) for px, py in _peer_chips(x, y)]
    index = (lambda px, py, pc: 2 * px + py) if kind == "scatter_chips" else (lambda px, py, pc: 4 * px + 2 * py + pc)
    me = index(x, y, c)
    sends, recvs = [], []
    for t, d in enumerate(land_refs):
        for j, p in enumerate(peers):
            src = d.at[me] if kind == "gather" else src_refs[t].at[index(*p)]
            k = len(peers) * t + j
            sems = dict(send_sem=send_sems.at[k], recv_sem=recv_sems.at[k], device_id=p, device_id_type=MESH)
            sends.append(pltpu.make_async_remote_copy(src_ref=src, dst_ref=d.at[me], **sems))
            recvs.append(pltpu.make_async_remote_copy(src_ref=src, dst_ref=d.at[index(*p)], **sems))
    return sends, recvs


_HBM = pl.BlockSpec(memory_space=pltpu.HBM)
_SEMS = pl.BlockSpec(memory_space=pltpu.SEMAPHORE)
_EFFECT = pltpu.SideEffectType.DATAFLOW_SIDE_EFFECTING


def _hbm(a):
    return pltpu.with_memory_space_constraint(a, pltpu.HBM)


def _ici_start(srcs, lands, after, name, kind):
    ns, n = len(srcs), len(lands)
    nt = ns + n

    def body(*refs):
        sends, _ = _ici_copies(refs[:ns], refs[ns:nt], refs[nt + 1], refs[nt + 2], kind)
        for cp in sends:
            cp.start()
        refs[-1][...] = jnp.zeros_like(refs[-1])

    thru = srcs + lands
    out = pl.pallas_call(
        body, name=name,
        out_shape=(pltpu.SemaphoreType.DMA((_copies_per_tensor(kind) * n,)),) * 2
        + tuple(pltpu.HBM(a.shape, a.dtype) for a in thru) + (_sds((8, LANES)),),
        in_specs=[_HBM] * nt + [_ANY],
        out_specs=(_SEMS, _SEMS) + (_HBM,) * nt + (pl.BlockSpec(memory_space=pltpu.VMEM),),
        input_output_aliases={k: 2 + k for k in range(nt)},
        compiler_params=pltpu.CompilerParams(has_side_effects=_EFFECT),
    )(*[_hbm(a) for a in thru], after)
    return out[0], out[1], list(out[2:2 + ns]), list(out[2 + ns:2 + nt]), out[-1]


def _ici_wait(started, after, name, kind):
    send_sems, recv_sems, srcs, lands, _ = started
    ns, n = len(srcs), len(lands)
    nt = ns + n

    def body(*refs):
        sends, recvs = _ici_copies(refs[:ns], refs[ns:nt], refs[nt], refs[nt + 1], kind)
        for s, r in zip(sends, recvs):
            s.wait_send()
            r.wait_recv()

    thru = srcs + lands
    out = pl.pallas_call(
        body, name=name, out_shape=tuple(pltpu.HBM(a.shape, a.dtype) for a in thru),
        in_specs=[_HBM] * nt + [_SEMS, _SEMS, _ANY], out_specs=(_HBM,) * nt,
        input_output_aliases={k: k for k in range(nt)},
        compiler_params=pltpu.CompilerParams(has_side_effects=_EFFECT),
    )(*thru, send_sems, recv_sems, after)
    return list(out[:ns]), list(out[ns:])


def _ag_d2d(lands):
    n = len(lands)

    def body(*refs):
        in_refs, out_refs = refs[:n], refs[n:2 * n]
        send_sems, recv_sems = refs[2 * n:]
        x, y, c = _coords()
        sends, recvs = [], []
        for t in range(n):
            for k, (px, py) in enumerate([(x, y)] + _peer_chips(x, y)):
                sems = dict(send_sem=send_sems.at[4 * t + k], recv_sem=recv_sems.at[4 * t + k],
                            device_id=(x, y, 1 - c), device_id_type=MESH)
                src = in_refs[t].at[4 * px + 2 * py + c]
                sends.append(pltpu.make_async_remote_copy(src_ref=src, dst_ref=out_refs[t].at[4 * px + 2 * py + c], **sems))
                recvs.append(pltpu.make_async_remote_copy(src_ref=src, dst_ref=out_refs[t].at[4 * px + 2 * py + 1 - c], **sems))
        for cp in sends:
            cp.start()
        for cp in recvs:
            cp.wait_recv()
        for cp in sends:
            cp.wait_send()

    return pl.pallas_call(
        body, name="ag_d2d", out_shape=[_sds(a.shape, a.dtype) for a in lands],
        in_specs=[_ANY] * n, out_specs=[_ANY] * n,
        input_output_aliases={k: k for k in range(n)},
        scratch_shapes=[pltpu.SemaphoreType.DMA((4 * n,)), pltpu.SemaphoreType.DMA((4 * n,))],
    )(*lands)


def _rs_sib(grads):
    nt = len(grads)

    def body(*refs):
        s_refs, ra_refs = refs[:nt], refs[nt:2 * nt]
        send_sems, recv_sems = refs[2 * nt:]
        x, y, c = _coords()
        cps = [pltpu.make_async_remote_copy(
            src_ref=s_refs[t].at[:, 1 - c], dst_ref=ra_refs[t], send_sem=send_sems.at[t], recv_sem=recv_sems.at[t],
            device_id=(x, y, 1 - c), device_id_type=MESH) for t in range(nt)]
        for cp in cps:
            cp.start()
        for cp in cps:
            cp.wait()

    return pl.pallas_call(
        body, name="rs_sibling",
        out_shape=[_sds((4,) + g.shape[2:], g.dtype) for g in grads],
        in_specs=[_ANY] * nt, out_specs=[_ANY] * nt,
        scratch_shapes=[pltpu.SemaphoreType.DMA((nt,)), pltpu.SemaphoreType.DMA((nt,))],
    )(*grads)


def _rs_add(grads, ras, core):
    nt = len(grads)

    def body(c_ref, *refs):
        s_refs, ra_refs, q_refs, rb_refs = refs[:nt], refs[nt:2 * nt], refs[2 * nt:3 * nt], refs[3 * nt:]
        for t in range(nt):
            q = (s_refs[t][0, 0].astype(F32) + ra_refs[t][0].astype(F32)).astype(q_refs[t].dtype)
            q_refs[t][0] = q
            rb_refs[t][0] = q

    nr = 4
    own = [pl.BlockSpec((1, 1, g.shape[2] // nr, g.shape[3]), lambda j, r, c: (j, c[0], r, 0)) for g in grads]
    blk = [pl.BlockSpec((1, g.shape[2] // nr, g.shape[3]), lambda j, r, c: (j, r, 0)) for g in grads]
    return pl.pallas_call(
        body, name="rs_add", out_shape=[_sds(r.shape, r.dtype) for r in ras] * 2,
        grid_spec=pltpu.PrefetchScalarGridSpec(num_scalar_prefetch=1, grid=(4, nr), in_specs=own + blk, out_specs=blk * 2),
        compiler_params=_cparams(("parallel", "parallel")),
    )(core, *grads, *ras)


def _adamw(w, g, m, v):
    m = ADAM_B1 * m + (1.0 - ADAM_B1) * g
    v = ADAM_B2 * v + (1.0 - ADAM_B2) * jnp.square(g)
    m_hat = m / (1.0 - ADAM_B1 ** ADAM_STEP)
    v_hat = v / (1.0 - ADAM_B2 ** ADAM_STEP)
    delta = -ADAM_LR * (m_hat / (jnp.sqrt(v_hat) + ADAM_EPS) + ADAM_WD * w)
    return delta, m, v


def _sum_partials(s_own, rb_ref, me, acc_ref):
    if s_own is None:
        g = rb_ref[0].astype(F32)
        for j in range(1, rb_ref.shape[0]):
            g = g + rb_ref[j].astype(F32)
        acc_ref[...] = g
        return
    for d0 in range(N_DEV):
        @pl.when(me == d0)
        def _():
            g = None
            for d in range(N_DEV):
                term = (s_own if d == d0 else rb_ref[d]).astype(F32)
                g = term if g is None else g + term
            acc_ref[...] = g


def _rs_final(s, rb, me, w, m, v, outs, l):
    _, R, C = w.shape
    cp = rb.shape[2]
    own = [] if s is None else [s]

    def body(me_ref, *refs):
        s_ref = None if s is None else refs[0]
        rb_ref, w_ref, m_ref, v_ref = refs[len(own):len(own) + 4]
        g_ref, d_ref, m2_ref, v2_ref, acc_ref = refs[len(own) + 8:]
        _sum_partials(None if s is None else s_ref[0], rb_ref, me_ref[0], acc_ref)
        g = acc_ref[...][:, :C]
        g_ref[0] = g
        d_ref[0], m2_ref[0], v2_ref[0] = _adamw(w_ref[0], g, m_ref[0], v_ref[0])

    blk = pl.BlockSpec((1, TM, C), lambda r, me: (l, r, 0))
    return pl.pallas_call(
        body, name="rs_final_adamw", out_shape=[_sds(w.shape)] * 4,
        grid_spec=pltpu.PrefetchScalarGridSpec(
            num_scalar_prefetch=1, grid=(R // TM,),
            in_specs=[pl.BlockSpec((1, TM, cp), lambda r, me: (me[0], r, 0))] * len(own)
            + [pl.BlockSpec((rb.shape[0], TM, cp), lambda r, me: (0, r, 0)), blk, blk, blk] + [_ANY] * 4,
            out_specs=[blk] * 4, scratch_shapes=[pltpu.VMEM((TM, cp), F32)]),
        input_output_aliases={5 + len(own) + k: k for k in range(4)},
        compiler_params=_cparams(("parallel",)),
    )(me, *own, rb, w, m, v, *outs)


def _rs_final_w_in(s, rb, me, wt, mt, vt, outs, l):
    shard = (W_IN_SHARD, D_MODEL)
    own = [] if s is None else [s]

    def body(me_ref, rb_ref, *refs):
        wt_ref, mt_ref, vt_ref = refs[len(own):len(own) + 3]
        g_ref, d_ref, m2_ref, v2_ref, sbuf, acc_ref, bufs, obufs, sems = refs[len(own) + 7:]
        me = me_ref[0]
        loads = [pltpu.make_async_copy(src.at[:, l, :], bufs.at[k], sems.at[k]) for k, src in enumerate((wt_ref, mt_ref, vt_ref))]
        if own:
            loads.append(pltpu.make_async_copy(refs[0].at[me], sbuf, sems.at[7]))
        for cp in loads:
            cp.start()
        if own:
            loads[3].wait()
        _sum_partials(sbuf[...] if own else None, rb_ref, me, acc_ref)
        g = acc_ref[...].T[:W_IN_SHARD]
        for cp in loads[:3]:
            cp.wait()
        obufs[0] = g
        obufs[1], obufs[2], obufs[3] = _adamw(bufs[0], g, bufs[1], bufs[2])
        stores = [pltpu.make_async_copy(obufs.at[k], dst.at[:, l, :], sems.at[3 + k])
                  for k, dst in enumerate((g_ref, d_ref, m2_ref, v2_ref))]
        for cp in stores:
            cp.start()
        for cp in stores:
            cp.wait()

    return pl.pallas_call(
        body, name="rs_final_adamw_w_in",
        in_specs=[pl.BlockSpec(memory_space=pltpu.SMEM), pl.BlockSpec(memory_space=pltpu.VMEM)] + [_ANY] * (7 + len(own)),
        out_specs=[_ANY] * 4, out_shape=[_sds(wt.shape)] * 4,
        input_output_aliases={5 + len(own) + k: k for k in range(4)},
        scratch_shapes=[pltpu.VMEM(rb.shape[1:], rb.dtype), pltpu.VMEM(rb.shape[1:], F32),
                        pltpu.VMEM((3,) + shard, F32), pltpu.VMEM((4,) + shard, F32), pltpu.SemaphoreType.DMA((8,))],
        compiler_params=_cparams(),
    )(me, rb, *own, wt, mt, vt, *outs)


def _sum8(g8):
    _, R, C = g8.shape

    def body(g_ref, o_ref):
        acc = g_ref[0]
        for d in range(1, N_DEV):
            acc = acc + g_ref[d]
        o_ref[...] = acc

    return pl.pallas_call(body, name="small_sum", out_shape=_sds((R, C)))(g8)


def _adamw_small(w, g, m, v):
    def body(w_ref, g_ref, m_ref, v_ref, d_ref, m2_ref, v2_ref):
        d_ref[...], m2_ref[...], v2_ref[...] = _adamw(w_ref[...], g_ref[...], m_ref[...], v_ref[...])

    return pl.pallas_call(body, name="small_adamw", out_shape=[_sds(w.shape)] * 3)(w, g, m, v)


REP = (("norm_mix_w", 1024), ("ssd_conv_b", 1536), ("ssd_dt_bias", 16), ("ssd_a_log", 16), ("ssd_d", 16),
       ("ssd_norm_w", 1024), ("q_norm_w", 64), ("k_norm_w", 64), ("attn_sinks", 8), ("cm_dw_b", 512),
       ("cm_ln_w", 512), ("cm_ln_b", 512), ("norm_mlp_w", 1024))
WEIGHTS = ("norm_mix_w", "w_in", "ssd_conv_w", "ssd_conv_b", "ssd_dt_bias", "ssd_a_log", "ssd_d", "ssd_norm_w",
           "q_norm_w", "k_norm_w", "attn_sinks", "cm_dw_w", "cm_dw_b", "cm_ln_w", "cm_ln_b", "w_out", "norm_mlp_w",
           "w_mlp_up", "w_mlp_down")
BIG = ("w_in", "w_out", "w_mlp_up", "w_mlp_down")
N_REP = DEPTH * sum(n for _, n in REP)
CONVW_SHARD = SSD_XBC // N_DEV
CMW_SHARD = CM_CHANNELS // N_DEV


def _to_rows(flat, rows):
    return jnp.pad(flat, (0, rows * LANES - flat.shape[0])).reshape(rows, LANES)


def kernel(x, norm_mix_w, w_in, ssd_conv_w, ssd_conv_b, ssd_dt_bias, ssd_a_log, ssd_d, ssd_norm_w, q_norm_w, k_norm_w, attn_sinks, cm_dw_w, cm_dw_b, cm_ln_w, cm_ln_b, w_out, norm_mlp_w, w_mlp_up, w_mlp_down, loss_target, m_norm_mix_w, m_w_in, m_ssd_conv_w, m_ssd_conv_b, m_ssd_dt_bias, m_ssd_a_log, m_ssd_d, m_ssd_norm_w, m_q_norm_w, m_k_norm_w, m_attn_sinks, m_cm_dw_w, m_cm_dw_b, m_cm_ln_w, m_cm_ln_b, m_w_out, m_norm_mlp_w, m_w_mlp_up, m_w_mlp_down, v_norm_mix_w, v_w_in, v_ssd_conv_w, v_ssd_conv_b, v_ssd_dt_bias, v_ssd_a_log, v_ssd_d, v_ssd_norm_w, v_q_norm_w, v_k_norm_w, v_attn_sinks, v_cm_dw_w, v_cm_dw_b, v_cm_ln_w, v_cm_ln_b, v_w_out, v_norm_mlp_w, v_w_mlp_up, v_w_mlp_down):
    w = dict(norm_mix_w=norm_mix_w, w_in=w_in, ssd_conv_w=ssd_conv_w, ssd_conv_b=ssd_conv_b, ssd_dt_bias=ssd_dt_bias, ssd_a_log=ssd_a_log, ssd_d=ssd_d, ssd_norm_w=ssd_norm_w, q_norm_w=q_norm_w, k_norm_w=k_norm_w, attn_sinks=attn_sinks, cm_dw_w=cm_dw_w, cm_dw_b=cm_dw_b, cm_ln_w=cm_ln_w, cm_ln_b=cm_ln_b, w_out=w_out, norm_mlp_w=norm_mlp_w, w_mlp_up=w_mlp_up, w_mlp_down=w_mlp_down)
    m = dict(norm_mix_w=m_norm_mix_w, w_in=m_w_in, ssd_conv_w=m_ssd_conv_w, ssd_conv_b=m_ssd_conv_b, ssd_dt_bias=m_ssd_dt_bias, ssd_a_log=m_ssd_a_log, ssd_d=m_ssd_d, ssd_norm_w=m_ssd_norm_w, q_norm_w=m_q_norm_w, k_norm_w=m_k_norm_w, attn_sinks=m_attn_sinks, cm_dw_w=m_cm_dw_w, cm_dw_b=m_cm_dw_b, cm_ln_w=m_cm_ln_w, cm_ln_b=m_cm_ln_b, w_out=m_w_out, norm_mlp_w=m_norm_mlp_w, w_mlp_up=m_w_mlp_up, w_mlp_down=m_w_mlp_down)
    v = dict(norm_mix_w=v_norm_mix_w, w_in=v_w_in, ssd_conv_w=v_ssd_conv_w, ssd_conv_b=v_ssd_conv_b, ssd_dt_bias=v_ssd_dt_bias, ssd_a_log=v_ssd_a_log, ssd_d=v_ssd_d, ssd_norm_w=v_ssd_norm_w, q_norm_w=v_q_norm_w, k_norm_w=v_k_norm_w, attn_sinks=v_attn_sinks, cm_dw_w=v_cm_dw_w, cm_dw_b=v_cm_dw_b, cm_ln_w=v_cm_ln_w, cm_ln_b=v_cm_ln_b, w_out=v_w_out, norm_mlp_w=v_norm_mlp_w, w_mlp_up=v_w_mlp_up, w_mlp_down=v_w_mlp_down)
    L = x.shape[1]
    xi, yi, ci = _coords()
    me = 4 * xi + 2 * yi + ci
    n_conv = DEPTH * SSD_CONV * CONVW_SHARD
    n_cm = DEPTH * CM_CONV * CMW_SHARD

    conv_rows = 88
    cw8, = _all_gather([_to_rows(jnp.concatenate([ssd_conv_w.reshape(-1), cm_dw_w.reshape(-1)]), conv_rows)], "ag_conv_w")
    cw8 = cw8.reshape(N_DEV, -1)
    conv_full = cw8[:, :n_conv].reshape(N_DEV, DEPTH, SSD_CONV, CONVW_SHARD).transpose(1, 2, 0, 3).reshape(DEPTH, SSD_CONV, SSD_XBC)
    cm_full = cw8[:, n_conv:n_conv + n_cm].reshape(N_DEV, DEPTH, CM_CONV, CMW_SHARD).transpose(1, 2, 0, 3).reshape(DEPTH, CM_CONV, CM_CHANNELS)
    me1 = jnp.reshape(me, (1,)).astype(jnp.int32)
    casts = [_cast_w_in(w_in, me1), _cast_shard(w_out, me1), _cast_shard(w_mlp_up, me1), _cast_shard(w_mlp_down, me1)]
    shards = [[c[l] for c in casts] for l in range(DEPTH)]

    def gather_start(lands, after):
        return _ici_start([], lands, after, "ag_ici_start", "gather")

    def gather_finish(started, after):
        return _ag_d2d(_ici_wait(started, after, "ag_ici_wait", "gather")[1])

    cos, sin = _rope_tables(L)
    p = _stacked_params({k: w[k] for k, _ in REP}, conv_full, cm_full)
    saved = []
    h = x[0]
    first = gather_start(shards[0][:1], cw8)
    rest0 = gather_start(shards[0][1:], first[4])
    w_in8, = gather_finish(first, rest0[4])
    token, rest = rest0[4], None
    late_rest = []
    for i in range(DEPTH):
        nxt, nxt_rest, fwd = None, None, []
        if i + 1 < DEPTH:
            if i == 0:
                nxt = gather_start(shards[1][:1], w_in8)
                nxt_rest = gather_start(shards[1][1:], nxt[4])
                token = nxt_rest[4]
            else:
                nxt = gather_start(shards[i + 1], w_in8)
                token = nxt[4]

        def after_in_proj(z, late_rest=late_rest, i=i):
            if i != 1:
                return z
            lands = _ici_wait(late_rest[0], z, "ag_ici_wait", "gather")[1]
            late_rest.append(_ici_start([], lands, z, "ag_d2d_start", "forward"))
            return late_rest[1][4]

        if i == 0:
            late = lambda ycat: gather_finish(rest0, ycat)
        elif i == 1:
            late = lambda ycat: _ici_wait(late_rest[1], ycat, "ag_d2d_wait", "forward")[1]
        else:
            late = lambda ycat, r=rest: r

        def before_down(act, nxt=nxt, fwd=fwd):
            if nxt is None:
                return act
            lands = _ici_wait(nxt, act, "ag_ici_wait", "gather")[1]
            fwd.append(_ici_start([], lands, act, "ag_d2d_start", "forward"))
            return fwd[0][4]

        h, s = _layer_fwd(h, p, w_in8, after_in_proj, late, before_down, i, cos, sin, token)
        saved.append(s)
        if nxt is not None:
            got = _ici_wait(fwd[0], h, "ag_d2d_wait", "forward")[1]
            w_in8, rest = got[0], got[1:]
        if nxt_rest is not None:
            late_rest.append(nxt_rest)
    d, loss_tile = _loss_head(h, loss_target[0])

    smalls = [None] * DEPTH
    big_out = {k: [lax.empty(w[k].shape, F32) for _ in range(4)] for k in BIG}
    to_t = lambda a: jnp.transpose(a, (2, 0, 1))
    w_in_t = [to_t(t["w_in"]) for t in (w, m, v)]
    big_out["w_in"] = [lax.empty(w_in_t[0].shape, F32) for _ in range(4)]

    core = jnp.reshape(ci, (1,)).astype(jnp.int32)

    def scatter_start(grads, after, l):
        if l > 0:
            lands = [lax.empty(g.shape, g.dtype) for g in grads]
            return _ici_start(list(grads), lands, after, "rs_ici_start", "scatter")
        g4 = [g.reshape((4, 2) + g.shape[1:]) for g in grads]
        out = _rs_add(g4, _rs_sib(g4), core)
        return _ici_start(list(out[:len(g4)]), list(out[len(g4):]), after, "rs_chips_start", "scatter_chips")

    def scatter_finish(started, after, l, names):
        if l > 0:
            srcs, rbs = _ici_wait(started, after, "rs_ici_wait", "scatter")
        else:
            srcs, rbs = [None] * len(names), _ici_wait(started, after, "rs_chips_wait", "scatter_chips")[1]
        for g, rb, k in zip(srcs, rbs, names):
            if k == "w_in":
                big_out[k] = _rs_final_w_in(g, rb, me1, *w_in_t, big_out[k], l)
            else:
                big_out[k] = _rs_final(g, rb, me1, w[k], m[k], v[k], big_out[k], l)

    def gather_small_grads():
        gvec = jnp.concatenate(
            [jnp.stack([smalls[i][k] for i in range(DEPTH)]).reshape(-1) for k, _ in REP]
            + [jnp.stack([smalls[i][k] for i in range(DEPTH)]).reshape(-1) for k in ("ssd_conv_w", "cm_dw_w")]
            + [loss_tile[0, :1]])
        g_rows = -(-gvec.shape[0] // (8 * LANES)) * 8
        return _all_gather([_to_rows(gvec, g_rows)], "ag_small_grads")[0]

    token, pending = loss_tile, []
    for i in reversed(range(DEPTH)):
        dx1, dcat, g_out, g_up, g_down, g_nw_mlp = _layer_bwd_mlp(d, p, i, saved[i], token)
        started = []
        if i == 0:
            started.append((scatter_start([g_out, g_up, g_down], dcat, i), i, BIG[1:]))
        d, g_in, smalls[i] = _layer_bwd_mix(dx1, dcat, g_nw_mlp, p, i, saved[i], cos, sin,
                                            started[0][0][4] if started else g_nw_mlp)
        if i == 0:
            g8 = gather_small_grads()
            started.append((scatter_start([g_in], g8, i), i, BIG[:1]))
        else:
            started.append((scatter_start([g_in, g_out, g_up, g_down], d, i), i, BIG))
        token = started[-1][0][4]
        for st, l, names in pending:
            scatter_finish(st, token, l, names)
        pending = started
    for st, l, names in pending:
        scatter_finish(st, token, l, names)

    gsum = _sum8(g8).reshape(-1)
    o_conv = N_REP
    o_cm = o_conv + DEPTH * SSD_CONV * SSD_XBC
    o_loss = o_cm + DEPTH * CM_CONV * CM_CHANNELS
    g_conv = lax.dynamic_slice_in_dim(gsum[o_conv:o_cm].reshape(DEPTH, SSD_CONV, SSD_XBC), me * CONVW_SHARD, CONVW_SHARD, axis=2)
    g_cm = lax.dynamic_slice_in_dim(gsum[o_cm:o_loss].reshape(DEPTH, CM_CONV, CM_CHANNELS), me * CMW_SHARD, CMW_SHARD, axis=2)
    loss = gsum[o_loss]
    s_rows = -(-(N_REP + n_conv + n_cm) // (8 * LANES)) * 8

    def pack_small(t):
        return _to_rows(jnp.concatenate([t[k].reshape(-1) for k, _ in REP] + [t["ssd_conv_w"].reshape(-1), t["cm_dw_w"].reshape(-1)]), s_rows)

    g_small = _to_rows(jnp.concatenate([gsum[:N_REP], g_conv.reshape(-1), g_cm.reshape(-1)]), s_rows)
    small_out = [g_small] + list(_adamw_small(pack_small(w), g_small, pack_small(m), pack_small(v)))

    def unpack_small(t):
        flat = t.reshape(-1)
        out, off = {}, 0
        for k, n in REP:
            out[k] = flat[off:off + DEPTH * n].reshape(DEPTH, n)
            off += DEPTH * n
        out["ssd_conv_w"] = flat[off:off + n_conv].reshape(DEPTH, SSD_CONV, CONVW_SHARD)
        off += n_conv
        out["cm_dw_w"] = flat[off:off + n_cm].reshape(DEPTH, CM_CONV, CMW_SHARD)
        return out

    outs = [loss, d[None]]
    for j, small_t in enumerate(small_out):
        t = unpack_small(small_t)
        for k in BIG:
            t[k] = big_out[k][j]
        t["w_in"] = jnp.transpose(t["w_in"], (1, 2, 0))
        outs += [t[k] for k in WEIGHTS]
    return tuple(outs)
```

```python
import math

import jax
import jax.numpy as jnp
from jax import lax
from jax.experimental import pallas as pl
from jax.experimental.pallas import tpu as pltpu

F32 = jnp.float32
_MM = jnp.bfloat16

D_MODEL = 1024
DEPTH = 4
SSD_WIDTH = 1024
SSD_HEADS = 16
SSD_STATE = 128
SSD_GROUPS = 2
SSD_CONV = 4
SSD_XBC = 1536
Q = 128
ATTN_HEAD_DIM = 64
ATTN_Q_HEADS = 8
CM_CHANNELS = 512
CM_CONV = 31
D_FF = 4096
D_MIX = 2048
N_IN = 4368
RMS_EPS = 1e-6
LN_EPS = 1e-5
ROPE_THETA = 10000.0
ADAM_LR = 0.001
ADAM_B1 = 0.9
ADAM_B2 = 0.999
ADAM_EPS = 1e-08
ADAM_WD = 0.01
ADAM_STEP = 10

N_DEV = 8
LANES = 128
TM = 256
N_IN_P = 4480
U_Z, U_XBC, U_QKV, U_GLU, U_DT = (0, 1024), (1024, 2560), (2560, 3328), (3328, 4352), (4352, 4480)
W_IN_SHARD = N_IN // N_DEV
W_IN_SHARD_P = 640
FF_SHARD = D_FF // N_DEV
OUT_SHARD = D_MIX // N_DEV

_NN = (((1,), (0,)), ((), ()))
_NT = (((1,), (1,)), ((), ()))
_TN = (((0,), (0,)), ((), ()))
_VMEM_LIMIT = 56 * 1024 * 1024


def _mm(a, b, dims=_NN):
    return lax.dot_general(a.astype(_MM), b.astype(_MM), dims, preferred_element_type=F32)


def _mmx(a, b, dims=_NN, exact="b"):
    m, v = (b, a) if exact == "b" else (a, b)
    m = m.astype(jnp.bfloat16)
    acc = None
    for _ in range(3):
        p = v.astype(jnp.bfloat16)
        v = v - p.astype(F32)
        t = lax.dot_general(p, m, dims, preferred_element_type=F32) if exact == "b" else \
            lax.dot_general(m, p, dims, preferred_element_type=F32)
        acc = t if acc is None else acc + t
    return acc


def _sds(shape, dtype=F32):
    return jax.ShapeDtypeStruct(tuple(shape), dtype)


def _full(shape):
    nd = len(shape)
    return pl.BlockSpec(tuple(shape), lambda *_: (0,) * nd)


def _rows(cols, tm=TM, col=0):
    return pl.BlockSpec((tm, cols), lambda i: (i, col))


TMM = 512


def _mrows(cols):
    return _rows(cols, TMM)


def _lp(n, i):
    return pl.BlockSpec((1, 1, n), lambda *_: (i, 0, 0))


def _lw(arr):
    return pl.BlockSpec(arr.shape, lambda *_: (0, 0, 0, 0))


_ANY = pl.BlockSpec(memory_space=pl.ANY)


def _cparams(sem=None):
    return pltpu.CompilerParams(dimension_semantics=sem, vmem_limit_bytes=_VMEM_LIMIT)


def _sigmoid(x):
    return 1.0 / (1.0 + jnp.exp(-x))


def _silu(x):
    return x * _sigmoid(x)


def _dsilu(x):
    s = _sigmoid(x)
    return s * (1.0 + x * (1.0 - s))


def _rms_bwd(dy, x, w, inv_n):
    r = lax.rsqrt(jnp.sum(x * x, axis=-1, keepdims=True) * inv_n + RMS_EPS)
    xh = x * r
    dxh = dy * w
    dx = r * (dxh - xh * (jnp.sum(dxh * xh, axis=-1, keepdims=True) * inv_n))
    return dx, dy * xh


def _cast_shard(w, me, cols_p=None):
    _, R, C = w.shape
    cp = C if cols_p is None else cols_p

    def body(me_ref, w_ref, *o_refs):
        v = w_ref[0]
        if cp != C:
            v = jnp.concatenate([v, jnp.zeros((R, cp - C), F32)], axis=1)
        for k in range(DEPTH):
            @pl.when(pl.program_id(0) == k)
            def _():
                o_refs[k][0, 0] = v.astype(_MM)

    return pl.pallas_call(
        body, name="cast_shard", out_shape=[_sds((N_DEV, 1, R, cp), _MM)] * DEPTH,
        grid_spec=pltpu.PrefetchScalarGridSpec(
            num_scalar_prefetch=1, grid=(DEPTH,),
            in_specs=[pl.BlockSpec((1, R, C), lambda l, me: (l, 0, 0))],
            out_specs=[pl.BlockSpec((1, 1, R, cp), lambda l, me: (me[0], 0, 0, 0))] * DEPTH),
        compiler_params=_cparams(("arbitrary",)),
    )(me, w)


def _cast_w_in(w_in, me):
    wt = jnp.transpose(w_in, (2, 0, 1))

    def body(me_ref, wt_ref, *rest):
        o_refs, buf, sem = rest[:DEPTH], rest[DEPTH], rest[DEPTH + 1]
        l = pl.program_id(0)
        cp = pltpu.make_async_copy(wt_ref.at[:, l, :], buf, sem)
        cp.start()
        cp.wait()
        v = jnp.concatenate([buf[...], jnp.zeros((W_IN_SHARD_P - W_IN_SHARD, D_MODEL), F32)], axis=0).T.astype(_MM)
        for k in range(DEPTH):
            @pl.when(l == k)
            def _():
                o_refs[k][0, 0] = v

    return pl.pallas_call(
        body, name="cast_w_in", out_shape=[_sds((N_DEV, 1, D_MODEL, W_IN_SHARD_P), _MM)] * DEPTH,
        grid_spec=pltpu.PrefetchScalarGridSpec(
            num_scalar_prefetch=1, grid=(DEPTH,), in_specs=[_ANY],
            out_specs=[pl.BlockSpec((1, 1, D_MODEL, W_IN_SHARD_P), lambda l, me: (me[0], 0, 0, 0))] * DEPTH,
            scratch_shapes=[pltpu.VMEM((W_IN_SHARD, D_MODEL), F32), pltpu.SemaphoreType.DMA]),
        compiler_params=_cparams(("arbitrary",)),
    )(me, wt)


def _w_in_regroup(w8, after):
    a, b = U_XBC[1], U_XBC[1] + SSD_HEADS

    def body(w_ref, after_ref, o_ref):
        w = jnp.concatenate([w_ref[j, 0][:, :W_IN_SHARD].astype(F32) for j in range(N_DEV)], axis=1)
        r = jnp.concatenate([w[:, :a], w[:, b:], w[:, a:b], jnp.zeros((TM, N_IN_P - N_IN), F32)], axis=1)
        o_ref[...] = r.astype(_MM)

    return pl.pallas_call(
        body, name="w_in_regroup", grid=(D_MODEL // TM,),
        in_specs=[pl.BlockSpec((N_DEV, 1, TM, W_IN_SHARD_P), lambda r: (0, 0, r, 0)), _ANY],
        out_specs=_rows(N_IN_P), out_shape=_sds((D_MODEL, N_IN_P), _MM),
        compiler_params=_cparams(("parallel",)),
    )(w8, after)


def _g_in_split(g):
    a = U_XBC[1]

    def body(g_ref, o_ref):
        v = g_ref[...].astype(F32)
        w = jnp.concatenate([v[:, :a], v[:, U_DT[0]:U_DT[0] + SSD_HEADS], v[:, a:U_DT[0]]], axis=1)
        pad = jnp.zeros((TM, W_IN_SHARD_P - W_IN_SHARD), F32)
        for j in range(N_DEV):
            o_ref[j] = jnp.concatenate([w[:, j * W_IN_SHARD:(j + 1) * W_IN_SHARD], pad], axis=1).astype(_MM)

    return pl.pallas_call(
        body, name="g_in_split", grid=(D_MODEL // TM,),
        in_specs=[_rows(N_IN_P)],
        out_specs=pl.BlockSpec((N_DEV, TM, W_IN_SHARD_P), lambda r: (0, r, 0)),
        out_shape=_sds((N_DEV, D_MODEL, W_IN_SHARD_P), _MM),
        compiler_params=_cparams(("parallel",)),
    )(g)


def _in_proj(x, nw, i, w):
    L = x.shape[0]
    splits = (U_Z, U_XBC, U_QKV, U_GLU, U_DT)

    def body(x_ref, nw_ref, w_ref, h_ref, *out_refs):
        xf = x_ref[...]
        r = lax.rsqrt(jnp.mean(xf * xf, axis=-1, keepdims=True) + RMS_EPS)
        h = (xf * r * nw_ref[0]).astype(_MM)
        h_ref[...] = h
        for ref, (a, b) in zip(out_refs, splits):
            ref[...] = lax.dot_general(h, w_ref[:, a:b], _NN, preferred_element_type=F32)

    return pl.pallas_call(
        body, name="in_proj", grid=(L // TMM,),
        in_specs=[_mrows(D_MODEL), _lp(D_MODEL, i), _full(w.shape)],
        out_specs=[_mrows(D_MODEL)] + [_mrows(b - a) for a, b in splits],
        out_shape=[_sds((L, D_MODEL), _MM)] + [_sds((L, b - a)) for a, b in splits],
        compiler_params=_cparams(("parallel",)),
    )(x, nw, w)


def _mlp_up(x, nw, i, w8):
    L = x.shape[0]

    def body(x_ref, nw_ref, w_ref, h_ref, up_ref, act_ref):
        xf = x_ref[...]
        r = lax.rsqrt(jnp.mean(xf * xf, axis=-1, keepdims=True) + RMS_EPS)
        h = (xf * r * nw_ref[0]).astype(_MM)
        h_ref[...] = h
        for j in range(N_DEV):
            sl = slice(j * FF_SHARD, (j + 1) * FF_SHARD)
            up = lax.dot_general(h, w_ref[j, 0], _NN, preferred_element_type=F32)
            up_ref[:, sl] = up
            act_ref[:, sl] = jnp.square(jnp.maximum(up, 0.0)).astype(_MM)

    return pl.pallas_call(
        body, name="mlp_up", grid=(L // TM,),
        in_specs=[_rows(D_MODEL), _lp(D_MODEL, i), _lw(w8)],
        out_specs=[_rows(D_MODEL), _rows(D_FF), _rows(D_FF)],
        out_shape=[_sds((L, D_MODEL), _MM), _sds((L, D_FF)), _sds((L, D_FF), _MM)],
        compiler_params=_cparams(("parallel",)),
    )(x, nw, w8)


def _mm_res(a, w8, i, res, name, after):
    L, K = a.shape
    N = w8.shape[3]

    def body(a_ref, w_ref, res_ref, after_ref, o_ref):
        w = w_ref[:, 0].reshape(K, N)
        o_ref[...] = res_ref[...] + lax.dot_general(a_ref[...], w, _NN, preferred_element_type=F32)

    return pl.pallas_call(
        body, name=name, grid=(L // TMM,),
        in_specs=[_mrows(K), _lw(w8), _mrows(N), _ANY],
        out_specs=_mrows(N), out_shape=_sds((L, N)),
        compiler_params=_cparams(("parallel",)),
    )(a, w8, res, after)


def _out_proj_bwd(a, w8, i, after):
    L = a.shape[0]

    def body(a_ref, w_ref, after_ref, o_ref):
        w = w_ref[:, 0].reshape(D_MIX, D_MODEL)
        o_ref[...] = lax.dot_general(a_ref[...].astype(_MM), w, _NT, preferred_element_type=F32)

    return pl.pallas_call(
        body, name="out_proj_bwd", grid=(L // TMM,),
        in_specs=[_mrows(D_MODEL), _lw(w8), _ANY],
        out_specs=_mrows(D_MIX), out_shape=_sds((L, D_MIX)),
        compiler_params=_cparams(("parallel",)),
    )(a, w8, after)


def _mlp_down_bwd(dy, w8, i, up, after):
    L = dy.shape[0]

    def body(dy_ref, w_ref, up_ref, after_ref, o_ref):
        d = dy_ref[...].astype(_MM)
        for j in range(N_DEV):
            sl = slice(j * FF_SHARD, (j + 1) * FF_SHARD)
            da = lax.dot_general(d, w_ref[j, 0], _NT, preferred_element_type=F32)
            o_ref[:, sl] = (da * (2.0 * jnp.maximum(up_ref[:, sl], 0.0))).astype(_MM)

    return pl.pallas_call(
        body, name="mlp_down_bwd", grid=(L // TMM,),
        in_specs=[_mrows(D_MODEL), _lw(w8), _mrows(D_FF), _ANY],
        out_specs=_mrows(D_FF), out_shape=_sds((L, D_FF), _MM),
        compiler_params=_cparams(("parallel",)),
    )(dy, w8, up, after)


def _rms_bwd_epilogue(dh, res_ref, x_ref, nw_ref, dx_ref, dnw_ref):
    dx, dwx = _rms_bwd(dh, x_ref[...], nw_ref[0], 1.0 / D_MODEL)
    dx_ref[...] = res_ref[...] + dx

    @pl.when(pl.program_id(0) == 0)
    def _():
        dnw_ref[...] = jnp.zeros_like(dnw_ref)

    dnw_ref[...] += jnp.sum(dwx, axis=0, keepdims=True)


def _mlp_up_bwd(d_up, w8, i, res, x, nw):
    L = d_up.shape[0]

    def body(a_ref, w_ref, res_ref, x_ref, nw_ref, dx_ref, dnw_ref):
        dh = jnp.zeros((TMM, D_MODEL), F32)
        for j in range(N_DEV):
            dh = dh + lax.dot_general(a_ref[:, j * FF_SHARD:(j + 1) * FF_SHARD], w_ref[j, 0], _NT, preferred_element_type=F32)
        _rms_bwd_epilogue(dh, res_ref, x_ref, nw_ref, dx_ref, dnw_ref)

    return pl.pallas_call(
        body, name="mlp_up_bwd", grid=(L // TMM,),
        in_specs=[_mrows(D_FF), _lw(w8), _mrows(D_MODEL), _mrows(D_MODEL), _lp(D_MODEL, i)],
        out_specs=[_mrows(D_MODEL), _full((1, D_MODEL))],
        out_shape=[_sds((L, D_MODEL)), _sds((1, D_MODEL))],
        compiler_params=_cparams(("arbitrary",)),
    )(d_up, w8, res, x, nw)


def _in_proj_bwd(pieces, w, res, x, nw, i):
    L = pieces[0].shape[0]
    n = len(pieces)

    def body(*refs):
        w_ref, res_ref, x_ref, nw_ref, dx_ref, dnw_ref, du_ref = refs[n:]
        off = 0
        for r in refs[:n]:
            du_ref[:, off:off + r.shape[1]] = r[...].astype(_MM)
            off += r.shape[1]
        dh = lax.dot_general(du_ref[...], w_ref[...], _NT, preferred_element_type=F32)
        _rms_bwd_epilogue(dh, res_ref, x_ref, nw_ref, dx_ref, dnw_ref)

    return pl.pallas_call(
        body, name="in_proj_bwd", grid=(L // TM,),
        in_specs=[_rows(q.shape[1]) for q in pieces] + [_full(w.shape), _rows(D_MODEL), _rows(D_MODEL), _lp(D_MODEL, i)],
        out_specs=[_rows(D_MODEL), _full((1, D_MODEL)), _rows(N_IN_P)],
        out_shape=[_sds((L, D_MODEL)), _sds((1, D_MODEL)), _sds((L, N_IN_P), _MM)],
        compiler_params=_cparams(("arbitrary",)),
    )(*pieces, w, res, x, nw)


def _mm_tn(a, g, name, tk, tn, out):
    L, K = a.shape
    N = g.shape[1]

    def body(a_ref, g_ref, o_ref):
        r = lax.dot_general(a_ref[...].astype(_MM), g_ref[...].astype(_MM), _TN, preferred_element_type=F32)
        o_ref[...] = r.astype(o_ref.dtype).reshape(o_ref.shape)

    if out == "flat":
        out_spec, out_shape = pl.BlockSpec((tk, tn), lambda i, j: (i, j)), _sds((K, N), _MM)
    elif out == "rows":
        assert tn == N and tk % (K // N_DEV) == 0
        nblk = tk // (K // N_DEV)
        out_spec, out_shape = pl.BlockSpec((nblk, K // N_DEV, N), lambda i, j: (i, 0, 0)), _sds((N_DEV, K // N_DEV, N), _MM)
    else:
        assert tk == K and tn == N // N_DEV
        out_spec, out_shape = pl.BlockSpec((1, K, tn), lambda i, j: (j, 0, 0)), _sds((N_DEV, K, tn), _MM)
    return pl.pallas_call(
        body, name=name, grid=(K // tk, N // tn),
        in_specs=[pl.BlockSpec((L, tk), lambda i, j: (0, i)), pl.BlockSpec((L, tn), lambda i, j: (0, j))],
        out_specs=out_spec, out_shape=out_shape,
        compiler_params=_cparams(("parallel", "parallel")),
    )(a, g)


def _loss_head(y, t):
    L = y.shape[0]

    def body(y_ref, t_ref, dy_ref, l_ref):
        e = y_ref[...] - t_ref[...]
        dy_ref[...] = e * (1.0 / D_MODEL)

        @pl.when(pl.program_id(0) == 0)
        def _():
            l_ref[...] = jnp.zeros_like(l_ref)

        l_ref[...] += jnp.sum(jnp.sum(e * e, axis=1, keepdims=True), axis=0, keepdims=True) * (0.5 / D_MODEL)

    return pl.pallas_call(
        body, name="loss_head", grid=(L // TM,),
        in_specs=[_rows(D_MODEL), _rows(D_MODEL)],
        out_specs=[_rows(D_MODEL), _full((8, LANES))],
        out_shape=[_sds((L, D_MODEL)), _sds((8, LANES))],
        compiler_params=_cparams(("arbitrary",)),
    )(y, t)


EDGE = 32


def _roll_rows(x, s):
    s = s % x.shape[0]
    return x if s == 0 else pltpu.roll(x, s, axis=0)


class _Rolls:
    def __init__(self, x):
        self.x, self.by_phase = x, {}

    def __call__(self, s):
        s = s % self.x.shape[0]
        b = s % 8
        if b not in self.by_phase:
            self.by_phase[b] = _roll_rows(self.x, b)
        return _roll_rows(self.by_phase[b], s - b)


def _conv_taps(x, w_ref, b, k_w):
    def taps(v, zero_fill):
        r = lax.broadcasted_iota(jnp.int32, v.shape, 0)
        acc = jnp.broadcast_to(b, v.shape)
        rolled = _Rolls(v)
        for k in range(k_w):
            s = k_w - 1 - k
            sh = rolled(s)
            if zero_fill and s:
                sh = jnp.where(r >= s, sh, 0.0)
            acc = acc + w_ref[0, k:k + 1, :] * sh
        return acc

    return jnp.concatenate([taps(x[:EDGE], True), taps(x, False)[EDGE:]], axis=0)


def _conv_bwd_taps(x, dc, w_ref, dw_ref, db_ref, k_w):
    n = x.shape[0]
    dc_tail, x_tail, dc_head = dc[n - EDGE:], x[n - EDGE:], dc[:EDGE]
    r = lax.broadcasted_iota(jnp.int32, dc_head.shape, 0)
    dx = jnp.zeros_like(x)
    dx_tail = jnp.zeros_like(dc_tail)
    dc_rolled, x_rolled = _Rolls(dc), _Rolls(x)
    for k in range(k_w):
        s = k_w - 1 - k
        wk = w_ref[0, k:k + 1, :]
        dx = dx + wk * dc_rolled(n - s)
        up = _roll_rows(dc_tail, EDGE - s)
        dx_tail = dx_tail + wk * (jnp.where(r < EDGE - s, up, 0.0) if s else up)
        dw = jnp.sum(dc * x_rolled(s), axis=0, keepdims=True)
        if s:
            dw = dw - jnp.sum(jnp.where(r < s, dc_head * _roll_rows(x_tail, s), 0.0), axis=0, keepdims=True)
        dw_ref[k:k + 1, :] = dw
    db_ref[...] = jnp.sum(dc, axis=0, keepdims=True)
    return jnp.concatenate([dx[:n - EDGE], dx_tail], axis=0)


def _cols(L, cb, off=0):
    return pl.BlockSpec((L, cb), lambda j: (0, j + off))


def _lcols(k, cb, i):
    return pl.BlockSpec((1, k, cb), lambda j: (i, 0, j))


SSD_CB = 256


def _ssd_conv_fwd(x, w, b, i, after):
    L, C = x.shape
    cb = SSD_CB

    def body(x_ref, w_ref, b_ref, after_ref, o_ref):
        o_ref[...] = _silu(_conv_taps(x_ref[...], w_ref, b_ref[0], SSD_CONV))

    return pl.pallas_call(
        body, name="ssd_conv_fwd", grid=(C // cb,),
        in_specs=[_cols(L, cb), _lcols(SSD_CONV, cb, i), _lcols(1, cb, i), _ANY],
        out_specs=_cols(L, cb), out_shape=_sds((L, C)),
        compiler_params=_cparams(("parallel",)),
    )(x, w, b, after)


def _ssd_conv_bwd(x, w, b, i, dy):
    L, C = x.shape
    cb = SSD_CB

    def body(x_ref, w_ref, b_ref, dy_ref, dx_ref, dw_ref, db_ref):
        x_ = x_ref[...]
        c = _conv_taps(x_, w_ref, b_ref[0], SSD_CONV)
        dc = dy_ref[...] * _dsilu(c)
        dx_ref[...] = _conv_bwd_taps(x_, dc, w_ref, dw_ref, db_ref, SSD_CONV).astype(dx_ref.dtype)

    return pl.pallas_call(
        body, name="ssd_conv_bwd", grid=(C // cb,),
        in_specs=[_cols(L, cb), _lcols(SSD_CONV, cb, i), _lcols(1, cb, i), _cols(L, cb)],
        out_specs=[_cols(L, cb), _cols(SSD_CONV, cb), _cols(1, cb)],
        out_shape=[_sds((L, C), _MM), _sds((SSD_CONV, C)), _sds((1, C))],
        compiler_params=_cparams(("parallel",)),
    )(x, w, b, dy)


def _cm_conv_fwd(glu, w, b, i):
    L = glu.shape[0]
    cb = LANES
    nb = CM_CHANNELS // cb

    def body(a_ref, g_ref, w_ref, b_ref, o_ref):
        h = a_ref[...] * _sigmoid(g_ref[...])
        o_ref[...] = _conv_taps(h, w_ref, b_ref[0], CM_CONV)

    return pl.pallas_call(
        body, name="cm_conv_fwd", grid=(nb,),
        in_specs=[_cols(L, cb), _cols(L, cb, nb), _lcols(CM_CONV, cb, i), _lcols(1, cb, i)],
        out_specs=_cols(L, cb), out_shape=_sds((L, CM_CHANNELS)),
        compiler_params=_cparams(("parallel",)),
    )(glu, glu, w, b)


def _cm_conv_bwd(glu, w, i, dc):
    L = glu.shape[0]
    cb = LANES
    nb = CM_CHANNELS // cb

    def body(a_ref, g_ref, w_ref, dc_ref, da_ref, dg_ref, dw_ref, db_ref):
        a = a_ref[...]
        sg = _sigmoid(g_ref[...])
        dh = _conv_bwd_taps(a * sg, dc_ref[...], w_ref, dw_ref, db_ref, CM_CONV)
        da_ref[...] = (dh * sg).astype(da_ref.dtype)
        dg_ref[...] = (dh * a * sg * (1.0 - sg)).astype(dg_ref.dtype)

    return pl.pallas_call(
        body, name="cm_conv_bwd", grid=(nb,),
        in_specs=[_cols(L, cb), _cols(L, cb, nb), _lcols(CM_CONV, cb, i), _cols(L, cb)],
        out_specs=[_cols(L, cb), _cols(L, cb), _cols(CM_CONV, cb), _cols(1, cb)],
        out_shape=[_sds((L, CM_CHANNELS), _MM), _sds((L, CM_CHANNELS), _MM), _sds((CM_CONV, CM_CHANNELS)), _sds((1, CM_CHANNELS))],
        compiler_params=_cparams(("parallel",)),
    )(glu, glu, w, dc)


GRP = SSD_WIDTH // SSD_GROUPS


def _mix_post(y, z, attn, c, snw, lw, lb, i):
    L = y.shape[0]

    def body(y_ref, z_ref, a_ref, c_ref, snw_ref, lw_ref, lb_ref, o_ref):
        g = y_ref[...] * _silu(z_ref[...])
        for k in range(SSD_GROUPS):
            sl = slice(k * GRP, (k + 1) * GRP)
            gg = g[:, sl]
            r = lax.rsqrt(jnp.mean(gg * gg, axis=-1, keepdims=True) + RMS_EPS)
            o_ref[:, sl] = (gg * r * snw_ref[0, :, sl]).astype(_MM)
        o_ref[:, SSD_WIDTH:SSD_WIDTH + 512] = a_ref[...].astype(_MM)
        cv = c_ref[...]
        mu = jnp.mean(cv, axis=-1, keepdims=True)
        xc = cv - mu
        rs = lax.rsqrt(jnp.mean(xc * xc, axis=-1, keepdims=True) + LN_EPS)
        o_ref[:, SSD_WIDTH + 512:] = _silu(xc * rs * lw_ref[0] + lb_ref[0]).astype(_MM)

    return pl.pallas_call(
        body, name="mix_post", grid=(L // TM,),
        in_specs=[_rows(SSD_WIDTH), _rows(SSD_WIDTH), _rows(512), _rows(512),
                  _lp(SSD_WIDTH, i), _lp(512, i), _lp(512, i)],
        out_specs=_rows(D_MIX), out_shape=_sds((L, D_MIX), _MM),
        compiler_params=_cparams(("parallel",)),
    )(y, z, attn, c, snw, lw, lb)


def _mix_post_bwd(dcat, y, z, c, snw, lw, lb, i, after):
    L = y.shape[0]

    def body(d_ref, y_ref, z_ref, c_ref, snw_ref, lw_ref, lb_ref, after_ref,
             dy_ref, dz_ref, dc_ref, dsnw_ref, dlw_ref, dlb_ref):
        @pl.when(pl.program_id(0) == 0)
        def _():
            dsnw_ref[...] = jnp.zeros_like(dsnw_ref)
            dlw_ref[...] = jnp.zeros_like(dlw_ref)
            dlb_ref[...] = jnp.zeros_like(dlb_ref)

        yv = y_ref[...]
        zv = z_ref[...]
        sz = _silu(zv)
        g = yv * sz
        for k in range(SSD_GROUPS):
            sl = slice(k * GRP, (k + 1) * GRP)
            dgg, dwx = _rms_bwd(d_ref[:, sl], g[:, sl], snw_ref[0, :, sl], 1.0 / GRP)
            dsnw_ref[:, sl] += jnp.sum(dwx, axis=0, keepdims=True)
            dy_ref[:, sl] = dgg * sz[:, sl]
            dz_ref[:, sl] = (dgg * yv[:, sl] * _dsilu(zv[:, sl])).astype(dz_ref.dtype)
        cv = c_ref[...]
        mu = jnp.mean(cv, axis=-1, keepdims=True)
        xc = cv - mu
        rs = lax.rsqrt(jnp.mean(xc * xc, axis=-1, keepdims=True) + LN_EPS)
        xh = xc * rs
        ln = xh * lw_ref[0] + lb_ref[0]
        dln = d_ref[:, SSD_WIDTH + 512:] * _dsilu(ln)
        dlb_ref[...] += jnp.sum(dln, axis=0, keepdims=True)
        dlw_ref[...] += jnp.sum(dln * xh, axis=0, keepdims=True)
        dxh = dln * lw_ref[0]
        dc_ref[...] = rs * (dxh - jnp.mean(dxh, axis=-1, keepdims=True)
                            - xh * jnp.mean(dxh * xh, axis=-1, keepdims=True))

    return pl.pallas_call(
        body, name="mix_post_bwd", grid=(L // TM,),
        in_specs=[_rows(D_MIX), _rows(SSD_WIDTH), _rows(SSD_WIDTH), _rows(512),
                  _lp(SSD_WIDTH, i), _lp(512, i), _lp(512, i), _ANY],
        out_specs=[_rows(SSD_WIDTH), _rows(SSD_WIDTH), _rows(512), _full((1, SSD_WIDTH)), _full((1, 512)), _full((1, 512))],
        out_shape=[_sds((L, SSD_WIDTH)), _sds((L, SSD_WIDTH), _MM), _sds((L, 512)), _sds((1, SSD_WIDTH)), _sds((1, 512)), _sds((1, 512))],
        compiler_params=_cparams(("arbitrary",)),
    )(dcat, y, z, c, snw, lw, lb, after)


def _seg_mean_matrix():
    i = lax.broadcasted_iota(jnp.int32, (LANES, LANES), 0)
    j = lax.broadcasted_iota(jnp.int32, (LANES, LANES), 1)
    return jnp.where(i // ATTN_HEAD_DIM == j // ATTN_HEAD_DIM, 1.0 / ATTN_HEAD_DIM, 0.0).astype(F32)


def _rot_matrix():
    i = lax.broadcasted_iota(jnp.int32, (LANES, LANES), 0)
    j = lax.broadcasted_iota(jnp.int32, (LANES, LANES), 1)
    half = ATTN_HEAD_DIM // 2
    lo = (j % ATTN_HEAD_DIM) < half
    return jnp.where(lo & (i == j + half), -1.0, jnp.where((~lo) & (i == j - half), 1.0, 0.0)).astype(F32)


N_QK_TILES = 5
QK_W = N_QK_TILES * LANES


def _qk_prep(qkv, qw, kw, i, cos, sin):
    L = qkv.shape[0]

    def body(x_ref, qw_ref, kw_ref, c_ref, s_ref, o_ref):
        m64 = _seg_mean_matrix()
        rot = _rot_matrix()
        cs, sn = c_ref[...], s_ref[...]
        for t in range(N_QK_TILES):
            sl = slice(t * LANES, (t + 1) * LANES)
            x = x_ref[:, sl]
            w = qw_ref[0] if t < 4 else kw_ref[0]
            xn = x * lax.rsqrt(_mmx(x * x, m64) + RMS_EPS) * w
            o_ref[:, sl] = xn * cs + _mmx(xn, rot) * sn

    return pl.pallas_call(
        body, name="qk_prep", grid=(L // TM,),
        in_specs=[_rows(QK_W), _lp(LANES, i), _lp(LANES, i), _rows(LANES), _rows(LANES)],
        out_specs=_rows(QK_W), out_shape=_sds((L, QK_W)),
        compiler_params=_cparams(("parallel",)),
    )(qkv, qw, kw, cos, sin)


def _qk_prep_bwd(dq, dk, qkv, qw, kw, i, cos, sin):
    L = qkv.shape[0]

    def body(dq_ref, dk_ref, x_ref, qw_ref, kw_ref, c_ref, s_ref, dx_ref, dqw_ref, dkw_ref):
        @pl.when(pl.program_id(0) == 0)
        def _():
            dqw_ref[...] = jnp.zeros_like(dqw_ref)
            dkw_ref[...] = jnp.zeros_like(dkw_ref)

        m64 = _seg_mean_matrix()
        rot = _rot_matrix()
        cs, sn = c_ref[...], s_ref[...]
        for t in range(N_QK_TILES):
            sl = slice(t * LANES, (t + 1) * LANES)
            x = x_ref[:, sl]
            dy = dq_ref[:, sl] if t < 4 else dk_ref[...]
            w = qw_ref[0] if t < 4 else kw_ref[0]
            dxn = dy * cs - _mmx(dy * sn, rot)
            r = lax.rsqrt(_mmx(x * x, m64) + RMS_EPS)
            xh = x * r
            dxh = dxn * w
            dx_ref[:, sl] = (r * (dxh - xh * _mmx(dxh * xh, m64))).astype(dx_ref.dtype)
            dw = jnp.sum(dxn * xh, axis=0, keepdims=True)
            if t < 4:
                dqw_ref[...] += dw
            else:
                dkw_ref[...] += dw

    return pl.pallas_call(
        body, name="qk_prep_bwd", grid=(L // TM,),
        in_specs=[_rows(512), _rows(LANES), _rows(QK_W), _lp(LANES, i), _lp(LANES, i), _rows(LANES), _rows(LANES)],
        out_specs=[_rows(QK_W), _full((1, LANES)), _full((1, LANES))],
        out_shape=[_sds((L, QK_W), _MM), _sds((1, LANES)), _sds((1, LANES))],
        compiler_params=_cparams(("arbitrary",)),
    )(dq, dk, qkv, qw, kw, cos, sin)


HPG = 4
SCALE = 1.0 / math.sqrt(ATTN_HEAD_DIM)


def _heads_to_rows(q, g):
    return jnp.concatenate([q[:, (HPG * g + r) * ATTN_HEAD_DIM:(HPG * g + r + 1) * ATTN_HEAD_DIM] for r in range(HPG)], axis=0)


def _rows_to_heads(parts):
    return jnp.concatenate([p[r * Q:(r + 1) * Q] for p in parts for r in range(HPG)], axis=1)


def _attn_probs(q, k_own, k_prev, n, sink_ref, base):
    s_own = _mm(q, k_own, _NT) * SCALE
    s_prev = _mm(q, k_prev, _NT) * SCALE
    own = lax.broadcasted_iota(jnp.int32, s_own.shape, 1) <= lax.broadcasted_iota(jnp.int32, s_own.shape, 0) % Q
    s = jnp.where(own, s_own, jnp.where(n >= 1, s_prev, -jnp.inf))
    hrow = lax.broadcasted_iota(jnp.int32, (HPG * Q, 1), 0) // Q
    sink = jnp.zeros((HPG * Q, 1), F32)
    for r in range(HPG):
        sink = jnp.where(hrow == r, sink_ref[base + r], sink)
    m = jnp.maximum(jnp.max(s, axis=1, keepdims=True), sink)
    p = jnp.exp(s - m)
    es = jnp.exp(sink - m)
    inv = 1.0 / (jnp.sum(p, axis=1, keepdims=True) + es)
    return p * inv, own, es * inv


def _kv_blocks(ref, n):
    own = ref[pl.ds(pl.multiple_of(n * Q, Q), Q), :]
    prev = ref[pl.ds(pl.multiple_of(jnp.maximum(n - 1, 0) * Q, Q), Q), :]
    return own, prev


def _attn_fwd(qk, qkv, sinks, i):
    L = qk.shape[0]

    def body(sink_ref, q_ref, k_ref, v_ref, o_ref):
        n = pl.program_id(0)
        q = q_ref[...]
        k_own, k_prev = _kv_blocks(k_ref, n)
        v_own, v_prev = _kv_blocks(v_ref, n)
        outs = []
        for g in range(2):
            sl = slice(g * ATTN_HEAD_DIM, (g + 1) * ATTN_HEAD_DIM)
            p, own, _ = _attn_probs(_heads_to_rows(q, g), k_own[:, sl], k_prev[:, sl], n, sink_ref, i * ATTN_Q_HEADS + g * HPG)
            outs.append(_mm(jnp.where(own, p, 0.0), v_own[:, sl]) + _mm(jnp.where(own, 0.0, p), v_prev[:, sl]))
        o_ref[...] = _rows_to_heads(outs)

    return pl.pallas_call(
        body, name="attn_fwd", grid=(L // Q,),
        in_specs=[pl.BlockSpec(memory_space=pltpu.SMEM), _rows(512, Q),
                  pl.BlockSpec((L, LANES), lambda n: (0, 4)), pl.BlockSpec((L, LANES), lambda n: (0, 5))],
        out_specs=_rows(512, Q), out_shape=_sds((L, 512)),
        compiler_params=_cparams(("parallel",)),
    )(sinks, qk, qk, qkv)


def _attn_bwd(qk, qkv, sinks, i, dcat):
    L = qk.shape[0]

    def body(sink_ref, q_ref, k_ref, v_ref, do_ref, dq_ref, dk_ref, dv_ref, ds_ref):
        n = pl.program_id(0)

        @pl.when(n == 0)
        def _():
            dk_ref[...] = jnp.zeros_like(dk_ref)
            dv_ref[...] = jnp.zeros_like(dv_ref)
            ds_ref[...] = jnp.zeros_like(ds_ref)

        q = q_ref[...]
        do_all = do_ref[...]
        k_own, k_prev = _kv_blocks(k_ref, n)
        v_own, v_prev = _kv_blocks(v_ref, n)
        hrow = lax.broadcasted_iota(jnp.int32, (HPG * Q, 1), 0) // Q
        orow = lax.broadcasted_iota(jnp.int32, (8, LANES), 0)
        dqs, dks, dvs = [], [[], []], [[], []]
        acc = jnp.zeros((8, LANES), F32)
        for g in range(2):
            sl = slice(g * ATTN_HEAD_DIM, (g + 1) * ATTN_HEAD_DIM)
            qg = _heads_to_rows(q, g)
            do = _heads_to_rows(do_all, g)
            p, own, ps = _attn_probs(qg, k_own[:, sl], k_prev[:, sl], n, sink_ref, i * ATTN_Q_HEADS + g * HPG)
            dp = jnp.where(own, _mm(do, v_own[:, sl], _NT), _mm(do, v_prev[:, sl], _NT))
            delta = jnp.sum(p * dp, axis=1, keepdims=True)
            ds = p * (dp - delta)
            parts = ((jnp.where(own, ds, 0.0), jnp.where(own, p, 0.0)), (jnp.where(own, 0.0, ds), jnp.where(own, 0.0, p)))
            dqs.append((_mm(parts[0][0], k_own[:, sl]) + _mm(parts[1][0], k_prev[:, sl])) * SCALE)
            for b, (ds_b, p_b) in enumerate(parts):
                dks[b].append(_mm(ds_b, qg, _TN) * SCALE)
                dvs[b].append(_mm(p_b, do, _TN))
            dsink = -(ps * delta)
            for r in range(HPG):
                tot = jnp.sum(jnp.where(hrow == r, dsink, 0.0), axis=0, keepdims=True)
                acc = acc + jnp.where(orow == g * HPG + r, tot, 0.0)
        dq_ref[...] = _rows_to_heads(dqs)
        so = pl.multiple_of(n * Q, Q)
        sp = pl.multiple_of(jnp.maximum(n - 1, 0) * Q, Q)
        dk_ref[pl.ds(so, Q), :] += jnp.concatenate(dks[0], axis=1)
        dv_ref[pl.ds(so, Q), :] += jnp.concatenate(dvs[0], axis=1)
        dk_ref[pl.ds(sp, Q), :] += jnp.concatenate(dks[1], axis=1)
        dv_ref[pl.ds(sp, Q), :] += jnp.concatenate(dvs[1], axis=1)
        ds_ref[...] += acc

    return pl.pallas_call(
        body, name="attn_bwd", grid=(L // Q,),
        in_specs=[pl.BlockSpec(memory_space=pltpu.SMEM), _rows(512, Q),
                  pl.BlockSpec((L, LANES), lambda n: (0, 4)), pl.BlockSpec((L, LANES), lambda n: (0, 5)),
                  _rows(512, Q, 2)],
        out_specs=[_rows(512, Q), _full((L, LANES)), _full((L, LANES)), _full((8, LANES))],
        out_shape=[_sds((L, 512)), _sds((L, LANES)), _sds((L, LANES)), _sds((8, LANES))],
        compiler_params=_cparams(("arbitrary",)),
    )(sinks, qk, qk, qkv, dcat)


N_PAIR = SSD_HEADS // 2
P = 64
OFF_B = SSD_WIDTH
OFF_C = SSD_WIDTH + SSD_GROUPS * SSD_STATE


def _expand_matrix():
    i = lax.broadcasted_iota(jnp.int32, (LANES, SSD_WIDTH), 0)
    j = lax.broadcasted_iota(jnp.int32, (LANES, SSD_WIDTH), 1)
    return jnp.where(j // P == i, 1.0, 0.0).astype(F32)


def _ssd_chunk_common(dtr_ref, bias_ref, alog_ref):
    dt = jax.nn.softplus(dtr_ref[...] + bias_ref[0])
    a = -jnp.exp(alog_ref[0])
    adt = dt * a
    ri = lax.broadcasted_iota(jnp.int32, (Q, Q), 0)
    ci = lax.broadcasted_iota(jnp.int32, (Q, Q), 1)
    causal = ri >= ci
    tri = jnp.where(causal, 1.0, 0.0).astype(F32)
    acs = _mmx(tri, adt, exact="a")
    em = _expand_matrix()
    acs_e = _mmx(acs, em)
    dt_e = _mmx(dt, em)
    alast_e = acs_e[Q - 1:Q, :]
    return dt, a, acs, causal, tri, em, acs_e, dt_e, alast_e


def _ssd_fwd(xbc, dtr, bias, alog, d_e, i):
    L = xbc.shape[0]
    nc = L // Q

    def body(xbc_ref, dtr_ref, bias_ref, alog_ref, de_ref, y_ref, hp_ref, st_ref):
        @pl.when(pl.program_id(0) == 0)
        def _():
            st_ref[...] = jnp.zeros_like(st_ref)

        dt, a, acs, causal, tri, em, acs_e, dt_e, alast_e = _ssd_chunk_common(dtr_ref, bias_ref, alog_ref)
        acs_t = acs.T
        x = xbc_ref[:, :SSD_WIDTH]
        xdt = x * dt_e
        ea_e = jnp.exp(acs_e)
        xds = xdt * jnp.exp(alast_e - acs_e)
        cd_e = jnp.exp(alast_e)
        lane = lax.broadcasted_iota(jnp.int32, (Q, LANES), 1)
        lo = lane < P
        for g in range(SSD_GROUPS):
            bg = xbc_ref[:, OFF_B + g * SSD_STATE:OFF_B + (g + 1) * SSD_STATE]
            cg = xbc_ref[:, OFF_C + g * SSD_STATE:OFF_C + (g + 1) * SSD_STATE]
            cb = _mm(cg, bg, _NT)
            for pp in range(N_PAIR // SSD_GROUPS):
                pr = g * (N_PAIR // SSD_GROUPS) + pp
                sl = slice(pr * LANES, (pr + 1) * LANES)
                xdt_p = xdt[:, sl]
                yd = jnp.zeros((Q, LANES), F32)
                for half in range(2):
                    h = 2 * pr + half
                    rowb = jnp.broadcast_to(acs_t[h:h + 1, :], (Q, Q))
                    lm = jnp.exp(jnp.where(causal, rowb.T - rowb, -jnp.inf))
                    xh = jnp.where(lo if half == 0 else ~lo, xdt_p, 0.0)
                    yd = yd + _mm(cb * lm, xh)
                hp = st_ref[pr]
                hp_ref[0, pr] = hp
                yoff = _mm(cg, hp) * ea_e[:, sl]
                y_ref[:, sl] = yd + yoff + x[:, sl] * de_ref[0, :, sl]
                st_ref[pr] = hp * cd_e[:, sl] + _mm(bg, xds[:, sl], _TN)

    return pl.pallas_call(
        body, name="ssd_fwd", grid=(nc,),
        in_specs=[_rows(SSD_XBC, Q), _rows(LANES, Q), _lp(LANES, i), _lp(LANES, i), _lp(SSD_WIDTH, i)],
        out_specs=[_rows(SSD_WIDTH, Q), pl.BlockSpec((1, N_PAIR, SSD_STATE, LANES), lambda c: (c, 0, 0, 0))],
        out_shape=[_sds((L, SSD_WIDTH)), _sds((nc, N_PAIR, SSD_STATE, LANES))],
        scratch_shapes=[pltpu.VMEM((N_PAIR, SSD_STATE, LANES), F32)],
        compiler_params=_cparams(("arbitrary",)),
    )(xbc, dtr, bias, alog, d_e)


def _ssd_bwd(xbc, dtr, bias, alog, d_e, i, hprev, dy):
    L = xbc.shape[0]
    nc = L // Q
    rev = lambda c: (nc - 1 - c, 0)

    def body(xbc_ref, dtr_ref, bias_ref, alog_ref, de_ref, hp_ref, dy_ref,
             dxbc_ref, ddtr_ref, dbias_ref, dalog_ref, dd_ref, dst_ref):
        @pl.when(pl.program_id(0) == 0)
        def _():
            dst_ref[...] = jnp.zeros_like(dst_ref)
            dbias_ref[...] = jnp.zeros_like(dbias_ref)
            dalog_ref[...] = jnp.zeros_like(dalog_ref)
            dd_ref[...] = jnp.zeros_like(dd_ref)

        dt, a, acs, causal, tri, em, acs_e, dt_e, alast_e = _ssd_chunk_common(dtr_ref, bias_ref, alog_ref)
        acs_t = acs.T
        x = xbc_ref[:, :SSD_WIDTH]
        dy = dy_ref[...]
        xdt = x * dt_e
        ea_e = jnp.exp(acs_e)
        dse = jnp.exp(alast_e - acs_e)
        xds = xdt * dse
        cd_e = jnp.exp(alast_e)
        lane = lax.broadcasted_iota(jnp.int32, (Q, LANES), 1)
        lo = lane < P
        sub = lax.broadcasted_iota(jnp.int32, (Q, Q), 0)
        lan = lax.broadcasted_iota(jnp.int32, (Q, Q), 1)

        da_rows = jnp.zeros((Q, Q), F32)
        da_cols_t = jnp.zeros((Q, Q), F32)
        dxdt_parts = []
        wyoff_parts = []
        dcd_parts = []
        dxds_parts = []
        for g in range(SSD_GROUPS):
            bg = xbc_ref[:, OFF_B + g * SSD_STATE:OFF_B + (g + 1) * SSD_STATE]
            cg = xbc_ref[:, OFF_C + g * SSD_STATE:OFF_C + (g + 1) * SSD_STATE]
            cb = _mm(cg, bg, _NT)
            dcb = jnp.zeros((Q, Q), F32)
            dcg = jnp.zeros((Q, SSD_STATE), F32)
            dbg = jnp.zeros((Q, SSD_STATE), F32)
            for pp in range(N_PAIR // SSD_GROUPS):
                pr = g * (N_PAIR // SSD_GROUPS) + pp
                sl = slice(pr * LANES, (pr + 1) * LANES)
                xdt_p = xdt[:, sl]
                dy_p = dy[:, sl]
                dxdt_p = jnp.zeros((Q, LANES), F32)
                for half in range(2):
                    h = 2 * pr + half
                    hm = lo if half == 0 else ~lo
                    rowb = jnp.broadcast_to(acs_t[h:h + 1, :], (Q, Q))
                    lm = jnp.exp(jnp.where(causal, rowb.T - rowb, -jnp.inf))
                    m = cb * lm
                    dyh = jnp.where(hm, dy_p, 0.0)
                    gmat = _mm(dyh, xdt_p, _NT)
                    w = gmat * m
                    da_rows = da_rows + jnp.where(lan == h, jnp.sum(w, axis=1, keepdims=True), 0.0)
                    da_cols_t = da_cols_t + jnp.where(sub == h, jnp.sum(w, axis=0, keepdims=True), 0.0)
                    dcb = dcb + gmat * lm
                    dxdt_p = dxdt_p + _mm(m, dyh, _TN)
                hp = hp_ref[0, pr]
                dt_off = dy_p * ea_e[:, sl]
                t_off = _mm(cg, hp)
                wyoff_parts.append(dt_off * t_off)
                dcg = dcg + _mm(dt_off, hp, _NT)
                dhp = _mm(cg, dt_off, _TN)
                dS = dst_ref[pr]
                dxds_p = _mm(bg, dS)
                dbg = dbg + _mm(xds[:, sl], dS, _NT)
                dxds_parts.append(dxds_p)
                dxdt_parts.append(dxdt_p + dxds_p * dse[:, sl])
                dcd_parts.append(jnp.sum(dS * hp, axis=0, keepdims=True))
                dst_ref[pr] = dS * cd_e[:, sl] + dhp
            dcg = dcg + _mm(dcb, bg)
            dbg = dbg + _mm(dcb, cg, _TN)
            dxbc_ref[:, OFF_C + g * SSD_STATE:OFF_C + (g + 1) * SSD_STATE] = dcg
            dxbc_ref[:, OFF_B + g * SSD_STATE:OFF_B + (g + 1) * SSD_STATE] = dbg
        dxdt = jnp.concatenate(dxdt_parts, axis=1)
        dxds = jnp.concatenate(dxds_parts, axis=1)
        wyoff = jnp.concatenate(wyoff_parts, axis=1)
        dcd = jnp.concatenate(dcd_parts, axis=1)
        dxbc_ref[:, :SSD_WIDTH] = dy * de_ref[0] + dxdt * dt_e
        zds = dxds * xds
        dacs = _mmx(wyoff - zds, em, _NT) + da_rows - da_cols_t.T
        dalast = _mmx(jnp.broadcast_to(jnp.sum(zds, axis=0, keepdims=True) + dcd * cd_e, (8, SSD_WIDTH)), em, _NT)[0:1, :]
        dacs = dacs + jnp.where(sub == Q - 1, dalast, 0.0)
        dadt = _mmx(tri, dacs, _TN, exact="a")
        ddt = dadt * a + _mmx(dxdt * x, em, _NT)
        ddtr = ddt * _sigmoid(dtr_ref[...] + bias_ref[0])
        ddtr_ref[...] = ddtr.astype(ddtr_ref.dtype)
        row0 = lax.broadcasted_iota(jnp.int32, (8, LANES), 0) == 0
        dbias_ref[...] += jnp.where(row0, jnp.sum(ddtr, axis=0, keepdims=True), 0.0)
        dalog_ref[...] += jnp.where(row0, jnp.sum(dadt * dt, axis=0, keepdims=True) * a, 0.0)
        ddx = _mmx(jnp.broadcast_to(jnp.sum(dy * x, axis=0, keepdims=True), (8, SSD_WIDTH)), em, _NT)
        dd_ref[...] += jnp.where(row0, ddx, 0.0)

    acc = _full((8, LANES))
    return pl.pallas_call(
        body, name="ssd_bwd", grid=(nc,),
        in_specs=[pl.BlockSpec((Q, SSD_XBC), rev), pl.BlockSpec((Q, LANES), rev),
                  _lp(LANES, i), _lp(LANES, i), _lp(SSD_WIDTH, i),
                  pl.BlockSpec((1, N_PAIR, SSD_STATE, LANES), lambda c: (nc - 1 - c, 0, 0, 0)), pl.BlockSpec((Q, SSD_WIDTH), rev)],
        out_specs=[pl.BlockSpec((Q, SSD_XBC), rev), pl.BlockSpec((Q, LANES), rev), acc, acc, acc],
        out_shape=[_sds((L, SSD_XBC)), _sds((L, LANES), _MM), _sds((8, LANES)), _sds((8, LANES)), _sds((8, LANES))],
        scratch_shapes=[pltpu.VMEM((N_PAIR, SSD_STATE, LANES), F32)],
        compiler_params=_cparams(("arbitrary",)),
    )(xbc, dtr, bias, alog, d_e, hprev, dy)


def _rope_tables(L):
    inv_freq = ROPE_THETA ** (-jnp.arange(0, ATTN_HEAD_DIM, 2, dtype=F32) / ATTN_HEAD_DIM)
    ang = jnp.arange(L, dtype=F32)[:, None] * inv_freq[None, :]
    return jnp.tile(jnp.cos(ang), (1, 4)), jnp.tile(jnp.sin(ang), (1, 4))


def _stacked_params(small, conv_w, cm_w):
    row = lambda a: a[:, None, :]
    pad = lambda a: jnp.pad(a, ((0, 0), (0, LANES - a.shape[1])))[:, None, :]
    return dict(
        nw_mix=row(small["norm_mix_w"]), conv_w=conv_w, conv_b=row(small["ssd_conv_b"]),
        dt_bias=pad(small["ssd_dt_bias"]), a_log=pad(small["ssd_a_log"]),
        d_e=row(jnp.repeat(small["ssd_d"], P, axis=1)), snw=row(small["ssd_norm_w"]),
        qw=row(jnp.tile(small["q_norm_w"], (1, 2))), kw=row(jnp.tile(small["k_norm_w"], (1, 2))),
        sinks=small["attn_sinks"].reshape(-1), cm_w=cm_w, cm_b=row(small["cm_dw_b"]),
        ln_w=row(small["cm_ln_w"]), ln_b=row(small["cm_ln_b"]), nw_mlp=row(small["norm_mlp_w"]))


def _layer_fwd(x, p, w_in8, after_in_proj, late_weights, before_down, i, cos, sin, after):
    w_in = _w_in_regroup(w_in8, after)
    h, z, xbc, qkv, glu, dtr = _in_proj(x, p["nw_mix"], i, w_in)
    xbc_c = _ssd_conv_fwd(xbc, p["conv_w"], p["conv_b"], i, after_in_proj(z))
    y_ssd, hprev = _ssd_fwd(xbc_c, dtr, p["dt_bias"], p["a_log"], p["d_e"], i)
    qk = _qk_prep(qkv, p["qw"], p["kw"], i, cos, sin)
    attn = _attn_fwd(qk, qkv, p["sinks"], i)
    c = _cm_conv_fwd(glu, p["cm_w"], p["cm_b"], i)
    ycat = _mix_post(y_ssd, z, attn, c, p["snw"], p["ln_w"], p["ln_b"], i)
    w_out8, w_up8, w_down8 = late_weights(ycat)
    x1 = _mm_res(ycat, w_out8, i, x, "out_proj", ycat)
    hm, up, act = _mlp_up(x1, p["nw_mlp"], i, w_up8)
    x2 = _mm_res(act, w_down8, i, x1, "mlp_down", before_down(act))
    saved = dict(x=x, h=h, z=z, xbc=xbc, qkv=qkv, glu=glu, dtr=dtr, xbc_c=xbc_c, y_ssd=y_ssd, hprev=hprev,
                 qk=qk, c=c, ycat=ycat, x1=x1, hm=hm, up=up, act=act, w_in=w_in,
                 w_out8=w_out8, w_up8=w_up8, w_down8=w_down8)
    return x2, saved


def _layer_bwd_mlp(dx2, p, i, s, after):
    d_up = _mlp_down_bwd(dx2, s["w_down8"], i, s["up"], after)
    g_down = _mm_tn(s["act"], dx2, "dw_down", 512, D_MODEL, "rows")
    g_up = _mm_tn(s["hm"], d_up, "dw_up", D_MODEL, FF_SHARD, "cols")
    dx1, g_nw_mlp = _mlp_up_bwd(d_up, s["w_up8"], i, dx2, s["x1"], p["nw_mlp"])
    dcat = _out_proj_bwd(dx1, s["w_out8"], i, g_nw_mlp)
    g_out = _mm_tn(s["ycat"], dx1, "dw_out", 512, D_MODEL, "rows")
    return dx1, dcat, g_out, g_up, g_down, g_nw_mlp


def _layer_bwd_mix(dx1, dcat, g_nw_mlp, p, i, s, cos, sin, after):
    dy_ssd, dz, dc, g_snw, g_lw, g_lb = _mix_post_bwd(dcat, s["y_ssd"], s["z"], s["c"], p["snw"], p["ln_w"], p["ln_b"], i, after)
    da, dg, g_cmw, g_cmb = _cm_conv_bwd(s["glu"], p["cm_w"], i, dc)
    dq, dk, dv, dsk = _attn_bwd(s["qk"], s["qkv"], p["sinks"], i, dcat)
    dqk_raw, g_qw, g_kw = _qk_prep_bwd(dq, dk, s["qkv"], p["qw"], p["kw"], i, cos, sin)
    dxbc_c, ddtr, g_bias, g_alog, g_d = _ssd_bwd(s["xbc_c"], s["dtr"], p["dt_bias"], p["a_log"], p["d_e"], i, s["hprev"], dy_ssd)
    dxbc, g_convw, g_convb = _ssd_conv_bwd(s["xbc"], p["conv_w"], p["conv_b"], i, dxbc_c)
    dx, g_nw_mix, du = _in_proj_bwd([dz, dxbc, dqk_raw, dv, da, dg, ddtr], s["w_in"], dx1, s["x"], p["nw_mix"], i)
    g_in = _g_in_split(_mm_tn(s["h"], du, "dw_in", 512, 640, "flat"))
    half = ATTN_HEAD_DIM
    small = dict(
        norm_mix_w=g_nw_mix[0], ssd_conv_b=g_convb[0], ssd_dt_bias=g_bias[0, :SSD_HEADS], ssd_a_log=g_alog[0, :SSD_HEADS],
        ssd_d=g_d[0, :SSD_HEADS], ssd_norm_w=g_snw[0], q_norm_w=g_qw[0, :half] + g_qw[0, half:],
        k_norm_w=g_kw[0, :half] + g_kw[0, half:], attn_sinks=dsk[:, 0],
        cm_dw_b=g_cmb[0], cm_ln_w=g_lw[0], cm_ln_b=g_lb[0], norm_mlp_w=g_nw_mlp[0],
        ssd_conv_w=g_convw, cm_dw_w=g_cmw)
    return dx, g_in, small


MESH = pl.DeviceIdType.MESH
_ANY = pl.BlockSpec(memory_space=pl.ANY)


def _coords():
    return lax.axis_index("x"), lax.axis_index("y"), lax.axis_index("c")


def _all_gather(xs, name):
    nt = len(xs)

    def body(*refs):
        x_refs, out_refs = refs[:nt], refs[nt:2 * nt]
        send_sems, recv_sems, local_sems = refs[2 * nt:]
        x, y, c = _coords()
        me, sibling = (x, y, c), (x, y, 1 - c)
        chips = [(1 - x, y), (x, 1 - y), (1 - x, 1 - y)]

        def slot(t, px, py, pc):
            return out_refs[t].at[4 * px + 2 * py + pc]

        def copy(t, k, block, to, src=None):
            return pltpu.make_async_remote_copy(
                src_ref=slot(t, *block) if src is None else src, dst_ref=slot(t, *block),
                send_sem=send_sems.at[7 * t + k], recv_sem=recv_sems.at[7 * t + k], device_id=to, device_id_type=MESH)

        mine = [pltpu.make_async_copy(x_refs[t], slot(t, *me), local_sems.at[t]) for t in range(nt)]
        for cp in mine:
            cp.start()
        first = []
        for t in range(nt):
            first.append(copy(t, 0, me, sibling, src=x_refs[t]))
            first += [copy(t, 1 + j, me, (*chip, c), src=x_refs[t]) for j, chip in enumerate(chips)]
        for cp in first:
            cp.start()
        passed = []
        for j, chip in enumerate(chips):
            for t in range(nt):
                copy(t, 1 + j, (*chip, c), me).wait_recv()
                passed.append(copy(t, 4 + j, (*chip, c), sibling))
                passed[-1].start()
        for t in range(nt):
            copy(t, 0, sibling, me).wait_recv()
            for j, chip in enumerate(chips):
                copy(t, 4 + j, (*chip, 1 - c), me).wait_recv()
        for cp in first + passed:
            cp.wait_send()
        for cp in mine:
            cp.wait()

    return pl.pallas_call(
        body, name=name, out_shape=[_sds((N_DEV,) + a.shape, a.dtype) for a in xs],
        in_specs=[_ANY] * nt, out_specs=[_ANY] * nt,
        scratch_shapes=[pltpu.SemaphoreType.DMA((7 * nt,)), pltpu.SemaphoreType.DMA((7 * nt,)), pltpu.SemaphoreType.DMA((nt,))],
    )(*xs)


def _peer_chips(x, y):
    return [(1 - x, y), (x, 1 - y), (1 - x, 1 - y)]


def _copies_per_tensor(kind):
    return {"gather": 3, "forward": 4, "scatter": 7, "scatter_chips": 3}[kind]


def _ici_copies(src_refs, land_refs, send_sems, recv_sems, kind):
    x, y, c = _coords()
    if kind == "forward":
        sends, recvs = [], []
        for t, d in enumerate(land_refs):
            for j, (px, py) in enumerate([(x, y)] + _peer_chips(x, y)):
                sems = dict(send_sem=send_sems.at[4 * t + j], recv_sem=recv_sems.at[4 * t + j],
                            device_id=(x, y, 1 - c), device_id_type=MESH)
                src = d.at[4 * px + 2 * py + c]
                sends.append(pltpu.make_async_remote_copy(src_ref=src, dst_ref=src, **sems))
                recvs.append(pltpu.make_async_remote_copy(src_ref=src, dst_ref=d.at[4 * px + 2 * py + 1 - c], **sems))
        return sends, recvs
    flip = lambda v, bit: 1 - v if bit else v
    if kind == "scatter":
        peers = [(flip(x, k & 4), flip(y, k & 2), flip(c, k & 1)) for k in range(1, N_DEV)]
    else:
        peers = [(px, py, c) for px, py in _peer_chips(x, y)]
    index = (lambda px, py, pc: 2 * px + py) if kind == "scatter_chips" else (lambda px, py, pc: 4 * px + 2 * py + pc)
    me = index(x, y, c)
    sends, recvs = [], []
    for t, d in enumerate(land_refs):
        for j, p in enumerate(peers):
            src = d.at[me] if kind == "gather" else src_refs[t].at[index(*p)]
            k = len(peers) * t + j
            sems = dict(send_sem=send_sems.at[k], recv_sem=recv_sems.at[k], device_id=p, device_id_type=MESH)
            sends.append(pltpu.make_async_remote_copy(src_ref=src, dst_ref=d.at[me], **sems))
            recvs.append(pltpu.make_async_remote_copy(src_ref=src, dst_ref=d.at[index(*p)], **sems))
    return sends, recvs


_HBM = pl.BlockSpec(memory_space=pltpu.HBM)
_SEMS = pl.BlockSpec(memory_space=pltpu.SEMAPHORE)
_EFFECT = pltpu.SideEffectType.DATAFLOW_SIDE_EFFECTING


def _hbm(a):
    return pltpu.with_memory_space_constraint(a, pltpu.HBM)


def _ici_start(srcs, lands, after, name, kind):
    ns, n = len(srcs), len(lands)
    nt = ns + n

    def body(*refs):
        sends, _ = _ici_copies(refs[:ns], refs[ns:nt], refs[nt + 1], refs[nt + 2], kind)
        for cp in sends:
            cp.start()
        refs[-1][...] = jnp.zeros_like(refs[-1])

    thru = srcs + lands
    out = pl.pallas_call(
        body, name=name,
        out_shape=(pltpu.SemaphoreType.DMA((_copies_per_tensor(kind) * n,)),) * 2
        + tuple(pltpu.HBM(a.shape, a.dtype) for a in thru) + (_sds((8, LANES)),),
        in_specs=[_HBM] * nt + [_ANY],
        out_specs=(_SEMS, _SEMS) + (_HBM,) * nt + (pl.BlockSpec(memory_space=pltpu.VMEM),),
        input_output_aliases={k: 2 + k for k in range(nt)},
        compiler_params=pltpu.CompilerParams(has_side_effects=_EFFECT),
    )(*[_hbm(a) for a in thru], after)
    return out[0], out[1], list(out[2:2 + ns]), list(out[2 + ns:2 + nt]), out[-1]


def _ici_wait(started, after, name, kind):
    send_sems, recv_sems, srcs, lands, _ = started
    ns, n = len(srcs), len(lands)
    nt = ns + n

    def body(*refs):
        sends, recvs = _ici_copies(refs[:ns], refs[ns:nt], refs[nt], refs[nt + 1], kind)
        for s, r in zip(sends, recvs):
            s.wait_send()
            r.wait_recv()

    thru = srcs + lands
    out = pl.pallas_call(
        body, name=name, out_shape=tuple(pltpu.HBM(a.shape, a.dtype) for a in thru),
        in_specs=[_HBM] * nt + [_SEMS, _SEMS, _ANY], out_specs=(_HBM,) * nt,
        input_output_aliases={k: k for k in range(nt)},
        compiler_params=pltpu.CompilerParams(has_side_effects=_EFFECT),
    )(*thru, send_sems, recv_sems, after)
    return list(out[:ns]), list(out[ns:])


def _ag_d2d(lands):
    n = len(lands)

    def body(*refs):
        in_refs, out_refs = refs[:n], refs[n:2 * n]
        send_sems, recv_sems = refs[2 * n:]
        x, y, c = _coords()
        sends, recvs = [], []
        for t in range(n):
            for k, (px, py) in enumerate([(x, y)] + _peer_chips(x, y)):
                sems = dict(send_sem=send_sems.at[4 * t + k], recv_sem=recv_sems.at[4 * t + k],
                            device_id=(x, y, 1 - c), device_id_type=MESH)
                src = in_refs[t].at[4 * px + 2 * py + c]
                sends.append(pltpu.make_async_remote_copy(src_ref=src, dst_ref=out_refs[t].at[4 * px + 2 * py + c], **sems))
                recvs.append(pltpu.make_async_remote_copy(src_ref=src, dst_ref=out_refs[t].at[4 * px + 2 * py + 1 - c], **sems))
        for cp in sends:
            cp.start()
        for cp in recvs:
            cp.wait_recv()
        for cp in sends:
            cp.wait_send()

    return pl.pallas_call(
        body, name="ag_d2d", out_shape=[_sds(a.shape, a.dtype) for a in lands],
        in_specs=[_ANY] * n, out_specs=[_ANY] * n,
        input_output_aliases={k: k for k in range(n)},
        scratch_shapes=[pltpu.SemaphoreType.DMA((4 * n,)), pltpu.SemaphoreType.DMA((4 * n,))],
    )(*lands)


def _rs_sib(grads):
    nt = len(grads)

    def body(*refs):
        s_refs, ra_refs = refs[:nt], refs[nt:2 * nt]
        send_sems, recv_sems = refs[2 * nt:]
        x, y, c = _coords()
        cps = [pltpu.make_async_remote_copy(
            src_ref=s_refs[t].at[:, 1 - c], dst_ref=ra_refs[t], send_sem=send_sems.at[t], recv_sem=recv_sems.at[t],
            device_id=(x, y, 1 - c), device_id_type=MESH) for t in range(nt)]
        for cp in cps:
            cp.start()
        for cp in cps:
            cp.wait()

    return pl.pallas_call(
        body, name="rs_sibling",
        out_shape=[_sds((4,) + g.shape[2:], g.dtype) for g in grads],
        in_specs=[_ANY] * nt, out_specs=[_ANY] * nt,
        scratch_shapes=[pltpu.SemaphoreType.DMA((nt,)), pltpu.SemaphoreType.DMA((nt,))],
    )(*grads)


def _rs_add(grads, ras, core):
    nt = len(grads)

    def body(c_ref, *refs):
        s_refs, ra_refs, q_refs, rb_refs = refs[:nt], refs[nt:2 * nt], refs[2 * nt:3 * nt], refs[3 * nt:]
        for t in range(nt):
            q = (s_refs[t][0, 0].astype(F32) + ra_refs[t][0].astype(F32)).astype(q_refs[t].dtype)
            q_refs[t][0] = q
            rb_refs[t][0] = q

    nr = 4
    own = [pl.BlockSpec((1, 1, g.shape[2] // nr, g.shape[3]), lambda j, r, c: (j, c[0], r, 0)) for g in grads]
    blk = [pl.BlockSpec((1, g.shape[2] // nr, g.shape[3]), lambda j, r, c: (j, r, 0)) for g in grads]
    return pl.pallas_call(
        body, name="rs_add", out_shape=[_sds(r.shape, r.dtype) for r in ras] * 2,
        grid_spec=pltpu.PrefetchScalarGridSpec(num_scalar_prefetch=1, grid=(4, nr), in_specs=own + blk, out_specs=blk * 2),
        compiler_params=_cparams(("parallel", "parallel")),
    )(core, *grads, *ras)


def _adamw(w, g, m, v):
    m = ADAM_B1 * m + (1.0 - ADAM_B1) * g
    v = ADAM_B2 * v + (1.0 - ADAM_B2) * jnp.square(g)
    m_hat = m / (1.0 - ADAM_B1 ** ADAM_STEP)
    v_hat = v / (1.0 - ADAM_B2 ** ADAM_STEP)
    delta = -ADAM_LR * (m_hat / (jnp.sqrt(v_hat) + ADAM_EPS) + ADAM_WD * w)
    return delta, m, v


def _sum_partials(s_own, rb_ref, me, acc_ref):
    if s_own is None:
        g = rb_ref[0].astype(F32)
        for j in range(1, rb_ref.shape[0]):
            g = g + rb_ref[j].astype(F32)
        acc_ref[...] = g
        return
    for d0 in range(N_DEV):
        @pl.when(me == d0)
        def _():
            g = None
            for d in range(N_DEV):
                term = (s_own if d == d0 else rb_ref[d]).astype(F32)
                g = term if g is None else g + term
            acc_ref[...] = g


def _rs_final(s, rb, me, w, m, v, outs, l):
    _, R, C = w.shape
    cp = rb.shape[2]
    own = [] if s is None else [s]

    def body(me_ref, *refs):
        s_ref = None if s is None else refs[0]
        rb_ref, w_ref, m_ref, v_ref = refs[len(own):len(own) + 4]
        g_ref, d_ref, m2_ref, v2_ref, acc_ref = refs[len(own) + 8:]
        _sum_partials(None if s is None else s_ref[0], rb_ref, me_ref[0], acc_ref)
        g = acc_ref[...][:, :C]
        g_ref[0] = g
        d_ref[0], m2_ref[0], v2_ref[0] = _adamw(w_ref[0], g, m_ref[0], v_ref[0])

    blk = pl.BlockSpec((1, TM, C), lambda r, me: (l, r, 0))
    return pl.pallas_call(
        body, name="rs_final_adamw", out_shape=[_sds(w.shape)] * 4,
        grid_spec=pltpu.PrefetchScalarGridSpec(
            num_scalar_prefetch=1, grid=(R // TM,),
            in_specs=[pl.BlockSpec((1, TM, cp), lambda r, me: (me[0], r, 0))] * len(own)
            + [pl.BlockSpec((rb.shape[0], TM, cp), lambda r, me: (0, r, 0)), blk, blk, blk] + [_ANY] * 4,
            out_specs=[blk] * 4, scratch_shapes=[pltpu.VMEM((TM, cp), F32)]),
        input_output_aliases={5 + len(own) + k: k for k in range(4)},
        compiler_params=_cparams(("parallel",)),
    )(me, *own, rb, w, m, v, *outs)


def _rs_final_w_in(s, rb, me, wt, mt, vt, outs, l):
    shard = (W_IN_SHARD, D_MODEL)
    own = [] if s is None else [s]

    def body(me_ref, rb_hbm, *refs):
        wt_ref, mt_ref, vt_ref = refs[len(own):len(own) + 3]
        g_ref, d_ref, m2_ref, v2_ref, rb_ref, sbuf, acc_ref, bufs, obufs, sems = refs[len(own) + 7:]
        me = me_ref[0]
        fetch = pltpu.make_async_copy(rb_hbm, rb_ref, sems.at[8])
        fetch.start()
        loads = [pltpu.make_async_copy(src.at[:, l, :], bufs.at[k], sems.at[k]) for k, src in enumerate((wt_ref, mt_ref, vt_ref))]
        if own:
            loads.append(pltpu.make_async_copy(refs[0].at[me], sbuf, sems.at[7]))
        for cp in loads:
            cp.start()
        fetch.wait()
        if own:
            loads[3].wait()
        _sum_partials(sbuf[...] if own else None, rb_ref, me, acc_ref)
        g = acc_ref[...].T[:W_IN_SHARD]
        for cp in loads[:3]:
            cp.wait()
        obufs[0] = g
        obufs[1], obufs[2], obufs[3] = _adamw(bufs[0], g, bufs[1], bufs[2])
        stores = [pltpu.make_async_copy(obufs.at[k], dst.at[:, l, :], sems.at[3 + k])
                  for k, dst in enumerate((g_ref, d_ref, m2_ref, v2_ref))]
        for cp in stores:
            cp.start()
        for cp in stores:
            cp.wait()

    return pl.pallas_call(
        body, name="rs_final_adamw_w_in",
        in_specs=[pl.BlockSpec(memory_space=pltpu.SMEM)] + [_ANY] * (8 + len(own)),
        out_specs=[_ANY] * 4, out_shape=[_sds(wt.shape)] * 4,
        input_output_aliases={5 + len(own) + k: k for k in range(4)},
        scratch_shapes=[pltpu.VMEM(rb.shape, rb.dtype), pltpu.VMEM(rb.shape[1:], rb.dtype), pltpu.VMEM(rb.shape[1:], F32),
                        pltpu.VMEM((3,) + shard, F32), pltpu.VMEM((4,) + shard, F32), pltpu.SemaphoreType.DMA((9,))],
        compiler_params=_cparams(),
    )(me, rb, *own, wt, mt, vt, *outs)


def _sum8(g8):
    _, R, C = g8.shape

    def body(g_ref, o_ref):
        acc = g_ref[0]
        for d in range(1, N_DEV):
            acc = acc + g_ref[d]
        o_ref[...] = acc

    return pl.pallas_call(body, name="small_sum", out_shape=_sds((R, C)))(g8)


def _adamw_small(w, g, m, v):
    def body(w_ref, g_ref, m_ref, v_ref, d_ref, m2_ref, v2_ref):
        d_ref[...], m2_ref[...], v2_ref[...] = _adamw(w_ref[...], g_ref[...], m_ref[...], v_ref[...])

    return pl.pallas_call(body, name="small_adamw", out_shape=[_sds(w.shape)] * 3)(w, g, m, v)


REP = (("norm_mix_w", 1024), ("ssd_conv_b", 1536), ("ssd_dt_bias", 16), ("ssd_a_log", 16), ("ssd_d", 16),
       ("ssd_norm_w", 1024), ("q_norm_w", 64), ("k_norm_w", 64), ("attn_sinks", 8), ("cm_dw_b", 512),
       ("cm_ln_w", 512), ("cm_ln_b", 512), ("norm_mlp_w", 1024))
WEIGHTS = ("norm_mix_w", "w_in", "ssd_conv_w", "ssd_conv_b", "ssd_dt_bias", "ssd_a_log", "ssd_d", "ssd_norm_w",
           "q_norm_w", "k_norm_w", "attn_sinks", "cm_dw_w", "cm_dw_b", "cm_ln_w", "cm_ln_b", "w_out", "norm_mlp_w",
           "w_mlp_up", "w_mlp_down")
BIG = ("w_in", "w_out", "w_mlp_up", "w_mlp_down")
N_REP = DEPTH * sum(n for _, n in REP)
CONVW_SHARD = SSD_XBC // N_DEV
CMW_SHARD = CM_CHANNELS // N_DEV


def _to_rows(flat, rows):
    return jnp.pad(flat, (0, rows * LANES - flat.shape[0])).reshape(rows, LANES)


def kernel(x, norm_mix_w, w_in, ssd_conv_w, ssd_conv_b, ssd_dt_bias, ssd_a_log, ssd_d, ssd_norm_w, q_norm_w, k_norm_w, attn_sinks, cm_dw_w, cm_dw_b, cm_ln_w, cm_ln_b, w_out, norm_mlp_w, w_mlp_up, w_mlp_down, loss_target, m_norm_mix_w, m_w_in, m_ssd_conv_w, m_ssd_conv_b, m_ssd_dt_bias, m_ssd_a_log, m_ssd_d, m_ssd_norm_w, m_q_norm_w, m_k_norm_w, m_attn_sinks, m_cm_dw_w, m_cm_dw_b, m_cm_ln_w, m_cm_ln_b, m_w_out, m_norm_mlp_w, m_w_mlp_up, m_w_mlp_down, v_norm_mix_w, v_w_in, v_ssd_conv_w, v_ssd_conv_b, v_ssd_dt_bias, v_ssd_a_log, v_ssd_d, v_ssd_norm_w, v_q_norm_w, v_k_norm_w, v_attn_sinks, v_cm_dw_w, v_cm_dw_b, v_cm_ln_w, v_cm_ln_b, v_w_out, v_norm_mlp_w, v_w_mlp_up, v_w_mlp_down):
    w = dict(norm_mix_w=norm_mix_w, w_in=w_in, ssd_conv_w=ssd_conv_w, ssd_conv_b=ssd_conv_b, ssd_dt_bias=ssd_dt_bias, ssd_a_log=ssd_a_log, ssd_d=ssd_d, ssd_norm_w=ssd_norm_w, q_norm_w=q_norm_w, k_norm_w=k_norm_w, attn_sinks=attn_sinks, cm_dw_w=cm_dw_w, cm_dw_b=cm_dw_b, cm_ln_w=cm_ln_w, cm_ln_b=cm_ln_b, w_out=w_out, norm_mlp_w=norm_mlp_w, w_mlp_up=w_mlp_up, w_mlp_down=w_mlp_down)
    m = dict(norm_mix_w=m_norm_mix_w, w_in=m_w_in, ssd_conv_w=m_ssd_conv_w, ssd_conv_b=m_ssd_conv_b, ssd_dt_bias=m_ssd_dt_bias, ssd_a_log=m_ssd_a_log, ssd_d=m_ssd_d, ssd_norm_w=m_ssd_norm_w, q_norm_w=m_q_norm_w, k_norm_w=m_k_norm_w, attn_sinks=m_attn_sinks, cm_dw_w=m_cm_dw_w, cm_dw_b=m_cm_dw_b, cm_ln_w=m_cm_ln_w, cm_ln_b=m_cm_ln_b, w_out=m_w_out, norm_mlp_w=m_norm_mlp_w, w_mlp_up=m_w_mlp_up, w_mlp_down=m_w_mlp_down)
    v = dict(norm_mix_w=v_norm_mix_w, w_in=v_w_in, ssd_conv_w=v_ssd_conv_w, ssd_conv_b=v_ssd_conv_b, ssd_dt_bias=v_ssd_dt_bias, ssd_a_log=v_ssd_a_log, ssd_d=v_ssd_d, ssd_norm_w=v_ssd_norm_w, q_norm_w=v_q_norm_w, k_norm_w=v_k_norm_w, attn_sinks=v_attn_sinks, cm_dw_w=v_cm_dw_w, cm_dw_b=v_cm_dw_b, cm_ln_w=v_cm_ln_w, cm_ln_b=v_cm_ln_b, w_out=v_w_out, norm_mlp_w=v_norm_mlp_w, w_mlp_up=v_w_mlp_up, w_mlp_down=v_w_mlp_down)
    L = x.shape[1]
    xi, yi, ci = _coords()
    me = 4 * xi + 2 * yi + ci
    n_conv = DEPTH * SSD_CONV * CONVW_SHARD
    n_cm = DEPTH * CM_CONV * CMW_SHARD

    conv_rows = 88
    cw8, = _all_gather([_to_rows(jnp.concatenate([ssd_conv_w.reshape(-1), cm_dw_w.reshape(-1)]), conv_rows)], "ag_conv_w")
    cw8 = cw8.reshape(N_DEV, -1)
    conv_full = cw8[:, :n_conv].reshape(N_DEV, DEPTH, SSD_CONV, CONVW_SHARD).transpose(1, 2, 0, 3).reshape(DEPTH, SSD_CONV, SSD_XBC)
    cm_full = cw8[:, n_conv:n_conv + n_cm].reshape(N_DEV, DEPTH, CM_CONV, CMW_SHARD).transpose(1, 2, 0, 3).reshape(DEPTH, CM_CONV, CM_CHANNELS)
    me1 = jnp.reshape(me, (1,)).astype(jnp.int32)
    casts = [_cast_w_in(w_in, me1), _cast_shard(w_out, me1), _cast_shard(w_mlp_up, me1), _cast_shard(w_mlp_down, me1)]
    shards = [[c[l] for c in casts] for l in range(DEPTH)]

    def gather_start(lands, after):
        return _ici_start([], lands, after, "ag_ici_start", "gather")

    def gather_finish(started, after):
        return _ag_d2d(_ici_wait(started, after, "ag_ici_wait", "gather")[1])

    cos, sin = _rope_tables(L)
    p = _stacked_params({k: w[k] for k, _ in REP}, conv_full, cm_full)
    saved = []
    h = x[0]
    first = gather_start(shards[0][:1], cw8)
    rest0 = gather_start(shards[0][1:], first[4])
    w_in8, = gather_finish(first, rest0[4])
    token, rest = rest0[4], None
    late_rest = []
    for i in range(DEPTH):
        nxt, nxt_rest, fwd = None, None, []
        if i + 1 < DEPTH:
            if i == 0:
                nxt = gather_start(shards[1][:1], w_in8)
                nxt_rest = gather_start(shards[1][1:], nxt[4])
                token = nxt_rest[4]
            else:
                nxt = gather_start(shards[i + 1], w_in8)
                token = nxt[4]

        def after_in_proj(z, late_rest=late_rest, i=i):
            if i != 1:
                return z
            lands = _ici_wait(late_rest[0], z, "ag_ici_wait", "gather")[1]
            late_rest.append(_ici_start([], lands, z, "ag_d2d_start", "forward"))
            return late_rest[1][4]

        if i == 0:
            late = lambda ycat: gather_finish(rest0, ycat)
        elif i == 1:
            late = lambda ycat: _ici_wait(late_rest[1], ycat, "ag_d2d_wait", "forward")[1]
        else:
            late = lambda ycat, r=rest: r

        def before_down(act, nxt=nxt, fwd=fwd):
            if nxt is None:
                return act
            lands = _ici_wait(nxt, act, "ag_ici_wait", "gather")[1]
            fwd.append(_ici_start([], lands, act, "ag_d2d_start", "forward"))
            return fwd[0][4]

        h, s = _layer_fwd(h, p, w_in8, after_in_proj, late, before_down, i, cos, sin, token)
        saved.append(s)
        if nxt is not None:
            got = _ici_wait(fwd[0], h, "ag_d2d_wait", "forward")[1]
            w_in8, rest = got[0], got[1:]
        if nxt_rest is not None:
            late_rest.append(nxt_rest)
    d, loss_tile = _loss_head(h, loss_target[0])

    smalls = [None] * DEPTH
    big_out = {k: [lax.empty(w[k].shape, F32) for _ in range(4)] for k in BIG}
    to_t = lambda a: jnp.transpose(a, (2, 0, 1))
    w_in_t = [to_t(t["w_in"]) for t in (w, m, v)]
    big_out["w_in"] = [lax.empty(w_in_t[0].shape, F32) for _ in range(4)]

    core = jnp.reshape(ci, (1,)).astype(jnp.int32)

    def scatter_start(grads, after, l):
        if l > 0:
            lands = [lax.empty(g.shape, g.dtype) for g in grads]
            return _ici_start(list(grads), lands, after, "rs_ici_start", "scatter")
        g4 = [g.reshape((4, 2) + g.shape[1:]) for g in grads]
        out = _rs_add(g4, _rs_sib(g4), core)
        return _ici_start(list(out[:len(g4)]), list(out[len(g4):]), after, "rs_chips_start", "scatter_chips")

    def scatter_finish(started, after, l, names):
        if l > 0:
            srcs, rbs = _ici_wait(started, after, "rs_ici_wait", "scatter")
        else:
            srcs, rbs = [None] * len(names), _ici_wait(started, after, "rs_chips_wait", "scatter_chips")[1]
        for g, rb, k in zip(srcs, rbs, names):
            if k == "w_in":
                big_out[k] = _rs_final_w_in(g, rb, me1, *w_in_t, big_out[k], l)
            else:
                big_out[k] = _rs_final(g, rb, me1, w[k], m[k], v[k], big_out[k], l)

    def gather_small_grads():
        gvec = jnp.concatenate(
            [jnp.stack([smalls[i][k] for i in range(DEPTH)]).reshape(-1) for k, _ in REP]
            + [jnp.stack([smalls[i][k] for i in range(DEPTH)]).reshape(-1) for k in ("ssd_conv_w", "cm_dw_w")]
            + [loss_tile[0, :1]])
        g_rows = -(-gvec.shape[0] // (8 * LANES)) * 8
        return _all_gather([_to_rows(gvec, g_rows)], "ag_small_grads")[0]

    token, pending = loss_tile, []
    for i in reversed(range(DEPTH)):
        dx1, dcat, g_out, g_up, g_down, g_nw_mlp = _layer_bwd_mlp(d, p, i, saved[i], token)
        started = []
        if i == 0:
            started.append((scatter_start([g_out, g_up, g_down], dcat, i), i, BIG[1:]))
        d, g_in, smalls[i] = _layer_bwd_mix(dx1, dcat, g_nw_mlp, p, i, saved[i], cos, sin,
                                            started[0][0][4] if started else g_nw_mlp)
        if i == 0:
            g8 = gather_small_grads()
            started.append((scatter_start([g_in], g8, i), i, BIG[:1]))
        else:
            started.append((scatter_start([g_in, g_out, g_up, g_down], d, i), i, BIG))
        token = started[-1][0][4]
        for st, l, names in pending:
            scatter_finish(st, token, l, names)
        pending = started
    for st, l, names in pending:
        scatter_finish(st, token, l, names)

    gsum = _sum8(g8).reshape(-1)
    o_conv = N_REP
    o_cm = o_conv + DEPTH * SSD_CONV * SSD_XBC
    o_loss = o_cm + DEPTH * CM_CONV * CM_CHANNELS
    g_conv = lax.dynamic_slice_in_dim(gsum[o_conv:o_cm].reshape(DEPTH, SSD_CONV, SSD_XBC), me * CONVW_SHARD, CONVW_SHARD, axis=2)
    g_cm = lax.dynamic_slice_in_dim(gsum[o_cm:o_loss].reshape(DEPTH, CM_CONV, CM_CHANNELS), me * CMW_SHARD, CMW_SHARD, axis=2)
    loss = gsum[o_loss]
    s_rows = -(-(N_REP + n_conv + n_cm) // (8 * LANES)) * 8

    def pack_small(t):
        return _to_rows(jnp.concatenate([t[k].reshape(-1) for k, _ in REP] + [t["ssd_conv_w"].reshape(-1), t["cm_dw_w"].reshape(-1)]), s_rows)

    g_small = _to_rows(jnp.concatenate([gsum[:N_REP], g_conv.reshape(-1), g_cm.reshape(-1)]), s_rows)
    small_out = [g_small] + list(_adamw_small(pack_small(w), g_small, pack_small(m), pack_small(v)))

    def unpack_small(t):
        flat = t.reshape(-1)
        out, off = {}, 0
        for k, n in REP:
            out[k] = flat[off:off + DEPTH * n].reshape(DEPTH, n)
            off += DEPTH * n
        out["ssd_conv_w"] = flat[off:off + n_conv].reshape(DEPTH, SSD_CONV, CONVW_SHARD)
        off += n_conv
        out["cm_dw_w"] = flat[off:off + n_cm].reshape(DEPTH, CM_CONV, CMW_SHARD)
        return out

    outs = [loss, d[None]]
    for j, small_t in enumerate(small_out):
        t = unpack_small(small_t)
        for k in BIG:
            t[k] = big_out[k][j]
        t["w_in"] = jnp.transpose(t["w_in"], (1, 2, 0))
        outs += [t[k] for k in WEIGHTS]
    return tuple(outs)
```
